```python
import jax
import jax.numpy as jnp
from jax import lax
import numpy as np

D_MODEL = 1024
BATCH = 16
SEQ = 2048
DEPTH = 1
DEC_BATCH = 16
DEC_SEQ = 16
PAST_LEN = 1024

CHUNK = 64
Q_BLOCK = 128
N_MEM = 256
H_R = 4
DK_R = 64
DV_R = 128
ROPE_BASE = 10000.0
H_F = 8
D_F = 64
H_M = 4
D_M = 128
N_BRANCH = 3
N_GROUPS = 4
E_PER_GROUP = 8
D_EXPERT = 256
TOP_K = 2
EPS = 1e-6

RQK_W = H_R * DK_R
RV_W = H_R * DV_R
FOX_W = H_F * D_F
MEM_W = H_M * D_M
GATE_W = N_BRANCH * D_MODEL
SPLIT_SIZES = (RQK_W, RQK_W, RV_W, RV_W, FOX_W, FOX_W, FOX_W, H_F, MEM_W, GATE_W)
SPLIT_POINTS = tuple(sum(SPLIT_SIZES[:i + 1]) for i in range(len(SPLIT_SIZES) - 1))
N_IN = sum(SPLIT_SIZES)

kernel_name = 'hybrid_retention_fox_hmoe_stream_step'

F32 = jnp.float32


def rmsnorm(x, g):
    xf = x.astype(F32)
    y = xf * lax.rsqrt(jnp.mean(xf * xf, axis=-1, keepdims=True) + EPS)
    return (y * g.astype(F32)).astype(x.dtype)


def heads(t, h, d):
    return t.reshape(t.shape[:-1] + (h, d))


def rope(t, pos):
    half = t.shape[-1] // 2
    inv = ROPE_BASE ** (-jnp.arange(half, dtype=F32) / half)
    ang = pos.astype(F32)[:, None] * inv[None, :]
    c = jnp.cos(ang)[:, None, :]
    s = jnp.sin(ang)[:, None, :]
    t1 = t[..., :half].astype(F32)
    t2 = t[..., half:].astype(F32)
    return jnp.concatenate([t1 * c - t2 * s, t1 * s + t2 * c], axis=-1).astype(t.dtype)


def retention_log_decay():
    return jnp.log1p(-jnp.exp2(-5.0 - jnp.arange(H_R, dtype=F32)))


def retention_chunk(state, q, k, v, lg):
    n = q.shape[1]
    idx = jnp.arange(n, dtype=F32)
    diff = idx[:, None] - idx[None, :]
    causal = diff >= 0
    dmat = jnp.where(causal[None], jnp.exp(jnp.where(causal, diff, 0.0)[None] * lg[:, None, None]), 0.0)
    sc = jnp.einsum('bihk,bjhk->bhij', q, k) * dmat[None]
    intra = jnp.einsum('bhij,bjhv->bihv', sc, v)
    q_dec = jnp.exp((idx + 1.0)[:, None] * lg[None, :])
    inter = jnp.einsum('bihk,bhkv->bihv', q * q_dec[None, :, :, None], state)
    k_dec = jnp.exp((n - 1.0 - idx)[:, None] * lg[None, :])
    new_state = (jnp.exp(n * lg)[None, :, None, None] * state
                 + jnp.einsum('bjhk,bjhv->bhkv', k * k_dec[None, :, :, None], v))
    return new_state, intra + inter


def retention_prompt(q, k, v):
    b, s, h, dk = q.shape
    nc = s // CHUNK
    lg = retention_log_decay()

    def to_chunks(t):
        return jnp.swapaxes(t.astype(F32).reshape((b, nc, CHUNK) + t.shape[2:]), 0, 1)

    def step(state, inp):
        qc, kc, vc = inp
        return retention_chunk(state, qc, kc, vc, lg)

    s0 = jnp.zeros((b, h, dk, DV_R), F32)
    s_fin, o = lax.scan(step, s0, (to_chunks(q), to_chunks(k), to_chunks(v)))
    return s_fin, jnp.swapaxes(o, 0, 1).reshape(b, s, h, DV_R)


def retention_out(o, rg, g_ret_out):
    normed = rmsnorm(o, g_ret_out).astype(rg.dtype)
    gated = normed * jax.nn.silu(heads(rg, H_R, DV_R))
    return gated.reshape(gated.shape[:2] + (RV_W,))


def fox_attend(q, fq, qpos, k, v, fk, kpos):
    s = jnp.einsum('bqhd,bkhd->bhqk', q, k).astype(F32) * (D_F ** -0.5)
    s = s + (jnp.swapaxes(fq, 1, 2)[:, :, :, None] - jnp.swapaxes(fk, 1, 2)[:, :, None, :])
    mask = qpos[:, None] >= kpos[None, :]
    s = jnp.where(mask[None, None], s, -jnp.inf)
    p = jax.nn.softmax(s, axis=-1).astype(v.dtype)
    return jnp.einsum('bhqk,bkhd->bqhd', p, v)


def fox_prompt(q, k, v, logf):
    b, s, h, d = q.shape
    nb = s // Q_BLOCK
    f_cum = jnp.cumsum(logf, axis=1)
    pos = jnp.arange(s)
    qb = jnp.swapaxes(q.reshape(b, nb, Q_BLOCK, h, d), 0, 1)
    fb = jnp.swapaxes(f_cum.reshape(b, nb, Q_BLOCK, h), 0, 1)
    pb = pos.reshape(nb, Q_BLOCK)
    out = lax.map(lambda a: fox_attend(a[0], a[1], a[2], k, v, f_cum, pos), (qb, fb, pb))
    return jnp.swapaxes(out, 0, 1).reshape(b, s, FOX_W)


def mem_kv(mem, g_mem_in, w_mem_kv, g_mem_k):
    hm = rmsnorm(mem, g_mem_in)
    mk, mv = jnp.split(hm @ w_mem_kv, 2, axis=-1)
    return rmsnorm(heads(mk, H_M, D_M), g_mem_k), heads(mv, H_M, D_M)


def mem_attend(q, k, v):
    s = jnp.einsum('bqhd,bkhd->bhqk', q, k).astype(F32) * (D_M ** -0.5)
    p = jax.nn.softmax(s, axis=-1).astype(v.dtype)
    o = jnp.einsum('bhqk,bkhd->bqhd', p, v)
    return o.reshape(o.shape[:2] + (MEM_W,))


def prep(x, pos, g_norm1, w_in, b_forget, g_fox_q, g_fox_k, g_mem_q):
    h = rmsnorm(x, g_norm1)
    z = h @ w_in
    rq, rk, rv, rg, fq, fk, fv, ff, mq, gl = jnp.split(z, SPLIT_POINTS, axis=-1)
    rq = rope(heads(rq, H_R, DK_R), pos)
    rk = rope(heads(rk, H_R, DK_R), pos) * (DK_R ** -0.5)
    rv = heads(rv, H_R, DV_R)
    fq = rmsnorm(heads(fq, H_F, D_F), g_fox_q)
    fk = rmsnorm(heads(fk, H_F, D_F), g_fox_k)
    fv = heads(fv, H_F, D_F)
    logf = jax.nn.log_sigmoid((ff + b_forget).astype(F32))
    mq = rmsnorm(heads(mq, H_M, D_M), g_mem_q)
    gates = jax.nn.sigmoid(gl.astype(F32)).reshape(gl.shape[:-1] + (N_BRANCH, D_MODEL)).astype(x.dtype)
    return rq, rk, rv, rg, fq, fk, fv, logf, mq, gates


def hier_moe(h, w_rg, b_rg, w_re, b_re, w_g, w_u, w_d):
    shp = h.shape
    t = h.reshape(-1, shp[-1])
    lg = (t @ w_rg + b_rg).astype(F32)
    oh_g = jax.nn.one_hot(jnp.argmax(lg, axis=-1), N_GROUPS, dtype=F32)
    p_sel = jnp.sum(jax.nn.softmax(lg, axis=-1) * oh_g, axis=-1, keepdims=True)
    le = (t @ w_re + b_re).astype(F32).reshape(-1, N_GROUPS, E_PER_GROUP)
    le_sel = jnp.einsum('tge,tg->te', le, oh_g)
    top_v, top_i = lax.top_k(le_sel, TOP_K)
    wk = jax.nn.softmax(top_v, axis=-1) * p_sel
    w_exp = jnp.sum(jax.nn.one_hot(top_i, E_PER_GROUP, dtype=F32) * wk[..., None], axis=1)
    comb = (oh_g[:, :, None] * w_exp[:, None, :]).astype(t.dtype)
    out = jnp.zeros_like(t)
    for g in range(N_GROUPS):
        a = jnp.einsum('td,edf->tef', t, w_g[g])
        u = jnp.einsum('td,edf->tef', t, w_u[g])
        out = out + jnp.einsum('tef,efd->td', jax.nn.silu(a) * u * comb[:, g, :, None], w_d[g])
    return out.reshape(shp)


def merge_and_ffn(x, o_r, o_f, o_m, gates, w_br_ret, w_br_fox, w_br_mem, w_out, g_norm2,
                  w_route_group, b_route_group, w_route_expert, b_route_expert,
                  w_exp_gate, w_exp_up, w_exp_down):
    merged = (gates[..., 0, :] * (o_r @ w_br_ret)
              + gates[..., 1, :] * (o_f @ w_br_fox)
              + gates[..., 2, :] * (o_m @ w_br_mem))
    x = x + merged @ w_out
    return x + hier_moe(rmsnorm(x, g_norm2), w_route_group, b_route_group, w_route_expert,
                        b_route_expert, w_exp_gate, w_exp_up, w_exp_down)


def setup_inputs(seed: int = 0) -> dict:
    key = jax.random.key(seed)
    ks = jax.random.split(key, 40)
    L = DEPTH

    def nrm(k, shape, scale=1.0):
        return jax.random.normal(k, shape, F32) * scale

    def gain(k, shape):
        return 1.0 + 0.1 * jax.random.normal(k, shape, F32)

    return {
        'x_prompt': nrm(ks[0], (BATCH, SEQ, D_MODEL)),
        'x_sample': nrm(ks[1], (DEC_BATCH, DEC_SEQ, D_MODEL)),
        'state_ret': nrm(ks[2], (L, DEC_BATCH, H_R, DK_R, DV_R), 0.5),
        'cache_fox_k': nrm(ks[3], (L, DEC_BATCH, PAST_LEN, H_F, D_F)),
        'cache_fox_v': nrm(ks[4], (L, DEC_BATCH, PAST_LEN, H_F, D_F)),
        'cache_fox_logf': jax.nn.log_sigmoid(nrm(ks[5], (L, DEC_BATCH, PAST_LEN, H_F)) + 2.0),
        'cache_mem_k': nrm(ks[6], (L, DEC_BATCH, N_MEM, H_M, D_M)),
        'cache_mem_v': nrm(ks[7], (L, DEC_BATCH, N_MEM, H_M, D_M)),
        'mem_prompt': nrm(ks[8], (BATCH, N_MEM, D_MODEL)),
        'g_norm1': gain(ks[9], (L, D_MODEL)),
        'w_in': nrm(ks[10], (L, D_MODEL, N_IN), D_MODEL ** -0.5),
        'b_forget': 2.0 + nrm(ks[11], (L, H_F), 0.1),
        'g_fox_q': gain(ks[12], (L, D_F)),
        'g_fox_k': gain(ks[13], (L, D_F)),
        'g_mem_q': gain(ks[14], (L, D_M)),
        'g_mem_in': gain(ks[15], (L, D_MODEL)),
        'w_mem_kv': nrm(ks[16], (L, D_MODEL, 2 * MEM_W), D_MODEL ** -0.5),
        'g_mem_k': gain(ks[17], (L, D_M)),
        'g_ret_out': gain(ks[18], (L, DV_R)),
        'w_br_ret': nrm(ks[19], (L, RV_W, D_MODEL), RV_W ** -0.5),
        'w_br_fox': nrm(ks[20], (L, FOX_W, D_MODEL), FOX_W ** -0.5),
        'w_br_mem': nrm(ks[21], (L, MEM_W, D_MODEL), MEM_W ** -0.5),
        'w_out': nrm(ks[22], (L, D_MODEL, D_MODEL), D_MODEL ** -0.5),
        'g_norm2': gain(ks[23], (L, D_MODEL)),
        'w_route_group': nrm(ks[24], (L, D_MODEL, N_GROUPS), D_MODEL ** -0.5),
        'b_route_group': nrm(ks[25], (L, N_GROUPS), 0.01),
        'w_route_expert': nrm(ks[26], (L, D_MODEL, N_GROUPS * E_PER_GROUP), D_MODEL ** -0.5),
        'b_route_expert': nrm(ks[27], (L, N_GROUPS * E_PER_GROUP), 0.01),
        'w_exp_gate': nrm(ks[28], (L, N_GROUPS, E_PER_GROUP, D_MODEL, D_EXPERT), D_MODEL ** -0.5),
        'w_exp_up': nrm(ks[29], (L, N_GROUPS, E_PER_GROUP, D_MODEL, D_EXPERT), D_MODEL ** -0.5),
        'w_exp_down': nrm(ks[30], (L, N_GROUPS, E_PER_GROUP, D_EXPERT, D_MODEL), D_EXPERT ** -0.5),
    }


def reference(x_prompt, x_sample, state_ret, cache_fox_k, cache_fox_v, cache_fox_logf,
              cache_mem_k, cache_mem_v, mem_prompt, g_norm1, w_in, b_forget, g_fox_q, g_fox_k,
              g_mem_q, g_mem_in, w_mem_kv, g_mem_k, g_ret_out, w_br_ret, w_br_fox, w_br_mem,
              w_out, g_norm2, w_route_group, b_route_group, w_route_expert, b_route_expert,
              w_exp_gate, w_exp_up, w_exp_down):
    seq = x_prompt.shape[1]
    n_new = x_sample.shape[1]
    past = cache_fox_k.shape[2]
    pos_p = jnp.arange(seq)
    pos_s = past + jnp.arange(n_new)
    kpos_s = jnp.arange(past + n_new)
    lg = retention_log_decay()
    xp = x_prompt
    xs = x_sample
    ret_p, fk_p, fv_p, fl_p, mk_p, mv_p = [], [], [], [], [], []
    ret_s, fk_s, fv_s, fl_s = [], [], [], []
    for l in range(DEPTH):
        rq, rk, rv, rg, fq, fk, fv, logf, mq, gates = prep(
            xp, pos_p, g_norm1[l], w_in[l], b_forget[l], g_fox_q[l], g_fox_k[l], g_mem_q[l])
        s_fin, o_ret = retention_prompt(rq, rk, rv)
        o_r = retention_out(o_ret, rg, g_ret_out[l])
        o_f = fox_prompt(fq, fk, fv, logf)
        mk, mv = mem_kv(mem_prompt, g_mem_in[l], w_mem_kv[l], g_mem_k[l])
        o_m = mem_attend(mq, mk, mv)
        xp = merge_and_ffn(xp, o_r, o_f, o_m, gates, w_br_ret[l], w_br_fox[l], w_br_mem[l],
                           w_out[l], g_norm2[l], w_route_group[l], b_route_group[l],
                           w_route_expert[l], b_route_expert[l], w_exp_gate[l], w_exp_up[l],
                           w_exp_down[l])
        ret_p.append(s_fin)
        fk_p.append(fk)
        fv_p.append(fv)
        fl_p.append(logf)
        mk_p.append(mk)
        mv_p.append(mv)

        rq, rk, rv, rg, fq, fk, fv, logf, mq, gates = prep(
            xs, pos_s, g_norm1[l], w_in[l], b_forget[l], g_fox_q[l], g_fox_k[l], g_mem_q[l])
        s_new, o_ret = retention_chunk(state_ret[l].astype(F32), rq.astype(F32), rk.astype(F32),
                                       rv.astype(F32), lg)
        o_r = retention_out(o_ret, rg, g_ret_out[l])
        f_all = jnp.cumsum(jnp.concatenate([cache_fox_logf[l].astype(F32), logf], axis=1), axis=1)
        k_all = jnp.concatenate([cache_fox_k[l].astype(fk.dtype), fk], axis=1)
        v_all = jnp.concatenate([cache_fox_v[l].astype(fv.dtype), fv], axis=1)
        o_f = fox_attend(fq, f_all[:, past:], pos_s, k_all, v_all, f_all, kpos_s)
        o_f = o_f.reshape(o_f.shape[:2] + (FOX_W,))
        o_m = mem_attend(mq, cache_mem_k[l].astype(mq.dtype), cache_mem_v[l].astype(mq.dtype))
        xs = merge_and_ffn(xs, o_r, o_f, o_m, gates, w_br_ret[l], w_br_fox[l], w_br_mem[l],
                           w_out[l], g_norm2[l], w_route_group[l], b_route_group[l],
                           w_route_expert[l], b_route_expert[l], w_exp_gate[l], w_exp_up[l],
                           w_exp_down[l])
        ret_s.append(s_new)
        fk_s.append(fk)
        fv_s.append(fv)
        fl_s.append(logf)
    return (xp, xs,
            jnp.stack(ret_p), jnp.stack(fk_p), jnp.stack(fv_p), jnp.stack(fl_p),
            jnp.stack(mk_p), jnp.stack(mv_p),
            jnp.stack(ret_s), jnp.stack(fk_s), jnp.stack(fv_s), jnp.stack(fl_s))
```

```python
import functools

import jax
import jax.numpy as jnp
from jax import lax
from jax.experimental import pallas as pl
from jax.experimental.pallas import tpu as pltpu

F32 = jnp.float32
BF16 = jnp.bfloat16

D_MODEL = 1024
H_R, DK_R, DV_R = 4, 64, 128
H_F, D_F = 8, 64
H_M, D_M = 4, 128
N_MEM = 256
N_GROUPS, E_PER_GROUP, D_EXPERT = 4, 8, 256
ROPE_BASE = 10000.0
EPS = 1e-6

RQK_W = H_R * DK_R
RV_W = H_R * DV_R
FOX_W = H_F * D_F
MEM_W = H_M * D_M
GATE_W = 3 * D_MODEL
LANES = 128

C_RQ = 0
C_RK = C_RQ + RQK_W
C_RV = C_RK + RQK_W
C_RG = C_RV + RV_W
C_FQ = C_RG + RV_W
C_FK = C_FQ + FOX_W
C_FV = C_FK + FOX_W
C_MQ = C_FV + FOX_W
C_GT = C_MQ + MEM_W
C_FF = C_GT + GATE_W
N_IN_PAD = C_FF + LANES

VMEM_LIMIT = 56 * 1024 * 1024


def _cparams(sem):
    return pltpu.CompilerParams(dimension_semantics=sem, vmem_limit_bytes=VMEM_LIMIT)


def _const_spec(shape):
    nd = len(shape)
    return pl.BlockSpec(shape, lambda *_: (0,) * nd, pipeline_mode=pl.Buffered(1))


def _dot(a, b):
    return jnp.dot(a, b, preferred_element_type=F32)


def _dot_nt(a, b):
    return lax.dot_general(a, b, (((1,), (1,)), ((), ())), preferred_element_type=F32)


def _dot_tn(a, b):
    return lax.dot_general(a, b, (((0,), (0,)), ((), ())), preferred_element_type=F32)


def _lane_rmsnorm(z):
    return z * lax.rsqrt(jnp.mean(z * z, axis=-1, keepdims=True) + EPS)


def _prep_kernel(x_ref, g1_ref, w_ref, cos_ref, sin_ref, bf_ref, gfq_ref, gfk_ref, gmq_ref, bd_ref,
                 rq_ref, rk_ref, rv_ref, rg_ref, fq_ref, fk_ref, fv_ref, lf_ref, mq_ref, gt_ref):
    x = x_ref[...]
    h = (x * lax.rsqrt(jnp.mean(x * x, axis=-1, keepdims=True) + EPS) * g1_ref[...]).astype(BF16)

    def proj(c0, width):
        return _dot(h, w_ref[:, c0:c0 + width])

    cos = cos_ref[...]
    sin = sin_ref[...]
    lane = lax.broadcasted_iota(jnp.int32, cos.shape, 1)
    first_half = (lane % DK_R) < (DK_R // 2)

    def rope(z):
        swapped = jnp.where(first_half, pltpu.roll(z, LANES - DK_R // 2, 1), pltpu.roll(z, DK_R // 2, 1))
        return z * cos + swapped * sin

    zq = proj(C_RQ, RQK_W)
    zk = proj(C_RK, RQK_W)
    for p in range(RQK_W // LANES):
        sl = slice(p * LANES, (p + 1) * LANES)
        rq_ref[:, sl] = rope(zq[:, sl])
        rk_ref[:, sl] = rope(zk[:, sl]) * (DK_R ** -0.5)
    rv_ref[...] = proj(C_RV, RV_W)
    rg_ref[...] = proj(C_RG, RV_W)

    def head64_norm(z, g_ref):
        ms = _dot((z * z).astype(BF16), bd_ref[...])
        return z * lax.rsqrt(ms + EPS) * g_ref[...]

    fq_ref[...] = (head64_norm(proj(C_FQ, FOX_W), gfq_ref) * (D_F ** -0.5)).astype(BF16)
    fk_ref[...] = head64_norm(proj(C_FK, FOX_W), gfk_ref)
    fv_ref[...] = proj(C_FV, FOX_W)

    zm = proj(C_MQ, MEM_W)
    for hh in range(H_M):
        sl = slice(hh * D_M, (hh + 1) * D_M)
        mq_ref[:, sl] = (_lane_rmsnorm(zm[:, sl]) * gmq_ref[...]).astype(BF16)

    for b in range(3):
        gt_ref[:, b * D_MODEL:(b + 1) * D_MODEL] = jax.nn.sigmoid(proj(C_GT + b * D_MODEL, D_MODEL)).astype(BF16)

    v = proj(C_FF, LANES) + bf_ref[...]
    lf = jnp.minimum(v, 0.0) - jnp.log1p(jnp.exp(-jnp.abs(v)))
    lf_ref[...] = lf[:, :H_F]


def _prep(x2d, tables, tm, n_pos_tiles, g1, w_in_r, bf_pad, gfq, gfk, gmq, bd):
    t = x2d.shape[0]
    cos_t, sin_t = tables
    row = lambda w: pl.BlockSpec((tm, w), lambda i: (i, 0))
    pos = pl.BlockSpec((tm, LANES), lambda i: (i % n_pos_tiles, 0))
    out_shapes = [
        jax.ShapeDtypeStruct((t, RQK_W), F32), jax.ShapeDtypeStruct((t, RQK_W), F32),
        jax.ShapeDtypeStruct((t, RV_W), F32), jax.ShapeDtypeStruct((t, RV_W), F32),
        jax.ShapeDtypeStruct((t, FOX_W), BF16), jax.ShapeDtypeStruct((t, FOX_W), F32),
        jax.ShapeDtypeStruct((t, FOX_W), F32), jax.ShapeDtypeStruct((t, H_F), F32),
        jax.ShapeDtypeStruct((t, MEM_W), BF16), jax.ShapeDtypeStruct((t, GATE_W), BF16),
    ]
    out_specs = [row(RQK_W), row(RQK_W), row(RV_W), row(RV_W), row(FOX_W), row(FOX_W), row(FOX_W),
                 row(H_F), row(MEM_W), row(GATE_W)]
    return pl.pallas_call(
        _prep_kernel,
        grid=(t // tm,),
        in_specs=[row(D_MODEL), _const_spec((1, D_MODEL)), _const_spec((D_MODEL, N_IN_PAD)), pos, pos,
                  _const_spec((1, LANES)), _const_spec((1, FOX_W)), _const_spec((1, FOX_W)),
                  _const_spec((1, D_M)), _const_spec((FOX_W, FOX_W))],
        out_specs=out_specs,
        out_shape=out_shapes,
        compiler_params=_cparams(("parallel",)),
        name="prep",
    )(x2d, g1, w_in_r, cos_t, sin_t, bf_pad, gfq, gfk, gmq, bd)


def _ret_kernel(*refs, has_init):
    if has_init:
        (rq_ref, rk_ref, rv_ref, rg_ref, dmat_ref, qdec_ref, kdec_ref, gpow_ref, gro_ref, s0_ref,
         o_ref, sfin_ref, state) = refs
    else:
        (rq_ref, rk_ref, rv_ref, rg_ref, dmat_ref, qdec_ref, kdec_ref, gpow_ref, gro_ref,
         o_ref, sfin_ref, state) = refs
    c = pl.program_id(1)

    @pl.when(c == 0)
    def _():
        state[...] = jnp.zeros(state.shape, F32)
        if has_init:
            for h in range(H_R):
                r0 = DK_R * (h % 2)
                state[h, r0:r0 + DK_R, :] = s0_ref[0, h]

    lane = lax.broadcasted_iota(jnp.int32, (rq_ref.shape[0], LANES), 1)
    for p in range(H_R // 2):
        sl = slice(p * LANES, (p + 1) * LANES)
        q2 = rq_ref[:, sl]
        k2 = rk_ref[:, sl]
        kd2 = k2 * kdec_ref[p]
        for hh in range(2):
            h = 2 * p + hh
            mine = (lane >= hh * DK_R) & (lane < (hh + 1) * DK_R)
            qm = jnp.where(mine, q2, 0.0).astype(BF16)
            kdm = jnp.where(mine, kd2, 0.0).astype(BF16)
            v = rv_ref[:, h * DV_R:(h + 1) * DV_R].astype(BF16)
            sc = _dot_nt(qm, k2.astype(BF16)) * dmat_ref[h]
            s_old = state[h]
            o = _dot(sc.astype(BF16), v) + _dot(qm, s_old.astype(BF16)) * qdec_ref[h]
            state[h] = gpow_ref[h] * s_old + _dot_tn(kdm, v)
            normed = _lane_rmsnorm(o) * gro_ref[...]
            rg = rg_ref[:, h * DV_R:(h + 1) * DV_R]
            o_ref[:, h * DV_R:(h + 1) * DV_R] = (normed * (rg * jax.nn.sigmoid(rg))).astype(BF16)

    @pl.when(c == pl.num_programs(1) - 1)
    def _():
        for h in range(H_R):
            r0 = DK_R * (h % 2)
            sfin_ref[0, h] = state[h, r0:r0 + DK_R, :]


def _retention(rq, rk, rv, rg, n_batch, seq, ch, g_ret_out, state0):
    nc = seq // ch
    lg = jnp.log1p(-jnp.exp2(-5.0 - jnp.arange(H_R, dtype=F32)))
    idx = jnp.arange(ch, dtype=F32)
    diff = idx[:, None] - idx[None, :]
    causal = diff >= 0
    dmat = jnp.where(causal[None], jnp.exp(jnp.where(causal, diff, 0.0)[None] * lg[:, None, None]), 0.0)
    q_dec = jnp.exp((idx + 1.0)[None, :] * lg[:, None])
    k_dec = jnp.exp((ch - 1.0 - idx)[None, :] * lg[:, None])
    qdec = jnp.broadcast_to(q_dec[:, :, None], (H_R, ch, DV_R))
    kdec = jnp.broadcast_to(k_dec[:, :, None], (H_R, ch, DK_R))
    kdec = kdec.reshape(H_R // 2, 2, ch, DK_R).transpose(0, 2, 1, 3).reshape(H_R // 2, ch, LANES)
    gpow = jnp.broadcast_to(jnp.exp(ch * lg)[:, None, None], (H_R, 1, DV_R))

    has_init = state0 is not None
    blk = lambda w: pl.BlockSpec((ch, w), lambda b, c: (b * nc + c, 0))
    in_specs = [blk(RQK_W), blk(RQK_W), blk(RV_W), blk(RV_W),
                _const_spec((H_R, ch, ch)), _const_spec((H_R, ch, DV_R)),
                _const_spec((H_R // 2, ch, LANES)), _const_spec((H_R, 1, DV_R)), _const_spec((1, DV_R))]
    args = [rq, rk, rv, rg, dmat, qdec, kdec, gpow, g_ret_out]
    if has_init:
        in_specs.append(pl.BlockSpec((1, H_R, DK_R, DV_R), lambda b, c: (b, 0, 0, 0)))
        args.append(state0)
    return pl.pallas_call(
        functools.partial(_ret_kernel, has_init=has_init),
        grid=(n_batch, nc),
        in_specs=in_specs,
        out_specs=[blk(RV_W), pl.BlockSpec((1, H_R, DK_R, DV_R), lambda b, c: (b, 0, 0, 0))],
        out_shape=[jax.ShapeDtypeStruct((n_batch * seq, RV_W), BF16),
                   jax.ShapeDtypeStruct((n_batch, H_R, DK_R, DV_R), F32)],
        scratch_shapes=[pltpu.VMEM((H_R, LANES, DV_R), F32)],
        compiler_params=_cparams(("parallel", "arbitrary")),
        name="retention",
    )(*args)


def _split3(x):
    hi = x.astype(BF16)
    r1 = x - hi.astype(F32)
    mid = r1.astype(BF16)
    lo = (r1 - mid.astype(F32)).astype(BF16)
    return hi, mid, lo


def _cumsum_kernel(x_ref, o_ref, *, blk):
    n = x_ref.shape[1]
    r = lax.broadcasted_iota(jnp.int32, (blk, blk), 0)
    c = lax.broadcasted_iota(jnp.int32, (blk, blk), 1)
    tri = jnp.where(r >= c, 1.0, 0.0).astype(BF16)
    carry = jnp.zeros((1, x_ref.shape[2]), F32)
    for i in range(n // blk):
        hi, mid, lo = _split3(x_ref[0, i * blk:(i + 1) * blk, :])
        cum = (_dot(tri, hi) + _dot(tri, mid)) + _dot(tri, lo) + carry
        o_ref[0, i * blk:(i + 1) * blk, :] = cum
        carry = cum[blk - 1:blk, :]


def _cumsum(x, blk):
    nb, t, w = x.shape
    spec = pl.BlockSpec((1, t, w), lambda b: (b, 0, 0))
    return pl.pallas_call(
        functools.partial(_cumsum_kernel, blk=blk),
        grid=(nb,),
        in_specs=[spec], out_specs=spec,
        out_shape=jax.ShapeDtypeStruct(x.shape, F32),
        compiler_params=_cparams(("parallel",)),
        name="cumsum",
    )(x)


def _softmax_step(s, m, l, acc, v):
    m_new = jnp.maximum(m, jnp.max(s, axis=-1, keepdims=True))
    alpha = jnp.exp(m - m_new)
    p = jnp.exp(s - m_new)
    l_new = alpha * l + jnp.sum(p, axis=-1, keepdims=True)
    acc_new = alpha * acc + _dot(p.astype(BF16), v)
    return m_new, l_new, acc_new


def _fox_kernel(q_ref, k_ref, v_ref, fcol_ref, frow_ref, o_ref, *, tq, tk):
    pair = pl.program_id(1)
    qi = pl.program_id(2)
    q2 = q_ref[...]
    lane = lax.broadcasted_iota(jnp.int32, (tq, LANES), 1)
    rows = lax.broadcasted_iota(jnp.int32, (tq, tk), 0)
    cols = lax.broadcasted_iota(jnp.int32, (tq, tk), 1)
    fcol = fcol_ref[...]
    outs = []
    for hh in range(2):
        mine = (lane >= hh * D_F) & (lane < (hh + 1) * D_F)
        qm = jnp.where(mine, q2, jnp.zeros_like(q2))
        hsel = lax.broadcasted_iota(jnp.int32, fcol.shape, 1) == (2 * pair + hh)
        fq = jnp.sum(jnp.where(hsel, fcol, 0.0), axis=-1, keepdims=True)

        def scores(j, qm=qm, fq=fq, hh=hh):
            k0 = pl.multiple_of(j * tk, tk)
            k = k_ref[pl.ds(k0, tk), :].astype(BF16)
            v = v_ref[pl.ds(k0, tk), :].astype(BF16)
            fk = frow_ref[0, pl.ds(hh, 1), pl.ds(k0, tk)]
            return _dot_nt(qm, k) + (fq - fk), v

        def body(j, carry, scores=scores):
            s, v = scores(j)
            return _softmax_step(s, *carry, v)

        init = (jnp.full((tq, 1), -jnp.inf, F32), jnp.zeros((tq, 1), F32), jnp.zeros((tq, LANES), F32))
        m, l, acc = lax.fori_loop(0, qi, body, init)
        s, v = scores(qi)
        s = jnp.where(rows >= cols, s, -jnp.inf)
        m, l, acc = _softmax_step(s, m, l, acc, v)
        outs.append((acc / l, mine))
    o_ref[...] = jnp.where(outs[0][1], outs[0][0], outs[1][0]).astype(BF16)


def _fox_prompt(fq, fk, fv, fcum, fcum_rows, n_batch, seq, tq):
    nq = seq // tq
    npair = H_F // 2
    return pl.pallas_call(
        functools.partial(_fox_kernel, tq=tq, tk=tq),
        grid=(n_batch, npair, nq),
        in_specs=[pl.BlockSpec((tq, LANES), lambda b, p, i: (b * nq + i, p)),
                  pl.BlockSpec((seq, LANES), lambda b, p, i: (b, p)),
                  pl.BlockSpec((seq, LANES), lambda b, p, i: (b, p)),
                  pl.BlockSpec((tq, H_F), lambda b, p, i: (b * nq + i, 0)),
                  pl.BlockSpec((1, 2, seq), lambda b, p, i: (b * npair + p, 0, 0))],
        out_specs=pl.BlockSpec((tq, LANES), lambda b, p, i: (b * nq + i, p)),
        out_shape=jax.ShapeDtypeStruct((n_batch * seq, FOX_W), BF16),
        compiler_params=_cparams(("parallel", "parallel", "arbitrary")),
        name="fox_prompt",
    )(fq, fk, fv, fcum, fcum_rows)


def _fox_sample_kernel(q_ref, kc_ref, vc_ref, kn_ref, vn_ref, fcol_ref, frc_ref, frn_ref, o_ref):
    pair = pl.program_id(1)
    n = q_ref.shape[0]
    q2 = q_ref[...]
    lane = lax.broadcasted_iota(jnp.int32, (n, LANES), 1)
    rows = lax.broadcasted_iota(jnp.int32, (n, n), 0)
    cols = lax.broadcasted_iota(jnp.int32, (n, n), 1)
    fcol = fcol_ref[...]
    kc = kc_ref[...].astype(BF16)
    vc = vc_ref[...].astype(BF16)
    kn = kn_ref[...].astype(BF16)
    vn = vn_ref[...].astype(BF16)
    outs = []
    for hh in range(2):
        mine = (lane >= hh * D_F) & (lane < (hh + 1) * D_F)
        qm = jnp.where(mine, q2, jnp.zeros_like(q2))
        hsel = lax.broadcasted_iota(jnp.int32, fcol.shape, 1) == (2 * pair + hh)
        fq = jnp.sum(jnp.where(hsel, fcol, 0.0), axis=-1, keepdims=True)
        s_c = _dot_nt(qm, kc) + (fq - frc_ref[0, hh:hh + 1, :])
        s_n = _dot_nt(qm, kn) + (fq - frn_ref[0, hh:hh + 1, :])
        s_n = jnp.where(rows >= cols, s_n, -jnp.inf)
        m = jnp.maximum(jnp.max(s_c, axis=-1, keepdims=True), jnp.max(s_n, axis=-1, keepdims=True))
        p_c = jnp.exp(s_c - m)
        p_n = jnp.exp(s_n - m)
        l = jnp.sum(p_c, axis=-1, keepdims=True) + jnp.sum(p_n, axis=-1, keepdims=True)
        acc = _dot(p_c.astype(BF16), vc) + _dot(p_n.astype(BF16), vn)
        outs.append((acc / l, mine))
    o_ref[...] = jnp.where(outs[0][1], outs[0][0], outs[1][0]).astype(BF16)


def _fox_sample(fq, kc, vc, kn, vn, fcol, frow_c, frow_n, n_batch, n_new, past):
    npair = H_F // 2
    bp = lambda rows_: pl.BlockSpec((rows_, LANES), lambda b, p: (b, p))
    return pl.pallas_call(
        _fox_sample_kernel,
        grid=(n_batch, npair),
        in_specs=[bp(n_new), bp(past), bp(past), bp(n_new), bp(n_new),
                  pl.BlockSpec((n_new, H_F), lambda b, p: (b, 0)),
                  pl.BlockSpec((1, 2, past), lambda b, p: (b * npair + p, 0, 0)),
                  pl.BlockSpec((1, 2, n_new), lambda b, p: (b * npair + p, 0, 0))],
        out_specs=bp(n_new),
        out_shape=jax.ShapeDtypeStruct((n_batch * n_new, FOX_W), BF16),
        compiler_params=_cparams(("parallel", "parallel")),
        name="fox_sample",
    )(fq, kc, vc, kn, vn, fcol, frow_c, frow_n)


def _memkv_kernel(m_ref, gin_ref, w_ref, gk_ref, k_ref, v_ref):
    x = m_ref[...]
    h = (x * lax.rsqrt(jnp.mean(x * x, axis=-1, keepdims=True) + EPS) * gin_ref[...]).astype(BF16)
    zk = _dot(h, w_ref[:, :MEM_W])
    for hh in range(H_M):
        sl = slice(hh * D_M, (hh + 1) * D_M)
        k_ref[:, sl] = _lane_rmsnorm(zk[:, sl]) * gk_ref[...]
    v_ref[...] = _dot(h, w_ref[:, MEM_W:])


def _memkv(mem2d, n_batch, g_mem_in, w_mem_kv, g_mem_k):
    blk = lambda w: pl.BlockSpec((N_MEM, w), lambda b: (b, 0))
    return pl.pallas_call(
        _memkv_kernel,
        grid=(n_batch,),
        in_specs=[blk(D_MODEL), _const_spec((1, D_MODEL)), _const_spec((D_MODEL, 2 * MEM_W)),
                  _const_spec((1, D_M))],
        out_specs=[blk(MEM_W), blk(MEM_W)],
        out_shape=[jax.ShapeDtypeStruct((n_batch * N_MEM, MEM_W), F32)] * 2,
        compiler_params=_cparams(("parallel",)),
        name="memkv",
    )(mem2d, g_mem_in, w_mem_kv, g_mem_k)


def _memattn_kernel(q_ref, k_ref, v_ref, o_ref):
    for hh in range(H_M):
        sl = slice(hh * D_M, (hh + 1) * D_M)
        s = _dot_nt(q_ref[:, sl], k_ref[:, sl].astype(BF16)) * (D_M ** -0.5)
        m = jnp.max(s, axis=-1, keepdims=True)
        p = jnp.exp(s - m)
        l = jnp.sum(p, axis=-1, keepdims=True)
        o_ref[:, sl] = (_dot(p.astype(BF16), v_ref[:, sl].astype(BF16)) / l).astype(BF16)


def _memattn(mq, mk, mv, n_batch, seq, tq):
    nq = seq // tq
    return pl.pallas_call(
        _memattn_kernel,
        grid=(n_batch, nq),
        in_specs=[pl.BlockSpec((tq, MEM_W), lambda b, i: (b * nq + i, 0)),
                  pl.BlockSpec((N_MEM, MEM_W), lambda b, i: (b, 0)),
                  pl.BlockSpec((N_MEM, MEM_W), lambda b, i: (b, 0))],
        out_specs=pl.BlockSpec((tq, MEM_W), lambda b, i: (b * nq + i, 0)),
        out_shape=jax.ShapeDtypeStruct((n_batch * seq, MEM_W), BF16),
        compiler_params=_cparams(("parallel", "arbitrary")),
        name="memattn",
    )(mq, mk, mv)


def _split2(x):
    hi = x.astype(BF16)
    return hi, (x - hi.astype(F32)).astype(BF16)


def _merge_kernel(x_ref, or_ref, of_ref, om_ref, gt_ref, wr_ref, wf_ref, wm_ref, wo_ref, g2_ref,
                  wrt_hi_ref, wrt_lo_ref, brt_ref, x1_ref, h2_ref, comb_ref):
    g = lambda b: gt_ref[:, b * D_MODEL:(b + 1) * D_MODEL].astype(F32)
    merged = (g(0) * _dot(or_ref[...], wr_ref[...]) + g(1) * _dot(of_ref[...], wf_ref[...])
              + g(2) * _dot(om_ref[...], wm_ref[...]))
    x1 = x_ref[...] + _dot(merged.astype(BF16), wo_ref[...])
    x1_ref[...] = x1
    h2 = x1 * lax.rsqrt(jnp.mean(x1 * x1, axis=-1, keepdims=True) + EPS) * g2_ref[...]
    h2_ref[...] = h2.astype(BF16)

    h_hi, h_lo = _split2(h2)
    logits = (_dot(h_hi, wrt_hi_ref[...]) + (_dot(h_hi, wrt_lo_ref[...]) + _dot(h_lo, wrt_hi_ref[...]))
              + brt_ref[...])
    n_e = N_GROUPS * E_PER_GROUP
    lane = lax.broadcasted_iota(jnp.int32, logits.shape, 1).astype(F32)
    neg = -jnp.inf
    first_idx = lambda mask: jnp.min(jnp.where(mask, lane, float(LANES)), axis=-1, keepdims=True)

    is_g = (lane >= n_e) & (lane < n_e + N_GROUPS)
    lg = jnp.where(is_g, logits, neg)
    mg = jnp.max(lg, axis=-1, keepdims=True)
    gsel = first_idx(lg == mg) - n_e
    p_sel = 1.0 / jnp.sum(jnp.exp(lg - mg), axis=-1, keepdims=True)

    in_grp = (lane >= gsel * E_PER_GROUP) & (lane < (gsel + 1) * E_PER_GROUP)
    le = jnp.where(in_grp, logits, neg)
    v1 = jnp.max(le, axis=-1, keepdims=True)
    i1 = first_idx(le == v1)
    le2 = jnp.where(lane == i1, neg, le)
    v2 = jnp.max(le2, axis=-1, keepdims=True)
    i2 = first_idx(le2 == v2)
    e2 = jnp.exp(v2 - v1)
    w1 = p_sel / (1.0 + e2)
    w2 = p_sel * e2 / (1.0 + e2)
    comb = jnp.where(lane == i1, w1, 0.0) + jnp.where(lane == i2, w2, 0.0)
    for gi in range(N_GROUPS):
        comb_ref[gi] = comb[:, gi * E_PER_GROUP:(gi + 1) * E_PER_GROUP]


def _merge(x2d, o_r, o_f, o_m, gates, tm, wr, wf, wm, wo, g2, wrt_hi, wrt_lo, brt):
    t = x2d.shape[0]
    row = lambda w: pl.BlockSpec((tm, w), lambda i: (i, 0))
    return pl.pallas_call(
        _merge_kernel,
        grid=(t // tm,),
        in_specs=[row(D_MODEL), row(RV_W), row(FOX_W), row(MEM_W), row(GATE_W),
                  _const_spec((RV_W, D_MODEL)), _const_spec((FOX_W, D_MODEL)), _const_spec((MEM_W, D_MODEL)),
                  _const_spec((D_MODEL, D_MODEL)), _const_spec((1, D_MODEL)),
                  _const_spec((D_MODEL, LANES)), _const_spec((D_MODEL, LANES)), _const_spec((1, LANES))],
        out_specs=[row(D_MODEL), row(D_MODEL),
                   pl.BlockSpec((N_GROUPS, tm, E_PER_GROUP), lambda i: (0, i, 0))],
        out_shape=[jax.ShapeDtypeStruct((t, D_MODEL), F32), jax.ShapeDtypeStruct((t, D_MODEL), BF16),
                   jax.ShapeDtypeStruct((N_GROUPS, t, E_PER_GROUP), F32)],
        compiler_params=_cparams(("parallel",)),
        name="merge",
    )(x2d, o_r, o_f, o_m, gates, wr, wf, wm, wo, g2, wrt_hi, wrt_lo, brt)


def _moe_kernel(h_ref, comb_ref, x1_ref, wg_ref, wu_ref, wd_ref, o_ref, act):
    g = pl.program_id(1)
    h = h_ref[...]
    comb = comb_ref[0]
    for e in range(E_PER_GROUP):
        a = _dot(h, wg_ref[0, e])
        u = _dot(h, wu_ref[0, e])
        act[:, e * D_EXPERT:(e + 1) * D_EXPERT] = (
            (a * jax.nn.sigmoid(a)) * u * comb[:, e:e + 1]).astype(BF16)
    y = _dot(act[...], wd_ref[0])

    @pl.when(g == 0)
    def _():
        o_ref[...] = x1_ref[...] + y

    @pl.when(g != 0)
    def _():
        o_ref[...] += y


def _moe(h2, comb, x1, tm, wg, wu, wd):
    t = h2.shape[0]
    return pl.pallas_call(
        _moe_kernel,
        grid=(t // tm, N_GROUPS),
        in_specs=[pl.BlockSpec((tm, D_MODEL), lambda i, g: (i, 0)),
                  pl.BlockSpec((1, tm, E_PER_GROUP), lambda i, g: (g, i, 0)),
                  pl.BlockSpec((tm, D_MODEL), lambda i, g: (i, 0)),
                  pl.BlockSpec((1, E_PER_GROUP, D_MODEL, D_EXPERT), lambda i, g: (g, 0, 0, 0)),
                  pl.BlockSpec((1, E_PER_GROUP, D_MODEL, D_EXPERT), lambda i, g: (g, 0, 0, 0)),
                  pl.BlockSpec((1, E_PER_GROUP * D_EXPERT, D_MODEL), lambda i, g: (g, 0, 0))],
        out_specs=pl.BlockSpec((tm, D_MODEL), lambda i, g: (i, 0)),
        out_shape=jax.ShapeDtypeStruct((t, D_MODEL), F32),
        scratch_shapes=[pltpu.VMEM((tm, E_PER_GROUP * D_EXPERT), BF16)],
        compiler_params=_cparams(("parallel", "arbitrary")),
        name="moe",
    )(h2, comb, x1, wg, wu, wd)


def _rope_tables(pos):
    half = DK_R // 2
    inv = ROPE_BASE ** (-jnp.arange(half, dtype=F32) / half)
    ang = pos.astype(F32)[:, None] * inv[None, :]
    c = jnp.cos(ang)
    s = jnp.sin(ang)
    return jnp.concatenate([c, c, c, c], axis=-1), jnp.concatenate([-s, s, -s, s], axis=-1)


def _pair_rows(fcum, n_batch, t):
    return jnp.swapaxes(fcum.reshape(n_batch, t, H_F), 1, 2).reshape(n_batch * (H_F // 2), 2, t)


def kernel(x_prompt, x_sample, state_ret, cache_fox_k, cache_fox_v, cache_fox_logf, cache_mem_k, cache_mem_v,
           mem_prompt, g_norm1, w_in, b_forget, g_fox_q, g_fox_k, g_mem_q, g_mem_in, w_mem_kv, g_mem_k,
           g_ret_out, w_br_ret, w_br_fox, w_br_mem, w_out, g_norm2, w_route_group, b_route_group,
           w_route_expert, b_route_expert, w_exp_gate, w_exp_up, w_exp_down):
    nb, seq, _ = x_prompt.shape
    nbs, n_new, _ = x_sample.shape
    past = cache_fox_k.shape[2]
    l = 0

    wi = w_in[l]
    o_ff = 2 * RQK_W + 2 * RV_W + 3 * FOX_W
    w_in_r = jnp.concatenate(
        [wi[:, :o_ff], wi[:, o_ff + H_F:], wi[:, o_ff:o_ff + H_F],
         jnp.zeros((D_MODEL, LANES - H_F), F32)], axis=1).astype(BF16)
    bf_pad = jnp.concatenate([b_forget[l], jnp.zeros((LANES - H_F,), F32)])[None, :]
    g1 = g_norm1[l][None, :]
    gfq = jnp.tile(g_fox_q[l], H_F)[None, :]
    gfk = jnp.tile(g_fox_k[l], H_F)[None, :]
    gmq = g_mem_q[l][None, :]
    hid = jnp.arange(FOX_W) // D_F
    bd = jnp.where(hid[:, None] == hid[None, :], 1.0 / D_F, 0.0).astype(BF16)
    wr = w_br_ret[l].astype(BF16)
    wf = w_br_fox[l].astype(BF16)
    wm = w_br_mem[l].astype(BF16)
    wo = w_out[l].astype(BF16)
    g2 = g_norm2[l][None, :]
    n_e = N_GROUPS * E_PER_GROUP
    wrt = jnp.concatenate([w_route_expert[l], w_route_group[l],
                           jnp.zeros((D_MODEL, LANES - n_e - N_GROUPS), F32)], axis=1)
    wrt_hi = wrt.astype(BF16)
    wrt_lo = (wrt - wrt_hi.astype(F32)).astype(BF16)
    brt = jnp.concatenate([b_route_expert[l], b_route_group[l],
                           jnp.zeros((LANES - n_e - N_GROUPS,), F32)])[None, :]
    wg = w_exp_gate[l].astype(BF16)
    wu = w_exp_up[l].astype(BF16)
    wd = w_exp_down[l].astype(BF16).reshape(N_GROUPS, E_PER_GROUP * D_EXPERT, D_MODEL)
    gro = g_ret_out[l][None, :]
    prep_w = (g1, w_in_r, bf_pad, gfq, gfk, gmq, bd)
    merge_w = (wr, wf, wm, wo, g2, wrt_hi, wrt_lo, brt)

    tm = 512
    xp = x_prompt.reshape(nb * seq, D_MODEL)
    (rq, rk, rv, rg, fq, fk, fv, lf, mq, gates) = _prep(
        xp, _rope_tables(jnp.arange(seq)), tm, seq // tm, *prep_w)
    o_r, s_fin = _retention(rq, rk, rv, rg, nb, seq, 256, gro, None)
    fcum = _cumsum(lf.reshape(nb, seq, H_F), 256).reshape(nb * seq, H_F)
    o_f = _fox_prompt(fq, fk, fv, fcum, _pair_rows(fcum, nb, seq), nb, seq, 256)
    mk, mv = _memkv(mem_prompt.reshape(nb * N_MEM, D_MODEL), nb, g_mem_in[l][None, :],
                    w_mem_kv[l].astype(BF16), g_mem_k[l][None, :])
    o_m = _memattn(mq, mk, mv, nb, seq, tm)
    x1, h2, comb = _merge(xp, o_r, o_f, o_m, gates, tm, *merge_w)
    y_prompt = _moe(h2, comb, x1, tm, wg, wu, wd).reshape(nb, seq, D_MODEL)

    ts = nbs * n_new
    xs = x_sample.reshape(ts, D_MODEL)
    pos_s = jnp.tile(past + jnp.arange(n_new), nbs)
    (rq_s, rk_s, rv_s, rg_s, fq_s, fk_s, fv_s, lf_s, mq_s, gates_s) = _prep(
        xs, _rope_tables(pos_s), ts, 1, *prep_w)
    o_r_s, s_new = _retention(rq_s, rk_s, rv_s, rg_s, nbs, n_new, n_new, gro, state_ret[l])
    pad = (-(past + n_new)) % 256
    lf_all = jnp.concatenate([cache_fox_logf[l], lf_s.reshape(nbs, n_new, H_F),
                              jnp.zeros((nbs, pad, H_F), F32)], axis=1)
    f_all = _cumsum(lf_all, 256)
    f_c = f_all[:, :past].reshape(nbs * past, H_F)
    f_n = f_all[:, past:past + n_new].reshape(ts, H_F)
    o_f_s = _fox_sample(fq_s, cache_fox_k[l].reshape(nbs * past, FOX_W),
                        cache_fox_v[l].reshape(nbs * past, FOX_W), fk_s, fv_s, f_n,
                        _pair_rows(f_c, nbs, past), _pair_rows(f_n, nbs, n_new), nbs, n_new, past)
    o_m_s = _memattn(mq_s, cache_mem_k[l].reshape(nbs * N_MEM, MEM_W),
                     cache_mem_v[l].reshape(nbs * N_MEM, MEM_W), nbs, n_new, n_new)
    x1_s, h2_s, comb_s = _merge(xs, o_r_s, o_f_s, o_m_s, gates_s, ts, *merge_w)
    y_sample = _moe(h2_s, comb_s, x1_s, ts, wg, wu, wd).reshape(nbs, n_new, D_MODEL)

    return (y_prompt, y_sample,
            s_fin[None], fk.reshape(1, nb, seq, H_F, D_F), fv.reshape(1, nb, seq, H_F, D_F),
            lf.reshape(1, nb, seq, H_F),
            mk.reshape(1, nb, N_MEM, H_M, D_M), mv.reshape(1, nb, N_MEM, H_M, D_M),
            s_new[None], fk_s.reshape(1, nbs, n_new, H_F, D_F), fv_s.reshape(1, nbs, n_new, H_F, D_F),
            lf_s.reshape(1, nbs, n_new, H_F))
```

```python
import functools

import jax
import jax.numpy as jnp
from jax import lax
from jax.experimental import pallas as pl
from jax.experimental.pallas import tpu as pltpu

F32 = jnp.float32
BF16 = jnp.bfloat16

D_MODEL = 1024
H_R, DK_R, DV_R = 4, 64, 128
H_F, D_F = 8, 64
H_M, D_M = 4, 128
N_MEM = 256
N_GROUPS, E_PER_GROUP, D_EXPERT = 4, 8, 256
ROPE_BASE = 10000.0
EPS = 1e-6
LOG2E = 1.4426950408889634

RQK_W = H_R * DK_R
RV_W = H_R * DV_R
FOX_W = H_F * D_F
MEM_W = H_M * D_M
GATE_W = 3 * D_MODEL
LANES = 128

C_RQ = 0
C_RK = C_RQ + RQK_W
C_RV = C_RK + RQK_W
C_RG = C_RV + RV_W
C_FQ = C_RG + RV_W
C_FK = C_FQ + FOX_W
C_FV = C_FK + FOX_W
C_MQ = C_FV + FOX_W
C_GT = C_MQ + MEM_W
C_FF = C_GT + GATE_W
N_IN_PAD = C_FF + LANES

VMEM_LIMIT = 56 * 1024 * 1024


def _cparams(sem):
    return pltpu.CompilerParams(dimension_semantics=sem, vmem_limit_bytes=VMEM_LIMIT)


def _const_spec(shape):
    nd = len(shape)
    return pl.BlockSpec(shape, lambda *_: (0,) * nd, pipeline_mode=pl.Buffered(1))


def _dot(a, b):
    return jnp.dot(a, b, preferred_element_type=F32)


def _dot_nt(a, b):
    return lax.dot_general(a, b, (((1,), (1,)), ((), ())), preferred_element_type=F32)


def _dot_tn(a, b):
    return lax.dot_general(a, b, (((0,), (0,)), ((), ())), preferred_element_type=F32)


def _lane_rmsnorm(z):
    return z * lax.rsqrt(jnp.mean(z * z, axis=-1, keepdims=True) + EPS)


def _prep_kernel(x_ref, g1_ref, w_ref, cos_ref, sin_ref, bf_ref, gfq_ref, gfk_ref, gmq_ref, bd_ref,
                 rq_ref, rk_ref, rv_ref, rg_ref, fq_ref, fk_ref, fv_ref, lf_ref, mq_ref, gt_ref, *extra,
                 seq_minor):
    x = x_ref[...]
    h = (x * lax.rsqrt(jnp.mean(x * x, axis=-1, keepdims=True) + EPS) * g1_ref[...]).astype(BF16)

    def proj(c0, width):
        return _dot(h, w_ref[:, c0:c0 + width])

    cos = cos_ref[...]
    sin = sin_ref[...]
    lane = lax.broadcasted_iota(jnp.int32, cos.shape, 1)
    first_half = (lane % DK_R) < (DK_R // 2)

    def rope(z):
        swapped = jnp.where(first_half, pltpu.roll(z, LANES - DK_R // 2, 1), pltpu.roll(z, DK_R // 2, 1))
        return z * cos + swapped * sin

    zq = proj(C_RQ, RQK_W)
    zk = proj(C_RK, RQK_W)
    for p in range(RQK_W // LANES):
        sl = slice(p * LANES, (p + 1) * LANES)
        rq_ref[:, sl] = rope(zq[:, sl])
        rk_ref[:, sl] = rope(zk[:, sl]) * (DK_R ** -0.5)
    rv_ref[...] = proj(C_RV, RV_W)
    rg_ref[...] = proj(C_RG, RV_W)

    def head64_norm(z, g_ref):
        ms = _dot((z * z).astype(BF16), bd_ref[...])
        return z * lax.rsqrt(ms + EPS) * g_ref[...]

    fq = head64_norm(proj(C_FQ, FOX_W), gfq_ref) * (D_F ** -0.5 * LOG2E)
    fk = head64_norm(proj(C_FK, FOX_W), gfk_ref)
    fv = proj(C_FV, FOX_W)
    if seq_minor:
        fq_ref[...] = fq.T.astype(BF16)
        fk_ref[...] = fk.T
        fv_ref[...] = fv.T
        extra[0][...] = fk.astype(BF16)
    else:
        fq_ref[...] = fq.astype(BF16)
        fk_ref[...] = fk
        fv_ref[...] = fv

    zm = proj(C_MQ, MEM_W)
    for hh in range(H_M):
        sl = slice(hh * D_M, (hh + 1) * D_M)
        mq_ref[:, sl] = (_lane_rmsnorm(zm[:, sl]) * gmq_ref[...]).astype(BF16)

    for b in range(3):
        gt_ref[:, b * D_MODEL:(b + 1) * D_MODEL] = jax.nn.sigmoid(proj(C_GT + b * D_MODEL, D_MODEL)).astype(BF16)

    v = proj(C_FF, LANES) + bf_ref[...]
    lf = jnp.minimum(v, 0.0) - jnp.log1p(jnp.exp(-jnp.abs(v)))
    if seq_minor:
        lf_ref[...] = lf.T[:H_F, :]
        extra[1][...] = lf[:, :H_F]
    else:
        lf_ref[...] = lf[:, :H_F]


def _prep(x2d, tables, tm, n_pos_tiles, seq_minor, g1, w_in_r, bf_pad, gfq, gfk, gmq, bd):
    t = x2d.shape[0]
    cos_t, sin_t = tables
    row = lambda w: pl.BlockSpec((tm, w), lambda i: (i, 0))
    pos = pl.BlockSpec((tm, LANES), lambda i: (i % n_pos_tiles, 0))
    sds = jax.ShapeDtypeStruct
    if seq_minor:
        nb, seq = t // (n_pos_tiles * tm), n_pos_tiles * tm
        fm = lambda rows_: pl.BlockSpec((rows_, tm), lambda i: (i // n_pos_tiles, i % n_pos_tiles))
        fox_shapes = [sds((nb * FOX_W, seq), BF16), sds((nb * FOX_W, seq), F32), sds((nb * FOX_W, seq), F32),
                      sds((nb * H_F, seq), F32)]
        fox_specs = [fm(FOX_W), fm(FOX_W), fm(FOX_W), fm(H_F)]
        extra_shapes, extra_specs = [sds((t, FOX_W), BF16), sds((t, H_F), F32)], [row(FOX_W), row(H_F)]
    else:
        fox_shapes = [sds((t, FOX_W), BF16), sds((t, FOX_W), F32), sds((t, FOX_W), F32), sds((t, H_F), F32)]
        fox_specs = [row(FOX_W), row(FOX_W), row(FOX_W), row(H_F)]
        extra_shapes, extra_specs = [], []
    out_shapes = [sds((t, RQK_W), F32), sds((t, RQK_W), F32), sds((t, RV_W), F32), sds((t, RV_W), F32),
                  *fox_shapes, sds((t, MEM_W), BF16), sds((t, GATE_W), BF16), *extra_shapes]
    out_specs = [row(RQK_W), row(RQK_W), row(RV_W), row(RV_W), *fox_specs, row(MEM_W), row(GATE_W),
                 *extra_specs]
    return pl.pallas_call(
        functools.partial(_prep_kernel, seq_minor=seq_minor),
        grid=(t // tm,),
        in_specs=[row(D_MODEL), _const_spec((1, D_MODEL)), _const_spec((D_MODEL, N_IN_PAD)), pos, pos,
                  _const_spec((1, LANES)), _const_spec((1, FOX_W)), _const_spec((1, FOX_W)),
                  _const_spec((1, D_M)), _const_spec((FOX_W, FOX_W))],
        out_specs=out_specs,
        out_shape=out_shapes,
        compiler_params=_cparams(("parallel",)),
        name="prep",
    )(x2d, g1, w_in_r, cos_t, sin_t, bf_pad, gfq, gfk, gmq, bd)


def _ret_kernel(*refs, has_init):
    if has_init:
        (rq_ref, rk_ref, rv_ref, rg_ref, dmat_ref, qdec_ref, kdec_ref, gpow_ref, gro_ref, s0_ref,
         o_ref, sfin_ref, state) = refs
    else:
        (rq_ref, rk_ref, rv_ref, rg_ref, dmat_ref, qdec_ref, kdec_ref, gpow_ref, gro_ref,
         o_ref, sfin_ref, state) = refs
    c = pl.program_id(1)

    @pl.when(c == 0)
    def _():
        state[...] = jnp.zeros(state.shape, F32)
        if has_init:
            for h in range(H_R):
                r0 = DK_R * (h % 2)
                state[h, r0:r0 + DK_R, :] = s0_ref[0, h]

    lane = lax.broadcasted_iota(jnp.int32, (rq_ref.shape[0], LANES), 1)
    for p in range(H_R // 2):
        sl = slice(p * LANES, (p + 1) * LANES)
        q2 = rq_ref[:, sl]
        k2 = rk_ref[:, sl]
        kd2 = k2 * kdec_ref[p]
        for hh in range(2):
            h = 2 * p + hh
            mine = (lane >= hh * DK_R) & (lane < (hh + 1) * DK_R)
            qm = jnp.where(mine, q2, 0.0).astype(BF16)
            kdm = jnp.where(mine, kd2, 0.0).astype(BF16)
            v = rv_ref[:, h * DV_R:(h + 1) * DV_R].astype(BF16)
            sc = _dot_nt(qm, k2.astype(BF16)) * dmat_ref[h]
            s_old = state[h]
            o = _dot(sc.astype(BF16), v) + _dot(qm, s_old.astype(BF16)) * qdec_ref[h]
            state[h] = gpow_ref[h] * s_old + _dot_tn(kdm, v)
            normed = _lane_rmsnorm(o) * gro_ref[...]
            rg = rg_ref[:, h * DV_R:(h + 1) * DV_R]
            o_ref[:, h * DV_R:(h + 1) * DV_R] = (normed * (rg * jax.nn.sigmoid(rg))).astype(BF16)

    @pl.when(c == pl.num_programs(1) - 1)
    def _():
        for h in range(H_R):
            r0 = DK_R * (h % 2)
            sfin_ref[0, h] = state[h, r0:r0 + DK_R, :]


def _retention(rq, rk, rv, rg, n_batch, seq, ch, g_ret_out, state0):
    nc = seq // ch
    lg = jnp.log1p(-jnp.exp2(-5.0 - jnp.arange(H_R, dtype=F32)))
    idx = jnp.arange(ch, dtype=F32)
    diff = idx[:, None] - idx[None, :]
    causal = diff >= 0
    dmat = jnp.where(causal[None], jnp.exp(jnp.where(causal, diff, 0.0)[None] * lg[:, None, None]), 0.0)
    q_dec = jnp.exp((idx + 1.0)[None, :] * lg[:, None])
    k_dec = jnp.exp((ch - 1.0 - idx)[None, :] * lg[:, None])
    qdec = jnp.broadcast_to(q_dec[:, :, None], (H_R, ch, DV_R))
    kdec = jnp.broadcast_to(k_dec[:, :, None], (H_R, ch, DK_R))
    kdec = kdec.reshape(H_R // 2, 2, ch, DK_R).transpose(0, 2, 1, 3).reshape(H_R // 2, ch, LANES)
    gpow = jnp.broadcast_to(jnp.exp(ch * lg)[:, None, None], (H_R, 1, DV_R))

    has_init = state0 is not None
    blk = lambda w: pl.BlockSpec((ch, w), lambda b, c: (b * nc + c, 0))
    in_specs = [blk(RQK_W), blk(RQK_W), blk(RV_W), blk(RV_W),
                _const_spec((H_R, ch, ch)), _const_spec((H_R, ch, DV_R)),
                _const_spec((H_R // 2, ch, LANES)), _const_spec((H_R, 1, DV_R)), _const_spec((1, DV_R))]
    args = [rq, rk, rv, rg, dmat, qdec, kdec, gpow, g_ret_out]
    if has_init:
        in_specs.append(pl.BlockSpec((1, H_R, DK_R, DV_R), lambda b, c: (b, 0, 0, 0)))
        args.append(state0)
    return pl.pallas_call(
        functools.partial(_ret_kernel, has_init=has_init),
        grid=(n_batch, nc),
        in_specs=in_specs,
        out_specs=[blk(RV_W), pl.BlockSpec((1, H_R, DK_R, DV_R), lambda b, c: (b, 0, 0, 0))],
        out_shape=[jax.ShapeDtypeStruct((n_batch * seq, RV_W), BF16),
                   jax.ShapeDtypeStruct((n_batch, H_R, DK_R, DV_R), F32)],
        scratch_shapes=[pltpu.VMEM((H_R, LANES, DV_R), F32)],
        compiler_params=_cparams(("parallel", "arbitrary")),
        name="retention",
    )(*args)


def _split3(x):
    hi = x.astype(BF16)
    r1 = x - hi.astype(F32)
    mid = r1.astype(BF16)
    lo = (r1 - mid.astype(F32)).astype(BF16)
    return hi, mid, lo


def _cumsum_kernel(x_ref, o_ref, *, blk):
    rows, n = x_ref.shape
    r = lax.broadcasted_iota(jnp.int32, (blk, blk), 0)
    c = lax.broadcasted_iota(jnp.int32, (blk, blk), 1)
    tri = jnp.where(r <= c, 1.0, 0.0).astype(BF16)
    carry = jnp.zeros((rows, 1), F32)
    for i in range(n // blk):
        hi, mid, lo = _split3(x_ref[:, i * blk:(i + 1) * blk])
        cum = (_dot(hi, tri) + _dot(mid, tri)) + _dot(lo, tri) + carry
        o_ref[:, i * blk:(i + 1) * blk] = cum
        carry = cum[:, blk - 1:blk]


def _cumsum_cols_kernel(x_ref, o_ref, *, blk):
    n = x_ref.shape[1]
    r = lax.broadcasted_iota(jnp.int32, (blk, blk), 0)
    c = lax.broadcasted_iota(jnp.int32, (blk, blk), 1)
    tri = jnp.where(r >= c, 1.0, 0.0).astype(BF16)
    carry = jnp.zeros((1, x_ref.shape[2]), F32)
    for i in range(n // blk):
        hi, mid, lo = _split3(x_ref[0, i * blk:(i + 1) * blk, :])
        cum = (_dot(tri, hi) + _dot(tri, mid)) + _dot(tri, lo) + carry
        o_ref[0, i * blk:(i + 1) * blk, :] = cum
        carry = cum[blk - 1:blk, :]


def _cumsum_cols(x, blk):
    nb, t, w = x.shape
    spec = pl.BlockSpec((1, t, w), lambda b: (b, 0, 0))
    return pl.pallas_call(
        functools.partial(_cumsum_cols_kernel, blk=blk),
        grid=(nb,),
        in_specs=[spec], out_specs=spec,
        out_shape=jax.ShapeDtypeStruct(x.shape, F32),
        compiler_params=_cparams(("parallel",)),
        name="cumsum_cols",
    )(x)


def _cumsum(x, blk):
    r, t = x.shape
    spec = pl.BlockSpec((H_F, t), lambda b: (b, 0))
    return pl.pallas_call(
        functools.partial(_cumsum_kernel, blk=blk),
        grid=(r // H_F,),
        in_specs=[spec], out_specs=spec,
        out_shape=jax.ShapeDtypeStruct(x.shape, F32),
        compiler_params=_cparams(("parallel",)),
        name="cumsum",
    )(x)


def _head_rows_mask(shape, hh):
    sub = lax.broadcasted_iota(jnp.int32, shape, 0)
    return (sub >= hh * D_F) & (sub < (hh + 1) * D_F)


N_AUG = 3


def _fox_kernel(qt_ref, k_ref, vt_ref, fcum_ref, o_ref, qa, ka, vb, *, tq):
    pair = pl.program_id(1)
    seq = k_ref.shape[0]
    tk = tq
    qt = qt_ref[...].astype(F32)
    k = k_ref[...].astype(F32)
    fc = fcum_ref[...] * LOG2E
    sub = lax.broadcasted_iota(jnp.int32, qt.shape, 0)
    lane = lax.broadcasted_iota(jnp.int32, k.shape, 1)
    head_col = lax.broadcasted_iota(jnp.int32, fc.shape, 1)
    for hh in range(2):
        own, oth = hh * D_F, (1 - hh) * D_F
        qa[hh] = jnp.where((sub >= own) & (sub < own + D_F), qt,
                           jnp.where((sub >= oth) & (sub < oth + N_AUG), 1.0, 0.0)).astype(BF16)
        bias = -jnp.sum(jnp.where(head_col == 2 * pair + hh, fc, 0.0), axis=-1, keepdims=True)
        kk = jnp.where((lane >= own) & (lane < own + D_F), k, 0.0)
        for a, piece in enumerate(_split3(bias)):
            kk = jnp.where(lane == oth + a, piece.astype(F32), kk)
        ka[hh] = kk.astype(BF16)
    vb[...] = vt_ref[...].astype(BF16)
    key_idx = lax.broadcasted_iota(jnp.int32, (tk, tq), 0)
    qry_idx = lax.broadcasted_iota(jnp.int32, (tk, tq), 1)

    def q_block(i, _):
        q0 = pl.multiple_of(i * tq, tq)

        def scores(j):
            k0 = pl.multiple_of(j * tk, tk)
            return tuple(_dot(ka[hh, pl.ds(k0, tk), :], qa[hh, :, pl.ds(q0, tq)]) for hh in range(2))

        def update(j, ts, carry, diagonal):
            k0 = pl.multiple_of(j * tk, tk)
            out = []
            for hh in range(2):
                m, l, acc = carry[hh]
                t = ts[hh]
                if diagonal:
                    t = jnp.where(qry_idx >= key_idx, t, -jnp.inf)
                m_new = jnp.maximum(m, jnp.max(t, axis=0, keepdims=True))
                alpha = jnp.exp2(m - m_new)
                p = jnp.exp2(t - m_new)
                l_new = alpha * l + jnp.sum(p, axis=0, keepdims=True)
                acc_new = alpha * acc + _dot(vb[hh * D_F:(hh + 1) * D_F, pl.ds(k0, tk)], p.astype(BF16))
                out.append((m_new, l_new, acc_new))
            return tuple(out)

        def body(j, c):
            carry, ts = c
            ts_next = scores(j + 1)
            return update(j, ts, carry, False), ts_next

        init = (jnp.full((1, tq), -jnp.inf, F32), jnp.zeros((1, tq), F32), jnp.zeros((D_F, tq), F32))
        carry, ts = lax.fori_loop(0, i, body, ((init, init), scores(0)))
        (_, l0, a0), (_, l1, a1) = update(i, ts, carry, True)
        o_t = jnp.concatenate([a0 / l0, a1 / l1], axis=0)
        o_ref[pl.ds(q0, tq), :] = o_t.T.astype(BF16)
        return 0

    lax.fori_loop(0, seq // tq, q_block, 0)


def _fox_prompt(fqt, fkb, fvt, fcum, n_batch, seq, tq):
    npair = H_F // 2
    fm_spec = pl.BlockSpec((LANES, seq), lambda b, p: (b * npair + p, 0))
    tok_spec = pl.BlockSpec((seq, LANES), lambda b, p: (b, p))
    return pl.pallas_call(
        functools.partial(_fox_kernel, tq=tq),
        grid=(n_batch, npair),
        in_specs=[fm_spec, tok_spec, fm_spec, pl.BlockSpec((seq, H_F), lambda b, p: (b, 0))],
        out_specs=tok_spec,
        out_shape=jax.ShapeDtypeStruct((n_batch * seq, FOX_W), BF16),
        scratch_shapes=[pltpu.VMEM((2, LANES, seq), BF16), pltpu.VMEM((2, seq, LANES), BF16),
                        pltpu.VMEM((LANES, seq), BF16)],
        compiler_params=_cparams(("parallel", "parallel")),
        name="fox_prompt",
    )(fqt, fkb, fvt, fcum)


def _fox_sample_kernel(q_ref, kct_ref, vct_ref, kn_ref, vn_ref, frc_ref, frn_ref, o_ref):
    n = q_ref.shape[0]
    q2 = q_ref[...]
    lane = lax.broadcasted_iota(jnp.int32, (n, LANES), 1)
    rows = lax.broadcasted_iota(jnp.int32, (n, n), 0)
    cols = lax.broadcasted_iota(jnp.int32, (n, n), 1)
    kct = kct_ref[...]
    vct = vct_ref[...].astype(BF16)
    kn = kn_ref[...].astype(BF16)
    vn = vn_ref[...].astype(BF16)
    outs = []
    for hh in range(2):
        mine = (lane >= hh * D_F) & (lane < (hh + 1) * D_F)
        kc = jnp.where(_head_rows_mask(kct.shape, hh), kct, 0.0).astype(BF16)
        t_c = _dot(q2, kc) - frc_ref[0, hh:hh + 1, :] * LOG2E
        t_n = _dot_nt(jnp.where(mine, q2, jnp.zeros_like(q2)), kn) - frn_ref[0, hh:hh + 1, :] * LOG2E
        t_n = jnp.where(rows >= cols, t_n, -jnp.inf)
        m = jnp.maximum(jnp.max(t_c, axis=-1, keepdims=True), jnp.max(t_n, axis=-1, keepdims=True))
        p_c = jnp.exp2(t_c - m)
        p_n = jnp.exp2(t_n - m)
        l = jnp.sum(p_c, axis=-1, keepdims=True) + jnp.sum(p_n, axis=-1, keepdims=True)
        acc = _dot_nt(p_c.astype(BF16), vct) + _dot(p_n.astype(BF16), vn)
        outs.append(acc / l)
    o_ref[...] = jnp.where(lane < D_F, outs[0], outs[1]).astype(BF16)


def _fox_sample(fq, kct, vct, kn, vn, frow_c, frow_n, n_batch, n_new, past):
    npair = H_F // 2
    tok = pl.BlockSpec((n_new, LANES), lambda b, p: (b, p))
    cache = pl.BlockSpec((LANES, past), lambda b, p: (b * npair + p, 0))
    return pl.pallas_call(
        _fox_sample_kernel,
        grid=(n_batch, npair),
        in_specs=[tok, cache, cache, tok, tok,
                  pl.BlockSpec((1, 2, past), lambda b, p: (b * npair + p, 0, 0)),
                  pl.BlockSpec((1, 2, n_new), lambda b, p: (b * npair + p, 0, 0))],
        out_specs=tok,
        out_shape=jax.ShapeDtypeStruct((n_batch * n_new, FOX_W), BF16),
        compiler_params=_cparams(("parallel", "parallel")),
        name="fox_sample",
    )(fq, kct, vct, kn, vn, frow_c, frow_n)


def _memkv_kernel(m_ref, gin_ref, w_ref, gk_ref, k_ref, v_ref):
    x = m_ref[...]
    h = (x * lax.rsqrt(jnp.mean(x * x, axis=-1, keepdims=True) + EPS) * gin_ref[...]).astype(BF16)
    zk = _dot(h, w_ref[:, :MEM_W])
    for hh in range(H_M):
        sl = slice(hh * D_M, (hh + 1) * D_M)
        k_ref[:, sl] = _lane_rmsnorm(zk[:, sl]) * gk_ref[...]
    v_ref[...] = _dot(h, w_ref[:, MEM_W:])


def _memkv(mem2d, n_batch, g_mem_in, w_mem_kv, g_mem_k):
    blk = lambda w: pl.BlockSpec((N_MEM, w), lambda b: (b, 0))
    return pl.pallas_call(
        _memkv_kernel,
        grid=(n_batch,),
        in_specs=[blk(D_MODEL), _const_spec((1, D_MODEL)), _const_spec((D_MODEL, 2 * MEM_W)),
                  _const_spec((1, D_M))],
        out_specs=[blk(MEM_W), blk(MEM_W)],
        out_shape=[jax.ShapeDtypeStruct((n_batch * N_MEM, MEM_W), F32)] * 2,
        compiler_params=_cparams(("parallel",)),
        name="memkv",
    )(mem2d, g_mem_in, w_mem_kv, g_mem_k)


def _memattn_kernel(q_ref, k_ref, v_ref, o_ref):
    for hh in range(H_M):
        sl = slice(hh * D_M, (hh + 1) * D_M)
        s = _dot_nt(q_ref[:, sl], k_ref[:, sl].astype(BF16)) * (D_M ** -0.5)
        m = jnp.max(s, axis=-1, keepdims=True)
        p = jnp.exp(s - m)
        l = jnp.sum(p, axis=-1, keepdims=True)
        o_ref[:, sl] = (_dot(p.astype(BF16), v_ref[:, sl].astype(BF16)) / l).astype(BF16)


def _memattn(mq, mk, mv, n_batch, seq, tq):
    nq = seq // tq
    return pl.pallas_call(
        _memattn_kernel,
        grid=(n_batch, nq),
        in_specs=[pl.BlockSpec((tq, MEM_W), lambda b, i: (b * nq + i, 0)),
                  pl.BlockSpec((N_MEM, MEM_W), lambda b, i: (b, 0)),
                  pl.BlockSpec((N_MEM, MEM_W), lambda b, i: (b, 0))],
        out_specs=pl.BlockSpec((tq, MEM_W), lambda b, i: (b * nq + i, 0)),
        out_shape=jax.ShapeDtypeStruct((n_batch * seq, MEM_W), BF16),
        compiler_params=_cparams(("parallel", "arbitrary")),
        name="memattn",
    )(mq, mk, mv)


def _split2(x):
    hi = x.astype(BF16)
    return hi, (x - hi.astype(F32)).astype(BF16)


def _merge_kernel(x_ref, or_ref, of_ref, om_ref, gt_ref, wr_ref, wf_ref, wm_ref, wo_ref, g2_ref,
                  wrt_hi_ref, wrt_lo_ref, brt_ref, x1_ref, h2_ref, comb_ref):
    g = lambda b: gt_ref[:, b * D_MODEL:(b + 1) * D_MODEL].astype(F32)
    merged = (g(0) * _dot(or_ref[...], wr_ref[...]) + g(1) * _dot(of_ref[...], wf_ref[...])
              + g(2) * _dot(om_ref[...], wm_ref[...]))
    x1 = x_ref[...] + _dot(merged.astype(BF16), wo_ref[...])
    x1_ref[...] = x1
    h2 = x1 * lax.rsqrt(jnp.mean(x1 * x1, axis=-1, keepdims=True) + EPS) * g2_ref[...]
    h2_ref[...] = h2.astype(BF16)

    h_hi, h_lo = _split2(h2)
    logits = (_dot(h_hi, wrt_hi_ref[...]) + (_dot(h_hi, wrt_lo_ref[...]) + _dot(h_lo, wrt_hi_ref[...]))
              + brt_ref[...])
    n_e = N_GROUPS * E_PER_GROUP
    lane = lax.broadcasted_iota(jnp.int32, logits.shape, 1).astype(F32)
    neg = -jnp.inf
    first_idx = lambda mask: jnp.min(jnp.where(mask, lane, float(LANES)), axis=-1, keepdims=True)

    is_g = (lane >= n_e) & (lane < n_e + N_GROUPS)
    lg = jnp.where(is_g, logits, neg)
    mg = jnp.max(lg, axis=-1, keepdims=True)
    gsel = first_idx(lg == mg) - n_e
    p_sel = 1.0 / jnp.sum(jnp.exp(lg - mg), axis=-1, keepdims=True)

    in_grp = (lane >= gsel * E_PER_GROUP) & (lane < (gsel + 1) * E_PER_GROUP)
    le = jnp.where(in_grp, logits, neg)
    v1 = jnp.max(le, axis=-1, keepdims=True)
    i1 = first_idx(le == v1)
    le2 = jnp.where(lane == i1, neg, le)
    v2 = jnp.max(le2, axis=-1, keepdims=True)
    i2 = first_idx(le2 == v2)
    e2 = jnp.exp(v2 - v1)
    w1 = p_sel / (1.0 + e2)
    w2 = p_sel * e2 / (1.0 + e2)
    comb = jnp.where(lane == i1, w1, 0.0) + jnp.where(lane == i2, w2, 0.0)
    for gi in range(N_GROUPS):
        comb_ref[gi] = comb[:, gi * E_PER_GROUP:(gi + 1) * E_PER_GROUP]


def _merge(x2d, o_r, o_f, o_m, gates, tm, wr, wf, wm, wo, g2, wrt_hi, wrt_lo, brt):
    t = x2d.shape[0]
    row = lambda w: pl.BlockSpec((tm, w), lambda i: (i, 0))
    return pl.pallas_call(
        _merge_kernel,
        grid=(t // tm,),
        in_specs=[row(D_MODEL), row(RV_W), row(FOX_W), row(MEM_W), row(GATE_W),
                  _const_spec((RV_W, D_MODEL)), _const_spec((FOX_W, D_MODEL)), _const_spec((MEM_W, D_MODEL)),
                  _const_spec((D_MODEL, D_MODEL)), _const_spec((1, D_MODEL)),
                  _const_spec((D_MODEL, LANES)), _const_spec((D_MODEL, LANES)), _const_spec((1, LANES))],
        out_specs=[row(D_MODEL), row(D_MODEL),
                   pl.BlockSpec((N_GROUPS, tm, E_PER_GROUP), lambda i: (0, i, 0))],
        out_shape=[jax.ShapeDtypeStruct((t, D_MODEL), F32), jax.ShapeDtypeStruct((t, D_MODEL), BF16),
                   jax.ShapeDtypeStruct((N_GROUPS, t, E_PER_GROUP), F32)],
        compiler_params=_cparams(("parallel",)),
        name="merge",
    )(x2d, o_r, o_f, o_m, gates, wr, wf, wm, wo, g2, wrt_hi, wrt_lo, brt)


def _moe_kernel(h_ref, comb_ref, x1_ref, wg_ref, wu_ref, wd_ref, o_ref, act):
    g = pl.program_id(1)
    h = h_ref[...]
    comb = comb_ref[0]
    for e in range(E_PER_GROUP):
        a = _dot(h, wg_ref[0, e])
        u = _dot(h, wu_ref[0, e])
        act[:, e * D_EXPERT:(e + 1) * D_EXPERT] = (
            (a * jax.nn.sigmoid(a)) * u * comb[:, e:e + 1]).astype(BF16)
    y = _dot(act[...], wd_ref[0])

    @pl.when(g == 0)
    def _():
        o_ref[...] = x1_ref[...] + y

    @pl.when(g != 0)
    def _():
        o_ref[...] += y


def _moe(h2, comb, x1, tm, wg, wu, wd):
    t = h2.shape[0]
    return pl.pallas_call(
        _moe_kernel,
        grid=(t // tm, N_GROUPS),
        in_specs=[pl.BlockSpec((tm, D_MODEL), lambda i, g: (i, 0)),
                  pl.BlockSpec((1, tm, E_PER_GROUP), lambda i, g: (g, i, 0)),
                  pl.BlockSpec((tm, D_MODEL), lambda i, g: (i, 0)),
                  pl.BlockSpec((1, E_PER_GROUP, D_MODEL, D_EXPERT), lambda i, g: (g, 0, 0, 0)),
                  pl.BlockSpec((1, E_PER_GROUP, D_MODEL, D_EXPERT), lambda i, g: (g, 0, 0, 0)),
                  pl.BlockSpec((1, E_PER_GROUP * D_EXPERT, D_MODEL), lambda i, g: (g, 0, 0))],
        out_specs=pl.BlockSpec((tm, D_MODEL), lambda i, g: (i, 0)),
        out_shape=jax.ShapeDtypeStruct((t, D_MODEL), F32),
        scratch_shapes=[pltpu.VMEM((tm, E_PER_GROUP * D_EXPERT), BF16)],
        compiler_params=_cparams(("parallel", "arbitrary")),
        name="moe",
    )(h2, comb, x1, wg, wu, wd)


def _rope_tables(pos):
    half = DK_R // 2
    inv = ROPE_BASE ** (-jnp.arange(half, dtype=F32) / half)
    ang = pos.astype(F32)[:, None] * inv[None, :]
    c = jnp.cos(ang)
    s = jnp.sin(ang)
    return jnp.concatenate([c, c, c, c], axis=-1), jnp.concatenate([-s, s, -s, s], axis=-1)


def _pair_rows(frows):
    return frows.reshape(frows.shape[0] // 2, 2, frows.shape[1])


def kernel(x_prompt, x_sample, state_ret, cache_fox_k, cache_fox_v, cache_fox_logf, cache_mem_k, cache_mem_v,
           mem_prompt, g_norm1, w_in, b_forget, g_fox_q, g_fox_k, g_mem_q, g_mem_in, w_mem_kv, g_mem_k,
           g_ret_out, w_br_ret, w_br_fox, w_br_mem, w_out, g_norm2, w_route_group, b_route_group,
           w_route_expert, b_route_expert, w_exp_gate, w_exp_up, w_exp_down):
    nb, seq, _ = x_prompt.shape
    nbs, n_new, _ = x_sample.shape
    past = cache_fox_k.shape[2]
    l = 0

    wi = w_in[l]
    o_ff = 2 * RQK_W + 2 * RV_W + 3 * FOX_W
    w_in_r = jnp.concatenate(
        [wi[:, :o_ff], wi[:, o_ff + H_F:], wi[:, o_ff:o_ff + H_F],
         jnp.zeros((D_MODEL, LANES - H_F), F32)], axis=1).astype(BF16)
    bf_pad = jnp.concatenate([b_forget[l], jnp.zeros((LANES - H_F,), F32)])[None, :]
    g1 = g_norm1[l][None, :]
    gfq = jnp.tile(g_fox_q[l], H_F)[None, :]
    gfk = jnp.tile(g_fox_k[l], H_F)[None, :]
    gmq = g_mem_q[l][None, :]
    hid = jnp.arange(FOX_W) // D_F
    bd = jnp.where(hid[:, None] == hid[None, :], 1.0 / D_F, 0.0).astype(BF16)
    wr = w_br_ret[l].astype(BF16)
    wf = w_br_fox[l].astype(BF16)
    wm = w_br_mem[l].astype(BF16)
    wo = w_out[l].astype(BF16)
    g2 = g_norm2[l][None, :]
    n_e = N_GROUPS * E_PER_GROUP
    wrt = jnp.concatenate([w_route_expert[l], w_route_group[l],
                           jnp.zeros((D_MODEL, LANES - n_e - N_GROUPS), F32)], axis=1)
    wrt_hi = wrt.astype(BF16)
    wrt_lo = (wrt - wrt_hi.astype(F32)).astype(BF16)
    brt = jnp.concatenate([b_route_expert[l], b_route_group[l],
                           jnp.zeros((LANES - n_e - N_GROUPS,), F32)])[None, :]
    wg = w_exp_gate[l].astype(BF16)
    wu = w_exp_up[l].astype(BF16)
    wd = w_exp_down[l].astype(BF16).reshape(N_GROUPS, E_PER_GROUP * D_EXPERT, D_MODEL)
    gro = g_ret_out[l][None, :]
    prep_w = (g1, w_in_r, bf_pad, gfq, gfk, gmq, bd)
    merge_w = (wr, wf, wm, wo, g2, wrt_hi, wrt_lo, brt)

    tm = 512
    xp = x_prompt.reshape(nb * seq, D_MODEL)
    (rq, rk, rv, rg, fqt, fkt, fvt, lft, mq, gates, fkb, lf) = _prep(
        xp, _rope_tables(jnp.arange(seq)), tm, seq // tm, True, *prep_w)
    o_r, s_fin = _retention(rq, rk, rv, rg, nb, seq, 256, gro, None)
    fcum = _cumsum_cols(lf.reshape(nb, seq, H_F), 256).reshape(nb * seq, H_F)
    o_f = _fox_prompt(fqt, fkb, fvt, fcum, nb, seq, 256)
    mk, mv = _memkv(mem_prompt.reshape(nb * N_MEM, D_MODEL), nb, g_mem_in[l][None, :],
                    w_mem_kv[l].astype(BF16), g_mem_k[l][None, :])
    o_m = _memattn(mq, mk, mv, nb, seq, tm)
    x1, h2, comb = _merge(xp, o_r, o_f, o_m, gates, tm, *merge_w)
    y_prompt = _moe(h2, comb, x1, tm, wg, wu, wd).reshape(nb, seq, D_MODEL)

    ts = nbs * n_new
    xs = x_sample.reshape(ts, D_MODEL)
    pos_s = jnp.tile(past + jnp.arange(n_new), nbs)
    (rq_s, rk_s, rv_s, rg_s, fq_s, fk_s, fv_s, lf_s, mq_s, gates_s) = _prep(
        xs, _rope_tables(pos_s), ts, 1, False, *prep_w)
    o_r_s, s_new = _retention(rq_s, rk_s, rv_s, rg_s, nbs, n_new, n_new, gro, state_ret[l])
    pad = (-(past + n_new)) % 256
    lf_rows = jnp.concatenate([jnp.swapaxes(cache_fox_logf[l], 1, 2),
                               jnp.swapaxes(lf_s.reshape(nbs, n_new, H_F), 1, 2),
                               jnp.zeros((nbs, H_F, pad), F32)], axis=2).reshape(nbs * H_F, past + n_new + pad)
    f_all = _cumsum(lf_rows, 256)
    feat_major = lambda c: jnp.transpose(c, (0, 2, 3, 1)).reshape(nbs * FOX_W, past)
    o_f_s = _fox_sample(fq_s, feat_major(cache_fox_k[l]), feat_major(cache_fox_v[l]), fk_s, fv_s,
                        _pair_rows(f_all[:, :past]), _pair_rows(f_all[:, past:past + n_new]),
                        nbs, n_new, past)
    o_m_s = _memattn(mq_s, cache_mem_k[l].reshape(nbs * N_MEM, MEM_W),
                     cache_mem_v[l].reshape(nbs * N_MEM, MEM_W), nbs, n_new, n_new)
    x1_s, h2_s, comb_s = _merge(xs, o_r_s, o_f_s, o_m_s, gates_s, ts, *merge_w)
    y_sample = _moe(h2_s, comb_s, x1_s, ts, wg, wu, wd).reshape(nbs, n_new, D_MODEL)

    token_major = lambda a: jnp.transpose(a.reshape(nb, H_F, D_F, seq), (0, 3, 1, 2))

    return (y_prompt, y_sample,
            s_fin[None], token_major(fkt)[None], token_major(fvt)[None],
            jnp.swapaxes(lft.reshape(nb, H_F, seq), 1, 2)[None],
            mk.reshape(1, nb, N_MEM, H_M, D_M), mv.reshape(1, nb, N_MEM, H_M, D_M),
            s_new[None], fk_s.reshape(1, nbs, n_new, H_F, D_F), fv_s.reshape(1, nbs, n_new, H_F, D_F),
            lf_s.reshape(1, nbs, n_new, H_F))
```

```python
import functools

import jax
import jax.numpy as jnp
from jax import lax
from jax.experimental import pallas as pl
from jax.experimental.pallas import tpu as pltpu

F32 = jnp.float32
BF16 = jnp.bfloat16

D_MODEL = 1024
H_R, DK_R, DV_R = 4, 64, 128
H_F, D_F = 8, 64
H_M, D_M = 4, 128
N_MEM = 256
N_GROUPS, E_PER_GROUP, D_EXPERT = 4, 8, 256
ROPE_BASE = 10000.0
EPS = 1e-6
LOG2E = 1.4426950408889634

RQK_W = H_R * DK_R
RV_W = H_R * DV_R
FOX_W = H_F * D_F
MEM_W = H_M * D_M
GATE_W = 3 * D_MODEL
LANES = 128

C_RQ = 0
C_RK = C_RQ + RQK_W
C_RV = C_RK + RQK_W
C_RG = C_RV + RV_W
C_FQ = C_RG + RV_W
C_FK = C_FQ + FOX_W
C_FV = C_FK + FOX_W
C_MQ = C_FV + FOX_W
C_GT = C_MQ + MEM_W
C_FF = C_GT + GATE_W
N_IN_PAD = C_FF + LANES

VMEM_LIMIT = 56 * 1024 * 1024


def _cparams(sem):
    return pltpu.CompilerParams(dimension_semantics=sem, vmem_limit_bytes=VMEM_LIMIT)


def _const_spec(shape):
    nd = len(shape)
    return pl.BlockSpec(shape, lambda *_: (0,) * nd, pipeline_mode=pl.Buffered(1))


def _dot(a, b):
    return jnp.dot(a, b, preferred_element_type=F32)


def _dot_nt(a, b):
    return lax.dot_general(a, b, (((1,), (1,)), ((), ())), preferred_element_type=F32)


def _dot_tn(a, b):
    return lax.dot_general(a, b, (((0,), (0,)), ((), ())), preferred_element_type=F32)


def _lane_rmsnorm(z):
    return z * lax.rsqrt(jnp.mean(z * z, axis=-1, keepdims=True) + EPS)


def _prep_kernel(x_ref, g1_ref, w_ref, cos_ref, sin_ref, bf_ref, gfq_ref, gfk_ref, gmq_ref, bd_ref,
                 rq_ref, rk_ref, rv_ref, rg_ref, fq_ref, fk_ref, fv_ref, lf_ref, mq_ref, gt_ref, *extra,
                 seq_minor):
    x = x_ref[...]
    h = (x * lax.rsqrt(jnp.mean(x * x, axis=-1, keepdims=True) + EPS) * g1_ref[...]).astype(BF16)

    def proj(c0, width):
        return _dot(h, w_ref[:, c0:c0 + width])

    cos = cos_ref[...]
    sin = sin_ref[...]
    lane = lax.broadcasted_iota(jnp.int32, cos.shape, 1)
    first_half = (lane % DK_R) < (DK_R // 2)

    def rope(z):
        swapped = jnp.where(first_half, pltpu.roll(z, LANES - DK_R // 2, 1), pltpu.roll(z, DK_R // 2, 1))
        return z * cos + swapped * sin

    zq = proj(C_RQ, RQK_W)
    zk = proj(C_RK, RQK_W)
    for p in range(RQK_W // LANES):
        sl = slice(p * LANES, (p + 1) * LANES)
        rq_ref[:, sl] = rope(zq[:, sl])
        rk_ref[:, sl] = rope(zk[:, sl]) * (DK_R ** -0.5)
    rv_ref[...] = proj(C_RV, RV_W)
    rg_ref[...] = proj(C_RG, RV_W)

    def head64_norm(z, g_ref):
        ms = _dot((z * z).astype(BF16), bd_ref[...])
        return z * lax.rsqrt(ms + EPS) * g_ref[...]

    fq = head64_norm(proj(C_FQ, FOX_W), gfq_ref) * (D_F ** -0.5 * LOG2E)
    fk = head64_norm(proj(C_FK, FOX_W), gfk_ref)
    fv = proj(C_FV, FOX_W)
    if seq_minor:
        fq_ref[...] = fq.T.astype(BF16)
        fk_ref[...] = fk.T
        fv_ref[...] = fv.T
        extra[0][...] = fk.astype(BF16)
    else:
        fq_ref[...] = fq.astype(BF16)
        fk_ref[...] = fk
        fv_ref[...] = fv

    zm = proj(C_MQ, MEM_W)
    for hh in range(H_M):
        sl = slice(hh * D_M, (hh + 1) * D_M)
        mq_ref[:, sl] = (_lane_rmsnorm(zm[:, sl]) * gmq_ref[...]).astype(BF16)

    for b in range(3):
        gt_ref[:, b * D_MODEL:(b + 1) * D_MODEL] = jax.nn.sigmoid(proj(C_GT + b * D_MODEL, D_MODEL)).astype(BF16)

    v = proj(C_FF, LANES) + bf_ref[...]
    lf = jnp.minimum(v, 0.0) - jnp.log1p(jnp.exp(-jnp.abs(v)))
    if seq_minor:
        lf_ref[...] = lf.T[:H_F, :]
        extra[1][...] = lf[:, :H_F]
    else:
        lf_ref[...] = lf[:, :H_F]


def _prep(x2d, tables, tm, n_pos_tiles, seq_minor, g1, w_in_r, bf_pad, gfq, gfk, gmq, bd):
    t = x2d.shape[0]
    cos_t, sin_t = tables
    row = lambda w: pl.BlockSpec((tm, w), lambda i: (i, 0))
    pos = pl.BlockSpec((tm, LANES), lambda i: (i % n_pos_tiles, 0))
    sds = jax.ShapeDtypeStruct
    if seq_minor:
        nb, seq = t // (n_pos_tiles * tm), n_pos_tiles * tm
        fm = lambda rows_: pl.BlockSpec((rows_, tm), lambda i: (i // n_pos_tiles, i % n_pos_tiles))
        fox_shapes = [sds((nb * FOX_W, seq), BF16), sds((nb * FOX_W, seq), F32), sds((nb * FOX_W, seq), F32),
                      sds((nb * H_F, seq), F32)]
        fox_specs = [fm(FOX_W), fm(FOX_W), fm(FOX_W), fm(H_F)]
        extra_shapes, extra_specs = [sds((t, FOX_W), BF16), sds((t, H_F), F32)], [row(FOX_W), row(H_F)]
    else:
        fox_shapes = [sds((t, FOX_W), BF16), sds((t, FOX_W), F32), sds((t, FOX_W), F32), sds((t, H_F), F32)]
        fox_specs = [row(FOX_W), row(FOX_W), row(FOX_W), row(H_F)]
        extra_shapes, extra_specs = [], []
    out_shapes = [sds((t, RQK_W), F32), sds((t, RQK_W), F32), sds((t, RV_W), F32), sds((t, RV_W), F32),
                  *fox_shapes, sds((t, MEM_W), BF16), sds((t, GATE_W), BF16), *extra_shapes]
    out_specs = [row(RQK_W), row(RQK_W), row(RV_W), row(RV_W), *fox_specs, row(MEM_W), row(GATE_W),
                 *extra_specs]
    return pl.pallas_call(
        functools.partial(_prep_kernel, seq_minor=seq_minor),
        grid=(t // tm,),
        in_specs=[row(D_MODEL), _const_spec((1, D_MODEL)), _const_spec((D_MODEL, N_IN_PAD)), pos, pos,
                  _const_spec((1, LANES)), _const_spec((1, FOX_W)), _const_spec((1, FOX_W)),
                  _const_spec((1, D_M)), _const_spec((FOX_W, FOX_W))],
        out_specs=out_specs,
        out_shape=out_shapes,
        compiler_params=_cparams(("parallel",)),
        name="prep",
    )(x2d, g1, w_in_r, cos_t, sin_t, bf_pad, gfq, gfk, gmq, bd)


def _ret_kernel(*refs, has_init):
    if has_init:
        (rq_ref, rk_ref, rv_ref, rg_ref, dmat_ref, qdec_ref, kdec_ref, gpow_ref, gro_ref, s0_ref,
         o_ref, sfin_ref, state) = refs
    else:
        (rq_ref, rk_ref, rv_ref, rg_ref, dmat_ref, qdec_ref, kdec_ref, gpow_ref, gro_ref,
         o_ref, sfin_ref, state) = refs
    c = pl.program_id(1)

    @pl.when(c == 0)
    def _():
        state[...] = jnp.zeros(state.shape, F32)
        if has_init:
            for h in range(H_R):
                r0 = DK_R * (h % 2)
                state[h, r0:r0 + DK_R, :] = s0_ref[0, h]

    lane = lax.broadcasted_iota(jnp.int32, (rq_ref.shape[0], LANES), 1)
    for p in range(H_R // 2):
        sl = slice(p * LANES, (p + 1) * LANES)
        q2 = rq_ref[:, sl]
        k2 = rk_ref[:, sl]
        kd2 = k2 * kdec_ref[p]
        for hh in range(2):
            h = 2 * p + hh
            mine = (lane >= hh * DK_R) & (lane < (hh + 1) * DK_R)
            qm = jnp.where(mine, q2, 0.0).astype(BF16)
            kdm = jnp.where(mine, kd2, 0.0).astype(BF16)
            v = rv_ref[:, h * DV_R:(h + 1) * DV_R].astype(BF16)
            sc = _dot_nt(qm, k2.astype(BF16)) * dmat_ref[h]
            s_old = state[h]
            o = _dot(sc.astype(BF16), v) + _dot(qm, s_old.astype(BF16)) * qdec_ref[h]
            state[h] = gpow_ref[h] * s_old + _dot_tn(kdm, v)
            normed = _lane_rmsnorm(o) * gro_ref[...]
            rg = rg_ref[:, h * DV_R:(h + 1) * DV_R]
            o_ref[:, h * DV_R:(h + 1) * DV_R] = (normed * (rg * jax.nn.sigmoid(rg))).astype(BF16)

    @pl.when(c == pl.num_programs(1) - 1)
    def _():
        for h in range(H_R):
            r0 = DK_R * (h % 2)
            sfin_ref[0, h] = state[h, r0:r0 + DK_R, :]


def _retention(rq, rk, rv, rg, n_batch, seq, ch, g_ret_out, state0):
    nc = seq // ch
    lg = jnp.log1p(-jnp.exp2(-5.0 - jnp.arange(H_R, dtype=F32)))
    idx = jnp.arange(ch, dtype=F32)
    diff = idx[:, None] - idx[None, :]
    causal = diff >= 0
    dmat = jnp.where(causal[None], jnp.exp(jnp.where(causal, diff, 0.0)[None] * lg[:, None, None]), 0.0)
    q_dec = jnp.exp((idx + 1.0)[None, :] * lg[:, None])
    k_dec = jnp.exp((ch - 1.0 - idx)[None, :] * lg[:, None])
    qdec = jnp.broadcast_to(q_dec[:, :, None], (H_R, ch, DV_R))
    kdec = jnp.broadcast_to(k_dec[:, :, None], (H_R, ch, DK_R))
    kdec = kdec.reshape(H_R // 2, 2, ch, DK_R).transpose(0, 2, 1, 3).reshape(H_R // 2, ch, LANES)
    gpow = jnp.broadcast_to(jnp.exp(ch * lg)[:, None, None], (H_R, 1, DV_R))

    has_init = state0 is not None
    blk = lambda w: pl.BlockSpec((ch, w), lambda b, c: (b * nc + c, 0))
    in_specs = [blk(RQK_W), blk(RQK_W), blk(RV_W), blk(RV_W),
                _const_spec((H_R, ch, ch)), _const_spec((H_R, ch, DV_R)),
                _const_spec((H_R // 2, ch, LANES)), _const_spec((H_R, 1, DV_R)), _const_spec((1, DV_R))]
    args = [rq, rk, rv, rg, dmat, qdec, kdec, gpow, g_ret_out]
    if has_init:
        in_specs.append(pl.BlockSpec((1, H_R, DK_R, DV_R), lambda b, c: (b, 0, 0, 0)))
        args.append(state0)
    return pl.pallas_call(
        functools.partial(_ret_kernel, has_init=has_init),
        grid=(n_batch, nc),
        in_specs=in_specs,
        out_specs=[blk(RV_W), pl.BlockSpec((1, H_R, DK_R, DV_R), lambda b, c: (b, 0, 0, 0))],
        out_shape=[jax.ShapeDtypeStruct((n_batch * seq, RV_W), BF16),
                   jax.ShapeDtypeStruct((n_batch, H_R, DK_R, DV_R), F32)],
        scratch_shapes=[pltpu.VMEM((H_R, LANES, DV_R), F32)],
        compiler_params=_cparams(("parallel", "arbitrary")),
        name="retention",
    )(*args)


def _split3(x):
    hi = x.astype(BF16)
    r1 = x - hi.astype(F32)
    mid = r1.astype(BF16)
    lo = (r1 - mid.astype(F32)).astype(BF16)
    return hi, mid, lo


def _cumsum_kernel(x_ref, o_ref, *, blk):
    rows, n = x_ref.shape
    r = lax.broadcasted_iota(jnp.int32, (blk, blk), 0)
    c = lax.broadcasted_iota(jnp.int32, (blk, blk), 1)
    tri = jnp.where(r <= c, 1.0, 0.0).astype(BF16)
    carry = jnp.zeros((rows, 1), F32)
    for i in range(n // blk):
        hi, mid, lo = _split3(x_ref[:, i * blk:(i + 1) * blk])
        cum = (_dot(hi, tri) + _dot(mid, tri)) + _dot(lo, tri) + carry
        o_ref[:, i * blk:(i + 1) * blk] = cum
        carry = cum[:, blk - 1:blk]


def _cumsum_cols_kernel(x_ref, o_ref, *, blk):
    n = x_ref.shape[1]
    r = lax.broadcasted_iota(jnp.int32, (blk, blk), 0)
    c = lax.broadcasted_iota(jnp.int32, (blk, blk), 1)
    tri = jnp.where(r >= c, 1.0, 0.0).astype(BF16)
    carry = jnp.zeros((1, x_ref.shape[2]), F32)
    for i in range(n // blk):
        hi, mid, lo = _split3(x_ref[0, i * blk:(i + 1) * blk, :])
        cum = (_dot(tri, hi) + _dot(tri, mid)) + _dot(tri, lo) + carry
        o_ref[0, i * blk:(i + 1) * blk, :] = cum
        carry = cum[blk - 1:blk, :]


def _cumsum_cols(x, blk):
    nb, t, w = x.shape
    spec = pl.BlockSpec((1, t, w), lambda b: (b, 0, 0))
    return pl.pallas_call(
        functools.partial(_cumsum_cols_kernel, blk=blk),
        grid=(nb,),
        in_specs=[spec], out_specs=spec,
        out_shape=jax.ShapeDtypeStruct(x.shape, F32),
        compiler_params=_cparams(("parallel",)),
        name="cumsum_cols",
    )(x)


def _cumsum(x, blk):
    r, t = x.shape
    spec = pl.BlockSpec((H_F, t), lambda b: (b, 0))
    return pl.pallas_call(
        functools.partial(_cumsum_kernel, blk=blk),
        grid=(r // H_F,),
        in_specs=[spec], out_specs=spec,
        out_shape=jax.ShapeDtypeStruct(x.shape, F32),
        compiler_params=_cparams(("parallel",)),
        name="cumsum",
    )(x)


def _head_rows_mask(shape, hh):
    sub = lax.broadcasted_iota(jnp.int32, shape, 0)
    return (sub >= hh * D_F) & (sub < (hh + 1) * D_F)


N_AUG = 3


def _fox_kernel(qt_ref, k_ref, vt_ref, fcum_ref, o_ref, qa, ka, vb, *, tq):
    pair = pl.program_id(1)
    seq = k_ref.shape[0]
    tk = tq
    qt = qt_ref[...].astype(F32)
    k = k_ref[...].astype(F32)
    fc = fcum_ref[...] * LOG2E
    sub = lax.broadcasted_iota(jnp.int32, qt.shape, 0)
    lane = lax.broadcasted_iota(jnp.int32, k.shape, 1)
    head_col = lax.broadcasted_iota(jnp.int32, fc.shape, 1)
    for hh in range(2):
        own, oth = hh * D_F, (1 - hh) * D_F
        qa[hh] = jnp.where((sub >= own) & (sub < own + D_F), qt,
                           jnp.where((sub >= oth) & (sub < oth + N_AUG), 1.0, 0.0)).astype(BF16)
        bias = -jnp.sum(jnp.where(head_col == 2 * pair + hh, fc, 0.0), axis=-1, keepdims=True)
        kk = jnp.where((lane >= own) & (lane < own + D_F), k, 0.0)
        for a, piece in enumerate(_split3(bias)):
            kk = jnp.where(lane == oth + a, piece.astype(F32), kk)
        ka[hh] = kk.astype(BF16)
    vb[...] = vt_ref[...].astype(BF16)
    key_idx = lax.broadcasted_iota(jnp.int32, (tk, tq), 0)
    qry_idx = lax.broadcasted_iota(jnp.int32, (tk, tq), 1)

    def q_block(i, _):
        q0 = pl.multiple_of(i * tq, tq)

        def scores(j):
            k0 = pl.multiple_of(j * tk, tk)
            return tuple(_dot(ka[hh, pl.ds(k0, tk), :], qa[hh, :, pl.ds(q0, tq)]) for hh in range(2))

        def update(j, ts, carry, diagonal):
            k0 = pl.multiple_of(j * tk, tk)
            out = []
            for hh in range(2):
                m, l, acc = carry[hh]
                t = ts[hh]
                if diagonal:
                    t = jnp.where(qry_idx >= key_idx, t, -jnp.inf)
                m_new = jnp.maximum(m, jnp.max(t, axis=0, keepdims=True))
                alpha = jnp.exp2(m - m_new)
                p = jnp.exp2(t - m_new)
                l_new = alpha * l + jnp.sum(p, axis=0, keepdims=True)
                acc_new = alpha * acc + _dot(vb[hh * D_F:(hh + 1) * D_F, pl.ds(k0, tk)], p.astype(BF16))
                out.append((m_new, l_new, acc_new))
            return tuple(out)

        def body(j, c):
            carry, ts = c
            ts_next = scores(j + 1)
            return update(j, ts, carry, False), ts_next

        init = (jnp.full((1, tq), -jnp.inf, F32), jnp.zeros((1, tq), F32), jnp.zeros((D_F, tq), F32))
        carry, ts = lax.fori_loop(0, i, body, ((init, init), scores(0)))
        (_, l0, a0), (_, l1, a1) = update(i, ts, carry, True)
        o_t = jnp.concatenate([a0 / l0, a1 / l1], axis=0)
        o_ref[pl.ds(q0, tq), :] = o_t.T.astype(BF16)
        return 0

    lax.fori_loop(0, seq // tq, q_block, 0)


def _fox_prompt(fqt, fkb, fvt, fcum, n_batch, seq, tq):
    npair = H_F // 2
    fm_spec = pl.BlockSpec((LANES, seq), lambda b, p: (b * npair + p, 0))
    tok_spec = pl.BlockSpec((seq, LANES), lambda b, p: (b, p))
    return pl.pallas_call(
        functools.partial(_fox_kernel, tq=tq),
        grid=(n_batch, npair),
        in_specs=[fm_spec, tok_spec, fm_spec, pl.BlockSpec((seq, H_F), lambda b, p: (b, 0))],
        out_specs=tok_spec,
        out_shape=jax.ShapeDtypeStruct((n_batch * seq, FOX_W), BF16),
        scratch_shapes=[pltpu.VMEM((2, LANES, seq), BF16), pltpu.VMEM((2, seq, LANES), BF16),
                        pltpu.VMEM((LANES, seq), BF16)],
        compiler_params=_cparams(("parallel", "parallel")),
        name="fox_prompt",
    )(fqt, fkb, fvt, fcum)


def _fox_sample_kernel(q_ref, kct_ref, vct_ref, kn_ref, vn_ref, frc_ref, frn_ref, o_ref):
    n = q_ref.shape[0]
    q2 = q_ref[...]
    lane = lax.broadcasted_iota(jnp.int32, (n, LANES), 1)
    rows = lax.broadcasted_iota(jnp.int32, (n, n), 0)
    cols = lax.broadcasted_iota(jnp.int32, (n, n), 1)
    kct = kct_ref[...]
    vct = vct_ref[...].astype(BF16)
    kn = kn_ref[...].astype(BF16)
    vn = vn_ref[...].astype(BF16)
    outs = []
    for hh in range(2):
        mine = (lane >= hh * D_F) & (lane < (hh + 1) * D_F)
        kc = jnp.where(_head_rows_mask(kct.shape, hh), kct, 0.0).astype(BF16)
        t_c = _dot(q2, kc) - frc_ref[0, hh:hh + 1, :] * LOG2E
        t_n = _dot_nt(jnp.where(mine, q2, jnp.zeros_like(q2)), kn) - frn_ref[0, hh:hh + 1, :] * LOG2E
        t_n = jnp.where(rows >= cols, t_n, -jnp.inf)
        m = jnp.maximum(jnp.max(t_c, axis=-1, keepdims=True), jnp.max(t_n, axis=-1, keepdims=True))
        p_c = jnp.exp2(t_c - m)
        p_n = jnp.exp2(t_n - m)
        l = jnp.sum(p_c, axis=-1, keepdims=True) + jnp.sum(p_n, axis=-1, keepdims=True)
        acc = _dot_nt(p_c.astype(BF16), vct) + _dot(p_n.astype(BF16), vn)
        outs.append(acc / l)
    o_ref[...] = jnp.where(lane < D_F, outs[0], outs[1]).astype(BF16)


def _fox_sample(fq, kct, vct, kn, vn, frow_c, frow_n, n_batch, n_new, past):
    npair = H_F // 2
    tok = pl.BlockSpec((n_new, LANES), lambda b, p: (b, p))
    cache = pl.BlockSpec((LANES, past), lambda b, p: (b * npair + p, 0))
    return pl.pallas_call(
        _fox_sample_kernel,
        grid=(n_batch, npair),
        in_specs=[tok, cache, cache, tok, tok,
                  pl.BlockSpec((1, 2, past), lambda b, p: (b * npair + p, 0, 0)),
                  pl.BlockSpec((1, 2, n_new), lambda b, p: (b * npair + p, 0, 0))],
        out_specs=tok,
        out_shape=jax.ShapeDtypeStruct((n_batch * n_new, FOX_W), BF16),
        compiler_params=_cparams(("parallel", "parallel")),
        name="fox_sample",
    )(fq, kct, vct, kn, vn, frow_c, frow_n)


def _memkv_kernel(m_ref, gin_ref, w_ref, gk_ref, k_ref, v_ref):
    x = m_ref[...]
    h = (x * lax.rsqrt(jnp.mean(x * x, axis=-1, keepdims=True) + EPS) * gin_ref[...]).astype(BF16)
    zk = _dot(h, w_ref[:, :MEM_W])
    for hh in range(H_M):
        sl = slice(hh * D_M, (hh + 1) * D_M)
        k_ref[:, sl] = _lane_rmsnorm(zk[:, sl]) * gk_ref[...]
    v_ref[...] = _dot(h, w_ref[:, MEM_W:])


def _memkv(mem2d, n_batch, g_mem_in, w_mem_kv, g_mem_k):
    blk = lambda w: pl.BlockSpec((N_MEM, w), lambda b: (b, 0))
    return pl.pallas_call(
        _memkv_kernel,
        grid=(n_batch,),
        in_specs=[blk(D_MODEL), _const_spec((1, D_MODEL)), _const_spec((D_MODEL, 2 * MEM_W)),
                  _const_spec((1, D_M))],
        out_specs=[blk(MEM_W), blk(MEM_W)],
        out_shape=[jax.ShapeDtypeStruct((n_batch * N_MEM, MEM_W), F32)] * 2,
        compiler_params=_cparams(("parallel",)),
        name="memkv",
    )(mem2d, g_mem_in, w_mem_kv, g_mem_k)


def _memattn_kernel(q_ref, k_ref, v_ref, o_ref):
    for hh in range(H_M):
        sl = slice(hh * D_M, (hh + 1) * D_M)
        s = _dot_nt(q_ref[:, sl], k_ref[:, sl].astype(BF16)) * (D_M ** -0.5)
        m = jnp.max(s, axis=-1, keepdims=True)
        p = jnp.exp(s - m)
        l = jnp.sum(p, axis=-1, keepdims=True)
        o_ref[:, sl] = (_dot(p.astype(BF16), v_ref[:, sl].astype(BF16)) / l).astype(BF16)


def _memattn(mq, mk, mv, n_batch, seq, tq):
    nq = seq // tq
    return pl.pallas_call(
        _memattn_kernel,
        grid=(n_batch, nq),
        in_specs=[pl.BlockSpec((tq, MEM_W), lambda b, i: (b * nq + i, 0)),
                  pl.BlockSpec((N_MEM, MEM_W), lambda b, i: (b, 0)),
                  pl.BlockSpec((N_MEM, MEM_W), lambda b, i: (b, 0))],
        out_specs=pl.BlockSpec((tq, MEM_W), lambda b, i: (b * nq + i, 0)),
        out_shape=jax.ShapeDtypeStruct((n_batch * seq, MEM_W), BF16),
        compiler_params=_cparams(("parallel", "arbitrary")),
        name="memattn",
    )(mq, mk, mv)


def _split2(x):
    hi = x.astype(BF16)
    return hi, (x - hi.astype(F32)).astype(BF16)


N_EXPERTS = N_GROUPS * E_PER_GROUP
N_SLABS = D_MODEL // LANES


def _router_logits(h2, wrt_hi_ref, wrt_lo_ref, brt_ref):
    h_hi, h_lo = _split2(h2)
    return (_dot(h_hi, wrt_hi_ref[...]) + (_dot(h_hi, wrt_lo_ref[...]) + _dot(h_lo, wrt_hi_ref[...]))
            + brt_ref[...])


def _route(logits, group=None):
    lane = lax.broadcasted_iota(jnp.int32, logits.shape, 1).astype(F32)
    neg = -jnp.inf
    first_idx = lambda mask: jnp.min(jnp.where(mask, lane, float(LANES)), axis=-1, keepdims=True)

    is_g = (lane >= N_EXPERTS) & (lane < N_EXPERTS + N_GROUPS)
    lg = jnp.where(is_g, logits, neg)
    mg = jnp.max(lg, axis=-1, keepdims=True)
    if group is None:
        group = first_idx(lg == mg) - N_EXPERTS
        p_sel = 1.0 / jnp.sum(jnp.exp(lg - mg), axis=-1, keepdims=True)
    else:
        lsel = jnp.max(jnp.where(lane == group + N_EXPERTS, logits, neg), axis=-1, keepdims=True)
        p_sel = jnp.exp(lsel - mg) / jnp.sum(jnp.exp(lg - mg), axis=-1, keepdims=True)

    in_grp = (lane >= group * E_PER_GROUP) & (lane < (group + 1) * E_PER_GROUP)
    le = jnp.where(in_grp, logits, neg)
    v1 = jnp.max(le, axis=-1, keepdims=True)
    i1 = first_idx(le == v1)
    le2 = jnp.where(lane == i1, neg, le)
    v2 = jnp.max(le2, axis=-1, keepdims=True)
    i2 = first_idx(le2 == v2)
    e2 = jnp.exp(v2 - v1)
    w1 = p_sel / (1.0 + e2)
    w2 = p_sel * e2 / (1.0 + e2)
    return group, jnp.where(lane == i1, w1, 0.0) + jnp.where(lane == i2, w2, 0.0)


def _merge_kernel(x_ref, or_ref, of_ref, om_ref, gt_ref, wr_ref, wf_ref, wm_ref, wo_ref, g2_ref,
                  wrt_hi_ref, wrt_lo_ref, brt_ref, x1_ref, h2_ref, route_ref, *, sorted_moe):
    g = lambda b: gt_ref[:, b * D_MODEL:(b + 1) * D_MODEL].astype(F32)
    merged = (g(0) * _dot(or_ref[...], wr_ref[...]) + g(1) * _dot(of_ref[...], wf_ref[...])
              + g(2) * _dot(om_ref[...], wm_ref[...]))
    x1 = x_ref[...] + _dot(merged.astype(BF16), wo_ref[...])
    x1_ref[...] = x1
    h2 = x1 * lax.rsqrt(jnp.mean(x1 * x1, axis=-1, keepdims=True) + EPS) * g2_ref[...]
    group, comb = _route(_router_logits(h2, wrt_hi_ref, wrt_lo_ref, brt_ref))
    if sorted_moe:
        tm = h2.shape[0]
        for s in range(N_SLABS):
            h2_ref[pl.ds(s, tm, stride=N_SLABS), :] = h2[:, s * LANES:(s + 1) * LANES]
        route_ref[...] = jnp.broadcast_to(group, route_ref.shape)
    else:
        h2_ref[...] = h2.astype(BF16)
        for gi in range(N_GROUPS):
            route_ref[gi] = comb[:, gi * E_PER_GROUP:(gi + 1) * E_PER_GROUP]


def _merge(x2d, o_r, o_f, o_m, gates, tm, sorted_moe, wr, wf, wm, wo, g2, wrt_hi, wrt_lo, brt):
    t = x2d.shape[0]
    row = lambda w: pl.BlockSpec((tm, w), lambda i: (i, 0))
    sds = jax.ShapeDtypeStruct
    if sorted_moe:
        h2_spec, h2_shape = pl.BlockSpec((tm * N_SLABS, LANES), lambda i: (i, 0)), sds((t * N_SLABS, LANES), F32)
        rt_spec, rt_shape = row(E_PER_GROUP), sds((t, E_PER_GROUP), F32)
    else:
        h2_spec, h2_shape = row(D_MODEL), sds((t, D_MODEL), BF16)
        rt_spec = pl.BlockSpec((N_GROUPS, tm, E_PER_GROUP), lambda i: (0, i, 0))
        rt_shape = sds((N_GROUPS, t, E_PER_GROUP), F32)
    return pl.pallas_call(
        functools.partial(_merge_kernel, sorted_moe=sorted_moe),
        grid=(t // tm,),
        in_specs=[row(D_MODEL), row(RV_W), row(FOX_W), row(MEM_W), row(GATE_W),
                  _const_spec((RV_W, D_MODEL)), _const_spec((FOX_W, D_MODEL)), _const_spec((MEM_W, D_MODEL)),
                  _const_spec((D_MODEL, D_MODEL)), _const_spec((1, D_MODEL)),
                  _const_spec((D_MODEL, LANES)), _const_spec((D_MODEL, LANES)), _const_spec((1, LANES))],
        out_specs=[row(D_MODEL), h2_spec, rt_spec],
        out_shape=[sds((t, D_MODEL), F32), h2_shape, rt_shape],
        compiler_params=_cparams(("parallel",)),
        name="merge",
    )(x2d, o_r, o_f, o_m, gates, wr, wf, wm, wo, g2, wrt_hi, wrt_lo, brt)


def _group_experts(h, cw, wg_ref, wu_ref, wd_ref, act):
    for e in range(E_PER_GROUP):
        a = _dot(h, wg_ref[0, e])
        u = _dot(h, wu_ref[0, e])
        act[:, e * D_EXPERT:(e + 1) * D_EXPERT] = ((a * jax.nn.sigmoid(a)) * u * cw(e)).astype(BF16)
    return _dot(act[...], wd_ref[0])


def _row_gather(idx_ref, base, src_hbm, dst, sem, n):
    def body(r, _):
        pltpu.make_async_copy(src_hbm.at[idx_ref[base + r]], dst.at[r], sem).start()
        return 0
    lax.fori_loop(0, n, body, 0, unroll=8)


def _row_gather_wait(src_hbm, dst, sem, n):
    pltpu.make_async_copy(src_hbm.at[pl.ds(0, n)], dst, sem).wait()


def _moe_sorted_kernel(tg_ref, nvt_ref, src_ref, h3_hbm, wrt_hi_ref, wrt_lo_ref, brt_ref,
                       wg_ref, wu_ref, wd_ref, y_ref, xbuf, sem, act, *, tm):
    k = pl.program_id(0)
    nvt = nvt_ref[0]
    slot = lax.rem(k, 2)

    @pl.when((k == 0) & (nvt > 0))
    def _():
        _row_gather(src_ref, 0, h3_hbm, xbuf.at[0], sem.at[0], tm)

    @pl.when(k < nvt)
    def _():
        _row_gather_wait(h3_hbm, xbuf.at[slot], sem.at[slot], tm)

        @pl.when(k + 1 < nvt)
        def _():
            _row_gather(src_ref, (k + 1) * tm, h3_hbm, xbuf.at[1 - slot], sem.at[1 - slot], tm)

        x = jnp.concatenate([xbuf[slot, :, s, :] for s in range(N_SLABS)], axis=1)
        group = tg_ref[k]
        _, comb = _route(_router_logits(x, wrt_hi_ref, wrt_lo_ref, brt_ref), group.astype(F32))
        lane = lax.broadcasted_iota(jnp.int32, comb.shape, 1)
        cw = lambda e: jnp.sum(jnp.where(lane == group * E_PER_GROUP + e, comb, 0.0), axis=-1, keepdims=True)
        y = _group_experts(x.astype(BF16), cw, wg_ref, wu_ref, wd_ref, act)
        for s in range(N_SLABS):
            y_ref[pl.ds(s, tm, stride=N_SLABS), :] = y[:, s * LANES:(s + 1) * LANES]

    @pl.when(k >= nvt)
    def _():
        y_ref[...] = jnp.zeros(y_ref.shape, F32)


def _moe_sorted(h3, tile_group, n_valid_tiles, src, n_tiles, tm, wrt_hi, wrt_lo, brt, wg, wu, wd):
    wspec = lambda shape: pl.BlockSpec(shape, lambda k, tg, nv, sr: (tg[k],) + (0,) * (len(shape) - 1))
    cspec = lambda shape: pl.BlockSpec(shape, lambda k, tg, nv, sr: (0,) * len(shape),
                                       pipeline_mode=pl.Buffered(1))
    return pl.pallas_call(
        functools.partial(_moe_sorted_kernel, tm=tm),
        grid_spec=pltpu.PrefetchScalarGridSpec(
            num_scalar_prefetch=3,
            grid=(n_tiles,),
            in_specs=[pl.BlockSpec(memory_space=pl.ANY),
                      cspec((D_MODEL, LANES)), cspec((D_MODEL, LANES)), cspec((1, LANES)),
                      wspec((1, E_PER_GROUP, D_MODEL, D_EXPERT)), wspec((1, E_PER_GROUP, D_MODEL, D_EXPERT)),
                      wspec((1, E_PER_GROUP * D_EXPERT, D_MODEL))],
            out_specs=pl.BlockSpec((tm * N_SLABS, LANES), lambda k, tg, nv, sr: (k, 0)),
            scratch_shapes=[pltpu.VMEM((2, tm, N_SLABS, LANES), F32), pltpu.SemaphoreType.DMA((2,)),
                            pltpu.VMEM((tm, E_PER_GROUP * D_EXPERT), BF16)]),
        out_shape=jax.ShapeDtypeStruct((n_tiles * tm * N_SLABS, LANES), F32),
        compiler_params=_cparams(("arbitrary",)),
        name="moe_sorted",
    )(tile_group, n_valid_tiles, src, h3, wrt_hi, wrt_lo, brt, wg, wu, wd)


def _combine_kernel(pos_ref, y3_hbm, x1_ref, o_ref, ybuf, sem, *, tm):
    k = pl.program_id(0)
    slot = lax.rem(k, 2)

    @pl.when(k == 0)
    def _():
        _row_gather(pos_ref, 0, y3_hbm, ybuf.at[0], sem.at[0], tm)

    _row_gather_wait(y3_hbm, ybuf.at[slot], sem.at[slot], tm)

    @pl.when(k + 1 < pl.num_programs(0))
    def _():
        _row_gather(pos_ref, (k + 1) * tm, y3_hbm, ybuf.at[1 - slot], sem.at[1 - slot], tm)

    for s in range(N_SLABS):
        sl = slice(s * LANES, (s + 1) * LANES)
        o_ref[:, sl] = x1_ref[:, sl] + ybuf[slot, :, s, :]


def _combine(y3, pos, x1, tm):
    t = x1.shape[0]
    return pl.pallas_call(
        functools.partial(_combine_kernel, tm=tm),
        grid_spec=pltpu.PrefetchScalarGridSpec(
            num_scalar_prefetch=1,
            grid=(t // tm,),
            in_specs=[pl.BlockSpec(memory_space=pl.ANY), pl.BlockSpec((tm, D_MODEL), lambda k, ps: (k, 0))],
            out_specs=pl.BlockSpec((tm, D_MODEL), lambda k, ps: (k, 0)),
            scratch_shapes=[pltpu.VMEM((2, tm, N_SLABS, LANES), F32), pltpu.SemaphoreType.DMA((2,))]),
        out_shape=jax.ShapeDtypeStruct((t, D_MODEL), F32),
        compiler_params=_cparams(("arbitrary",)),
        name="combine",
    )(pos, y3, x1)


def _moe_kernel(h_ref, comb_ref, x1_ref, wg_ref, wu_ref, wd_ref, o_ref, act):
    g = pl.program_id(1)
    comb = comb_ref[0]
    y = _group_experts(h_ref[...], lambda e: comb[:, e:e + 1], wg_ref, wu_ref, wd_ref, act)

    @pl.when(g == 0)
    def _():
        o_ref[...] = x1_ref[...] + y

    @pl.when(g != 0)
    def _():
        o_ref[...] += y


def _moe(h2, comb, x1, tm, wg, wu, wd):
    t = h2.shape[0]
    return pl.pallas_call(
        _moe_kernel,
        grid=(t // tm, N_GROUPS),
        in_specs=[pl.BlockSpec((tm, D_MODEL), lambda i, g: (i, 0)),
                  pl.BlockSpec((1, tm, E_PER_GROUP), lambda i, g: (g, i, 0)),
                  pl.BlockSpec((tm, D_MODEL), lambda i, g: (i, 0)),
                  pl.BlockSpec((1, E_PER_GROUP, D_MODEL, D_EXPERT), lambda i, g: (g, 0, 0, 0)),
                  pl.BlockSpec((1, E_PER_GROUP, D_MODEL, D_EXPERT), lambda i, g: (g, 0, 0, 0)),
                  pl.BlockSpec((1, E_PER_GROUP * D_EXPERT, D_MODEL), lambda i, g: (g, 0, 0))],
        out_specs=pl.BlockSpec((tm, D_MODEL), lambda i, g: (i, 0)),
        out_shape=jax.ShapeDtypeStruct((t, D_MODEL), F32),
        scratch_shapes=[pltpu.VMEM((tm, E_PER_GROUP * D_EXPERT), BF16)],
        compiler_params=_cparams(("parallel", "arbitrary")),
        name="moe",
    )(h2, comb, x1, wg, wu, wd)


def _rope_tables(pos):
    half = DK_R // 2
    inv = ROPE_BASE ** (-jnp.arange(half, dtype=F32) / half)
    ang = pos.astype(F32)[:, None] * inv[None, :]
    c = jnp.cos(ang)
    s = jnp.sin(ang)
    return jnp.concatenate([c, c, c, c], axis=-1), jnp.concatenate([-s, s, -s, s], axis=-1)


def _pair_rows(frows):
    return frows.reshape(frows.shape[0] // 2, 2, frows.shape[1])


def kernel(x_prompt, x_sample, state_ret, cache_fox_k, cache_fox_v, cache_fox_logf, cache_mem_k, cache_mem_v,
           mem_prompt, g_norm1, w_in, b_forget, g_fox_q, g_fox_k, g_mem_q, g_mem_in, w_mem_kv, g_mem_k,
           g_ret_out, w_br_ret, w_br_fox, w_br_mem, w_out, g_norm2, w_route_group, b_route_group,
           w_route_expert, b_route_expert, w_exp_gate, w_exp_up, w_exp_down):
    nb, seq, _ = x_prompt.shape
    nbs, n_new, _ = x_sample.shape
    past = cache_fox_k.shape[2]
    l = 0

    wi = w_in[l]
    o_ff = 2 * RQK_W + 2 * RV_W + 3 * FOX_W
    w_in_r = jnp.concatenate(
        [wi[:, :o_ff], wi[:, o_ff + H_F:], wi[:, o_ff:o_ff + H_F],
         jnp.zeros((D_MODEL, LANES - H_F), F32)], axis=1).astype(BF16)
    bf_pad = jnp.concatenate([b_forget[l], jnp.zeros((LANES - H_F,), F32)])[None, :]
    g1 = g_norm1[l][None, :]
    gfq = jnp.tile(g_fox_q[l], H_F)[None, :]
    gfk = jnp.tile(g_fox_k[l], H_F)[None, :]
    gmq = g_mem_q[l][None, :]
    hid = jnp.arange(FOX_W) // D_F
    bd = jnp.where(hid[:, None] == hid[None, :], 1.0 / D_F, 0.0).astype(BF16)
    wr = w_br_ret[l].astype(BF16)
    wf = w_br_fox[l].astype(BF16)
    wm = w_br_mem[l].astype(BF16)
    wo = w_out[l].astype(BF16)
    g2 = g_norm2[l][None, :]
    n_e = N_GROUPS * E_PER_GROUP
    wrt = jnp.concatenate([w_route_expert[l], w_route_group[l],
                           jnp.zeros((D_MODEL, LANES - n_e - N_GROUPS), F32)], axis=1)
    wrt_hi = wrt.astype(BF16)
    wrt_lo = (wrt - wrt_hi.astype(F32)).astype(BF16)
    brt = jnp.concatenate([b_route_expert[l], b_route_group[l],
                           jnp.zeros((LANES - n_e - N_GROUPS,), F32)])[None, :]
    wg = w_exp_gate[l].astype(BF16)
    wu = w_exp_up[l].astype(BF16)
    wd = w_exp_down[l].astype(BF16).reshape(N_GROUPS, E_PER_GROUP * D_EXPERT, D_MODEL)
    gro = g_ret_out[l][None, :]
    prep_w = (g1, w_in_r, bf_pad, gfq, gfk, gmq, bd)
    merge_w = (wr, wf, wm, wo, g2, wrt_hi, wrt_lo, brt)

    tm = 512
    xp = x_prompt.reshape(nb * seq, D_MODEL)
    (rq, rk, rv, rg, fqt, fkt, fvt, lft, mq, gates, fkb, lf) = _prep(
        xp, _rope_tables(jnp.arange(seq)), tm, seq // tm, True, *prep_w)
    o_r, s_fin = _retention(rq, rk, rv, rg, nb, seq, 256, gro, None)
    fcum = _cumsum_cols(lf.reshape(nb, seq, H_F), 256).reshape(nb * seq, H_F)
    o_f = _fox_prompt(fqt, fkb, fvt, fcum, nb, seq, 256)
    mk, mv = _memkv(mem_prompt.reshape(nb * N_MEM, D_MODEL), nb, g_mem_in[l][None, :],
                    w_mem_kv[l].astype(BF16), g_mem_k[l][None, :])
    o_m = _memattn(mq, mk, mv, nb, seq, tm)
    x1, h3, grp = _merge(xp, o_r, o_f, o_m, gates, tm, True, *merge_w)
    t_p = nb * seq
    n_tiles = t_p // tm + N_GROUPS
    gi = grp[:, 0].astype(jnp.int32)
    onehot = (gi[:, None] == jnp.arange(N_GROUPS, dtype=jnp.int32)[None, :]).astype(jnp.int32)
    csum = jnp.cumsum(onehot, axis=0)
    rank = jnp.sum(csum * onehot, axis=1) - 1
    tiles_g = (csum[-1] + tm - 1) // tm
    tile_end = jnp.cumsum(tiles_g)
    row_start = (tile_end - tiles_g) * tm
    pos = row_start[gi] + rank
    src = jnp.zeros((n_tiles * tm,), jnp.int32).at[pos].set(jnp.arange(t_p, dtype=jnp.int32))
    tile_ids = jnp.arange(n_tiles, dtype=jnp.int32)
    tile_group = jnp.minimum(jnp.sum((tile_ids[:, None] >= tile_end[None, :]).astype(jnp.int32), axis=1),
                             N_GROUPS - 1)
    y3 = _moe_sorted(h3.reshape(t_p, N_SLABS, LANES), tile_group, tile_end[-1:], src, n_tiles, tm,
                     wrt_hi, wrt_lo, brt, wg, wu, wd)
    y_prompt = _combine(y3.reshape(n_tiles * tm, N_SLABS, LANES), pos, x1, tm).reshape(nb, seq, D_MODEL)

    ts = nbs * n_new
    xs = x_sample.reshape(ts, D_MODEL)
    pos_s = jnp.tile(past + jnp.arange(n_new), nbs)
    (rq_s, rk_s, rv_s, rg_s, fq_s, fk_s, fv_s, lf_s, mq_s, gates_s) = _prep(
        xs, _rope_tables(pos_s), ts, 1, False, *prep_w)
    o_r_s, s_new = _retention(rq_s, rk_s, rv_s, rg_s, nbs, n_new, n_new, gro, state_ret[l])
    pad = (-(past + n_new)) % 256
    lf_rows = jnp.concatenate([jnp.swapaxes(cache_fox_logf[l], 1, 2),
                               jnp.swapaxes(lf_s.reshape(nbs, n_new, H_F), 1, 2),
                               jnp.zeros((nbs, H_F, pad), F32)], axis=2).reshape(nbs * H_F, past + n_new + pad)
    f_all = _cumsum(lf_rows, 256)
    feat_major = lambda c: jnp.transpose(c, (0, 2, 3, 1)).reshape(nbs * FOX_W, past)
    o_f_s = _fox_sample(fq_s, feat_major(cache_fox_k[l]), feat_major(cache_fox_v[l]), fk_s, fv_s,
                        _pair_rows(f_all[:, :past]), _pair_rows(f_all[:, past:past + n_new]),
                        nbs, n_new, past)
    o_m_s = _memattn(mq_s, cache_mem_k[l].reshape(nbs * N_MEM, MEM_W),
                     cache_mem_v[l].reshape(nbs * N_MEM, MEM_W), nbs, n_new, n_new)
    x1_s, h2_s, comb_s = _merge(xs, o_r_s, o_f_s, o_m_s, gates_s, ts, False, *merge_w)
    y_sample = _moe(h2_s, comb_s, x1_s, ts, wg, wu, wd).reshape(nbs, n_new, D_MODEL)

    token_major = lambda a: jnp.transpose(a.reshape(nb, H_F, D_F, seq), (0, 3, 1, 2))

    return (y_prompt, y_sample,
            s_fin[None], token_major(fkt)[None], token_major(fvt)[None],
            jnp.swapaxes(lft.reshape(nb, H_F, seq), 1, 2)[None],
            mk.reshape(1, nb, N_MEM, H_M, D_M), mv.reshape(1, nb, N_MEM, H_M, D_M),
            s_new[None], fk_s.reshape(1, nbs, n_new, H_F, D_F), fv_s.reshape(1, nbs, n_new, H_F, D_F),
            lf_s.reshape(1, nbs, n_new, H_F))
```

```python
import functools

import jax
import jax.numpy as jnp
from jax import lax
from jax.experimental import pallas as pl
from jax.experimental.pallas import tpu as pltpu

F32 = jnp.float32
BF16 = jnp.bfloat16

D_MODEL = 1024
H_R, DK_R, DV_R = 4, 64, 128
H_F, D_F = 8, 64
H_M, D_M = 4, 128
N_MEM = 256
N_GROUPS, E_PER_GROUP, D_EXPERT = 4, 8, 256
ROPE_BASE = 10000.0
EPS = 1e-6
LOG2E = 1.4426950408889634

RQK_W = H_R * DK_R
RV_W = H_R * DV_R
FOX_W = H_F * D_F
MEM_W = H_M * D_M
GATE_W = 3 * D_MODEL
LANES = 128

C_RQ = 0
C_RK = C_RQ + RQK_W
C_RV = C_RK + RQK_W
C_RG = C_RV + RV_W
C_FQ = C_RG + RV_W
C_FK = C_FQ + FOX_W
C_FV = C_FK + FOX_W
C_MQ = C_FV + FOX_W
C_GT = C_MQ + MEM_W
C_FF = C_GT + GATE_W
N_IN_PAD = C_FF + LANES

VMEM_LIMIT = 56 * 1024 * 1024


def _cparams(sem):
    return pltpu.CompilerParams(dimension_semantics=sem, vmem_limit_bytes=VMEM_LIMIT)


def _const_spec(shape):
    nd = len(shape)
    return pl.BlockSpec(shape, lambda *_: (0,) * nd, pipeline_mode=pl.Buffered(1))


def _dot(a, b):
    return jnp.dot(a, b, preferred_element_type=F32)


def _dot_nt(a, b):
    return lax.dot_general(a, b, (((1,), (1,)), ((), ())), preferred_element_type=F32)


def _dot_tn(a, b):
    return lax.dot_general(a, b, (((0,), (0,)), ((), ())), preferred_element_type=F32)


def _lane_rmsnorm(z):
    return z * lax.rsqrt(jnp.mean(z * z, axis=-1, keepdims=True) + EPS)


def _prep_kernel(x_ref, g1_ref, w_ref, cos_ref, sin_ref, bf_ref, gfq_ref, gfk_ref, gmq_ref, bd_ref,
                 rq_ref, rk_ref, rv_ref, rg_ref, fq_ref, fk_ref, fv_ref, lf_ref, mq_ref, gt_ref, *extra,
                 seq_minor):
    x = x_ref[...]
    h = (x * lax.rsqrt(jnp.mean(x * x, axis=-1, keepdims=True) + EPS) * g1_ref[...]).astype(BF16)

    def proj(c0, width):
        return _dot(h, w_ref[:, c0:c0 + width])

    cos = cos_ref[...]
    sin = sin_ref[...]
    lane = lax.broadcasted_iota(jnp.int32, cos.shape, 1)
    first_half = (lane % DK_R) < (DK_R // 2)

    def rope(z):
        swapped = jnp.where(first_half, pltpu.roll(z, LANES - DK_R // 2, 1), pltpu.roll(z, DK_R // 2, 1))
        return z * cos + swapped * sin

    zq = proj(C_RQ, RQK_W)
    zk = proj(C_RK, RQK_W)
    for p in range(RQK_W // LANES):
        sl = slice(p * LANES, (p + 1) * LANES)
        rq_ref[:, sl] = rope(zq[:, sl])
        rk_ref[:, sl] = rope(zk[:, sl]) * (DK_R ** -0.5)
    rv_ref[...] = proj(C_RV, RV_W)
    rg_ref[...] = proj(C_RG, RV_W)

    def head64_norm(z, g_ref):
        ms = _dot((z * z).astype(BF16), bd_ref[...])
        return z * lax.rsqrt(ms + EPS) * g_ref[...]

    fq = head64_norm(proj(C_FQ, FOX_W), gfq_ref) * (D_F ** -0.5 * LOG2E)
    fk = head64_norm(proj(C_FK, FOX_W), gfk_ref)
    fv = proj(C_FV, FOX_W)
    if seq_minor:
        fq_ref[...] = fq.T.astype(BF16)
        fk_ref[...] = fk.T
        fv_ref[...] = fv.T
        extra[0][...] = fk.astype(BF16)
    else:
        fq_ref[...] = fq.astype(BF16)
        fk_ref[...] = fk
        fv_ref[...] = fv

    zm = proj(C_MQ, MEM_W)
    for hh in range(H_M):
        sl = slice(hh * D_M, (hh + 1) * D_M)
        mq_ref[:, sl] = (_lane_rmsnorm(zm[:, sl]) * gmq_ref[...]).astype(BF16)

    for b in range(3):
        gt_ref[:, b * D_MODEL:(b + 1) * D_MODEL] = jax.nn.sigmoid(proj(C_GT + b * D_MODEL, D_MODEL)).astype(BF16)

    v = proj(C_FF, LANES) + bf_ref[...]
    lf = jnp.minimum(v, 0.0) - jnp.log1p(jnp.exp(-jnp.abs(v)))
    if seq_minor:
        lf_ref[...] = lf.T[:H_F, :]
        extra[1][...] = lf[:, :H_F]
    else:
        lf_ref[...] = lf[:, :H_F]


def _prep(x2d, tables, tm, n_pos_tiles, seq_minor, g1, w_in_r, bf_pad, gfq, gfk, gmq, bd):
    t = x2d.shape[0]
    cos_t, sin_t = tables
    row = lambda w: pl.BlockSpec((tm, w), lambda i: (i, 0))
    pos = pl.BlockSpec((tm, LANES), lambda i: (i % n_pos_tiles, 0))
    sds = jax.ShapeDtypeStruct
    if seq_minor:
        nb, seq = t // (n_pos_tiles * tm), n_pos_tiles * tm
        fm = lambda rows_: pl.BlockSpec((rows_, tm), lambda i: (i // n_pos_tiles, i % n_pos_tiles))
        fox_shapes = [sds((nb * FOX_W, seq), BF16), sds((nb * FOX_W, seq), F32), sds((nb * FOX_W, seq), F32),
                      sds((nb * H_F, seq), F32)]
        fox_specs = [fm(FOX_W), fm(FOX_W), fm(FOX_W), fm(H_F)]
        extra_shapes, extra_specs = [sds((t, FOX_W), BF16), sds((t, H_F), F32)], [row(FOX_W), row(H_F)]
    else:
        fox_shapes = [sds((t, FOX_W), BF16), sds((t, FOX_W), F32), sds((t, FOX_W), F32), sds((t, H_F), F32)]
        fox_specs = [row(FOX_W), row(FOX_W), row(FOX_W), row(H_F)]
        extra_shapes, extra_specs = [], []
    out_shapes = [sds((t, RQK_W), F32), sds((t, RQK_W), F32), sds((t, RV_W), F32), sds((t, RV_W), F32),
                  *fox_shapes, sds((t, MEM_W), BF16), sds((t, GATE_W), BF16), *extra_shapes]
    out_specs = [row(RQK_W), row(RQK_W), row(RV_W), row(RV_W), *fox_specs, row(MEM_W), row(GATE_W),
                 *extra_specs]
    return pl.pallas_call(
        functools.partial(_prep_kernel, seq_minor=seq_minor),
        grid=(t // tm,),
        in_specs=[row(D_MODEL), _const_spec((1, D_MODEL)), _const_spec((D_MODEL, N_IN_PAD)), pos, pos,
                  _const_spec((1, LANES)), _const_spec((1, FOX_W)), _const_spec((1, FOX_W)),
                  _const_spec((1, D_M)), _const_spec((FOX_W, FOX_W))],
        out_specs=out_specs,
        out_shape=out_shapes,
        compiler_params=_cparams(("parallel",)),
        name="prep",
    )(x2d, g1, w_in_r, cos_t, sin_t, bf_pad, gfq, gfk, gmq, bd)


def _ret_kernel(*refs, has_init):
    if has_init:
        (rq_ref, rk_ref, rv_ref, rg_ref, dmat_ref, qdec_ref, kdec_ref, gpow_ref, gro_ref, s0_ref,
         o_ref, sfin_ref, state) = refs
    else:
        (rq_ref, rk_ref, rv_ref, rg_ref, dmat_ref, qdec_ref, kdec_ref, gpow_ref, gro_ref,
         o_ref, sfin_ref, state) = refs
    c = pl.program_id(1)

    @pl.when(c == 0)
    def _():
        state[...] = jnp.zeros(state.shape, F32)
        if has_init:
            for h in range(H_R):
                r0 = DK_R * (h % 2)
                state[h, r0:r0 + DK_R, :] = s0_ref[0, h]

    lane = lax.broadcasted_iota(jnp.int32, (rq_ref.shape[0], LANES), 1)
    for p in range(H_R // 2):
        sl = slice(p * LANES, (p + 1) * LANES)
        q2 = rq_ref[:, sl]
        k2 = rk_ref[:, sl]
        kd2 = k2 * kdec_ref[p]
        for hh in range(2):
            h = 2 * p + hh
            mine = (lane >= hh * DK_R) & (lane < (hh + 1) * DK_R)
            qm = jnp.where(mine, q2, 0.0).astype(BF16)
            kdm = jnp.where(mine, kd2, 0.0).astype(BF16)
            v = rv_ref[:, h * DV_R:(h + 1) * DV_R].astype(BF16)
            sc = _dot_nt(qm, k2.astype(BF16)) * dmat_ref[h]
            s_old = state[h]
            o = _dot(sc.astype(BF16), v) + _dot(qm, s_old.astype(BF16)) * qdec_ref[h]
            state[h] = gpow_ref[h] * s_old + _dot_tn(kdm, v)
            normed = _lane_rmsnorm(o) * gro_ref[...]
            rg = rg_ref[:, h * DV_R:(h + 1) * DV_R]
            o_ref[:, h * DV_R:(h + 1) * DV_R] = (normed * (rg * jax.nn.sigmoid(rg))).astype(BF16)

    @pl.when(c == pl.num_programs(1) - 1)
    def _():
        for h in range(H_R):
            r0 = DK_R * (h % 2)
            sfin_ref[0, h] = state[h, r0:r0 + DK_R, :]


def _retention(rq, rk, rv, rg, n_batch, seq, ch, g_ret_out, state0):
    nc = seq // ch
    lg = jnp.log1p(-jnp.exp2(-5.0 - jnp.arange(H_R, dtype=F32)))
    idx = jnp.arange(ch, dtype=F32)
    diff = idx[:, None] - idx[None, :]
    causal = diff >= 0
    dmat = jnp.where(causal[None], jnp.exp(jnp.where(causal, diff, 0.0)[None] * lg[:, None, None]), 0.0)
    q_dec = jnp.exp((idx + 1.0)[None, :] * lg[:, None])
    k_dec = jnp.exp((ch - 1.0 - idx)[None, :] * lg[:, None])
    qdec = jnp.broadcast_to(q_dec[:, :, None], (H_R, ch, DV_R))
    kdec = jnp.broadcast_to(k_dec[:, :, None], (H_R, ch, DK_R))
    kdec = kdec.reshape(H_R // 2, 2, ch, DK_R).transpose(0, 2, 1, 3).reshape(H_R // 2, ch, LANES)
    gpow = jnp.broadcast_to(jnp.exp(ch * lg)[:, None, None], (H_R, 1, DV_R))

    has_init = state0 is not None
    blk = lambda w: pl.BlockSpec((ch, w), lambda b, c: (b * nc + c, 0))
    in_specs = [blk(RQK_W), blk(RQK_W), blk(RV_W), blk(RV_W),
                _const_spec((H_R, ch, ch)), _const_spec((H_R, ch, DV_R)),
                _const_spec((H_R // 2, ch, LANES)), _const_spec((H_R, 1, DV_R)), _const_spec((1, DV_R))]
    args = [rq, rk, rv, rg, dmat, qdec, kdec, gpow, g_ret_out]
    if has_init:
        in_specs.append(pl.BlockSpec((1, H_R, DK_R, DV_R), lambda b, c: (b, 0, 0, 0)))
        args.append(state0)
    return pl.pallas_call(
        functools.partial(_ret_kernel, has_init=has_init),
        grid=(n_batch, nc),
        in_specs=in_specs,
        out_specs=[blk(RV_W), pl.BlockSpec((1, H_R, DK_R, DV_R), lambda b, c: (b, 0, 0, 0))],
        out_shape=[jax.ShapeDtypeStruct((n_batch * seq, RV_W), BF16),
                   jax.ShapeDtypeStruct((n_batch, H_R, DK_R, DV_R), F32)],
        scratch_shapes=[pltpu.VMEM((H_R, LANES, DV_R), F32)],
        compiler_params=_cparams(("parallel", "arbitrary")),
        name="retention",
    )(*args)


def _split3(x):
    hi = x.astype(BF16)
    r1 = x - hi.astype(F32)
    mid = r1.astype(BF16)
    lo = (r1 - mid.astype(F32)).astype(BF16)
    return hi, mid, lo


def _cumsum_kernel(x_ref, o_ref, *, blk):
    rows, n = x_ref.shape
    r = lax.broadcasted_iota(jnp.int32, (blk, blk), 0)
    c = lax.broadcasted_iota(jnp.int32, (blk, blk), 1)
    tri = jnp.where(r <= c, 1.0, 0.0).astype(BF16)
    carry = jnp.zeros((rows, 1), F32)
    for i in range(n // blk):
        hi, mid, lo = _split3(x_ref[:, i * blk:(i + 1) * blk])
        cum = (_dot(hi, tri) + _dot(mid, tri)) + _dot(lo, tri) + carry
        o_ref[:, i * blk:(i + 1) * blk] = cum
        carry = cum[:, blk - 1:blk]


def _cumsum_cols_kernel(x_ref, o_ref, *, blk):
    n = x_ref.shape[1]
    r = lax.broadcasted_iota(jnp.int32, (blk, blk), 0)
    c = lax.broadcasted_iota(jnp.int32, (blk, blk), 1)
    tri = jnp.where(r >= c, 1.0, 0.0).astype(BF16)
    carry = jnp.zeros((1, x_ref.shape[2]), F32)
    for i in range(n // blk):
        hi, mid, lo = _split3(x_ref[0, i * blk:(i + 1) * blk, :])
        cum = (_dot(tri, hi) + _dot(tri, mid)) + _dot(tri, lo) + carry
        o_ref[0, i * blk:(i + 1) * blk, :] = cum
        carry = cum[blk - 1:blk, :]


def _cumsum_cols(x, blk):
    nb, t, w = x.shape
    spec = pl.BlockSpec((1, t, w), lambda b: (b, 0, 0))
    return pl.pallas_call(
        functools.partial(_cumsum_cols_kernel, blk=blk),
        grid=(nb,),
        in_specs=[spec], out_specs=spec,
        out_shape=jax.ShapeDtypeStruct(x.shape, F32),
        compiler_params=_cparams(("parallel",)),
        name="cumsum_cols",
    )(x)


def _cumsum(x, blk):
    r, t = x.shape
    spec = pl.BlockSpec((H_F, t), lambda b: (b, 0))
    return pl.pallas_call(
        functools.partial(_cumsum_kernel, blk=blk),
        grid=(r // H_F,),
        in_specs=[spec], out_specs=spec,
        out_shape=jax.ShapeDtypeStruct(x.shape, F32),
        compiler_params=_cparams(("parallel",)),
        name="cumsum",
    )(x)


def _head_rows_mask(shape, hh):
    sub = lax.broadcasted_iota(jnp.int32, shape, 0)
    return (sub >= hh * D_F) & (sub < (hh + 1) * D_F)


N_AUG = 3


def _fox_kernel(qt_ref, k_ref, vt_ref, fcum_ref, o_ref, qa, ka, vb, *, tq):
    seq = k_ref.shape[0]
    tk = tq
    qt = qt_ref[...].astype(F32)
    k = k_ref[...].astype(F32)
    sub = lax.broadcasted_iota(jnp.int32, qt.shape, 0)
    lane = lax.broadcasted_iota(jnp.int32, k.shape, 1)
    for hh in range(2):
        own, oth = hh * D_F, (1 - hh) * D_F
        qa[hh] = jnp.where((sub >= own) & (sub < own + D_F), qt,
                           jnp.where((sub >= oth) & (sub < oth + N_AUG), 1.0, 0.0)).astype(BF16)
        bias = fcum_ref[:, hh:hh + 1] * (-LOG2E)
        kk = jnp.where((lane >= own) & (lane < own + D_F), k, 0.0)
        for a, piece in enumerate(_split3(bias)):
            kk = jnp.where(lane == oth + a, piece.astype(F32), kk)
        ka[hh] = kk.astype(BF16)
    vb[...] = vt_ref[...].astype(BF16)
    causal = (lax.broadcasted_iota(jnp.int32, (tk, tq), 1) >= lax.broadcasted_iota(jnp.int32, (tk, tq), 0))

    pairs = [(i, j) for i in range(seq // tq) for j in range(i + 1)]

    def stage_a(i, j):
        return tuple(_dot(ka[hh, j * tk:(j + 1) * tk, :], qa[hh, :, i * tq:(i + 1) * tq]) for hh in range(2))

    def stage_b(i, j, ts, stats):
        new_stats, probs = [], []
        for hh in range(2):
            t = jnp.where(causal, ts[hh], -jnp.inf) if j == i else ts[hh]
            t_max = jnp.max(t, axis=0, keepdims=True)
            if j == 0:
                m_new, alpha = t_max, None
                p = jnp.exp2(t - m_new)
                l_new = jnp.sum(p, axis=0, keepdims=True)
            else:
                m, l = stats[hh]
                m_new = jnp.maximum(m, t_max)
                alpha = jnp.exp2(m - m_new)
                p = jnp.exp2(t - m_new)
                l_new = alpha * l + jnp.sum(p, axis=0, keepdims=True)
            new_stats.append((m_new, l_new))
            probs.append((p.astype(BF16), alpha))
        return new_stats, probs

    def stage_c(i, j, probs, accs):
        out = []
        for hh in range(2):
            p, alpha = probs[hh]
            pv = _dot(vb[hh * D_F:(hh + 1) * D_F, j * tk:(j + 1) * tk], p)
            out.append(pv if j == 0 else alpha * accs[hh] + pv)
        return out

    scores, probs, stats, accs = {}, {}, None, None
    for s in range(len(pairs) + 2):
        if s < len(pairs):
            scores[s] = stage_a(*pairs[s])
        if 0 <= s - 1 < len(pairs):
            stats, probs[s - 1] = stage_b(*pairs[s - 1], scores.pop(s - 1), stats)
            if pairs[s - 1][1] == pairs[s - 1][0]:
                final_l = [l for _, l in stats]
        if 0 <= s - 2 < len(pairs):
            i, j = pairs[s - 2]
            accs = stage_c(i, j, probs.pop(s - 2), accs)
            if j == i:
                o_t = jnp.concatenate([accs[0] / l_done[0], accs[1] / l_done[1]], axis=0)
                o_ref[i * tq:(i + 1) * tq, :] = o_t.T.astype(BF16)
        if 0 <= s - 1 < len(pairs) and pairs[s - 1][1] == pairs[s - 1][0]:
            l_done = final_l


def _fox_prompt(fqt, fkb, fvt, fcum, n_batch, seq, tq):
    npair = H_F // 2
    fm_spec = pl.BlockSpec((LANES, seq), lambda b, p: (b * npair + p, 0))
    tok_spec = pl.BlockSpec((seq, LANES), lambda b, p: (b, p))
    return pl.pallas_call(
        functools.partial(_fox_kernel, tq=tq),
        grid=(n_batch, npair),
        in_specs=[fm_spec, tok_spec, fm_spec, pl.BlockSpec((seq, 2), lambda b, p: (b * npair + p, 0))],
        out_specs=tok_spec,
        out_shape=jax.ShapeDtypeStruct((n_batch * seq, FOX_W), BF16),
        scratch_shapes=[pltpu.VMEM((2, LANES, seq), BF16), pltpu.VMEM((2, seq, LANES), BF16),
                        pltpu.VMEM((LANES, seq), BF16)],
        compiler_params=_cparams(("parallel", "parallel")),
        name="fox_prompt",
    )(fqt, fkb, fvt, fcum)


def _fox_sample_kernel(q_ref, kct_ref, vct_ref, kn_ref, vn_ref, frc_ref, frn_ref, o_ref):
    n = q_ref.shape[0]
    q2 = q_ref[...]
    lane = lax.broadcasted_iota(jnp.int32, (n, LANES), 1)
    rows = lax.broadcasted_iota(jnp.int32, (n, n), 0)
    cols = lax.broadcasted_iota(jnp.int32, (n, n), 1)
    kct = kct_ref[...]
    vct = vct_ref[...].astype(BF16)
    kn = kn_ref[...].astype(BF16)
    vn = vn_ref[...].astype(BF16)
    outs = []
    for hh in range(2):
        mine = (lane >= hh * D_F) & (lane < (hh + 1) * D_F)
        kc = jnp.where(_head_rows_mask(kct.shape, hh), kct, 0.0).astype(BF16)
        t_c = _dot(q2, kc) - frc_ref[0, hh:hh + 1, :] * LOG2E
        t_n = _dot_nt(jnp.where(mine, q2, jnp.zeros_like(q2)), kn) - frn_ref[0, hh:hh + 1, :] * LOG2E
        t_n = jnp.where(rows >= cols, t_n, -jnp.inf)
        m = jnp.maximum(jnp.max(t_c, axis=-1, keepdims=True), jnp.max(t_n, axis=-1, keepdims=True))
        p_c = jnp.exp2(t_c - m)
        p_n = jnp.exp2(t_n - m)
        l = jnp.sum(p_c, axis=-1, keepdims=True) + jnp.sum(p_n, axis=-1, keepdims=True)
        acc = _dot_nt(p_c.astype(BF16), vct) + _dot(p_n.astype(BF16), vn)
        outs.append(acc / l)
    o_ref[...] = jnp.where(lane < D_F, outs[0], outs[1]).astype(BF16)


def _fox_sample(fq, kct, vct, kn, vn, frow_c, frow_n, n_batch, n_new, past):
    npair = H_F // 2
    tok = pl.BlockSpec((n_new, LANES), lambda b, p: (b, p))
    cache = pl.BlockSpec((LANES, past), lambda b, p: (b * npair + p, 0))
    return pl.pallas_call(
        _fox_sample_kernel,
        grid=(n_batch, npair),
        in_specs=[tok, cache, cache, tok, tok,
                  pl.BlockSpec((1, 2, past), lambda b, p: (b * npair + p, 0, 0)),
                  pl.BlockSpec((1, 2, n_new), lambda b, p: (b * npair + p, 0, 0))],
        out_specs=tok,
        out_shape=jax.ShapeDtypeStruct((n_batch * n_new, FOX_W), BF16),
        compiler_params=_cparams(("parallel", "parallel")),
        name="fox_sample",
    )(fq, kct, vct, kn, vn, frow_c, frow_n)


def _memkv_kernel(m_ref, gin_ref, w_ref, gk_ref, k_ref, v_ref):
    x = m_ref[...]
    h = (x * lax.rsqrt(jnp.mean(x * x, axis=-1, keepdims=True) + EPS) * gin_ref[...]).astype(BF16)
    zk = _dot(h, w_ref[:, :MEM_W])
    for hh in range(H_M):
        sl = slice(hh * D_M, (hh + 1) * D_M)
        k_ref[:, sl] = _lane_rmsnorm(zk[:, sl]) * gk_ref[...]
    v_ref[...] = _dot(h, w_ref[:, MEM_W:])


def _memkv(mem2d, n_batch, g_mem_in, w_mem_kv, g_mem_k):
    blk = lambda w: pl.BlockSpec((N_MEM, w), lambda b: (b, 0))
    return pl.pallas_call(
        _memkv_kernel,
        grid=(n_batch,),
        in_specs=[blk(D_MODEL), _const_spec((1, D_MODEL)), _const_spec((D_MODEL, 2 * MEM_W)),
                  _const_spec((1, D_M))],
        out_specs=[blk(MEM_W), blk(MEM_W)],
        out_shape=[jax.ShapeDtypeStruct((n_batch * N_MEM, MEM_W), F32)] * 2,
        compiler_params=_cparams(("parallel",)),
        name="memkv",
    )(mem2d, g_mem_in, w_mem_kv, g_mem_k)


def _memattn_kernel(q_ref, k_ref, v_ref, o_ref):
    for hh in range(H_M):
        sl = slice(hh * D_M, (hh + 1) * D_M)
        s = _dot_nt(q_ref[:, sl], k_ref[:, sl].astype(BF16)) * (D_M ** -0.5)
        m = jnp.max(s, axis=-1, keepdims=True)
        p = jnp.exp(s - m)
        l = jnp.sum(p, axis=-1, keepdims=True)
        o_ref[:, sl] = (_dot(p.astype(BF16), v_ref[:, sl].astype(BF16)) / l).astype(BF16)


def _memattn(mq, mk, mv, n_batch, seq, tq):
    nq = seq // tq
    return pl.pallas_call(
        _memattn_kernel,
        grid=(n_batch, nq),
        in_specs=[pl.BlockSpec((tq, MEM_W), lambda b, i: (b * nq + i, 0)),
                  pl.BlockSpec((N_MEM, MEM_W), lambda b, i: (b, 0)),
                  pl.BlockSpec((N_MEM, MEM_W), lambda b, i: (b, 0))],
        out_specs=pl.BlockSpec((tq, MEM_W), lambda b, i: (b * nq + i, 0)),
        out_shape=jax.ShapeDtypeStruct((n_batch * seq, MEM_W), BF16),
        compiler_params=_cparams(("parallel", "arbitrary")),
        name="memattn",
    )(mq, mk, mv)


def _split2(x):
    hi = x.astype(BF16)
    return hi, (x - hi.astype(F32)).astype(BF16)


N_EXPERTS = N_GROUPS * E_PER_GROUP
N_SLABS = D_MODEL // LANES


def _router_logits(h2, wrt_hi_ref, wrt_lo_ref, brt_ref):
    h_hi, h_lo = _split2(h2)
    return (_dot(h_hi, wrt_hi_ref[...]) + (_dot(h_hi, wrt_lo_ref[...]) + _dot(h_lo, wrt_hi_ref[...]))
            + brt_ref[...])


def _route(logits, group=None):
    lane = lax.broadcasted_iota(jnp.int32, logits.shape, 1).astype(F32)
    neg = -jnp.inf
    first_idx = lambda mask: jnp.min(jnp.where(mask, lane, float(LANES)), axis=-1, keepdims=True)

    is_g = (lane >= N_EXPERTS) & (lane < N_EXPERTS + N_GROUPS)
    lg = jnp.where(is_g, logits, neg)
    mg = jnp.max(lg, axis=-1, keepdims=True)
    if group is None:
        group = first_idx(lg == mg) - N_EXPERTS
        p_sel = 1.0 / jnp.sum(jnp.exp(lg - mg), axis=-1, keepdims=True)
    else:
        lsel = jnp.max(jnp.where(lane == group + N_EXPERTS, logits, neg), axis=-1, keepdims=True)
        p_sel = jnp.exp(lsel - mg) / jnp.sum(jnp.exp(lg - mg), axis=-1, keepdims=True)

    in_grp = (lane >= group * E_PER_GROUP) & (lane < (group + 1) * E_PER_GROUP)
    le = jnp.where(in_grp, logits, neg)
    v1 = jnp.max(le, axis=-1, keepdims=True)
    i1 = first_idx(le == v1)
    le2 = jnp.where(lane == i1, neg, le)
    v2 = jnp.max(le2, axis=-1, keepdims=True)
    i2 = first_idx(le2 == v2)
    e2 = jnp.exp(v2 - v1)
    w1 = p_sel / (1.0 + e2)
    w2 = p_sel * e2 / (1.0 + e2)
    return group, jnp.where(lane == i1, w1, 0.0) + jnp.where(lane == i2, w2, 0.0)


def _merge_kernel(x_ref, or_ref, of_ref, om_ref, gt_ref, wr_ref, wf_ref, wm_ref, wo_ref, g2_ref,
                  wrt_hi_ref, wrt_lo_ref, brt_ref, x1_ref, h2_ref, route_ref, *, sorted_moe):
    g = lambda b: gt_ref[:, b * D_MODEL:(b + 1) * D_MODEL].astype(F32)
    merged = (g(0) * _dot(or_ref[...], wr_ref[...]) + g(1) * _dot(of_ref[...], wf_ref[...])
              + g(2) * _dot(om_ref[...], wm_ref[...]))
    x1 = x_ref[...] + _dot(merged.astype(BF16), wo_ref[...])
    x1_ref[...] = x1
    h2 = x1 * lax.rsqrt(jnp.mean(x1 * x1, axis=-1, keepdims=True) + EPS) * g2_ref[...]
    group, comb = _route(_router_logits(h2, wrt_hi_ref, wrt_lo_ref, brt_ref))
    if sorted_moe:
        tm = h2.shape[0]
        for s in range(N_SLABS):
            h2_ref[pl.ds(s, tm, stride=N_SLABS), :] = h2[:, s * LANES:(s + 1) * LANES]
        route_ref[...] = jnp.broadcast_to(group, route_ref.shape)
    else:
        h2_ref[...] = h2.astype(BF16)
        for gi in range(N_GROUPS):
            route_ref[gi] = comb[:, gi * E_PER_GROUP:(gi + 1) * E_PER_GROUP]


def _merge(x2d, o_r, o_f, o_m, gates, tm, sorted_moe, wr, wf, wm, wo, g2, wrt_hi, wrt_lo, brt):
    t = x2d.shape[0]
    row = lambda w: pl.BlockSpec((tm, w), lambda i: (i, 0))
    sds = jax.ShapeDtypeStruct
    if sorted_moe:
        h2_spec, h2_shape = pl.BlockSpec((tm * N_SLABS, LANES), lambda i: (i, 0)), sds((t * N_SLABS, LANES), F32)
        rt_spec, rt_shape = row(E_PER_GROUP), sds((t, E_PER_GROUP), F32)
    else:
        h2_spec, h2_shape = row(D_MODEL), sds((t, D_MODEL), BF16)
        rt_spec = pl.BlockSpec((N_GROUPS, tm, E_PER_GROUP), lambda i: (0, i, 0))
        rt_shape = sds((N_GROUPS, t, E_PER_GROUP), F32)
    return pl.pallas_call(
        functools.partial(_merge_kernel, sorted_moe=sorted_moe),
        grid=(t // tm,),
        in_specs=[row(D_MODEL), row(RV_W), row(FOX_W), row(MEM_W), row(GATE_W),
                  _const_spec((RV_W, D_MODEL)), _const_spec((FOX_W, D_MODEL)), _const_spec((MEM_W, D_MODEL)),
                  _const_spec((D_MODEL, D_MODEL)), _const_spec((1, D_MODEL)),
                  _const_spec((D_MODEL, LANES)), _const_spec((D_MODEL, LANES)), _const_spec((1, LANES))],
        out_specs=[row(D_MODEL), h2_spec, rt_spec],
        out_shape=[sds((t, D_MODEL), F32), h2_shape, rt_shape],
        compiler_params=_cparams(("parallel",)),
        name="merge",
    )(x2d, o_r, o_f, o_m, gates, wr, wf, wm, wo, g2, wrt_hi, wrt_lo, brt)


def _group_experts(h, cw, wg_ref, wu_ref, wd_ref, act):
    for e in range(E_PER_GROUP):
        a = _dot(h, wg_ref[0, e])
        u = _dot(h, wu_ref[0, e])
        act[:, e * D_EXPERT:(e + 1) * D_EXPERT] = ((a * jax.nn.sigmoid(a)) * u * cw(e)).astype(BF16)
    return _dot(act[...], wd_ref[0])


def _row_gather(idx_ref, base, src_hbm, dst, sem, n):
    def body(r, _):
        pltpu.make_async_copy(src_hbm.at[idx_ref[base + r]], dst.at[r], sem).start()
        return 0
    lax.fori_loop(0, n, body, 0, unroll=8)


def _row_gather_wait(src_hbm, dst, sem, n):
    pltpu.make_async_copy(src_hbm.at[pl.ds(0, n)], dst, sem).wait()


def _moe_sorted_kernel(tg_ref, nvt_ref, pos_ref, h3_hbm, wrt_hi_ref, wrt_lo_ref, brt_ref,
                       wg_ref, wu_ref, wd_ref, y_ref, xbuf, sem, act, src_ref, *, tm):
    k = pl.program_id(0)
    nvt = nvt_ref[0]
    slot = lax.rem(k, 2)

    @pl.when(k == 0)
    def _():
        def clear(r, _):
            src_ref[r] = 0
            return 0

        def invert(t, _):
            src_ref[pos_ref[t]] = t
            return 0

        lax.fori_loop(0, src_ref.shape[0], clear, 0, unroll=8)
        lax.fori_loop(0, pos_ref.shape[0], invert, 0, unroll=8)

    @pl.when((k == 0) & (nvt > 0))
    def _():
        _row_gather(src_ref, 0, h3_hbm, xbuf.at[0], sem.at[0], tm)

    @pl.when(k < nvt)
    def _():
        _row_gather_wait(h3_hbm, xbuf.at[slot], sem.at[slot], tm)

        @pl.when(k + 1 < nvt)
        def _():
            _row_gather(src_ref, (k + 1) * tm, h3_hbm, xbuf.at[1 - slot], sem.at[1 - slot], tm)

        x = jnp.concatenate([xbuf[slot, :, s, :] for s in range(N_SLABS)], axis=1)
        group = tg_ref[k]
        _, comb = _route(_router_logits(x, wrt_hi_ref, wrt_lo_ref, brt_ref), group.astype(F32))
        lane = lax.broadcasted_iota(jnp.int32, comb.shape, 1)
        cw = lambda e: jnp.sum(jnp.where(lane == group * E_PER_GROUP + e, comb, 0.0), axis=-1, keepdims=True)
        y = _group_experts(x.astype(BF16), cw, wg_ref, wu_ref, wd_ref, act)
        for s in range(N_SLABS):
            y_ref[pl.ds(s, tm, stride=N_SLABS), :] = y[:, s * LANES:(s + 1) * LANES]

    @pl.when(k >= nvt)
    def _():
        y_ref[...] = jnp.zeros(y_ref.shape, F32)


def _moe_sorted(h3, tile_group, n_valid_tiles, pos, n_tiles, tm, wrt_hi, wrt_lo, brt, wg, wu, wd):
    wspec = lambda shape: pl.BlockSpec(shape, lambda k, tg, nv, sr: (tg[k],) + (0,) * (len(shape) - 1))
    cspec = lambda shape: pl.BlockSpec(shape, lambda k, tg, nv, sr: (0,) * len(shape),
                                       pipeline_mode=pl.Buffered(1))
    return pl.pallas_call(
        functools.partial(_moe_sorted_kernel, tm=tm),
        grid_spec=pltpu.PrefetchScalarGridSpec(
            num_scalar_prefetch=3,
            grid=(n_tiles,),
            in_specs=[pl.BlockSpec(memory_space=pl.ANY),
                      cspec((D_MODEL, LANES)), cspec((D_MODEL, LANES)), cspec((1, LANES)),
                      wspec((1, E_PER_GROUP, D_MODEL, D_EXPERT)), wspec((1, E_PER_GROUP, D_MODEL, D_EXPERT)),
                      wspec((1, E_PER_GROUP * D_EXPERT, D_MODEL))],
            out_specs=pl.BlockSpec((tm * N_SLABS, LANES), lambda k, tg, nv, sr: (k, 0)),
            scratch_shapes=[pltpu.VMEM((2, tm, N_SLABS, LANES), F32), pltpu.SemaphoreType.DMA((2,)),
                            pltpu.VMEM((tm, E_PER_GROUP * D_EXPERT), BF16),
                            pltpu.SMEM((n_tiles * tm,), jnp.int32)]),
        out_shape=jax.ShapeDtypeStruct((n_tiles * tm * N_SLABS, LANES), F32),
        compiler_params=_cparams(("arbitrary",)),
        name="moe_sorted",
    )(tile_group, n_valid_tiles, pos, h3, wrt_hi, wrt_lo, brt, wg, wu, wd)


def _combine_kernel(pos_ref, y3_hbm, x1_ref, o_ref, ybuf, sem, *, tm):
    k = pl.program_id(0)
    slot = lax.rem(k, 2)

    @pl.when(k == 0)
    def _():
        _row_gather(pos_ref, 0, y3_hbm, ybuf.at[0], sem.at[0], tm)

    _row_gather_wait(y3_hbm, ybuf.at[slot], sem.at[slot], tm)

    @pl.when(k + 1 < pl.num_programs(0))
    def _():
        _row_gather(pos_ref, (k + 1) * tm, y3_hbm, ybuf.at[1 - slot], sem.at[1 - slot], tm)

    for s in range(N_SLABS):
        sl = slice(s * LANES, (s + 1) * LANES)
        o_ref[:, sl] = x1_ref[:, sl] + ybuf[slot, :, s, :]


def _combine(y3, pos, x1, tm):
    t = x1.shape[0]
    return pl.pallas_call(
        functools.partial(_combine_kernel, tm=tm),
        grid_spec=pltpu.PrefetchScalarGridSpec(
            num_scalar_prefetch=1,
            grid=(t // tm,),
            in_specs=[pl.BlockSpec(memory_space=pl.ANY), pl.BlockSpec((tm, D_MODEL), lambda k, ps: (k, 0))],
            out_specs=pl.BlockSpec((tm, D_MODEL), lambda k, ps: (k, 0)),
            scratch_shapes=[pltpu.VMEM((2, tm, N_SLABS, LANES), F32), pltpu.SemaphoreType.DMA((2,))]),
        out_shape=jax.ShapeDtypeStruct((t, D_MODEL), F32),
        compiler_params=_cparams(("arbitrary",)),
        name="combine",
    )(pos, y3, x1)


def _moe_kernel(h_ref, comb_ref, x1_ref, wg_ref, wu_ref, wd_ref, o_ref, act):
    g = pl.program_id(1)
    comb = comb_ref[0]
    y = _group_experts(h_ref[...], lambda e: comb[:, e:e + 1], wg_ref, wu_ref, wd_ref, act)

    @pl.when(g == 0)
    def _():
        o_ref[...] = x1_ref[...] + y

    @pl.when(g != 0)
    def _():
        o_ref[...] += y


def _moe(h2, comb, x1, tm, wg, wu, wd):
    t = h2.shape[0]
    return pl.pallas_call(
        _moe_kernel,
        grid=(t // tm, N_GROUPS),
        in_specs=[pl.BlockSpec((tm, D_MODEL), lambda i, g: (i, 0)),
                  pl.BlockSpec((1, tm, E_PER_GROUP), lambda i, g: (g, i, 0)),
                  pl.BlockSpec((tm, D_MODEL), lambda i, g: (i, 0)),
                  pl.BlockSpec((1, E_PER_GROUP, D_MODEL, D_EXPERT), lambda i, g: (g, 0, 0, 0)),
                  pl.BlockSpec((1, E_PER_GROUP, D_MODEL, D_EXPERT), lambda i, g: (g, 0, 0, 0)),
                  pl.BlockSpec((1, E_PER_GROUP * D_EXPERT, D_MODEL), lambda i, g: (g, 0, 0))],
        out_specs=pl.BlockSpec((tm, D_MODEL), lambda i, g: (i, 0)),
        out_shape=jax.ShapeDtypeStruct((t, D_MODEL), F32),
        scratch_shapes=[pltpu.VMEM((tm, E_PER_GROUP * D_EXPERT), BF16)],
        compiler_params=_cparams(("parallel", "arbitrary")),
        name="moe",
    )(h2, comb, x1, wg, wu, wd)


def _rope_tables(pos):
    half = DK_R // 2
    inv = ROPE_BASE ** (-jnp.arange(half, dtype=F32) / half)
    ang = pos.astype(F32)[:, None] * inv[None, :]
    c = jnp.cos(ang)
    s = jnp.sin(ang)
    return jnp.concatenate([c, c, c, c], axis=-1), jnp.concatenate([-s, s, -s, s], axis=-1)


def _pair_rows(frows):
    return frows.reshape(frows.shape[0] // 2, 2, frows.shape[1])


def kernel(x_prompt, x_sample, state_ret, cache_fox_k, cache_fox_v, cache_fox_logf, cache_mem_k, cache_mem_v,
           mem_prompt, g_norm1, w_in, b_forget, g_fox_q, g_fox_k, g_mem_q, g_mem_in, w_mem_kv, g_mem_k,
           g_ret_out, w_br_ret, w_br_fox, w_br_mem, w_out, g_norm2, w_route_group, b_route_group,
           w_route_expert, b_route_expert, w_exp_gate, w_exp_up, w_exp_down):
    nb, seq, _ = x_prompt.shape
    nbs, n_new, _ = x_sample.shape
    past = cache_fox_k.shape[2]
    l = 0

    wi = w_in[l]
    o_ff = 2 * RQK_W + 2 * RV_W + 3 * FOX_W
    w_in_r = jnp.concatenate(
        [wi[:, :o_ff], wi[:, o_ff + H_F:], wi[:, o_ff:o_ff + H_F],
         jnp.zeros((D_MODEL, LANES - H_F), F32)], axis=1).astype(BF16)
    bf_pad = jnp.concatenate([b_forget[l], jnp.zeros((LANES - H_F,), F32)])[None, :]
    g1 = g_norm1[l][None, :]
    gfq = jnp.tile(g_fox_q[l], H_F)[None, :]
    gfk = jnp.tile(g_fox_k[l], H_F)[None, :]
    gmq = g_mem_q[l][None, :]
    hid = jnp.arange(FOX_W) // D_F
    bd = jnp.where(hid[:, None] == hid[None, :], 1.0 / D_F, 0.0).astype(BF16)
    wr = w_br_ret[l].astype(BF16)
    wf = w_br_fox[l].astype(BF16)
    wm = w_br_mem[l].astype(BF16)
    wo = w_out[l].astype(BF16)
    g2 = g_norm2[l][None, :]
    n_e = N_GROUPS * E_PER_GROUP
    wrt = jnp.concatenate([w_route_expert[l], w_route_group[l],
                           jnp.zeros((D_MODEL, LANES - n_e - N_GROUPS), F32)], axis=1)
    wrt_hi = wrt.astype(BF16)
    wrt_lo = (wrt - wrt_hi.astype(F32)).astype(BF16)
    brt = jnp.concatenate([b_route_expert[l], b_route_group[l],
                           jnp.zeros((LANES - n_e - N_GROUPS,), F32)])[None, :]
    wg = w_exp_gate[l].astype(BF16)
    wu = w_exp_up[l].astype(BF16)
    wd = w_exp_down[l].astype(BF16).reshape(N_GROUPS, E_PER_GROUP * D_EXPERT, D_MODEL)
    gro = g_ret_out[l][None, :]
    prep_w = (g1, w_in_r, bf_pad, gfq, gfk, gmq, bd)
    merge_w = (wr, wf, wm, wo, g2, wrt_hi, wrt_lo, brt)

    tm = 512
    xp = x_prompt.reshape(nb * seq, D_MODEL)
    (rq, rk, rv, rg, fqt, fkt, fvt, lft, mq, gates, fkb, lf) = _prep(
        xp, _rope_tables(jnp.arange(seq)), tm, seq // tm, True, *prep_w)
    o_r, s_fin = _retention(rq, rk, rv, rg, nb, seq, 256, gro, None)
    fcum = _cumsum_cols(lf.reshape(nb, seq, H_F), 256)
    fcum = jnp.swapaxes(fcum.reshape(nb, seq, H_F // 2, 2), 1, 2).reshape(nb * (H_F // 2) * seq, 2)
    o_f = _fox_prompt(fqt, fkb, fvt, fcum, nb, seq, 256)
    mk, mv = _memkv(mem_prompt.reshape(nb * N_MEM, D_MODEL), nb, g_mem_in[l][None, :],
                    w_mem_kv[l].astype(BF16), g_mem_k[l][None, :])
    o_m = _memattn(mq, mk, mv, nb, seq, tm)
    x1, h3, grp = _merge(xp, o_r, o_f, o_m, gates, tm, True, *merge_w)
    t_p = nb * seq
    n_tiles = t_p // tm + N_GROUPS
    gi = grp[:, 0].astype(jnp.int32)
    onehot = (gi[:, None] == jnp.arange(N_GROUPS, dtype=jnp.int32)[None, :]).astype(jnp.int32)
    csum = jnp.cumsum(onehot, axis=0)
    rank = jnp.sum(csum * onehot, axis=1) - 1
    tiles_g = (csum[-1] + tm - 1) // tm
    tile_end = jnp.cumsum(tiles_g)
    row_start = (tile_end - tiles_g) * tm
    pos = row_start[gi] + rank
    tile_ids = jnp.arange(n_tiles, dtype=jnp.int32)
    tile_group = jnp.minimum(jnp.sum((tile_ids[:, None] >= tile_end[None, :]).astype(jnp.int32), axis=1),
                             N_GROUPS - 1)
    y3 = _moe_sorted(h3.reshape(t_p, N_SLABS, LANES), tile_group, tile_end[-1:], pos, n_tiles, tm,
                     wrt_hi, wrt_lo, brt, wg, wu, wd)
    y_prompt = _combine(y3.reshape(n_tiles * tm, N_SLABS, LANES), pos, x1, tm).reshape(nb, seq, D_MODEL)

    ts = nbs * n_new
    xs = x_sample.reshape(ts, D_MODEL)
    pos_s = jnp.tile(past + jnp.arange(n_new), nbs)
    (rq_s, rk_s, rv_s, rg_s, fq_s, fk_s, fv_s, lf_s, mq_s, gates_s) = _prep(
        xs, _rope_tables(pos_s), ts, 1, False, *prep_w)
    o_r_s, s_new = _retention(rq_s, rk_s, rv_s, rg_s, nbs, n_new, n_new, gro, state_ret[l])
    pad = (-(past + n_new)) % 256
    lf_rows = jnp.concatenate([jnp.swapaxes(cache_fox_logf[l], 1, 2),
                               jnp.swapaxes(lf_s.reshape(nbs, n_new, H_F), 1, 2),
                               jnp.zeros((nbs, H_F, pad), F32)], axis=2).reshape(nbs * H_F, past + n_new + pad)
    f_all = _cumsum(lf_rows, 256)
    feat_major = lambda c: jnp.transpose(c, (0, 2, 3, 1)).reshape(nbs * FOX_W, past)
    o_f_s = _fox_sample(fq_s, feat_major(cache_fox_k[l]), feat_major(cache_fox_v[l]), fk_s, fv_s,
                        _pair_rows(f_all[:, :past]), _pair_rows(f_all[:, past:past + n_new]),
                        nbs, n_new, past)
    o_m_s = _memattn(mq_s, cache_mem_k[l].reshape(nbs * N_MEM, MEM_W),
                     cache_mem_v[l].reshape(nbs * N_MEM, MEM_W), nbs, n_new, n_new)
    x1_s, h2_s, comb_s = _merge(xs, o_r_s, o_f_s, o_m_s, gates_s, ts, False, *merge_w)
    y_sample = _moe(h2_s, comb_s, x1_s, ts, wg, wu, wd).reshape(nbs, n_new, D_MODEL)

    token_major = lambda a: jnp.transpose(a.reshape(nb, H_F, D_F, seq), (0, 3, 1, 2))

    return (y_prompt, y_sample,
            s_fin[None], token_major(fkt)[None], token_major(fvt)[None],
            jnp.swapaxes(lft.reshape(nb, H_F, seq), 1, 2)[None],
            mk.reshape(1, nb, N_MEM, H_M, D_M), mv.reshape(1, nb, N_MEM, H_M, D_M),
            s_new[None], fk_s.reshape(1, nbs, n_new, H_F, D_F), fv_s.reshape(1, nbs, n_new, H_F, D_F),
            lf_s.reshape(1, nbs, n_new, H_F))
```

```python
import functools

import jax
import jax.numpy as jnp
from jax import lax
from jax.experimental import pallas as pl
from jax.experimental.pallas import tpu as pltpu

F32 = jnp.float32
BF16 = jnp.bfloat16

D_MODEL = 1024
H_R, DK_R, DV_R = 4, 64, 128
H_F, D_F = 8, 64
H_M, D_M = 4, 128
N_MEM = 256
N_GROUPS, E_PER_GROUP, D_EXPERT = 4, 8, 256
ROPE_BASE = 10000.0
EPS = 1e-6
LOG2E = 1.4426950408889634

RQK_W = H_R * DK_R
RV_W = H_R * DV_R
FOX_W = H_F * D_F
MEM_W = H_M * D_M
GATE_W = 3 * D_MODEL
LANES = 128

C_RQ = 0
C_RK = C_RQ + RQK_W
C_RV = C_RK + RQK_W
C_RG = C_RV + RV_W
C_FQ = C_RG + RV_W
C_FK = C_FQ + FOX_W
C_FV = C_FK + FOX_W
C_MQ = C_FV + FOX_W
C_GT = C_MQ + MEM_W
C_FF = C_GT + GATE_W
N_IN_PAD = C_FF + LANES

VMEM_LIMIT = 56 * 1024 * 1024


def _cparams(sem):
    return pltpu.CompilerParams(dimension_semantics=sem, vmem_limit_bytes=VMEM_LIMIT)


def _const_spec(shape):
    nd = len(shape)
    return pl.BlockSpec(shape, lambda *_: (0,) * nd, pipeline_mode=pl.Buffered(1))


def _dot(a, b):
    return jnp.dot(a, b, preferred_element_type=F32)


def _dot_nt(a, b):
    return lax.dot_general(a, b, (((1,), (1,)), ((), ())), preferred_element_type=F32)


def _dot_tn(a, b):
    return lax.dot_general(a, b, (((0,), (0,)), ((), ())), preferred_element_type=F32)


def _lane_rmsnorm(z):
    return z * lax.rsqrt(jnp.mean(z * z, axis=-1, keepdims=True) + EPS)


def _prep_kernel(x_ref, g1_ref, w_ref, cos_ref, sin_ref, bf_ref, gfq_ref, gfk_ref, gmq_ref, bd_ref,
                 rq_ref, rk_ref, rv_ref, rg_ref, fq_ref, fk_ref, fv_ref, lf_ref, mq_ref, gt_ref, *extra,
                 seq_minor):
    x = x_ref[...]
    h = (x * lax.rsqrt(jnp.mean(x * x, axis=-1, keepdims=True) + EPS) * g1_ref[...]).astype(BF16)

    def proj(c0, width):
        return _dot(h, w_ref[:, c0:c0 + width])

    cos = cos_ref[...]
    sin = sin_ref[...]
    lane = lax.broadcasted_iota(jnp.int32, cos.shape, 1)
    first_half = (lane % DK_R) < (DK_R // 2)

    def rope(z):
        swapped = jnp.where(first_half, pltpu.roll(z, LANES - DK_R // 2, 1), pltpu.roll(z, DK_R // 2, 1))
        return z * cos + swapped * sin

    zq = proj(C_RQ, RQK_W)
    zk = proj(C_RK, RQK_W)
    for p in range(RQK_W // LANES):
        sl = slice(p * LANES, (p + 1) * LANES)
        rq_ref[:, sl] = rope(zq[:, sl])
        rk_ref[:, sl] = rope(zk[:, sl]) * (DK_R ** -0.5)
    rv_ref[...] = proj(C_RV, RV_W)
    rg_ref[...] = proj(C_RG, RV_W)

    def head64_norm(z, g_ref):
        ms = _dot((z * z).astype(BF16), bd_ref[...])
        return z * lax.rsqrt(ms + EPS) * g_ref[...]

    fq = head64_norm(proj(C_FQ, FOX_W), gfq_ref) * (D_F ** -0.5 * LOG2E)
    fk = head64_norm(proj(C_FK, FOX_W), gfk_ref)
    fv = proj(C_FV, FOX_W)
    if seq_minor:
        fq_ref[...] = fq.T.astype(BF16)
        fk_ref[...] = fk.T
        fv_ref[...] = fv.T
        extra[0][...] = fk.astype(BF16)
    else:
        fq_ref[...] = fq.astype(BF16)
        fk_ref[...] = fk
        fv_ref[...] = fv

    zm = proj(C_MQ, MEM_W)
    for hh in range(H_M):
        sl = slice(hh * D_M, (hh + 1) * D_M)
        mq_ref[:, sl] = (_lane_rmsnorm(zm[:, sl]) * gmq_ref[...]).astype(BF16)

    for b in range(3):
        gt_ref[:, b * D_MODEL:(b + 1) * D_MODEL] = jax.nn.sigmoid(proj(C_GT + b * D_MODEL, D_MODEL)).astype(BF16)

    v = proj(C_FF, LANES) + bf_ref[...]
    lf = jnp.minimum(v, 0.0) - jnp.log1p(jnp.exp(-jnp.abs(v)))
    if seq_minor:
        lf_ref[...] = lf.T[:H_F, :]
        extra[1][...] = lf[:, :H_F]
    else:
        lf_ref[...] = lf[:, :H_F]


def _prep(x2d, tables, tm, n_pos_tiles, seq_minor, g1, w_in_r, bf_pad, gfq, gfk, gmq, bd):
    t = x2d.shape[0]
    cos_t, sin_t = tables
    row = lambda w: pl.BlockSpec((tm, w), lambda i: (i, 0))
    pos = pl.BlockSpec((tm, LANES), lambda i: (i % n_pos_tiles, 0))
    sds = jax.ShapeDtypeStruct
    if seq_minor:
        nb, seq = t // (n_pos_tiles * tm), n_pos_tiles * tm
        fm = lambda rows_: pl.BlockSpec((rows_, tm), lambda i: (i // n_pos_tiles, i % n_pos_tiles))
        fox_shapes = [sds((nb * FOX_W, seq), BF16), sds((nb * FOX_W, seq), F32), sds((nb * FOX_W, seq), F32),
                      sds((nb * H_F, seq), F32)]
        fox_specs = [fm(FOX_W), fm(FOX_W), fm(FOX_W), fm(H_F)]
        extra_shapes, extra_specs = [sds((t, FOX_W), BF16), sds((t, H_F), F32)], [row(FOX_W), row(H_F)]
    else:
        fox_shapes = [sds((t, FOX_W), BF16), sds((t, FOX_W), F32), sds((t, FOX_W), F32), sds((t, H_F), F32)]
        fox_specs = [row(FOX_W), row(FOX_W), row(FOX_W), row(H_F)]
        extra_shapes, extra_specs = [], []
    out_shapes = [sds((t, RQK_W), F32), sds((t, RQK_W), F32), sds((t, RV_W), F32), sds((t, RV_W), F32),
                  *fox_shapes, sds((t, MEM_W), BF16), sds((t, GATE_W), BF16), *extra_shapes]
    out_specs = [row(RQK_W), row(RQK_W), row(RV_W), row(RV_W), *fox_specs, row(MEM_W), row(GATE_W),
                 *extra_specs]
    return pl.pallas_call(
        functools.partial(_prep_kernel, seq_minor=seq_minor),
        grid=(t // tm,),
        in_specs=[row(D_MODEL), _const_spec((1, D_MODEL)), _const_spec((D_MODEL, N_IN_PAD)), pos, pos,
                  _const_spec((1, LANES)), _const_spec((1, FOX_W)), _const_spec((1, FOX_W)),
                  _const_spec((1, D_M)), _const_spec((FOX_W, FOX_W))],
        out_specs=out_specs,
        out_shape=out_shapes,
        compiler_params=_cparams(("parallel",)),
        name="prep",
    )(x2d, g1, w_in_r, cos_t, sin_t, bf_pad, gfq, gfk, gmq, bd)


def _ret_kernel(*refs, has_init):
    if has_init:
        (rq_ref, rk_ref, rv_ref, rg_ref, dmat_ref, qdec_ref, kdec_ref, gpow_ref, gro_ref, s0_ref,
         o_ref, sfin_ref, state) = refs
    else:
        (rq_ref, rk_ref, rv_ref, rg_ref, dmat_ref, qdec_ref, kdec_ref, gpow_ref, gro_ref,
         o_ref, sfin_ref, state) = refs
    c = pl.program_id(1)

    @pl.when(c == 0)
    def _():
        state[...] = jnp.zeros(state.shape, F32)
        if has_init:
            for h in range(H_R):
                r0 = DK_R * (h % 2)
                state[h, r0:r0 + DK_R, :] = s0_ref[0, h]

    lane = lax.broadcasted_iota(jnp.int32, (rq_ref.shape[0], LANES), 1)
    for p in range(H_R // 2):
        sl = slice(p * LANES, (p + 1) * LANES)
        q2 = rq_ref[:, sl]
        k2 = rk_ref[:, sl]
        kd2 = k2 * kdec_ref[p]
        for hh in range(2):
            h = 2 * p + hh
            mine = (lane >= hh * DK_R) & (lane < (hh + 1) * DK_R)
            qm = jnp.where(mine, q2, 0.0).astype(BF16)
            kdm = jnp.where(mine, kd2, 0.0).astype(BF16)
            v = rv_ref[:, h * DV_R:(h + 1) * DV_R].astype(BF16)
            sc = _dot_nt(qm, k2.astype(BF16)) * dmat_ref[h]
            s_old = state[h]
            o = _dot(sc.astype(BF16), v) + _dot(qm, s_old.astype(BF16)) * qdec_ref[h]
            state[h] = gpow_ref[h] * s_old + _dot_tn(kdm, v)
            normed = _lane_rmsnorm(o) * gro_ref[...]
            rg = rg_ref[:, h * DV_R:(h + 1) * DV_R]
            o_ref[:, h * DV_R:(h + 1) * DV_R] = (normed * (rg * jax.nn.sigmoid(rg))).astype(BF16)

    @pl.when(c == pl.num_programs(1) - 1)
    def _():
        for h in range(H_R):
            r0 = DK_R * (h % 2)
            sfin_ref[0, h] = state[h, r0:r0 + DK_R, :]


def _retention(rq, rk, rv, rg, n_batch, seq, ch, g_ret_out, state0):
    nc = seq // ch
    lg = jnp.log1p(-jnp.exp2(-5.0 - jnp.arange(H_R, dtype=F32)))
    idx = jnp.arange(ch, dtype=F32)
    diff = idx[:, None] - idx[None, :]
    causal = diff >= 0
    dmat = jnp.where(causal[None], jnp.exp(jnp.where(causal, diff, 0.0)[None] * lg[:, None, None]), 0.0)
    q_dec = jnp.exp((idx + 1.0)[None, :] * lg[:, None])
    k_dec = jnp.exp((ch - 1.0 - idx)[None, :] * lg[:, None])
    qdec = jnp.broadcast_to(q_dec[:, :, None], (H_R, ch, DV_R))
    kdec = jnp.broadcast_to(k_dec[:, :, None], (H_R, ch, DK_R))
    kdec = kdec.reshape(H_R // 2, 2, ch, DK_R).transpose(0, 2, 1, 3).reshape(H_R // 2, ch, LANES)
    gpow = jnp.broadcast_to(jnp.exp(ch * lg)[:, None, None], (H_R, 1, DV_R))

    has_init = state0 is not None
    blk = lambda w: pl.BlockSpec((ch, w), lambda b, c: (b * nc + c, 0))
    in_specs = [blk(RQK_W), blk(RQK_W), blk(RV_W), blk(RV_W),
                _const_spec((H_R, ch, ch)), _const_spec((H_R, ch, DV_R)),
                _const_spec((H_R // 2, ch, LANES)), _const_spec((H_R, 1, DV_R)), _const_spec((1, DV_R))]
    args = [rq, rk, rv, rg, dmat, qdec, kdec, gpow, g_ret_out]
    if has_init:
        in_specs.append(pl.BlockSpec((1, H_R, DK_R, DV_R), lambda b, c: (b, 0, 0, 0)))
        args.append(state0)
    return pl.pallas_call(
        functools.partial(_ret_kernel, has_init=has_init),
        grid=(n_batch, nc),
        in_specs=in_specs,
        out_specs=[blk(RV_W), pl.BlockSpec((1, H_R, DK_R, DV_R), lambda b, c: (b, 0, 0, 0))],
        out_shape=[jax.ShapeDtypeStruct((n_batch * seq, RV_W), BF16),
                   jax.ShapeDtypeStruct((n_batch, H_R, DK_R, DV_R), F32)],
        scratch_shapes=[pltpu.VMEM((H_R, LANES, DV_R), F32)],
        compiler_params=_cparams(("parallel", "arbitrary")),
        name="retention",
    )(*args)


def _split3(x):
    hi = x.astype(BF16)
    r1 = x - hi.astype(F32)
    mid = r1.astype(BF16)
    lo = (r1 - mid.astype(F32)).astype(BF16)
    return hi, mid, lo


def _cumsum_kernel(x_ref, o_ref, *, blk):
    rows, n = x_ref.shape
    r = lax.broadcasted_iota(jnp.int32, (blk, blk), 0)
    c = lax.broadcasted_iota(jnp.int32, (blk, blk), 1)
    tri = jnp.where(r <= c, 1.0, 0.0).astype(BF16)
    carry = jnp.zeros((rows, 1), F32)
    for i in range(n // blk):
        hi, mid, lo = _split3(x_ref[:, i * blk:(i + 1) * blk])
        cum = (_dot(hi, tri) + _dot(mid, tri)) + _dot(lo, tri) + carry
        o_ref[:, i * blk:(i + 1) * blk] = cum
        carry = cum[:, blk - 1:blk]


def _cumsum_cols_kernel(x_ref, o_ref, *, blk):
    n = x_ref.shape[1]
    r = lax.broadcasted_iota(jnp.int32, (blk, blk), 0)
    c = lax.broadcasted_iota(jnp.int32, (blk, blk), 1)
    tri = jnp.where(r >= c, 1.0, 0.0).astype(BF16)
    carry = jnp.zeros((1, x_ref.shape[2]), F32)
    for i in range(n // blk):
        hi, mid, lo = _split3(x_ref[0, i * blk:(i + 1) * blk, :])
        cum = (_dot(tri, hi) + _dot(tri, mid)) + _dot(tri, lo) + carry
        o_ref[0, i * blk:(i + 1) * blk, :] = cum
        carry = cum[blk - 1:blk, :]


def _cumsum_cols(x, blk):
    nb, t, w = x.shape
    spec = pl.BlockSpec((1, t, w), lambda b: (b, 0, 0))
    return pl.pallas_call(
        functools.partial(_cumsum_cols_kernel, blk=blk),
        grid=(nb,),
        in_specs=[spec], out_specs=spec,
        out_shape=jax.ShapeDtypeStruct(x.shape, F32),
        compiler_params=_cparams(("parallel",)),
        name="cumsum_cols",
    )(x)


def _cumsum(x, blk):
    r, t = x.shape
    spec = pl.BlockSpec((H_F, t), lambda b: (b, 0))
    return pl.pallas_call(
        functools.partial(_cumsum_kernel, blk=blk),
        grid=(r // H_F,),
        in_specs=[spec], out_specs=spec,
        out_shape=jax.ShapeDtypeStruct(x.shape, F32),
        compiler_params=_cparams(("parallel",)),
        name="cumsum",
    )(x)


def _head_rows_mask(shape, hh):
    sub = lax.broadcasted_iota(jnp.int32, shape, 0)
    return (sub >= hh * D_F) & (sub < (hh + 1) * D_F)


N_AUG = 3


def _fox_kernel(qt_ref, k_ref, vt_ref, fcum_ref, o_ref, qa, ka, vb, *, tq):
    seq = k_ref.shape[0]
    tk = tq
    qt = qt_ref[...].astype(F32)
    k = k_ref[...].astype(F32)
    sub = lax.broadcasted_iota(jnp.int32, qt.shape, 0)
    lane = lax.broadcasted_iota(jnp.int32, k.shape, 1)
    for hh in range(2):
        own, oth = hh * D_F, (1 - hh) * D_F
        qa[hh] = jnp.where((sub >= own) & (sub < own + D_F), qt,
                           jnp.where((sub >= oth) & (sub < oth + N_AUG), 1.0, 0.0)).astype(BF16)
        bias = fcum_ref[:, hh:hh + 1] * (-LOG2E)
        kk = jnp.where((lane >= own) & (lane < own + D_F), k, 0.0)
        for a, piece in enumerate(_split3(bias)):
            kk = jnp.where(lane == oth + a, piece.astype(F32), kk)
        ka[hh] = kk.astype(BF16)
    vb[...] = vt_ref[...].astype(BF16)
    causal = (lax.broadcasted_iota(jnp.int32, (tk, tq), 1) >= lax.broadcasted_iota(jnp.int32, (tk, tq), 0))

    pairs = [(i, j) for i in range(seq // tq) for j in range(i + 1)]

    def stage_a(i, j):
        return tuple(_dot(ka[hh, j * tk:(j + 1) * tk, :], qa[hh, :, i * tq:(i + 1) * tq]) for hh in range(2))

    def stage_b(i, j, ts, stats):
        new_stats, probs = [], []
        for hh in range(2):
            t = jnp.where(causal, ts[hh], -jnp.inf) if j == i else ts[hh]
            t_max = jnp.max(t, axis=0, keepdims=True)
            if j == 0:
                m_new, alpha = t_max, None
                p = jnp.exp2(t - m_new)
                l_new = jnp.sum(p, axis=0, keepdims=True)
            else:
                m, l = stats[hh]
                m_new = jnp.maximum(m, t_max)
                alpha = jnp.exp2(m - m_new)
                p = jnp.exp2(t - m_new)
                l_new = alpha * l + jnp.sum(p, axis=0, keepdims=True)
            new_stats.append((m_new, l_new))
            probs.append((p.astype(BF16), alpha))
        return new_stats, probs

    def stage_c(i, j, probs, accs):
        out = []
        for hh in range(2):
            p, alpha = probs[hh]
            pv = _dot(vb[hh * D_F:(hh + 1) * D_F, j * tk:(j + 1) * tk], p)
            out.append(pv if j == 0 else alpha * accs[hh] + pv)
        return out

    scores, probs, stats, accs = {}, {}, None, None
    for s in range(len(pairs) + 2):
        if s < len(pairs):
            scores[s] = stage_a(*pairs[s])
        if 0 <= s - 1 < len(pairs):
            stats, probs[s - 1] = stage_b(*pairs[s - 1], scores.pop(s - 1), stats)
            if pairs[s - 1][1] == pairs[s - 1][0]:
                final_l = [l for _, l in stats]
        if 0 <= s - 2 < len(pairs):
            i, j = pairs[s - 2]
            accs = stage_c(i, j, probs.pop(s - 2), accs)
            if j == i:
                o_t = jnp.concatenate([accs[0] / l_done[0], accs[1] / l_done[1]], axis=0)
                o_ref[i * tq:(i + 1) * tq, :] = o_t.T.astype(BF16)
        if 0 <= s - 1 < len(pairs) and pairs[s - 1][1] == pairs[s - 1][0]:
            l_done = final_l


def _fox_prompt(fqt, fkb, fvt, fcum, n_batch, seq, tq):
    npair = H_F // 2
    fm_spec = pl.BlockSpec((LANES, seq), lambda b, p: (b * npair + p, 0))
    tok_spec = pl.BlockSpec((seq, LANES), lambda b, p: (b, p))
    return pl.pallas_call(
        functools.partial(_fox_kernel, tq=tq),
        grid=(n_batch, npair),
        in_specs=[fm_spec, tok_spec, fm_spec, pl.BlockSpec((seq, 2), lambda b, p: (b * npair + p, 0))],
        out_specs=tok_spec,
        out_shape=jax.ShapeDtypeStruct((n_batch * seq, FOX_W), BF16),
        scratch_shapes=[pltpu.VMEM((2, LANES, seq), BF16), pltpu.VMEM((2, seq, LANES), BF16),
                        pltpu.VMEM((LANES, seq), BF16)],
        compiler_params=_cparams(("parallel", "parallel")),
        name="fox_prompt",
    )(fqt, fkb, fvt, fcum)


def _fox_sample_kernel(q_ref, kct_ref, vct_ref, kn_ref, vn_ref, frc_ref, frn_ref, o_ref):
    n = q_ref.shape[0]
    q2 = q_ref[...]
    lane = lax.broadcasted_iota(jnp.int32, (n, LANES), 1)
    rows = lax.broadcasted_iota(jnp.int32, (n, n), 0)
    cols = lax.broadcasted_iota(jnp.int32, (n, n), 1)
    kct = kct_ref[...]
    vct = vct_ref[...].astype(BF16)
    kn = kn_ref[...].astype(BF16)
    vn = vn_ref[...].astype(BF16)
    outs = []
    for hh in range(2):
        mine = (lane >= hh * D_F) & (lane < (hh + 1) * D_F)
        kc = jnp.where(_head_rows_mask(kct.shape, hh), kct, 0.0).astype(BF16)
        t_c = _dot(q2, kc) - frc_ref[0, hh:hh + 1, :] * LOG2E
        t_n = _dot_nt(jnp.where(mine, q2, jnp.zeros_like(q2)), kn) - frn_ref[0, hh:hh + 1, :] * LOG2E
        t_n = jnp.where(rows >= cols, t_n, -jnp.inf)
        m = jnp.maximum(jnp.max(t_c, axis=-1, keepdims=True), jnp.max(t_n, axis=-1, keepdims=True))
        p_c = jnp.exp2(t_c - m)
        p_n = jnp.exp2(t_n - m)
        l = jnp.sum(p_c, axis=-1, keepdims=True) + jnp.sum(p_n, axis=-1, keepdims=True)
        acc = _dot_nt(p_c.astype(BF16), vct) + _dot(p_n.astype(BF16), vn)
        outs.append(acc / l)
    o_ref[...] = jnp.where(lane < D_F, outs[0], outs[1]).astype(BF16)


def _fox_sample(fq, kct, vct, kn, vn, frow_c, frow_n, n_batch, n_new, past):
    npair = H_F // 2
    tok = pl.BlockSpec((n_new, LANES), lambda b, p: (b, p))
    cache = pl.BlockSpec((LANES, past), lambda b, p: (b * npair + p, 0))
    return pl.pallas_call(
        _fox_sample_kernel,
        grid=(n_batch, npair),
        in_specs=[tok, cache, cache, tok, tok,
                  pl.BlockSpec((1, 2, past), lambda b, p: (b * npair + p, 0, 0)),
                  pl.BlockSpec((1, 2, n_new), lambda b, p: (b * npair + p, 0, 0))],
        out_specs=tok,
        out_shape=jax.ShapeDtypeStruct((n_batch * n_new, FOX_W), BF16),
        compiler_params=_cparams(("parallel", "parallel")),
        name="fox_sample",
    )(fq, kct, vct, kn, vn, frow_c, frow_n)


def _memkv_kernel(m_ref, gin_ref, w_ref, gk_ref, k_ref, v_ref):
    x = m_ref[...]
    h = (x * lax.rsqrt(jnp.mean(x * x, axis=-1, keepdims=True) + EPS) * gin_ref[...]).astype(BF16)
    zk = _dot(h, w_ref[:, :MEM_W])
    for hh in range(H_M):
        sl = slice(hh * D_M, (hh + 1) * D_M)
        k_ref[:, sl] = _lane_rmsnorm(zk[:, sl]) * gk_ref[...]
    v_ref[...] = _dot(h, w_ref[:, MEM_W:])


def _memkv(mem2d, n_batch, g_mem_in, w_mem_kv, g_mem_k):
    blk = lambda w: pl.BlockSpec((N_MEM, w), lambda b: (b, 0))
    return pl.pallas_call(
        _memkv_kernel,
        grid=(n_batch,),
        in_specs=[blk(D_MODEL), _const_spec((1, D_MODEL)), _const_spec((D_MODEL, 2 * MEM_W)),
                  _const_spec((1, D_M))],
        out_specs=[blk(MEM_W), blk(MEM_W)],
        out_shape=[jax.ShapeDtypeStruct((n_batch * N_MEM, MEM_W), F32)] * 2,
        compiler_params=_cparams(("parallel",)),
        name="memkv",
    )(mem2d, g_mem_in, w_mem_kv, g_mem_k)


def _memattn_kernel(q_ref, k_ref, v_ref, o_ref):
    for hh in range(H_M):
        sl = slice(hh * D_M, (hh + 1) * D_M)
        s = _dot_nt(q_ref[:, sl], k_ref[:, sl].astype(BF16)) * (D_M ** -0.5)
        m = jnp.max(s, axis=-1, keepdims=True)
        p = jnp.exp(s - m)
        l = jnp.sum(p, axis=-1, keepdims=True)
        o_ref[:, sl] = (_dot(p.astype(BF16), v_ref[:, sl].astype(BF16)) / l).astype(BF16)


def _memattn(mq, mk, mv, n_batch, seq, tq):
    nq = seq // tq
    return pl.pallas_call(
        _memattn_kernel,
        grid=(n_batch, nq),
        in_specs=[pl.BlockSpec((tq, MEM_W), lambda b, i: (b * nq + i, 0)),
                  pl.BlockSpec((N_MEM, MEM_W), lambda b, i: (b, 0)),
                  pl.BlockSpec((N_MEM, MEM_W), lambda b, i: (b, 0))],
        out_specs=pl.BlockSpec((tq, MEM_W), lambda b, i: (b * nq + i, 0)),
        out_shape=jax.ShapeDtypeStruct((n_batch * seq, MEM_W), BF16),
        compiler_params=_cparams(("parallel", "arbitrary")),
        name="memattn",
    )(mq, mk, mv)


def _split2(x):
    hi = x.astype(BF16)
    return hi, (x - hi.astype(F32)).astype(BF16)


N_EXPERTS = N_GROUPS * E_PER_GROUP


N_PAIRS = E_PER_GROUP * (E_PER_GROUP - 1) // 2


def _router_logits(h2, wrt_ref, brt_ref):
    h_hi, h_lo = _split2(h2)
    z = _dot(h_hi, wrt_ref[...])
    return (z[:, :LANES] + z[:, LANES:]) + _dot(h_lo, wrt_ref[:, :LANES]) + brt_ref[...]


def _route(logits, group=None):
    lane = lax.broadcasted_iota(jnp.int32, logits.shape, 1).astype(F32)
    neg = -jnp.inf
    first_idx = lambda mask: jnp.min(jnp.where(mask, lane, float(LANES)), axis=-1, keepdims=True)

    is_g = (lane >= N_EXPERTS) & (lane < N_EXPERTS + N_GROUPS)
    lg = jnp.where(is_g, logits, neg)
    mg = jnp.max(lg, axis=-1, keepdims=True)
    if group is None:
        group = first_idx(lg == mg) - N_EXPERTS
        p_sel = 1.0 / jnp.sum(jnp.exp(lg - mg), axis=-1, keepdims=True)
    else:
        lsel = jnp.max(jnp.where(lane == group + N_EXPERTS, logits, neg), axis=-1, keepdims=True)
        p_sel = jnp.exp(lsel - mg) / jnp.sum(jnp.exp(lg - mg), axis=-1, keepdims=True)

    in_grp = (lane >= group * E_PER_GROUP) & (lane < (group + 1) * E_PER_GROUP)
    le = jnp.where(in_grp, logits, neg)
    v1 = jnp.max(le, axis=-1, keepdims=True)
    i1 = first_idx(le == v1)
    le2 = jnp.where(lane == i1, neg, le)
    v2 = jnp.max(le2, axis=-1, keepdims=True)
    i2 = first_idx(le2 == v2)
    e2 = jnp.exp(v2 - v1)
    w1 = p_sel / (1.0 + e2)
    w2 = p_sel * e2 / (1.0 + e2)
    return group, jnp.where(lane == i1, w1, 0.0) + jnp.where(lane == i2, w2, 0.0), i1, i2


def _merge_kernel(x_ref, or_ref, of_ref, om_ref, gt_ref, wr_ref, wf_ref, wm_ref, wo_ref, g2_ref,
                  wrt_ref, brt_ref, x1_ref, h2_ref, route_ref, *counts_ref, sorted_moe):
    g = lambda b: gt_ref[:, b * D_MODEL:(b + 1) * D_MODEL].astype(F32)
    merged = (g(0) * _dot(or_ref[...], wr_ref[...]) + g(1) * _dot(of_ref[...], wf_ref[...])
              + g(2) * _dot(om_ref[...], wm_ref[...]))
    x1 = x_ref[...] + _dot(merged.astype(BF16), wo_ref[...])
    x1_ref[...] = x1
    h2 = x1 * lax.rsqrt(jnp.mean(x1 * x1, axis=-1, keepdims=True) + EPS) * g2_ref[...]
    group, comb, i1, i2 = _route(_router_logits(h2, wrt_ref, brt_ref))
    if sorted_moe:
        tm = h2.shape[0]
        _to_token_tiles(h2_ref, h2)
        e_lo = jnp.minimum(i1, i2) - group * E_PER_GROUP
        e_hi = jnp.maximum(i1, i2) - group * E_PER_GROUP
        cls = group * N_PAIRS + (e_lo * E_PER_GROUP - e_lo * (e_lo + 1.0) * 0.5 + (e_hi - e_lo - 1.0))
        lane = lax.broadcasted_iota(jnp.int32, (tm, LANES), 1).astype(F32)
        onehot = jnp.where(lane == cls, 1.0, 0.0)
        r = lax.broadcasted_iota(jnp.int32, (tm, tm), 0)
        c = lax.broadcasted_iota(jnp.int32, (tm, tm), 1)
        before = _dot(jnp.where(r > c, 1.0, 0.0).astype(BF16), onehot.astype(BF16))
        rank = jnp.sum(before * onehot, axis=-1, keepdims=True)
        col = lax.broadcasted_iota(jnp.int32, route_ref.shape, 1)
        route_ref[...] = jnp.where(col == 0, group, jnp.where(col == 1, cls, jnp.where(col == 2, rank, 0.0)))
        counts_ref[0][...] = jnp.broadcast_to(jnp.sum(onehot, axis=0, keepdims=True), counts_ref[0].shape)
    else:
        h2_ref[...] = h2.astype(BF16)
        for gi in range(N_GROUPS):
            route_ref[gi] = comb[:, gi * E_PER_GROUP:(gi + 1) * E_PER_GROUP]


def _merge(x2d, o_r, o_f, o_m, gates, tm, sorted_moe, wr, wf, wm, wo, g2, wrt, brt):
    t = x2d.shape[0]
    row = lambda w: pl.BlockSpec((tm, w), lambda i: (i, 0))
    sds = jax.ShapeDtypeStruct
    if sorted_moe:
        h2_spec, h2_shape = pl.BlockSpec((tm * N_SLABS, LANES), lambda i: (i, 0)), sds((t * N_SLABS, LANES), F32)
        rt_specs = [row(E_PER_GROUP), pl.BlockSpec((1, 8, LANES), lambda i: (i, 0, 0))]
        rt_shapes = [sds((t, E_PER_GROUP), F32), sds((t // tm, 8, LANES), F32)]
    else:
        h2_spec, h2_shape = row(D_MODEL), sds((t, D_MODEL), BF16)
        rt_specs = [pl.BlockSpec((N_GROUPS, tm, E_PER_GROUP), lambda i: (0, i, 0))]
        rt_shapes = [sds((N_GROUPS, t, E_PER_GROUP), F32)]
    return pl.pallas_call(
        functools.partial(_merge_kernel, sorted_moe=sorted_moe),
        grid=(t // tm,),
        in_specs=[row(D_MODEL), row(RV_W), row(FOX_W), row(MEM_W), row(GATE_W),
                  _const_spec((RV_W, D_MODEL)), _const_spec((FOX_W, D_MODEL)), _const_spec((MEM_W, D_MODEL)),
                  _const_spec((D_MODEL, D_MODEL)), _const_spec((1, D_MODEL)),
                  _const_spec((D_MODEL, 2 * LANES)), _const_spec((1, LANES))],
        out_specs=[row(D_MODEL), h2_spec, *rt_specs],
        out_shape=[sds((t, D_MODEL), F32), h2_shape, *rt_shapes],
        compiler_params=_cparams(("parallel",)),
        name="merge",
    )(x2d, o_r, o_f, o_m, gates, wr, wf, wm, wo, g2, wrt, brt)


def _positions_kernel(route_ref, first_ref, pos_ref):
    route = route_ref[...]
    lane = lax.broadcasted_iota(jnp.int32, (route.shape[0], LANES), 1).astype(F32)
    first = jnp.sum(jnp.where(lane == route[:, 1:2], first_ref[0, 0:1, :], 0.0), axis=-1, keepdims=True)
    pos_ref[...] = jnp.broadcast_to(first + route[:, 2:3], pos_ref.shape).astype(jnp.int32)


def _positions(route, first_row, tm):
    t = route.shape[0]
    return pl.pallas_call(
        _positions_kernel,
        grid=(t // tm,),
        in_specs=[pl.BlockSpec((tm, E_PER_GROUP), lambda i: (i, 0)),
                  pl.BlockSpec((1, 8, LANES), lambda i: (i, 0, 0))],
        out_specs=pl.BlockSpec((tm, E_PER_GROUP), lambda i: (i, 0)),
        out_shape=jax.ShapeDtypeStruct((t, E_PER_GROUP), jnp.int32),
        compiler_params=_cparams(("parallel",)),
        name="positions",
    )(route, first_row)[:, 0]


def _group_experts(h, cw, wg_ref, wu_ref, wd_ref, act):
    for e in range(E_PER_GROUP):
        a = _dot(h, wg_ref[0, e])
        u = _dot(h, wu_ref[0, e])
        act[:, e * D_EXPERT:(e + 1) * D_EXPERT] = ((a * jax.nn.sigmoid(a)) * u * cw(e)).astype(BF16)
    return _dot(act[...], wd_ref[0])


N_SLABS = D_MODEL // LANES
SUBLANES = 8


def _to_token_tiles(ref, x):
    for s in range(N_SLABS):
        ref[pl.ds(s, x.shape[0], stride=N_SLABS), :] = x[:, s * LANES:(s + 1) * LANES]


def _landing_shape(rows):
    return (rows // SUBLANES, N_SLABS, SUBLANES, LANES)


def _from_landing(buf):
    rows = buf.shape[0] * SUBLANES
    return jnp.concatenate([buf[:, s].reshape(rows, LANES) for s in range(N_SLABS)], axis=1)


def _row_gather(idx_ref, base, src_hbm, dst, sem):
    def body(i, _):
        for u in range(SUBLANES):
            pltpu.make_async_copy(src_hbm.at[idx_ref[base + i * SUBLANES + u]], dst.at[i, :, u, :], sem).start()
        return 0
    lax.fori_loop(0, dst.shape[0], body, 0)


def _row_gather_wait(dst, sem):
    pltpu.make_async_copy(dst, dst, sem).wait()


def _moe_sorted_kernel(tg_ref, nvt_ref, pos_ref, h3_hbm, zeros_hbm, wrt_ref, brt_ref,
                       wg_ref, wu_ref, wd_ref, y_ref, xbuf, sem, hb, yacc, src_ref, clear_sem, *, tm):
    k = pl.program_id(0)
    nvt = nvt_ref[0]
    slot = lax.rem(k, 2)

    @pl.when(k == 0)
    def _():
        clear = pltpu.make_async_copy(zeros_hbm, src_ref, clear_sem)
        clear.start()
        clear.wait()

        def invert(t, _):
            src_ref[pos_ref[t]] = t
            return 0

        lax.fori_loop(0, pos_ref.shape[0], invert, 0, unroll=32)

    @pl.when((k == 0) & (nvt > 0))
    def _():
        _row_gather(src_ref, 0, h3_hbm, xbuf.at[0], sem.at[0])

    @pl.when(k < nvt)
    def _():
        _row_gather_wait(xbuf.at[slot], sem.at[slot])

        @pl.when(k + 1 < nvt)
        def _():
            _row_gather(src_ref, (k + 1) * tm, h3_hbm, xbuf.at[1 - slot], sem.at[1 - slot])

        x = _from_landing(xbuf.at[slot])
        group = tg_ref[k]
        _, comb, _, _ = _route(_router_logits(x, wrt_ref, brt_ref), group.astype(F32))
        lane = lax.broadcasted_iota(jnp.int32, comb.shape, 1)
        hb[...] = x.astype(BF16)
        yacc[...] = jnp.zeros(yacc.shape, F32)
        col_max = jnp.max(comb, axis=0, keepdims=True)
        used = [jnp.max(jnp.where(lane[:1] == group * E_PER_GROUP + e, col_max, 0.0)) for e in range(E_PER_GROUP)]
        for e in range(E_PER_GROUP):
            @pl.when(used[e] > 0.0)
            def _(e=e):
                cw = jnp.sum(jnp.where(lane == group * E_PER_GROUP + e, comb, 0.0), axis=-1, keepdims=True)
                h = hb[...]
                a = _dot(h, wg_ref[0, e])
                u = _dot(h, wu_ref[0, e])
                act = ((a * jax.nn.sigmoid(a)) * u * cw).astype(BF16)
                yacc[...] += _dot(act, wd_ref[0, e * D_EXPERT:(e + 1) * D_EXPERT, :])

        _to_token_tiles(y_ref, yacc[...])

    @pl.when(k >= nvt)
    def _():
        y_ref[...] = jnp.zeros(y_ref.shape, F32)


def _moe_sorted(h3, tile_group, n_valid_tiles, pos, n_tiles, tm, wrt, brt, wg, wu, wd):
    wspec = lambda shape: pl.BlockSpec(shape, lambda k, tg, nv, ps: (tg[k],) + (0,) * (len(shape) - 1))
    cspec = lambda shape: pl.BlockSpec(shape, lambda k, tg, nv, ps: (0,) * len(shape),
                                       pipeline_mode=pl.Buffered(1))
    return pl.pallas_call(
        functools.partial(_moe_sorted_kernel, tm=tm),
        grid_spec=pltpu.PrefetchScalarGridSpec(
            num_scalar_prefetch=3,
            grid=(n_tiles,),
            in_specs=[pl.BlockSpec(memory_space=pl.ANY), pl.BlockSpec(memory_space=pl.ANY),
                      cspec((D_MODEL, 2 * LANES)), cspec((1, LANES)),
                      wspec((1, E_PER_GROUP, D_MODEL, D_EXPERT)), wspec((1, E_PER_GROUP, D_MODEL, D_EXPERT)),
                      wspec((1, E_PER_GROUP * D_EXPERT, D_MODEL))],
            out_specs=pl.BlockSpec((tm * N_SLABS, LANES), lambda k, tg, nv, ps: (k, 0)),
            scratch_shapes=[pltpu.VMEM((2,) + _landing_shape(tm), F32), pltpu.SemaphoreType.DMA((2,)),
                            pltpu.VMEM((tm, D_MODEL), BF16), pltpu.VMEM((tm, D_MODEL), F32),
                            pltpu.SMEM((n_tiles * tm,), jnp.int32), pltpu.SemaphoreType.DMA(())]),
        out_shape=jax.ShapeDtypeStruct((n_tiles * tm * N_SLABS, LANES), F32),
        compiler_params=_cparams(("arbitrary",)),
        name="moe_sorted",
    )(tile_group, n_valid_tiles, pos, h3, jnp.zeros((n_tiles * tm,), jnp.int32), wrt, brt, wg, wu, wd)


def _combine_kernel(pos_ref, y3_hbm, x1_ref, o_ref, ybuf, sem, *, tm):
    k = pl.program_id(0)
    slot = lax.rem(k, 2)

    @pl.when(k == 0)
    def _():
        _row_gather(pos_ref, 0, y3_hbm, ybuf.at[0], sem.at[0])

    _row_gather_wait(ybuf.at[slot], sem.at[slot])

    @pl.when(k + 1 < pl.num_programs(0))
    def _():
        _row_gather(pos_ref, (k + 1) * tm, y3_hbm, ybuf.at[1 - slot], sem.at[1 - slot])

    o_ref[...] = x1_ref[...] + _from_landing(ybuf.at[slot])


def _combine(y3, pos, x1, tm):
    t = x1.shape[0]
    return pl.pallas_call(
        functools.partial(_combine_kernel, tm=tm),
        grid_spec=pltpu.PrefetchScalarGridSpec(
            num_scalar_prefetch=1,
            grid=(t // tm,),
            in_specs=[pl.BlockSpec(memory_space=pl.ANY), pl.BlockSpec((tm, D_MODEL), lambda k, ps: (k, 0))],
            out_specs=pl.BlockSpec((tm, D_MODEL), lambda k, ps: (k, 0)),
            scratch_shapes=[pltpu.VMEM((2,) + _landing_shape(tm), F32), pltpu.SemaphoreType.DMA((2,))]),
        out_shape=jax.ShapeDtypeStruct((t, D_MODEL), F32),
        compiler_params=_cparams(("arbitrary",)),
        name="combine",
    )(pos, y3, x1)


def _moe_kernel(h_ref, comb_ref, x1_ref, wg_ref, wu_ref, wd_ref, o_ref, act):
    g = pl.program_id(1)
    comb = comb_ref[0]
    y = _group_experts(h_ref[...], lambda e: comb[:, e:e + 1], wg_ref, wu_ref, wd_ref, act)

    @pl.when(g == 0)
    def _():
        o_ref[...] = x1_ref[...] + y

    @pl.when(g != 0)
    def _():
        o_ref[...] += y


def _moe(h2, comb, x1, tm, wg, wu, wd):
    t = h2.shape[0]
    return pl.pallas_call(
        _moe_kernel,
        grid=(t // tm, N_GROUPS),
        in_specs=[pl.BlockSpec((tm, D_MODEL), lambda i, g: (i, 0)),
                  pl.BlockSpec((1, tm, E_PER_GROUP), lambda i, g: (g, i, 0)),
                  pl.BlockSpec((tm, D_MODEL), lambda i, g: (i, 0)),
                  pl.BlockSpec((1, E_PER_GROUP, D_MODEL, D_EXPERT), lambda i, g: (g, 0, 0, 0)),
                  pl.BlockSpec((1, E_PER_GROUP, D_MODEL, D_EXPERT), lambda i, g: (g, 0, 0, 0)),
                  pl.BlockSpec((1, E_PER_GROUP * D_EXPERT, D_MODEL), lambda i, g: (g, 0, 0))],
        out_specs=pl.BlockSpec((tm, D_MODEL), lambda i, g: (i, 0)),
        out_shape=jax.ShapeDtypeStruct((t, D_MODEL), F32),
        scratch_shapes=[pltpu.VMEM((tm, E_PER_GROUP * D_EXPERT), BF16)],
        compiler_params=_cparams(("parallel", "arbitrary")),
        name="moe",
    )(h2, comb, x1, wg, wu, wd)


def _rope_tables(pos):
    half = DK_R // 2
    inv = ROPE_BASE ** (-jnp.arange(half, dtype=F32) / half)
    ang = pos.astype(F32)[:, None] * inv[None, :]
    c = jnp.cos(ang)
    s = jnp.sin(ang)
    return jnp.concatenate([c, c, c, c], axis=-1), jnp.concatenate([-s, s, -s, s], axis=-1)


def _pair_rows(frows):
    return frows.reshape(frows.shape[0] // 2, 2, frows.shape[1])


def kernel(x_prompt, x_sample, state_ret, cache_fox_k, cache_fox_v, cache_fox_logf, cache_mem_k, cache_mem_v,
           mem_prompt, g_norm1, w_in, b_forget, g_fox_q, g_fox_k, g_mem_q, g_mem_in, w_mem_kv, g_mem_k,
           g_ret_out, w_br_ret, w_br_fox, w_br_mem, w_out, g_norm2, w_route_group, b_route_group,
           w_route_expert, b_route_expert, w_exp_gate, w_exp_up, w_exp_down):
    nb, seq, _ = x_prompt.shape
    nbs, n_new, _ = x_sample.shape
    past = cache_fox_k.shape[2]
    l = 0

    wi = w_in[l]
    o_ff = 2 * RQK_W + 2 * RV_W + 3 * FOX_W
    w_in_r = jnp.concatenate(
        [wi[:, :o_ff], wi[:, o_ff + H_F:], wi[:, o_ff:o_ff + H_F],
         jnp.zeros((D_MODEL, LANES - H_F), F32)], axis=1).astype(BF16)
    bf_pad = jnp.concatenate([b_forget[l], jnp.zeros((LANES - H_F,), F32)])[None, :]
    g1 = g_norm1[l][None, :]
    gfq = jnp.tile(g_fox_q[l], H_F)[None, :]
    gfk = jnp.tile(g_fox_k[l], H_F)[None, :]
    gmq = g_mem_q[l][None, :]
    hid = jnp.arange(FOX_W) // D_F
    bd = jnp.where(hid[:, None] == hid[None, :], 1.0 / D_F, 0.0).astype(BF16)
    wr = w_br_ret[l].astype(BF16)
    wf = w_br_fox[l].astype(BF16)
    wm = w_br_mem[l].astype(BF16)
    wo = w_out[l].astype(BF16)
    g2 = g_norm2[l][None, :]
    n_e = N_GROUPS * E_PER_GROUP
    wrt = jnp.concatenate([w_route_expert[l], w_route_group[l],
                           jnp.zeros((D_MODEL, LANES - n_e - N_GROUPS), F32)], axis=1)
    wrt_hi = wrt.astype(BF16)
    wrt2 = jnp.concatenate([wrt_hi, (wrt - wrt_hi.astype(F32)).astype(BF16)], axis=1)
    brt = jnp.concatenate([b_route_expert[l], b_route_group[l],
                           jnp.zeros((LANES - n_e - N_GROUPS,), F32)])[None, :]
    wg = w_exp_gate[l].astype(BF16)
    wu = w_exp_up[l].astype(BF16)
    wd = w_exp_down[l].astype(BF16).reshape(N_GROUPS, E_PER_GROUP * D_EXPERT, D_MODEL)
    gro = g_ret_out[l][None, :]
    prep_w = (g1, w_in_r, bf_pad, gfq, gfk, gmq, bd)
    merge_w = (wr, wf, wm, wo, g2, wrt2, brt)

    tm = 512
    xp = x_prompt.reshape(nb * seq, D_MODEL)
    (rq, rk, rv, rg, fqt, fkt, fvt, lft, mq, gates, fkb, lf) = _prep(
        xp, _rope_tables(jnp.arange(seq)), tm, seq // tm, True, *prep_w)
    o_r, s_fin = _retention(rq, rk, rv, rg, nb, seq, 256, gro, None)
    fcum = _cumsum_cols(lf.reshape(nb, seq, H_F), 256)
    fcum = jnp.swapaxes(fcum.reshape(nb, seq, H_F // 2, 2), 1, 2).reshape(nb * (H_F // 2) * seq, 2)
    o_f = _fox_prompt(fqt, fkb, fvt, fcum, nb, seq, 256)
    mk, mv = _memkv(mem_prompt.reshape(nb * N_MEM, D_MODEL), nb, g_mem_in[l][None, :],
                    w_mem_kv[l].astype(BF16), g_mem_k[l][None, :])
    o_m = _memattn(mq, mk, mv, nb, seq, tm)
    x1, h3, route, counts = _merge(xp, o_r, o_f, o_m, gates, tm, True, *merge_w)
    t_p = nb * seq
    n_tiles = t_p // tm + N_GROUPS
    n_cls = N_GROUPS * N_PAIRS
    cnt = counts[:, 0, :n_cls].astype(jnp.int32)
    before_tile = jnp.cumsum(cnt, axis=0) - cnt
    cls_tot = jnp.sum(cnt, axis=0)
    grp_tot = jnp.sum(cls_tot.reshape(N_GROUPS, N_PAIRS), axis=1)
    tiles_g = (grp_tot + tm - 1) // tm
    tile_end = jnp.cumsum(tiles_g)
    row_start = (tile_end - tiles_g) * tm
    in_grp = cls_tot.reshape(N_GROUPS, N_PAIRS)
    cls_start = (row_start[:, None] + jnp.cumsum(in_grp, axis=1) - in_grp).reshape(n_cls)
    first_row = jnp.pad((cls_start[None, :] + before_tile).astype(F32), ((0, 0), (0, LANES - n_cls)))
    pos = _positions(route, jnp.broadcast_to(first_row[:, None, :], (t_p // tm, 8, LANES)), tm)
    tile_ids = jnp.arange(n_tiles, dtype=jnp.int32)
    tile_group = jnp.minimum(jnp.sum((tile_ids[:, None] >= tile_end[None, :]).astype(jnp.int32), axis=1),
                             N_GROUPS - 1)
    y3 = _moe_sorted(h3.reshape(t_p, N_SLABS, LANES), tile_group, tile_end[-1:], pos, n_tiles, tm,
                     wrt2, brt, wg, wu, wd)
    y_prompt = _combine(y3.reshape(n_tiles * tm, N_SLABS, LANES), pos, x1, tm).reshape(nb, seq, D_MODEL)

    ts = nbs * n_new
    xs = x_sample.reshape(ts, D_MODEL)
    pos_s = jnp.tile(past + jnp.arange(n_new), nbs)
    (rq_s, rk_s, rv_s, rg_s, fq_s, fk_s, fv_s, lf_s, mq_s, gates_s) = _prep(
        xs, _rope_tables(pos_s), ts, 1, False, *prep_w)
    o_r_s, s_new = _retention(rq_s, rk_s, rv_s, rg_s, nbs, n_new, n_new, gro, state_ret[l])
    pad = (-(past + n_new)) % 256
    lf_rows = jnp.concatenate([jnp.swapaxes(cache_fox_logf[l], 1, 2),
                               jnp.swapaxes(lf_s.reshape(nbs, n_new, H_F), 1, 2),
                               jnp.zeros((nbs, H_F, pad), F32)], axis=2).reshape(nbs * H_F, past + n_new + pad)
    f_all = _cumsum(lf_rows, 256)
    feat_major = lambda c: jnp.transpose(c, (0, 2, 3, 1)).reshape(nbs * FOX_W, past)
    o_f_s = _fox_sample(fq_s, feat_major(cache_fox_k[l]), feat_major(cache_fox_v[l]), fk_s, fv_s,
                        _pair_rows(f_all[:, :past]), _pair_rows(f_all[:, past:past + n_new]),
                        nbs, n_new, past)
    o_m_s = _memattn(mq_s, cache_mem_k[l].reshape(nbs * N_MEM, MEM_W),
                     cache_mem_v[l].reshape(nbs * N_MEM, MEM_W), nbs, n_new, n_new)
    x1_s, h2_s, comb_s = _merge(xs, o_r_s, o_f_s, o_m_s, gates_s, ts, False, *merge_w)
    y_sample = _moe(h2_s, comb_s, x1_s, ts, wg, wu, wd).reshape(nbs, n_new, D_MODEL)

    token_major = lambda a: jnp.transpose(a.reshape(nb, H_F, D_F, seq), (0, 3, 1, 2))

    return (y_prompt, y_sample,
            s_fin[None], token_major(fkt)[None], token_major(fvt)[None],
            jnp.swapaxes(lft.reshape(nb, H_F, seq), 1, 2)[None],
            mk.reshape(1, nb, N_MEM, H_M, D_M), mv.reshape(1, nb, N_MEM, H_M, D_M),
            s_new[None], fk_s.reshape(1, nbs, n_new, H_F, D_F), fv_s.reshape(1, nbs, n_new, H_F, D_F),
            lf_s.reshape(1, nbs, n_new, H_F))
```

```python
import functools

import jax
import jax.numpy as jnp
from jax import lax
from jax.experimental import pallas as pl
from jax.experimental.pallas import tpu as pltpu

F32 = jnp.float32
BF16 = jnp.bfloat16

D_MODEL = 1024
H_R, DK_R, DV_R = 4, 64, 128
H_F, D_F = 8, 64
H_M, D_M = 4, 128
N_MEM = 256
N_GROUPS, E_PER_GROUP, D_EXPERT = 4, 8, 256
ROPE_BASE = 10000.0
EPS = 1e-6
LOG2E = 1.4426950408889634

RQK_W = H_R * DK_R
RV_W = H_R * DV_R
FOX_W = H_F * D_F
MEM_W = H_M * D_M
GATE_W = 3 * D_MODEL
LANES = 128

C_RQ = 0
C_RK = C_RQ + RQK_W
C_RV = C_RK + RQK_W
C_RG = C_RV + RV_W
C_FQ = C_RG + RV_W
C_FK = C_FQ + FOX_W
C_FV = C_FK + FOX_W
C_MQ = C_FV + FOX_W
C_GT = C_MQ + MEM_W
C_FF = C_GT + GATE_W
N_IN_PAD = C_FF + LANES

VMEM_LIMIT = 56 * 1024 * 1024


def _cparams(sem):
    return pltpu.CompilerParams(dimension_semantics=sem, vmem_limit_bytes=VMEM_LIMIT)


def _const_spec(shape):
    nd = len(shape)
    return pl.BlockSpec(shape, lambda *_: (0,) * nd, pipeline_mode=pl.Buffered(1))


def _dot(a, b):
    return jnp.dot(a, b, preferred_element_type=F32)


def _dot_nt(a, b):
    return lax.dot_general(a, b, (((1,), (1,)), ((), ())), preferred_element_type=F32)


def _dot_tn(a, b):
    return lax.dot_general(a, b, (((0,), (0,)), ((), ())), preferred_element_type=F32)


def _lane_rmsnorm(z):
    return z * lax.rsqrt(jnp.mean(z * z, axis=-1, keepdims=True) + EPS)


def _prep_kernel(x_ref, g1_ref, w_ref, cos_ref, sin_ref, bf_ref, gfq_ref, gfk_ref, gmq_ref, bd_ref,
                 rq_ref, rk_ref, rv_ref, rg_ref, fq_ref, fk_ref, fv_ref, lf_ref, mq_ref, gt_ref, *extra,
                 seq_minor):
    x = x_ref[...]
    h = (x * lax.rsqrt(jnp.mean(x * x, axis=-1, keepdims=True) + EPS) * g1_ref[...]).astype(BF16)

    def proj(c0, width):
        return _dot(h, w_ref[:, c0:c0 + width])

    cos = cos_ref[...]
    sin = sin_ref[...]
    lane = lax.broadcasted_iota(jnp.int32, cos.shape, 1)
    first_half = (lane % DK_R) < (DK_R // 2)

    def rope(z):
        swapped = jnp.where(first_half, pltpu.roll(z, LANES - DK_R // 2, 1), pltpu.roll(z, DK_R // 2, 1))
        return z * cos + swapped * sin

    zq = proj(C_RQ, RQK_W)
    zk = proj(C_RK, RQK_W)
    for p in range(RQK_W // LANES):
        sl = slice(p * LANES, (p + 1) * LANES)
        rq_ref[:, sl] = rope(zq[:, sl])
        rk_ref[:, sl] = rope(zk[:, sl]) * (DK_R ** -0.5)
    rv_ref[...] = proj(C_RV, RV_W)
    rg_ref[...] = proj(C_RG, RV_W)

    def head64_norm(z, g_ref):
        ms = _dot((z * z).astype(BF16), bd_ref[...])
        return z * lax.rsqrt(ms + EPS) * g_ref[...]

    fq = head64_norm(proj(C_FQ, FOX_W), gfq_ref) * (D_F ** -0.5 * LOG2E)
    fk = head64_norm(proj(C_FK, FOX_W), gfk_ref)
    fv = proj(C_FV, FOX_W)
    if seq_minor:
        fq_ref[...] = fq.T.astype(BF16)
        fk_ref[...] = fk.T
        fv_ref[...] = fv.T
        extra[0][...] = fk.astype(BF16)
    else:
        fq_ref[...] = fq.astype(BF16)
        fk_ref[...] = fk
        fv_ref[...] = fv

    zm = proj(C_MQ, MEM_W)
    for hh in range(H_M):
        sl = slice(hh * D_M, (hh + 1) * D_M)
        mq_ref[:, sl] = (_lane_rmsnorm(zm[:, sl]) * gmq_ref[...]).astype(BF16)

    for b in range(3):
        gt_ref[:, b * D_MODEL:(b + 1) * D_MODEL] = jax.nn.sigmoid(proj(C_GT + b * D_MODEL, D_MODEL)).astype(BF16)

    v = proj(C_FF, LANES) + bf_ref[...]
    lf = jnp.minimum(v, 0.0) - jnp.log1p(jnp.exp(-jnp.abs(v)))
    if seq_minor:
        lf_ref[...] = lf.T[:H_F, :]
        extra[1][...] = lf[:, :H_F]
    else:
        lf_ref[...] = lf[:, :H_F]


def _prep(x2d, tables, tm, n_pos_tiles, seq_minor, g1, w_in_r, bf_pad, gfq, gfk, gmq, bd):
    t = x2d.shape[0]
    cos_t, sin_t = tables
    row = lambda w: pl.BlockSpec((tm, w), lambda i: (i, 0))
    pos = pl.BlockSpec((tm, LANES), lambda i: (i % n_pos_tiles, 0))
    sds = jax.ShapeDtypeStruct
    if seq_minor:
        nb, seq = t // (n_pos_tiles * tm), n_pos_tiles * tm
        fm = lambda rows_: pl.BlockSpec((rows_, tm), lambda i: (i // n_pos_tiles, i % n_pos_tiles))
        fox_shapes = [sds((nb * FOX_W, seq), BF16), sds((nb * FOX_W, seq), F32), sds((nb * FOX_W, seq), F32),
                      sds((nb * H_F, seq), F32)]
        fox_specs = [fm(FOX_W), fm(FOX_W), fm(FOX_W), fm(H_F)]
        extra_shapes, extra_specs = [sds((t, FOX_W), BF16), sds((t, H_F), F32)], [row(FOX_W), row(H_F)]
    else:
        fox_shapes = [sds((t, FOX_W), BF16), sds((t, FOX_W), F32), sds((t, FOX_W), F32), sds((t, H_F), F32)]
        fox_specs = [row(FOX_W), row(FOX_W), row(FOX_W), row(H_F)]
        extra_shapes, extra_specs = [], []
    out_shapes = [sds((t, RQK_W), F32), sds((t, RQK_W), F32), sds((t, RV_W), F32), sds((t, RV_W), F32),
                  *fox_shapes, sds((t, MEM_W), BF16), sds((t, GATE_W), BF16), *extra_shapes]
    out_specs = [row(RQK_W), row(RQK_W), row(RV_W), row(RV_W), *fox_specs, row(MEM_W), row(GATE_W),
                 *extra_specs]
    return pl.pallas_call(
        functools.partial(_prep_kernel, seq_minor=seq_minor),
        grid=(t // tm,),
        in_specs=[row(D_MODEL), _const_spec((1, D_MODEL)), _const_spec((D_MODEL, N_IN_PAD)), pos, pos,
                  _const_spec((1, LANES)), _const_spec((1, FOX_W)), _const_spec((1, FOX_W)),
                  _const_spec((1, D_M)), _const_spec((FOX_W, FOX_W))],
        out_specs=out_specs,
        out_shape=out_shapes,
        compiler_params=_cparams(("parallel",)),
        name="prep",
    )(x2d, g1, w_in_r, cos_t, sin_t, bf_pad, gfq, gfk, gmq, bd)


def _ret_kernel(*refs, has_init):
    if has_init:
        (rq_ref, rk_ref, rv_ref, rg_ref, dmat_ref, qdec_ref, kdec_ref, gpow_ref, gro_ref, s0_ref,
         o_ref, sfin_ref, state) = refs
    else:
        (rq_ref, rk_ref, rv_ref, rg_ref, dmat_ref, qdec_ref, kdec_ref, gpow_ref, gro_ref,
         o_ref, sfin_ref, state) = refs
    c = pl.program_id(1)

    @pl.when(c == 0)
    def _():
        state[...] = jnp.zeros(state.shape, F32)
        if has_init:
            for h in range(H_R):
                r0 = DK_R * (h % 2)
                state[h, r0:r0 + DK_R, :] = s0_ref[0, h]

    lane = lax.broadcasted_iota(jnp.int32, (rq_ref.shape[0], LANES), 1)
    for p in range(H_R // 2):
        sl = slice(p * LANES, (p + 1) * LANES)
        q2 = rq_ref[:, sl]
        k2 = rk_ref[:, sl]
        kd2 = k2 * kdec_ref[p]
        for hh in range(2):
            h = 2 * p + hh
            mine = (lane >= hh * DK_R) & (lane < (hh + 1) * DK_R)
            qm = jnp.where(mine, q2, 0.0).astype(BF16)
            kdm = jnp.where(mine, kd2, 0.0).astype(BF16)
            v = rv_ref[:, h * DV_R:(h + 1) * DV_R].astype(BF16)
            sc = _dot_nt(qm, k2.astype(BF16)) * dmat_ref[h]
            s_old = state[h]
            o = _dot(sc.astype(BF16), v) + _dot(qm, s_old.astype(BF16)) * qdec_ref[h]
            state[h] = gpow_ref[h] * s_old + _dot_tn(kdm, v)
            normed = _lane_rmsnorm(o) * gro_ref[...]
            rg = rg_ref[:, h * DV_R:(h + 1) * DV_R]
            o_ref[:, h * DV_R:(h + 1) * DV_R] = (normed * (rg * jax.nn.sigmoid(rg))).astype(BF16)

    @pl.when(c == pl.num_programs(1) - 1)
    def _():
        for h in range(H_R):
            r0 = DK_R * (h % 2)
            sfin_ref[0, h] = state[h, r0:r0 + DK_R, :]


def _retention(rq, rk, rv, rg, n_batch, seq, ch, g_ret_out, state0):
    nc = seq // ch
    lg = jnp.log1p(-jnp.exp2(-5.0 - jnp.arange(H_R, dtype=F32)))
    idx = jnp.arange(ch, dtype=F32)
    diff = idx[:, None] - idx[None, :]
    causal = diff >= 0
    dmat = jnp.where(causal[None], jnp.exp(jnp.where(causal, diff, 0.0)[None] * lg[:, None, None]), 0.0)
    q_dec = jnp.exp((idx + 1.0)[None, :] * lg[:, None])
    k_dec = jnp.exp((ch - 1.0 - idx)[None, :] * lg[:, None])
    qdec = jnp.broadcast_to(q_dec[:, :, None], (H_R, ch, DV_R))
    kdec = jnp.broadcast_to(k_dec[:, :, None], (H_R, ch, DK_R))
    kdec = kdec.reshape(H_R // 2, 2, ch, DK_R).transpose(0, 2, 1, 3).reshape(H_R // 2, ch, LANES)
    gpow = jnp.broadcast_to(jnp.exp(ch * lg)[:, None, None], (H_R, 1, DV_R))

    has_init = state0 is not None
    blk = lambda w: pl.BlockSpec((ch, w), lambda b, c: (b * nc + c, 0))
    in_specs = [blk(RQK_W), blk(RQK_W), blk(RV_W), blk(RV_W),
                _const_spec((H_R, ch, ch)), _const_spec((H_R, ch, DV_R)),
                _const_spec((H_R // 2, ch, LANES)), _const_spec((H_R, 1, DV_R)), _const_spec((1, DV_R))]
    args = [rq, rk, rv, rg, dmat, qdec, kdec, gpow, g_ret_out]
    if has_init:
        in_specs.append(pl.BlockSpec((1, H_R, DK_R, DV_R), lambda b, c: (b, 0, 0, 0)))
        args.append(state0)
    return pl.pallas_call(
        functools.partial(_ret_kernel, has_init=has_init),
        grid=(n_batch, nc),
        in_specs=in_specs,
        out_specs=[blk(RV_W), pl.BlockSpec((1, H_R, DK_R, DV_R), lambda b, c: (b, 0, 0, 0))],
        out_shape=[jax.ShapeDtypeStruct((n_batch * seq, RV_W), BF16),
                   jax.ShapeDtypeStruct((n_batch, H_R, DK_R, DV_R), F32)],
        scratch_shapes=[pltpu.VMEM((H_R, LANES, DV_R), F32)],
        compiler_params=_cparams(("parallel", "arbitrary")),
        name="retention",
    )(*args)


def _split3(x):
    hi = x.astype(BF16)
    r1 = x - hi.astype(F32)
    mid = r1.astype(BF16)
    lo = (r1 - mid.astype(F32)).astype(BF16)
    return hi, mid, lo


def _cumsum_kernel(x_ref, o_ref, *, blk):
    rows, n = x_ref.shape
    r = lax.broadcasted_iota(jnp.int32, (blk, blk), 0)
    c = lax.broadcasted_iota(jnp.int32, (blk, blk), 1)
    tri = jnp.where(r <= c, 1.0, 0.0).astype(BF16)
    carry = jnp.zeros((rows, 1), F32)
    for i in range(n // blk):
        hi, mid, lo = _split3(x_ref[:, i * blk:(i + 1) * blk])
        cum = (_dot(hi, tri) + _dot(mid, tri)) + _dot(lo, tri) + carry
        o_ref[:, i * blk:(i + 1) * blk] = cum
        carry = cum[:, blk - 1:blk]


def _cumsum_cols_kernel(x_ref, o_ref, *, blk):
    n = x_ref.shape[1]
    r = lax.broadcasted_iota(jnp.int32, (blk, blk), 0)
    c = lax.broadcasted_iota(jnp.int32, (blk, blk), 1)
    tri = jnp.where(r >= c, 1.0, 0.0).astype(BF16)
    carry = jnp.zeros((1, x_ref.shape[2]), F32)
    for i in range(n // blk):
        hi, mid, lo = _split3(x_ref[0, i * blk:(i + 1) * blk, :])
        cum = (_dot(tri, hi) + _dot(tri, mid)) + _dot(tri, lo) + carry
        o_ref[0, i * blk:(i + 1) * blk, :] = cum
        carry = cum[blk - 1:blk, :]


def _cumsum_cols(x, blk):
    nb, t, w = x.shape
    spec = pl.BlockSpec((1, t, w), lambda b: (b, 0, 0))
    return pl.pallas_call(
        functools.partial(_cumsum_cols_kernel, blk=blk),
        grid=(nb,),
        in_specs=[spec], out_specs=spec,
        out_shape=jax.ShapeDtypeStruct(x.shape, F32),
        compiler_params=_cparams(("parallel",)),
        name="cumsum_cols",
    )(x)


def _cumsum(x, blk):
    r, t = x.shape
    spec = pl.BlockSpec((H_F, t), lambda b: (b, 0))
    return pl.pallas_call(
        functools.partial(_cumsum_kernel, blk=blk),
        grid=(r // H_F,),
        in_specs=[spec], out_specs=spec,
        out_shape=jax.ShapeDtypeStruct(x.shape, F32),
        compiler_params=_cparams(("parallel",)),
        name="cumsum",
    )(x)


def _head_rows_mask(shape, hh):
    sub = lax.broadcasted_iota(jnp.int32, shape, 0)
    return (sub >= hh * D_F) & (sub < (hh + 1) * D_F)


N_AUG = 3
V_ROWS = D_F + 16


def _fox_kernel(qt_ref, k_ref, vt_ref, fcum_ref, o_ref, qa, ka, vb, *, tq):
    seq = k_ref.shape[0]
    tk = tq
    qt = qt_ref[...].astype(F32)
    k = k_ref[...].astype(F32)
    sub = lax.broadcasted_iota(jnp.int32, qt.shape, 0)
    lane = lax.broadcasted_iota(jnp.int32, k.shape, 1)
    for hh in range(2):
        own, oth = hh * D_F, (1 - hh) * D_F
        qa[hh] = jnp.where((sub >= own) & (sub < own + D_F), qt,
                           jnp.where((sub >= oth) & (sub < oth + N_AUG), 1.0, 0.0)).astype(BF16)
        bias = fcum_ref[:, hh:hh + 1] * (-LOG2E)
        kk = jnp.where((lane >= own) & (lane < own + D_F), k, 0.0)
        for a, piece in enumerate(_split3(bias)):
            kk = jnp.where(lane == oth + a, piece.astype(F32), kk)
        ka[hh] = kk.astype(BF16)
    for hh in range(2):
        vb[hh, :D_F, :] = vt_ref[hh * D_F:(hh + 1) * D_F, :].astype(BF16)
        vb[hh, D_F:, :] = jnp.ones((V_ROWS - D_F, seq), BF16)
    causal = (lax.broadcasted_iota(jnp.int32, (tk, tq), 1) >= lax.broadcasted_iota(jnp.int32, (tk, tq), 0))

    pairs = [(i, j) for i in range(seq // tq) for j in range(i + 1)]

    def stage_a(i, j):
        return tuple(_dot(ka[hh, j * tk:(j + 1) * tk, :], qa[hh, :, i * tq:(i + 1) * tq]) for hh in range(2))

    def stage_b(i, j, ts, maxes):
        new_maxes, probs = [], []
        for hh in range(2):
            t = jnp.where(causal, ts[hh], -jnp.inf) if j == i else ts[hh]
            t_max = jnp.max(t, axis=0, keepdims=True)
            if j == 0:
                m_new, alpha = t_max, None
            else:
                m_new = jnp.maximum(maxes[hh], t_max)
                alpha = jnp.exp2(maxes[hh] - m_new)
            new_maxes.append(m_new)
            probs.append((jnp.exp2(t - m_new).astype(BF16), alpha))
        return new_maxes, probs

    def stage_c(i, j, probs, accs):
        out = []
        for hh in range(2):
            p, alpha = probs[hh]
            pv = _dot(vb[hh, :, j * tk:(j + 1) * tk], p)
            out.append(pv if j == 0 else alpha * accs[hh] + pv)
        return out

    scores, probs, maxes, accs = {}, {}, None, None
    for s in range(len(pairs) + 2):
        if s < len(pairs):
            scores[s] = stage_a(*pairs[s])
        if 0 <= s - 1 < len(pairs):
            maxes, probs[s - 1] = stage_b(*pairs[s - 1], scores.pop(s - 1), maxes)
        if 0 <= s - 2 < len(pairs):
            i, j = pairs[s - 2]
            accs = stage_c(i, j, probs.pop(s - 2), accs)
            if j == i:
                o_t = jnp.concatenate([a[:D_F] / a[D_F:D_F + 1] for a in accs], axis=0)
                o_ref[i * tq:(i + 1) * tq, :] = o_t.T.astype(BF16)


def _fox_prompt(fqt, fkb, fvt, fcum, n_batch, seq, tq):
    npair = H_F // 2
    fm_spec = pl.BlockSpec((LANES, seq), lambda b, p: (b * npair + p, 0))
    tok_spec = pl.BlockSpec((seq, LANES), lambda b, p: (b, p))
    return pl.pallas_call(
        functools.partial(_fox_kernel, tq=tq),
        grid=(n_batch, npair),
        in_specs=[fm_spec, tok_spec, fm_spec, pl.BlockSpec((seq, 2), lambda b, p: (b * npair + p, 0))],
        out_specs=tok_spec,
        out_shape=jax.ShapeDtypeStruct((n_batch * seq, FOX_W), BF16),
        scratch_shapes=[pltpu.VMEM((2, LANES, seq), BF16), pltpu.VMEM((2, seq, LANES), BF16),
                        pltpu.VMEM((2, V_ROWS, seq), BF16)],
        compiler_params=_cparams(("parallel", "parallel")),
        name="fox_prompt",
    )(fqt, fkb, fvt, fcum)


def _fox_sample_kernel(q_ref, kct_ref, vct_ref, kn_ref, vn_ref, frc_ref, frn_ref, o_ref):
    n = q_ref.shape[0]
    lane = lax.broadcasted_iota(jnp.int32, (n, LANES), 1)
    rows = lax.broadcasted_iota(jnp.int32, (n, n), 0)
    cols = lax.broadcasted_iota(jnp.int32, (n, n), 1)
    for pair in range(H_F // 2):
        sl = slice(pair * LANES, (pair + 1) * LANES)
        q2 = q_ref[:, sl]
        kct = kct_ref[sl, :]
        vct = vct_ref[sl, :].astype(BF16)
        kn = kn_ref[:, sl].astype(BF16)
        vn = vn_ref[:, sl].astype(BF16)
        outs = []
        for hh in range(2):
            h = 2 * pair + hh
            mine = (lane >= hh * D_F) & (lane < (hh + 1) * D_F)
            kc = jnp.where(_head_rows_mask(kct.shape, hh), kct, 0.0).astype(BF16)
            t_c = _dot(q2, kc) - frc_ref[0, h:h + 1, :] * LOG2E
            t_n = _dot_nt(jnp.where(mine, q2, jnp.zeros_like(q2)), kn) - frn_ref[0, h:h + 1, :] * LOG2E
            t_n = jnp.where(rows >= cols, t_n, -jnp.inf)
            m = jnp.maximum(jnp.max(t_c, axis=-1, keepdims=True), jnp.max(t_n, axis=-1, keepdims=True))
            p_c = jnp.exp2(t_c - m)
            p_n = jnp.exp2(t_n - m)
            l = jnp.sum(p_c, axis=-1, keepdims=True) + jnp.sum(p_n, axis=-1, keepdims=True)
            acc = _dot_nt(p_c.astype(BF16), vct) + _dot(p_n.astype(BF16), vn)
            outs.append(acc / l)
        o_ref[:, sl] = jnp.where(lane < D_F, outs[0], outs[1]).astype(BF16)


def _fox_sample(fq, kct, vct, kn, vn, frow_c, frow_n, n_batch, n_new, past):
    tok = pl.BlockSpec((n_new, FOX_W), lambda b: (b, 0))
    cache = pl.BlockSpec((FOX_W, past), lambda b: (b, 0))
    return pl.pallas_call(
        _fox_sample_kernel,
        grid=(n_batch,),
        in_specs=[tok, cache, cache, tok, tok,
                  pl.BlockSpec((1, H_F, past), lambda b: (b, 0, 0)),
                  pl.BlockSpec((1, H_F, n_new), lambda b: (b, 0, 0))],
        out_specs=tok,
        out_shape=jax.ShapeDtypeStruct((n_batch * n_new, FOX_W), BF16),
        compiler_params=_cparams(("parallel",)),
        name="fox_sample",
    )(fq, kct, vct, kn, vn, frow_c, frow_n)


def _memkv_kernel(m_ref, gin_ref, w_ref, gk_ref, k_ref, v_ref):
    x = m_ref[...]
    h = (x * lax.rsqrt(jnp.mean(x * x, axis=-1, keepdims=True) + EPS) * gin_ref[...]).astype(BF16)
    zk = _dot(h, w_ref[:, :MEM_W])
    for hh in range(H_M):
        sl = slice(hh * D_M, (hh + 1) * D_M)
        k_ref[:, sl] = _lane_rmsnorm(zk[:, sl]) * gk_ref[...]
    v_ref[...] = _dot(h, w_ref[:, MEM_W:])


def _memkv(mem2d, n_batch, g_mem_in, w_mem_kv, g_mem_k):
    blk = lambda w: pl.BlockSpec((N_MEM, w), lambda b: (b, 0))
    return pl.pallas_call(
        _memkv_kernel,
        grid=(n_batch,),
        in_specs=[blk(D_MODEL), _const_spec((1, D_MODEL)), _const_spec((D_MODEL, 2 * MEM_W)),
                  _const_spec((1, D_M))],
        out_specs=[blk(MEM_W), blk(MEM_W)],
        out_shape=[jax.ShapeDtypeStruct((n_batch * N_MEM, MEM_W), F32)] * 2,
        compiler_params=_cparams(("parallel",)),
        name="memkv",
    )(mem2d, g_mem_in, w_mem_kv, g_mem_k)


def _memattn_kernel(q_ref, k_ref, v_ref, o_ref):
    for hh in range(H_M):
        sl = slice(hh * D_M, (hh + 1) * D_M)
        s = _dot_nt(q_ref[:, sl], k_ref[:, sl].astype(BF16)) * (D_M ** -0.5)
        m = jnp.max(s, axis=-1, keepdims=True)
        p = jnp.exp(s - m)
        l = jnp.sum(p, axis=-1, keepdims=True)
        o_ref[:, sl] = (_dot(p.astype(BF16), v_ref[:, sl].astype(BF16)) / l).astype(BF16)


def _memattn(mq, mk, mv, n_batch, seq, tq):
    nq = seq // tq
    return pl.pallas_call(
        _memattn_kernel,
        grid=(n_batch, nq),
        in_specs=[pl.BlockSpec((tq, MEM_W), lambda b, i: (b * nq + i, 0)),
                  pl.BlockSpec((N_MEM, MEM_W), lambda b, i: (b, 0)),
                  pl.BlockSpec((N_MEM, MEM_W), lambda b, i: (b, 0))],
        out_specs=pl.BlockSpec((tq, MEM_W), lambda b, i: (b * nq + i, 0)),
        out_shape=jax.ShapeDtypeStruct((n_batch * seq, MEM_W), BF16),
        compiler_params=_cparams(("parallel", "arbitrary")),
        name="memattn",
    )(mq, mk, mv)


def _split2(x):
    hi = x.astype(BF16)
    return hi, (x - hi.astype(F32)).astype(BF16)


N_EXPERTS = N_GROUPS * E_PER_GROUP


N_PAIRS = E_PER_GROUP * (E_PER_GROUP - 1) // 2


def _router_logits(h2, wrt_ref, brt_ref):
    h_hi, h_lo = _split2(h2)
    z = _dot(h_hi, wrt_ref[...])
    return (z[:, :LANES] + z[:, LANES:]) + _dot(h_lo, wrt_ref[:, :LANES]) + brt_ref[...]


def _route(logits, group=None):
    lane = lax.broadcasted_iota(jnp.int32, logits.shape, 1).astype(F32)
    neg = -jnp.inf
    first_idx = lambda mask: jnp.min(jnp.where(mask, lane, float(LANES)), axis=-1, keepdims=True)

    is_g = (lane >= N_EXPERTS) & (lane < N_EXPERTS + N_GROUPS)
    lg = jnp.where(is_g, logits, neg)
    mg = jnp.max(lg, axis=-1, keepdims=True)
    if group is None:
        group = first_idx(lg == mg) - N_EXPERTS
        p_sel = 1.0 / jnp.sum(jnp.exp(lg - mg), axis=-1, keepdims=True)
    else:
        lsel = jnp.max(jnp.where(lane == group + N_EXPERTS, logits, neg), axis=-1, keepdims=True)
        p_sel = jnp.exp(lsel - mg) / jnp.sum(jnp.exp(lg - mg), axis=-1, keepdims=True)

    in_grp = (lane >= group * E_PER_GROUP) & (lane < (group + 1) * E_PER_GROUP)
    le = jnp.where(in_grp, logits, neg)
    v1 = jnp.max(le, axis=-1, keepdims=True)
    i1 = first_idx(le == v1)
    le2 = jnp.where(lane == i1, neg, le)
    v2 = jnp.max(le2, axis=-1, keepdims=True)
    i2 = first_idx(le2 == v2)
    e2 = jnp.exp(v2 - v1)
    w1 = p_sel / (1.0 + e2)
    w2 = p_sel * e2 / (1.0 + e2)
    return group, jnp.where(lane == i1, w1, 0.0) + jnp.where(lane == i2, w2, 0.0), i1, i2


def _merge_kernel(x_ref, or_ref, of_ref, om_ref, gt_ref, wr_ref, wf_ref, wm_ref, wo_ref, g2_ref,
                  wrt_ref, brt_ref, x1_ref, h2_ref, route_ref, *counts_ref, sorted_moe):
    g = lambda b: gt_ref[:, b * D_MODEL:(b + 1) * D_MODEL].astype(F32)
    merged = (g(0) * _dot(or_ref[...], wr_ref[...]) + g(1) * _dot(of_ref[...], wf_ref[...])
              + g(2) * _dot(om_ref[...], wm_ref[...]))
    x1 = x_ref[...] + _dot(merged.astype(BF16), wo_ref[...])
    x1_ref[...] = x1
    h2 = x1 * lax.rsqrt(jnp.mean(x1 * x1, axis=-1, keepdims=True) + EPS) * g2_ref[...]
    group, comb, i1, i2 = _route(_router_logits(h2, wrt_ref, brt_ref))
    if sorted_moe:
        tm = h2.shape[0]
        _to_token_tiles(h2_ref, h2)
        e_lo = jnp.minimum(i1, i2) - group * E_PER_GROUP
        e_hi = jnp.maximum(i1, i2) - group * E_PER_GROUP
        cls = group * N_PAIRS + (e_lo * E_PER_GROUP - e_lo * (e_lo + 1.0) * 0.5 + (e_hi - e_lo - 1.0))
        lane = lax.broadcasted_iota(jnp.int32, (tm, LANES), 1).astype(F32)
        onehot = jnp.where(lane == cls, 1.0, 0.0)
        r = lax.broadcasted_iota(jnp.int32, (tm, tm), 0)
        c = lax.broadcasted_iota(jnp.int32, (tm, tm), 1)
        before = _dot(jnp.where(r > c, 1.0, 0.0).astype(BF16), onehot.astype(BF16))
        rank = jnp.sum(before * onehot, axis=-1, keepdims=True)
        col = lax.broadcasted_iota(jnp.int32, route_ref.shape, 1)
        route_ref[...] = jnp.where(col == 0, group, jnp.where(col == 1, cls, jnp.where(col == 2, rank, 0.0)))
        counts_ref[0][...] = jnp.broadcast_to(jnp.sum(onehot, axis=0, keepdims=True), counts_ref[0].shape)
    else:
        h2_ref[...] = h2.astype(BF16)
        for gi in range(N_GROUPS):
            route_ref[gi] = comb[:, gi * E_PER_GROUP:(gi + 1) * E_PER_GROUP]


def _merge(x2d, o_r, o_f, o_m, gates, tm, sorted_moe, wr, wf, wm, wo, g2, wrt, brt):
    t = x2d.shape[0]
    row = lambda w: pl.BlockSpec((tm, w), lambda i: (i, 0))
    sds = jax.ShapeDtypeStruct
    if sorted_moe:
        h2_spec, h2_shape = pl.BlockSpec((tm * N_SLABS, LANES), lambda i: (i, 0)), sds((t * N_SLABS, LANES), F32)
        rt_specs = [row(E_PER_GROUP), pl.BlockSpec((1, 8, LANES), lambda i: (i, 0, 0))]
        rt_shapes = [sds((t, E_PER_GROUP), F32), sds((t // tm, 8, LANES), F32)]
    else:
        h2_spec, h2_shape = row(D_MODEL), sds((t, D_MODEL), BF16)
        rt_specs = [pl.BlockSpec((N_GROUPS, tm, E_PER_GROUP), lambda i: (0, i, 0))]
        rt_shapes = [sds((N_GROUPS, t, E_PER_GROUP), F32)]
    return pl.pallas_call(
        functools.partial(_merge_kernel, sorted_moe=sorted_moe),
        grid=(t // tm,),
        in_specs=[row(D_MODEL), row(RV_W), row(FOX_W), row(MEM_W), row(GATE_W),
                  _const_spec((RV_W, D_MODEL)), _const_spec((FOX_W, D_MODEL)), _const_spec((MEM_W, D_MODEL)),
                  _const_spec((D_MODEL, D_MODEL)), _const_spec((1, D_MODEL)),
                  _const_spec((D_MODEL, 2 * LANES)), _const_spec((1, LANES))],
        out_specs=[row(D_MODEL), h2_spec, *rt_specs],
        out_shape=[sds((t, D_MODEL), F32), h2_shape, *rt_shapes],
        compiler_params=_cparams(("parallel",)),
        name="merge",
    )(x2d, o_r, o_f, o_m, gates, wr, wf, wm, wo, g2, wrt, brt)


def _positions_kernel(route_ref, first_ref, pos_ref, *, tm):
    lane = lax.broadcasted_iota(jnp.int32, (tm, LANES), 1).astype(F32)
    for i in range(first_ref.shape[0]):
        route = route_ref[i * tm:(i + 1) * tm, :]
        first = jnp.sum(jnp.where(lane == route[:, 1:2], first_ref[i, 0:1, :], 0.0), axis=-1, keepdims=True)
        pos_ref[i * tm:(i + 1) * tm, :] = jnp.broadcast_to(first + route[:, 2:3], route.shape).astype(jnp.int32)


def _positions(route, first_row, tm, tiles_per_step):
    t = route.shape[0]
    rows = tm * tiles_per_step
    return pl.pallas_call(
        functools.partial(_positions_kernel, tm=tm),
        grid=(t // rows,),
        in_specs=[pl.BlockSpec((rows, E_PER_GROUP), lambda i: (i, 0)),
                  pl.BlockSpec((tiles_per_step, 8, LANES), lambda i: (i, 0, 0))],
        out_specs=pl.BlockSpec((rows, E_PER_GROUP), lambda i: (i, 0)),
        out_shape=jax.ShapeDtypeStruct((t, E_PER_GROUP), jnp.int32),
        compiler_params=_cparams(("parallel",)),
        name="positions",
    )(route, first_row)[:, 0]


def _group_experts(h, cw, wg_ref, wu_ref, wd_ref, act):
    for e in range(E_PER_GROUP):
        a = _dot(h, wg_ref[0, e])
        u = _dot(h, wu_ref[0, e])
        act[:, e * D_EXPERT:(e + 1) * D_EXPERT] = ((a * jax.nn.sigmoid(a)) * u * cw(e)).astype(BF16)
    return _dot(act[...], wd_ref[0])


N_SLABS = D_MODEL // LANES
SUBLANES = 8


def _to_token_tiles(ref, x):
    for s in range(N_SLABS):
        ref[pl.ds(s, x.shape[0], stride=N_SLABS), :] = x[:, s * LANES:(s + 1) * LANES]


def _landing_shape(rows):
    return (rows // SUBLANES, N_SLABS, SUBLANES, LANES)


def _from_landing(buf):
    rows = buf.shape[0] * SUBLANES
    return jnp.concatenate([buf[:, s].reshape(rows, LANES) for s in range(N_SLABS)], axis=1)


def _row_gather(idx_ref, base, src_hbm, dst, sem, straight_line=False):
    def body(i, _):
        for u in range(SUBLANES):
            pltpu.make_async_copy(src_hbm.at[idx_ref[base + i * SUBLANES + u]], dst.at[i, :, u, :], sem).start()
        return 0
    if straight_line:
        for i in range(dst.shape[0]):
            body(i, 0)
    else:
        lax.fori_loop(0, dst.shape[0], body, 0)


def _row_gather_wait(dst, sem):
    pltpu.make_async_copy(dst, dst, sem).wait()


def _moe_sorted_kernel(tg_ref, nvt_ref, pos_ref, h3_hbm, zeros_hbm, wrt_ref, brt_ref,
                       wg_ref, wu_ref, wd_ref, y_ref, xbuf, sem, hb, yacc, src_ref, clear_sem, *, tm):
    k = pl.program_id(0)
    nvt = nvt_ref[0]
    slot = lax.rem(k, 2)

    @pl.when(k == 0)
    def _():
        clear = pltpu.make_async_copy(zeros_hbm, src_ref, clear_sem)
        clear.start()
        clear.wait()

        def invert(t, _):
            src_ref[pos_ref[t]] = t
            return 0

        lax.fori_loop(0, pos_ref.shape[0], invert, 0, unroll=32)

    @pl.when((k == 0) & (nvt > 0))
    def _():
        _row_gather(src_ref, 0, h3_hbm, xbuf.at[0], sem.at[0])

    @pl.when(k < nvt)
    def _():
        _row_gather_wait(xbuf.at[slot], sem.at[slot])
        _row_gather(src_ref, jnp.minimum(k + 1, nvt - 1) * tm, h3_hbm, xbuf.at[1 - slot], sem.at[1 - slot],
                    straight_line=True)

        x = _from_landing(xbuf.at[slot])
        group = tg_ref[k]
        _, comb, _, _ = _route(_router_logits(x, wrt_ref, brt_ref), group.astype(F32))
        lane = lax.broadcasted_iota(jnp.int32, comb.shape, 1)
        hb[...] = x.astype(BF16)
        yacc[...] = jnp.zeros(yacc.shape, F32)
        col_max = jnp.max(comb, axis=0, keepdims=True)
        used = [jnp.max(jnp.where(lane[:1] == group * E_PER_GROUP + e, col_max, 0.0)) for e in range(E_PER_GROUP)]
        for e in range(E_PER_GROUP):
            @pl.when(used[e] > 0.0)
            def _(e=e):
                cw = jnp.sum(jnp.where(lane == group * E_PER_GROUP + e, comb, 0.0), axis=-1, keepdims=True)
                h = hb[...]
                a = _dot(h, wg_ref[0, e])
                u = _dot(h, wu_ref[0, e])
                act = ((a * jax.nn.sigmoid(a)) * u * cw).astype(BF16)
                yacc[...] += _dot(act, wd_ref[0, e * D_EXPERT:(e + 1) * D_EXPERT, :])

        _to_token_tiles(y_ref, yacc[...])

        @pl.when(k + 1 >= nvt)
        def _():
            _row_gather_wait(xbuf.at[1 - slot], sem.at[1 - slot])

    @pl.when(k >= nvt)
    def _():
        y_ref[...] = jnp.zeros(y_ref.shape, F32)


def _moe_sorted(h3, tile_group, n_valid_tiles, pos, n_tiles, tm, wrt, brt, wg, wu, wd):
    wspec = lambda shape: pl.BlockSpec(shape, lambda k, tg, nv, ps: (tg[k],) + (0,) * (len(shape) - 1))
    cspec = lambda shape: pl.BlockSpec(shape, lambda k, tg, nv, ps: (0,) * len(shape),
                                       pipeline_mode=pl.Buffered(1))
    return pl.pallas_call(
        functools.partial(_moe_sorted_kernel, tm=tm),
        grid_spec=pltpu.PrefetchScalarGridSpec(
            num_scalar_prefetch=3,
            grid=(n_tiles,),
            in_specs=[pl.BlockSpec(memory_space=pl.ANY), pl.BlockSpec(memory_space=pl.ANY),
                      cspec((D_MODEL, 2 * LANES)), cspec((1, LANES)),
                      wspec((1, E_PER_GROUP, D_MODEL, D_EXPERT)), wspec((1, E_PER_GROUP, D_MODEL, D_EXPERT)),
                      wspec((1, E_PER_GROUP * D_EXPERT, D_MODEL))],
            out_specs=pl.BlockSpec((tm * N_SLABS, LANES), lambda k, tg, nv, ps: (k, 0)),
            scratch_shapes=[pltpu.VMEM((2,) + _landing_shape(tm), F32), pltpu.SemaphoreType.DMA((2,)),
                            pltpu.VMEM((tm, D_MODEL), BF16), pltpu.VMEM((tm, D_MODEL), F32),
                            pltpu.SMEM((n_tiles * tm,), jnp.int32), pltpu.SemaphoreType.DMA(())]),
        out_shape=jax.ShapeDtypeStruct((n_tiles * tm * N_SLABS, LANES), F32),
        compiler_params=_cparams(("arbitrary",)),
        name="moe_sorted",
    )(tile_group, n_valid_tiles, pos, h3, jnp.zeros((n_tiles * tm,), jnp.int32), wrt, brt, wg, wu, wd)


def _combine_kernel(pos_ref, y3_hbm, x1_ref, o_ref, ybuf, sem, *, tm):
    k = pl.program_id(0)
    slot = lax.rem(k, 2)

    @pl.when(k == 0)
    def _():
        _row_gather(pos_ref, 0, y3_hbm, ybuf.at[0], sem.at[0])

    _row_gather_wait(ybuf.at[slot], sem.at[slot])

    @pl.when(k + 1 < pl.num_programs(0))
    def _():
        _row_gather(pos_ref, (k + 1) * tm, y3_hbm, ybuf.at[1 - slot], sem.at[1 - slot])

    o_ref[...] = x1_ref[...] + _from_landing(ybuf.at[slot])


def _combine(y3, pos, x1, tm):
    t = x1.shape[0]
    return pl.pallas_call(
        functools.partial(_combine_kernel, tm=tm),
        grid_spec=pltpu.PrefetchScalarGridSpec(
            num_scalar_prefetch=1,
            grid=(t // tm,),
            in_specs=[pl.BlockSpec(memory_space=pl.ANY), pl.BlockSpec((tm, D_MODEL), lambda k, ps: (k, 0))],
            out_specs=pl.BlockSpec((tm, D_MODEL), lambda k, ps: (k, 0)),
            scratch_shapes=[pltpu.VMEM((2,) + _landing_shape(tm), F32), pltpu.SemaphoreType.DMA((2,))]),
        out_shape=jax.ShapeDtypeStruct((t, D_MODEL), F32),
        compiler_params=_cparams(("arbitrary",)),
        name="combine",
    )(pos, y3, x1)


def _moe_kernel(h_ref, comb_ref, x1_ref, wg_ref, wu_ref, wd_ref, o_ref, act):
    g = pl.program_id(1)
    comb = comb_ref[0]
    y = _group_experts(h_ref[...], lambda e: comb[:, e:e + 1], wg_ref, wu_ref, wd_ref, act)

    @pl.when(g == 0)
    def _():
        o_ref[...] = x1_ref[...] + y

    @pl.when(g != 0)
    def _():
        o_ref[...] += y


def _moe(h2, comb, x1, tm, wg, wu, wd):
    t = h2.shape[0]
    return pl.pallas_call(
        _moe_kernel,
        grid=(t // tm, N_GROUPS),
        in_specs=[pl.BlockSpec((tm, D_MODEL), lambda i, g: (i, 0)),
                  pl.BlockSpec((1, tm, E_PER_GROUP), lambda i, g: (g, i, 0)),
                  pl.BlockSpec((tm, D_MODEL), lambda i, g: (i, 0)),
                  pl.BlockSpec((1, E_PER_GROUP, D_MODEL, D_EXPERT), lambda i, g: (g, 0, 0, 0)),
                  pl.BlockSpec((1, E_PER_GROUP, D_MODEL, D_EXPERT), lambda i, g: (g, 0, 0, 0)),
                  pl.BlockSpec((1, E_PER_GROUP * D_EXPERT, D_MODEL), lambda i, g: (g, 0, 0))],
        out_specs=pl.BlockSpec((tm, D_MODEL), lambda i, g: (i, 0)),
        out_shape=jax.ShapeDtypeStruct((t, D_MODEL), F32),
        scratch_shapes=[pltpu.VMEM((tm, E_PER_GROUP * D_EXPERT), BF16)],
        compiler_params=_cparams(("parallel", "arbitrary")),
        name="moe",
    )(h2, comb, x1, wg, wu, wd)


def _rope_tables(pos):
    half = DK_R // 2
    inv = ROPE_BASE ** (-jnp.arange(half, dtype=F32) / half)
    ang = pos.astype(F32)[:, None] * inv[None, :]
    c = jnp.cos(ang)
    s = jnp.sin(ang)
    return jnp.concatenate([c, c, c, c], axis=-1), jnp.concatenate([-s, s, -s, s], axis=-1)


def kernel(x_prompt, x_sample, state_ret, cache_fox_k, cache_fox_v, cache_fox_logf, cache_mem_k, cache_mem_v,
           mem_prompt, g_norm1, w_in, b_forget, g_fox_q, g_fox_k, g_mem_q, g_mem_in, w_mem_kv, g_mem_k,
           g_ret_out, w_br_ret, w_br_fox, w_br_mem, w_out, g_norm2, w_route_group, b_route_group,
           w_route_expert, b_route_expert, w_exp_gate, w_exp_up, w_exp_down):
    nb, seq, _ = x_prompt.shape
    nbs, n_new, _ = x_sample.shape
    past = cache_fox_k.shape[2]
    l = 0

    wi = w_in[l]
    o_ff = 2 * RQK_W + 2 * RV_W + 3 * FOX_W
    w_in_r = jnp.concatenate(
        [wi[:, :o_ff], wi[:, o_ff + H_F:], wi[:, o_ff:o_ff + H_F],
         jnp.zeros((D_MODEL, LANES - H_F), F32)], axis=1).astype(BF16)
    bf_pad = jnp.concatenate([b_forget[l], jnp.zeros((LANES - H_F,), F32)])[None, :]
    g1 = g_norm1[l][None, :]
    gfq = jnp.tile(g_fox_q[l], H_F)[None, :]
    gfk = jnp.tile(g_fox_k[l], H_F)[None, :]
    gmq = g_mem_q[l][None, :]
    hid = jnp.arange(FOX_W) // D_F
    bd = jnp.where(hid[:, None] == hid[None, :], 1.0 / D_F, 0.0).astype(BF16)
    wr = w_br_ret[l].astype(BF16)
    wf = w_br_fox[l].astype(BF16)
    wm = w_br_mem[l].astype(BF16)
    wo = w_out[l].astype(BF16)
    g2 = g_norm2[l][None, :]
    n_e = N_GROUPS * E_PER_GROUP
    wrt = jnp.concatenate([w_route_expert[l], w_route_group[l],
                           jnp.zeros((D_MODEL, LANES - n_e - N_GROUPS), F32)], axis=1)
    wrt_hi = wrt.astype(BF16)
    wrt2 = jnp.concatenate([wrt_hi, (wrt - wrt_hi.astype(F32)).astype(BF16)], axis=1)
    brt = jnp.concatenate([b_route_expert[l], b_route_group[l],
                           jnp.zeros((LANES - n_e - N_GROUPS,), F32)])[None, :]
    wg = w_exp_gate[l].astype(BF16)
    wu = w_exp_up[l].astype(BF16)
    wd = w_exp_down[l].astype(BF16).reshape(N_GROUPS, E_PER_GROUP * D_EXPERT, D_MODEL)
    gro = g_ret_out[l][None, :]
    prep_w = (g1, w_in_r, bf_pad, gfq, gfk, gmq, bd)
    merge_w = (wr, wf, wm, wo, g2, wrt2, brt)

    tm = 512
    xp = x_prompt.reshape(nb * seq, D_MODEL)
    (rq, rk, rv, rg, fqt, fkt, fvt, lft, mq, gates, fkb, lf) = _prep(
        xp, _rope_tables(jnp.arange(seq)), tm, seq // tm, True, *prep_w)
    o_r, s_fin = _retention(rq, rk, rv, rg, nb, seq, 256, gro, None)
    fcum = _cumsum_cols(lf.reshape(nb, seq, H_F), 256)
    fcum = jnp.swapaxes(fcum.reshape(nb, seq, H_F // 2, 2), 1, 2).reshape(nb * (H_F // 2) * seq, 2)
    o_f = _fox_prompt(fqt, fkb, fvt, fcum, nb, seq, 256)
    mk, mv = _memkv(mem_prompt.reshape(nb * N_MEM, D_MODEL), nb, g_mem_in[l][None, :],
                    w_mem_kv[l].astype(BF16), g_mem_k[l][None, :])
    o_m = _memattn(mq, mk, mv, nb, seq, tm)
    x1, h3, route, counts = _merge(xp, o_r, o_f, o_m, gates, tm, True, *merge_w)
    t_p = nb * seq
    n_tiles = t_p // tm + N_GROUPS
    n_cls = N_GROUPS * N_PAIRS
    cnt = counts[:, 0, :n_cls].astype(jnp.int32)
    before_tile = jnp.cumsum(cnt, axis=0) - cnt
    cls_tot = jnp.sum(cnt, axis=0)
    grp_tot = jnp.sum(cls_tot.reshape(N_GROUPS, N_PAIRS), axis=1)
    tiles_g = (grp_tot + tm - 1) // tm
    tile_end = jnp.cumsum(tiles_g)
    row_start = (tile_end - tiles_g) * tm
    in_grp = cls_tot.reshape(N_GROUPS, N_PAIRS)
    cls_start = (row_start[:, None] + jnp.cumsum(in_grp, axis=1) - in_grp).reshape(n_cls)
    first_row = jnp.pad((cls_start[None, :] + before_tile).astype(F32), ((0, 0), (0, LANES - n_cls)))
    pos = _positions(route, jnp.broadcast_to(first_row[:, None, :], (t_p // tm, 8, LANES)), tm,
                     min(8, t_p // tm))
    tile_ids = jnp.arange(n_tiles, dtype=jnp.int32)
    tile_group = jnp.minimum(jnp.sum((tile_ids[:, None] >= tile_end[None, :]).astype(jnp.int32), axis=1),
                             N_GROUPS - 1)
    y3 = _moe_sorted(h3.reshape(t_p, N_SLABS, LANES), tile_group, tile_end[-1:], pos, n_tiles, tm,
                     wrt2, brt, wg, wu, wd)
    y_prompt = _combine(y3.reshape(n_tiles * tm, N_SLABS, LANES), pos, x1, tm).reshape(nb, seq, D_MODEL)

    ts = nbs * n_new
    xs = x_sample.reshape(ts, D_MODEL)
    pos_s = jnp.tile(past + jnp.arange(n_new), nbs)
    (rq_s, rk_s, rv_s, rg_s, fq_s, fk_s, fv_s, lf_s, mq_s, gates_s) = _prep(
        xs, _rope_tables(pos_s), ts, 1, False, *prep_w)
    o_r_s, s_new = _retention(rq_s, rk_s, rv_s, rg_s, nbs, n_new, n_new, gro, state_ret[l])
    pad = (-(past + n_new)) % 256
    lf_rows = jnp.concatenate([jnp.swapaxes(cache_fox_logf[l], 1, 2),
                               jnp.swapaxes(lf_s.reshape(nbs, n_new, H_F), 1, 2),
                               jnp.zeros((nbs, H_F, pad), F32)], axis=2).reshape(nbs * H_F, past + n_new + pad)
    f_all = _cumsum(lf_rows, 256)
    feat_major = lambda c: jnp.transpose(c, (0, 2, 3, 1)).reshape(nbs * FOX_W, past)
    o_f_s = _fox_sample(fq_s, feat_major(cache_fox_k[l]), feat_major(cache_fox_v[l]), fk_s, fv_s,
                        f_all[:, :past].reshape(nbs, H_F, past),
                        f_all[:, past:past + n_new].reshape(nbs, H_F, n_new), nbs, n_new, past)
    o_m_s = _memattn(mq_s, cache_mem_k[l].reshape(nbs * N_MEM, MEM_W),
                     cache_mem_v[l].reshape(nbs * N_MEM, MEM_W), nbs, n_new, n_new)
    x1_s, h2_s, comb_s = _merge(xs, o_r_s, o_f_s, o_m_s, gates_s, ts, False, *merge_w)
    y_sample = _moe(h2_s, comb_s, x1_s, ts, wg, wu, wd).reshape(nbs, n_new, D_MODEL)

    token_major = lambda a: jnp.transpose(a.reshape(nb, H_F, D_F, seq), (0, 3, 1, 2))

    return (y_prompt, y_sample,
            s_fin[None], token_major(fkt)[None], token_major(fvt)[None],
            jnp.swapaxes(lft.reshape(nb, H_F, seq), 1, 2)[None],
            mk.reshape(1, nb, N_MEM, H_M, D_M), mv.reshape(1, nb, N_MEM, H_M, D_M),
            s_new[None], fk_s.reshape(1, nbs, n_new, H_F, D_F), fv_s.reshape(1, nbs, n_new, H_F, D_F),
            lf_s.reshape(1, nbs, n_new, H_F))
```

```python
import functools

import jax
import jax.numpy as jnp
from jax import lax
from jax.experimental import pallas as pl
from jax.experimental.pallas import tpu as pltpu

F32 = jnp.float32
BF16 = jnp.bfloat16

D_MODEL = 1024
H_R, DK_R, DV_R = 4, 64, 128
H_F, D_F = 8, 64
H_M, D_M = 4, 128
N_MEM = 256
N_GROUPS, E_PER_GROUP, D_EXPERT = 4, 8, 256
ROPE_BASE = 10000.0
EPS = 1e-6
LOG2E = 1.4426950408889634

RQK_W = H_R * DK_R
RV_W = H_R * DV_R
FOX_W = H_F * D_F
MEM_W = H_M * D_M
GATE_W = 3 * D_MODEL
LANES = 128
MXU_DIM = 256

C_RQ = 0
C_RK = C_RQ + RQK_W
C_RV = C_RK + RQK_W
C_RG = C_RV + RV_W
C_FQ = C_RG + RV_W
C_FK = C_FQ + FOX_W
C_FV = C_FK + FOX_W
C_MQ = C_FV + FOX_W
C_GT = C_MQ + MEM_W
C_FF = C_GT + GATE_W
N_IN_PAD = C_FF + LANES

VMEM_LIMIT = 56 * 1024 * 1024


def _cparams(sem):
    return pltpu.CompilerParams(dimension_semantics=sem, vmem_limit_bytes=VMEM_LIMIT)


def _const_spec(shape):
    nd = len(shape)
    return pl.BlockSpec(shape, lambda *_: (0,) * nd, pipeline_mode=pl.Buffered(1))


def _dot(a, b):
    return jnp.dot(a, b, preferred_element_type=F32)


def _dot_nt(a, b):
    return lax.dot_general(a, b, (((1,), (1,)), ((), ())), preferred_element_type=F32)


def _dot_tn(a, b):
    return lax.dot_general(a, b, (((0,), (0,)), ((), ())), preferred_element_type=F32)


def _lane_rmsnorm(z):
    return z * lax.rsqrt(jnp.mean(z * z, axis=-1, keepdims=True) + EPS)


def _prep_kernel(x_ref, g1_ref, w_ref, cos_ref, sin_ref, bf_ref, gfq_ref, gfk_ref, gmq_ref, bd_ref,
                 rq_ref, rk_ref, rv_ref, rg_ref, fq_ref, fk_ref, fv_ref, lf_ref, mq_ref, gt_ref, *extra,
                 seq_minor):
    x = x_ref[...]
    h = (x * lax.rsqrt(jnp.mean(x * x, axis=-1, keepdims=True) + EPS) * g1_ref[...]).astype(BF16)

    def proj(c0, width):
        return _dot(h, w_ref[:, c0:c0 + width])

    cos = cos_ref[...]
    sin = sin_ref[...]
    lane = lax.broadcasted_iota(jnp.int32, cos.shape, 1)
    first_half = (lane % DK_R) < (DK_R // 2)

    def rope(z):
        swapped = jnp.where(first_half, pltpu.roll(z, LANES - DK_R // 2, 1), pltpu.roll(z, DK_R // 2, 1))
        return z * cos + swapped * sin

    zq = proj(C_RQ, RQK_W)
    zk = proj(C_RK, RQK_W)
    for p in range(RQK_W // LANES):
        sl = slice(p * LANES, (p + 1) * LANES)
        rq_ref[:, sl] = rope(zq[:, sl])
        rk_ref[:, sl] = rope(zk[:, sl]) * (DK_R ** -0.5)
    rv_ref[...] = proj(C_RV, RV_W)
    rg_ref[...] = proj(C_RG, RV_W)

    def head64_norm(z, g_ref):
        zz = (z * z).astype(BF16)
        w = bd_ref.shape[0]
        ms = jnp.concatenate([_dot(zz[:, c:c + w], bd_ref[...]) for c in range(0, FOX_W, w)], axis=1)
        return z * lax.rsqrt(ms + EPS) * g_ref[...]

    fq = head64_norm(proj(C_FQ, FOX_W), gfq_ref) * (D_F ** -0.5 * LOG2E)
    fk = head64_norm(proj(C_FK, FOX_W), gfk_ref)
    fv = proj(C_FV, FOX_W)
    if seq_minor:
        fq_ref[...] = fq.T.astype(BF16)
        fk_ref[...] = fk.T
        fv_ref[...] = fv.T
        extra[0][...] = fk.astype(BF16)
    else:
        fq_ref[...] = fq.astype(BF16)
        fk_ref[...] = fk
        fv_ref[...] = fv

    zm = proj(C_MQ, MEM_W)
    for hh in range(H_M):
        sl = slice(hh * D_M, (hh + 1) * D_M)
        mq_ref[:, sl] = (_lane_rmsnorm(zm[:, sl]) * gmq_ref[...]).astype(BF16)

    for b in range(3):
        gt_ref[:, b * D_MODEL:(b + 1) * D_MODEL] = jax.nn.sigmoid(proj(C_GT + b * D_MODEL, D_MODEL)).astype(BF16)

    v = proj(C_FF, LANES) + bf_ref[...]
    lf = jnp.minimum(v, 0.0) - jnp.log1p(jnp.exp(-jnp.abs(v)))
    if seq_minor:
        lf_ref[...] = lf.T[:H_F, :]
        extra[1][...] = lf[:, :H_F]
    else:
        lf_ref[...] = lf[:, :H_F]


def _prep(x2d, tables, tm, n_pos_tiles, seq_minor, g1, w_in_r, bf_pad, gfq, gfk, gmq, bd):
    t = x2d.shape[0]
    cos_t, sin_t = tables
    row = lambda w: pl.BlockSpec((tm, w), lambda i: (i, 0))
    pos = pl.BlockSpec((tm, LANES), lambda i: (i % n_pos_tiles, 0))
    sds = jax.ShapeDtypeStruct
    if seq_minor:
        nb, seq = t // (n_pos_tiles * tm), n_pos_tiles * tm
        fm = lambda rows_: pl.BlockSpec((rows_, tm), lambda i: (i // n_pos_tiles, i % n_pos_tiles))
        fox_shapes = [sds((nb * FOX_W, seq), BF16), sds((nb * FOX_W, seq), F32), sds((nb * FOX_W, seq), F32),
                      sds((nb * H_F, seq), F32)]
        fox_specs = [fm(FOX_W), fm(FOX_W), fm(FOX_W), fm(H_F)]
        extra_shapes, extra_specs = [sds((t, FOX_W), BF16), sds((t, H_F), F32)], [row(FOX_W), row(H_F)]
    else:
        fox_shapes = [sds((t, FOX_W), BF16), sds((t, FOX_W), F32), sds((t, FOX_W), F32), sds((t, H_F), F32)]
        fox_specs = [row(FOX_W), row(FOX_W), row(FOX_W), row(H_F)]
        extra_shapes, extra_specs = [], []
    out_shapes = [sds((t, RQK_W), F32), sds((t, RQK_W), F32), sds((t, RV_W), F32), sds((t, RV_W), F32),
                  *fox_shapes, sds((t, MEM_W), BF16), sds((t, GATE_W), BF16), *extra_shapes]
    out_specs = [row(RQK_W), row(RQK_W), row(RV_W), row(RV_W), *fox_specs, row(MEM_W), row(GATE_W),
                 *extra_specs]
    return pl.pallas_call(
        functools.partial(_prep_kernel, seq_minor=seq_minor),
        grid=(t // tm,),
        in_specs=[row(D_MODEL), _const_spec((1, D_MODEL)), _const_spec((D_MODEL, N_IN_PAD)), pos, pos,
                  _const_spec((1, LANES)), _const_spec((1, FOX_W)), _const_spec((1, FOX_W)),
                  _const_spec((1, D_M)), _const_spec((MXU_DIM, MXU_DIM))],
        out_specs=out_specs,
        out_shape=out_shapes,
        compiler_params=_cparams(("parallel",)),
        name="prep",
    )(x2d, g1, w_in_r, cos_t, sin_t, bf_pad, gfq, gfk, gmq, bd)


def _ret_kernel(*refs, has_init):
    if has_init:
        (rq_ref, rk_ref, rv_ref, rg_ref, dmat_ref, qdec_ref, kdec_ref, gpow_ref, gro_ref, s0_ref,
         o_ref, sfin_ref, state) = refs
    else:
        (rq_ref, rk_ref, rv_ref, rg_ref, dmat_ref, qdec_ref, kdec_ref, gpow_ref, gro_ref,
         o_ref, sfin_ref, state) = refs
    c = pl.program_id(1)

    @pl.when(c == 0)
    def _():
        state[...] = jnp.zeros(state.shape, F32)
        if has_init:
            for h in range(H_R):
                r0 = DK_R * (h % 2)
                state[h, r0:r0 + DK_R, :] = s0_ref[0, h]

    lane = lax.broadcasted_iota(jnp.int32, (rq_ref.shape[0], LANES), 1)
    for p in range(H_R // 2):
        sl = slice(p * LANES, (p + 1) * LANES)
        q2 = rq_ref[:, sl]
        k2 = rk_ref[:, sl]
        kd2 = k2 * kdec_ref[p]
        for hh in range(2):
            h = 2 * p + hh
            mine = (lane >= hh * DK_R) & (lane < (hh + 1) * DK_R)
            qm = jnp.where(mine, q2, 0.0).astype(BF16)
            kdm = jnp.where(mine, kd2, 0.0).astype(BF16)
            v = rv_ref[:, h * DV_R:(h + 1) * DV_R].astype(BF16)
            sc = _dot_nt(qm, k2.astype(BF16)) * dmat_ref[h]
            s_old = state[h]
            o = _dot(sc.astype(BF16), v) + _dot(qm, s_old.astype(BF16)) * qdec_ref[h]
            state[h] = gpow_ref[h] * s_old + _dot_tn(kdm, v)
            normed = _lane_rmsnorm(o) * gro_ref[...]
            rg = rg_ref[:, h * DV_R:(h + 1) * DV_R]
            o_ref[:, h * DV_R:(h + 1) * DV_R] = (normed * (rg * jax.nn.sigmoid(rg))).astype(BF16)

    @pl.when(c == pl.num_programs(1) - 1)
    def _():
        for h in range(H_R):
            r0 = DK_R * (h % 2)
            sfin_ref[0, h] = state[h, r0:r0 + DK_R, :]


def _retention(rq, rk, rv, rg, n_batch, seq, ch, g_ret_out, state0):
    nc = seq // ch
    lg = jnp.log1p(-jnp.exp2(-5.0 - jnp.arange(H_R, dtype=F32)))
    idx = jnp.arange(ch, dtype=F32)
    diff = idx[:, None] - idx[None, :]
    causal = diff >= 0
    dmat = jnp.where(causal[None], jnp.exp(jnp.where(causal, diff, 0.0)[None] * lg[:, None, None]), 0.0)
    q_dec = jnp.exp((idx + 1.0)[None, :] * lg[:, None])
    k_dec = jnp.exp((ch - 1.0 - idx)[None, :] * lg[:, None])
    qdec = jnp.broadcast_to(q_dec[:, :, None], (H_R, ch, DV_R))
    kdec = jnp.broadcast_to(k_dec[:, :, None], (H_R, ch, DK_R))
    kdec = kdec.reshape(H_R // 2, 2, ch, DK_R).transpose(0, 2, 1, 3).reshape(H_R // 2, ch, LANES)
    gpow = jnp.broadcast_to(jnp.exp(ch * lg)[:, None, None], (H_R, 1, DV_R))

    has_init = state0 is not None
    blk = lambda w: pl.BlockSpec((ch, w), lambda b, c: (b * nc + c, 0))
    in_specs = [blk(RQK_W), blk(RQK_W), blk(RV_W), blk(RV_W),
                _const_spec((H_R, ch, ch)), _const_spec((H_R, ch, DV_R)),
                _const_spec((H_R // 2, ch, LANES)), _const_spec((H_R, 1, DV_R)), _const_spec((1, DV_R))]
    args = [rq, rk, rv, rg, dmat, qdec, kdec, gpow, g_ret_out]
    if has_init:
        in_specs.append(pl.BlockSpec((1, H_R, DK_R, DV_R), lambda b, c: (b, 0, 0, 0)))
        args.append(state0)
    return pl.pallas_call(
        functools.partial(_ret_kernel, has_init=has_init),
        grid=(n_batch, nc),
        in_specs=in_specs,
        out_specs=[blk(RV_W), pl.BlockSpec((1, H_R, DK_R, DV_R), lambda b, c: (b, 0, 0, 0))],
        out_shape=[jax.ShapeDtypeStruct((n_batch * seq, RV_W), BF16),
                   jax.ShapeDtypeStruct((n_batch, H_R, DK_R, DV_R), F32)],
        scratch_shapes=[pltpu.VMEM((H_R, LANES, DV_R), F32)],
        compiler_params=_cparams(("parallel", "arbitrary")),
        name="retention",
    )(*args)


def _split3(x):
    hi = x.astype(BF16)
    r1 = x - hi.astype(F32)
    mid = r1.astype(BF16)
    lo = (r1 - mid.astype(F32)).astype(BF16)
    return hi, mid, lo


def _cumsum_kernel(x_ref, o_ref, *, blk):
    rows, n = x_ref.shape
    r = lax.broadcasted_iota(jnp.int32, (blk, blk), 0)
    c = lax.broadcasted_iota(jnp.int32, (blk, blk), 1)
    tri = jnp.where(r <= c, 1.0, 0.0).astype(BF16)
    carry = jnp.zeros((rows, 1), F32)
    for i in range(n // blk):
        hi, mid, lo = _split3(x_ref[:, i * blk:(i + 1) * blk])
        cum = (_dot(hi, tri) + _dot(mid, tri)) + _dot(lo, tri) + carry
        o_ref[:, i * blk:(i + 1) * blk] = cum
        carry = cum[:, blk - 1:blk]


def _cumsum_cols_kernel(x_ref, o_ref, *, blk):
    n = x_ref.shape[1]
    r = lax.broadcasted_iota(jnp.int32, (blk, blk), 0)
    c = lax.broadcasted_iota(jnp.int32, (blk, blk), 1)
    tri = jnp.where(r >= c, 1.0, 0.0).astype(BF16)
    carry = jnp.zeros((1, x_ref.shape[2]), F32)
    for i in range(n // blk):
        hi, mid, lo = _split3(x_ref[0, i * blk:(i + 1) * blk, :])
        cum = (_dot(tri, hi) + _dot(tri, mid)) + _dot(tri, lo) + carry
        o_ref[0, i * blk:(i + 1) * blk, :] = cum
        carry = cum[blk - 1:blk, :]


def _cumsum_cols(x, blk):
    nb, t, w = x.shape
    spec = pl.BlockSpec((1, t, w), lambda b: (b, 0, 0))
    return pl.pallas_call(
        functools.partial(_cumsum_cols_kernel, blk=blk),
        grid=(nb,),
        in_specs=[spec], out_specs=spec,
        out_shape=jax.ShapeDtypeStruct(x.shape, F32),
        compiler_params=_cparams(("parallel",)),
        name="cumsum_cols",
    )(x)


def _cumsum(x, blk):
    r, t = x.shape
    spec = pl.BlockSpec((H_F, t), lambda b: (b, 0))
    return pl.pallas_call(
        functools.partial(_cumsum_kernel, blk=blk),
        grid=(r // H_F,),
        in_specs=[spec], out_specs=spec,
        out_shape=jax.ShapeDtypeStruct(x.shape, F32),
        compiler_params=_cparams(("parallel",)),
        name="cumsum",
    )(x)


def _head_rows_mask(shape, hh):
    sub = lax.broadcasted_iota(jnp.int32, shape, 0)
    return (sub >= hh * D_F) & (sub < (hh + 1) * D_F)


N_AUG = 3
V_ROWS = D_F + 16


def _fox_kernel(qt_ref, k_ref, vt_ref, fcum_ref, o_ref, qa, ka, vb, *, tq):
    seq = k_ref.shape[0]
    tk = tq
    qt = qt_ref[...].astype(F32)
    k = k_ref[...].astype(F32)
    sub = lax.broadcasted_iota(jnp.int32, qt.shape, 0)
    lane = lax.broadcasted_iota(jnp.int32, k.shape, 1)
    for hh in range(2):
        own, oth = hh * D_F, (1 - hh) * D_F
        qa[hh] = jnp.where((sub >= own) & (sub < own + D_F), qt,
                           jnp.where((sub >= oth) & (sub < oth + N_AUG), 1.0, 0.0)).astype(BF16)
        bias = fcum_ref[:, hh:hh + 1] * (-LOG2E)
        kk = jnp.where((lane >= own) & (lane < own + D_F), k, 0.0)
        for a, piece in enumerate(_split3(bias)):
            kk = jnp.where(lane == oth + a, piece.astype(F32), kk)
        ka[hh] = kk.astype(BF16)
    for hh in range(2):
        vb[hh, :D_F, :] = vt_ref[hh * D_F:(hh + 1) * D_F, :].astype(BF16)
        vb[hh, D_F:, :] = jnp.ones((V_ROWS - D_F, seq), BF16)
    causal = (lax.broadcasted_iota(jnp.int32, (tk, tq), 1) >= lax.broadcasted_iota(jnp.int32, (tk, tq), 0))

    pairs = [(i, j) for i in range(seq // tq) for j in range(i + 1)]

    def stage_a(i, j):
        return tuple(_dot(ka[hh, j * tk:(j + 1) * tk, :], qa[hh, :, i * tq:(i + 1) * tq]) for hh in range(2))

    def stage_b(i, j, ts, maxes):
        new_maxes, probs = [], []
        for hh in range(2):
            t = jnp.where(causal, ts[hh], -jnp.inf) if j == i else ts[hh]
            t_max = jnp.max(t, axis=0, keepdims=True)
            if j == 0:
                m_new, alpha = t_max, None
            else:
                m_new = jnp.maximum(maxes[hh], t_max)
                alpha = jnp.exp2(maxes[hh] - m_new)
            new_maxes.append(m_new)
            probs.append((jnp.exp2(t - m_new).astype(BF16), alpha))
        return new_maxes, probs

    def stage_c(i, j, probs, accs):
        out = []
        for hh in range(2):
            p, alpha = probs[hh]
            pv = _dot(vb[hh, :, j * tk:(j + 1) * tk], p)
            out.append(pv if j == 0 else alpha * accs[hh] + pv)
        return out

    scores, probs, maxes, accs = {}, {}, None, None
    for s in range(len(pairs) + 2):
        if s < len(pairs):
            scores[s] = stage_a(*pairs[s])
        if 0 <= s - 1 < len(pairs):
            maxes, probs[s - 1] = stage_b(*pairs[s - 1], scores.pop(s - 1), maxes)
        if 0 <= s - 2 < len(pairs):
            i, j = pairs[s - 2]
            accs = stage_c(i, j, probs.pop(s - 2), accs)
            if j == i:
                o_t = jnp.concatenate([a[:D_F] / a[D_F:D_F + 1] for a in accs], axis=0)
                o_ref[i * tq:(i + 1) * tq, :] = o_t.T.astype(BF16)


def _fox_prompt(fqt, fkb, fvt, fcum, n_batch, seq, tq):
    npair = H_F // 2
    fm_spec = pl.BlockSpec((LANES, seq), lambda b, p: (b * npair + p, 0))
    tok_spec = pl.BlockSpec((seq, LANES), lambda b, p: (b, p))
    return pl.pallas_call(
        functools.partial(_fox_kernel, tq=tq),
        grid=(n_batch, npair),
        in_specs=[fm_spec, tok_spec, fm_spec, pl.BlockSpec((seq, 2), lambda b, p: (b * npair + p, 0))],
        out_specs=tok_spec,
        out_shape=jax.ShapeDtypeStruct((n_batch * seq, FOX_W), BF16),
        scratch_shapes=[pltpu.VMEM((2, LANES, seq), BF16), pltpu.VMEM((2, seq, LANES), BF16),
                        pltpu.VMEM((2, V_ROWS, seq), BF16)],
        compiler_params=_cparams(("parallel", "parallel")),
        name="fox_prompt",
    )(fqt, fkb, fvt, fcum)


def _fox_sample_kernel(q_ref, kct_ref, vct_ref, kn_ref, vn_ref, frc_ref, frn_ref, o_ref):
    n = q_ref.shape[0]
    lane = lax.broadcasted_iota(jnp.int32, (n, LANES), 1)
    rows = lax.broadcasted_iota(jnp.int32, (n, n), 0)
    cols = lax.broadcasted_iota(jnp.int32, (n, n), 1)
    for pair in range(H_F // 2):
        sl = slice(pair * LANES, (pair + 1) * LANES)
        q2 = q_ref[:, sl]
        kct = kct_ref[sl, :]
        vct = vct_ref[sl, :].astype(BF16)
        kn = kn_ref[:, sl].astype(BF16)
        vn = vn_ref[:, sl].astype(BF16)
        outs = []
        for hh in range(2):
            h = 2 * pair + hh
            mine = (lane >= hh * D_F) & (lane < (hh + 1) * D_F)
            kc = jnp.where(_head_rows_mask(kct.shape, hh), kct, 0.0).astype(BF16)
            t_c = _dot(q2, kc) - frc_ref[0, h:h + 1, :] * LOG2E
            t_n = _dot_nt(jnp.where(mine, q2, jnp.zeros_like(q2)), kn) - frn_ref[0, h:h + 1, :] * LOG2E
            t_n = jnp.where(rows >= cols, t_n, -jnp.inf)
            m = jnp.maximum(jnp.max(t_c, axis=-1, keepdims=True), jnp.max(t_n, axis=-1, keepdims=True))
            p_c = jnp.exp2(t_c - m)
            p_n = jnp.exp2(t_n - m)
            l = jnp.sum(p_c, axis=-1, keepdims=True) + jnp.sum(p_n, axis=-1, keepdims=True)
            acc = _dot_nt(p_c.astype(BF16), vct) + _dot(p_n.astype(BF16), vn)
            outs.append(acc / l)
        o_ref[:, sl] = jnp.where(lane < D_F, outs[0], outs[1]).astype(BF16)


def _fox_sample(fq, kct, vct, kn, vn, frow_c, frow_n, n_batch, n_new, past):
    tok = pl.BlockSpec((n_new, FOX_W), lambda b: (b, 0))
    cache = pl.BlockSpec((FOX_W, past), lambda b: (b, 0))
    return pl.pallas_call(
        _fox_sample_kernel,
        grid=(n_batch,),
        in_specs=[tok, cache, cache, tok, tok,
                  pl.BlockSpec((1, H_F, past), lambda b: (b, 0, 0)),
                  pl.BlockSpec((1, H_F, n_new), lambda b: (b, 0, 0))],
        out_specs=tok,
        out_shape=jax.ShapeDtypeStruct((n_batch * n_new, FOX_W), BF16),
        compiler_params=_cparams(("parallel",)),
        name="fox_sample",
    )(fq, kct, vct, kn, vn, frow_c, frow_n)


def _memkv_kernel(m_ref, gin_ref, w_ref, gk_ref, k_ref, v_ref):
    x = m_ref[...]
    h = (x * lax.rsqrt(jnp.mean(x * x, axis=-1, keepdims=True) + EPS) * gin_ref[...]).astype(BF16)
    zk = _dot(h, w_ref[:, :MEM_W])
    for hh in range(H_M):
        sl = slice(hh * D_M, (hh + 1) * D_M)
        k_ref[:, sl] = _lane_rmsnorm(zk[:, sl]) * gk_ref[...]
    v_ref[...] = _dot(h, w_ref[:, MEM_W:])


def _memkv(mem2d, n_batch, g_mem_in, w_mem_kv, g_mem_k):
    blk = lambda w: pl.BlockSpec((N_MEM, w), lambda b: (b, 0))
    return pl.pallas_call(
        _memkv_kernel,
        grid=(n_batch,),
        in_specs=[blk(D_MODEL), _const_spec((1, D_MODEL)), _const_spec((D_MODEL, 2 * MEM_W)),
                  _const_spec((1, D_M))],
        out_specs=[blk(MEM_W), blk(MEM_W)],
        out_shape=[jax.ShapeDtypeStruct((n_batch * N_MEM, MEM_W), F32)] * 2,
        compiler_params=_cparams(("parallel",)),
        name="memkv",
    )(mem2d, g_mem_in, w_mem_kv, g_mem_k)


def _mem_attend(q_ref, k_ref, v_ref):
    for hh in range(H_M):
        sl = slice(hh * D_M, (hh + 1) * D_M)
        s = _dot_nt(q_ref[:, sl], k_ref[:, sl].astype(BF16)) * (D_M ** -0.5)
        m = jnp.max(s, axis=-1, keepdims=True)
        p = jnp.exp(s - m)
        l = jnp.sum(p, axis=-1, keepdims=True)
        yield sl, (_dot(p.astype(BF16), v_ref[:, sl].astype(BF16)) / l).astype(BF16)


def _memattn_kernel(q_ref, k_ref, v_ref, o_ref):
    for sl, o in _mem_attend(q_ref, k_ref, v_ref):
        o_ref[:, sl] = o


def _memattn(mq, mk, mv, n_batch, seq, tq):
    nq = seq // tq
    return pl.pallas_call(
        _memattn_kernel,
        grid=(n_batch, nq),
        in_specs=[pl.BlockSpec((tq, MEM_W), lambda b, i: (b * nq + i, 0)),
                  pl.BlockSpec((N_MEM, MEM_W), lambda b, i: (b, 0)),
                  pl.BlockSpec((N_MEM, MEM_W), lambda b, i: (b, 0))],
        out_specs=pl.BlockSpec((tq, MEM_W), lambda b, i: (b * nq + i, 0)),
        out_shape=jax.ShapeDtypeStruct((n_batch * seq, MEM_W), BF16),
        compiler_params=_cparams(("parallel", "arbitrary")),
        name="memattn",
    )(mq, mk, mv)


def _split2(x):
    hi = x.astype(BF16)
    return hi, (x - hi.astype(F32)).astype(BF16)


N_EXPERTS = N_GROUPS * E_PER_GROUP


N_PAIRS = E_PER_GROUP * (E_PER_GROUP - 1) // 2


def _router_logits(h2, wrt_ref, brt_ref):
    h_hi, h_lo = _split2(h2)
    z = _dot(h_hi, wrt_ref[...])
    return (z[:, :LANES] + z[:, LANES:]) + _dot(h_lo, wrt_ref[:, :LANES]) + brt_ref[...]


def _route(logits, group=None):
    lane = lax.broadcasted_iota(jnp.int32, logits.shape, 1).astype(F32)
    neg = -jnp.inf
    first_idx = lambda mask: jnp.min(jnp.where(mask, lane, float(LANES)), axis=-1, keepdims=True)

    is_g = (lane >= N_EXPERTS) & (lane < N_EXPERTS + N_GROUPS)
    lg = jnp.where(is_g, logits, neg)
    mg = jnp.max(lg, axis=-1, keepdims=True)
    if group is None:
        group = first_idx(lg == mg) - N_EXPERTS
        p_sel = 1.0 / jnp.sum(jnp.exp(lg - mg), axis=-1, keepdims=True)
    else:
        lsel = jnp.max(jnp.where(lane == group + N_EXPERTS, logits, neg), axis=-1, keepdims=True)
        p_sel = jnp.exp(lsel - mg) / jnp.sum(jnp.exp(lg - mg), axis=-1, keepdims=True)

    in_grp = (lane >= group * E_PER_GROUP) & (lane < (group + 1) * E_PER_GROUP)
    le = jnp.where(in_grp, logits, neg)
    v1 = jnp.max(le, axis=-1, keepdims=True)
    i1 = first_idx(le == v1)
    le2 = jnp.where(lane == i1, neg, le)
    v2 = jnp.max(le2, axis=-1, keepdims=True)
    i2 = first_idx(le2 == v2)
    e2 = jnp.exp(v2 - v1)
    w1 = p_sel / (1.0 + e2)
    w2 = p_sel * e2 / (1.0 + e2)
    return group, jnp.where(lane == i1, w1, 0.0) + jnp.where(lane == i2, w2, 0.0), i1, i2


def _merge_kernel(*refs, sorted_moe):
    if sorted_moe:
        (x_ref, or_ref, of_ref, mq_ref, mk_ref, mv_ref, gt_ref, wr_ref, wf_ref, wm_ref, wo_ref, g2_ref,
         wrt_ref, brt_ref, x1_ref, h2_ref, route_ref, *counts_ref) = refs
        o_m = jnp.concatenate([o for _, o in _mem_attend(mq_ref, mk_ref, mv_ref)], axis=1)
    else:
        (x_ref, or_ref, of_ref, om_ref, gt_ref, wr_ref, wf_ref, wm_ref, wo_ref, g2_ref,
         wrt_ref, brt_ref, x1_ref, h2_ref, route_ref) = refs
        o_m = om_ref[...]
    g = lambda b: gt_ref[:, b * D_MODEL:(b + 1) * D_MODEL].astype(F32)
    merged = (g(0) * _dot(or_ref[...], wr_ref[...]) + g(1) * _dot(of_ref[...], wf_ref[...])
              + g(2) * _dot(o_m, wm_ref[...]))
    x1 = x_ref[...] + _dot(merged.astype(BF16), wo_ref[...])
    x1_ref[...] = x1
    h2 = x1 * lax.rsqrt(jnp.mean(x1 * x1, axis=-1, keepdims=True) + EPS) * g2_ref[...]
    group, comb, i1, i2 = _route(_router_logits(h2, wrt_ref, brt_ref))
    if sorted_moe:
        tm = h2.shape[0]
        _to_token_tiles(h2_ref, h2)
        e_lo = jnp.minimum(i1, i2) - group * E_PER_GROUP
        e_hi = jnp.maximum(i1, i2) - group * E_PER_GROUP
        cls = group * N_PAIRS + (e_lo * E_PER_GROUP - e_lo * (e_lo + 1.0) * 0.5 + (e_hi - e_lo - 1.0))
        lane = lax.broadcasted_iota(jnp.int32, (tm, LANES), 1).astype(F32)
        onehot = jnp.where(lane == cls, 1.0, 0.0)
        r = lax.broadcasted_iota(jnp.int32, (tm, tm), 0)
        c = lax.broadcasted_iota(jnp.int32, (tm, tm), 1)
        before = _dot(jnp.where(r > c, 1.0, 0.0).astype(BF16), onehot.astype(BF16))
        rank = jnp.sum(before * onehot, axis=-1, keepdims=True)
        col = lax.broadcasted_iota(jnp.int32, route_ref.shape, 1)
        route_ref[...] = jnp.where(col == 0, group, jnp.where(col == 1, cls, jnp.where(col == 2, rank, 0.0)))
        counts_ref[0][...] = jnp.broadcast_to(jnp.sum(onehot, axis=0, keepdims=True), counts_ref[0].shape)
    else:
        h2_ref[...] = h2.astype(BF16)
        for gi in range(N_GROUPS):
            route_ref[gi] = comb[:, gi * E_PER_GROUP:(gi + 1) * E_PER_GROUP]


def _merge(x2d, o_r, o_f, mem, gates, tm, sorted_moe, wr, wf, wm, wo, g2, wrt, brt):
    t = x2d.shape[0]
    row = lambda w: pl.BlockSpec((tm, w), lambda i: (i, 0))
    sds = jax.ShapeDtypeStruct
    if sorted_moe:
        h2_spec, h2_shape = pl.BlockSpec((tm * N_SLABS, LANES), lambda i: (i, 0)), sds((t * N_SLABS, LANES), F32)
        rt_specs = [row(E_PER_GROUP), pl.BlockSpec((1, 8, LANES), lambda i: (i, 0, 0))]
        rt_shapes = [sds((t, E_PER_GROUP), F32), sds((t // tm, 8, LANES), F32)]
        mq, mk, mv, tiles_per_seq = mem
        mem_args = [mq, mk, mv]
        mem_kv = pl.BlockSpec((N_MEM, MEM_W), lambda i: (i // tiles_per_seq, 0))
        mem_specs = [row(MEM_W), mem_kv, mem_kv]
    else:
        h2_spec, h2_shape = row(D_MODEL), sds((t, D_MODEL), BF16)
        rt_specs = [pl.BlockSpec((N_GROUPS, tm, E_PER_GROUP), lambda i: (0, i, 0))]
        rt_shapes = [sds((N_GROUPS, t, E_PER_GROUP), F32)]
        mem_args, mem_specs = [mem], [row(MEM_W)]
    return pl.pallas_call(
        functools.partial(_merge_kernel, sorted_moe=sorted_moe),
        grid=(t // tm,),
        in_specs=[row(D_MODEL), row(RV_W), row(FOX_W), *mem_specs, row(GATE_W),
                  _const_spec((RV_W, D_MODEL)), _const_spec((FOX_W, D_MODEL)), _const_spec((MEM_W, D_MODEL)),
                  _const_spec((D_MODEL, D_MODEL)), _const_spec((1, D_MODEL)),
                  _const_spec((D_MODEL, 2 * LANES)), _const_spec((1, LANES))],
        out_specs=[row(D_MODEL), h2_spec, *rt_specs],
        out_shape=[sds((t, D_MODEL), F32), h2_shape, *rt_shapes],
        compiler_params=_cparams(("parallel",)),
        name="merge",
    )(x2d, o_r, o_f, *mem_args, gates, wr, wf, wm, wo, g2, wrt, brt)


def _positions_kernel(route_ref, first_ref, pos_ref, *, tm):
    lane = lax.broadcasted_iota(jnp.int32, (tm, LANES), 1).astype(F32)
    for i in range(first_ref.shape[0]):
        route = route_ref[i * tm:(i + 1) * tm, :]
        first = jnp.sum(jnp.where(lane == route[:, 1:2], first_ref[i, 0:1, :], 0.0), axis=-1, keepdims=True)
        pos_ref[i * tm:(i + 1) * tm, :] = jnp.broadcast_to(first + route[:, 2:3], route.shape).astype(jnp.int32)


def _positions(route, first_row, tm, tiles_per_step):
    t = route.shape[0]
    rows = tm * tiles_per_step
    return pl.pallas_call(
        functools.partial(_positions_kernel, tm=tm),
        grid=(t // rows,),
        in_specs=[pl.BlockSpec((rows, E_PER_GROUP), lambda i: (i, 0)),
                  pl.BlockSpec((tiles_per_step, 8, LANES), lambda i: (i, 0, 0))],
        out_specs=pl.BlockSpec((rows, E_PER_GROUP), lambda i: (i, 0)),
        out_shape=jax.ShapeDtypeStruct((t, E_PER_GROUP), jnp.int32),
        compiler_params=_cparams(("parallel",)),
        name="positions",
    )(route, first_row)[:, 0]


def _group_experts(h, cw, wg_ref, wu_ref, wd_ref, act):
    for e in range(E_PER_GROUP):
        a = _dot(h, wg_ref[0, e])
        u = _dot(h, wu_ref[0, e])
        act[:, e * D_EXPERT:(e + 1) * D_EXPERT] = ((a * jax.nn.sigmoid(a)) * u * cw(e)).astype(BF16)
    return _dot(act[...], wd_ref[0])


N_SLABS = D_MODEL // LANES
SUBLANES = 8


def _to_token_tiles(ref, x):
    for s in range(N_SLABS):
        ref[pl.ds(s, x.shape[0], stride=N_SLABS), :] = x[:, s * LANES:(s + 1) * LANES]


def _landing_shape(rows):
    return (rows // SUBLANES, N_SLABS, SUBLANES, LANES)


def _from_landing(buf):
    rows = buf.shape[0] * SUBLANES
    return jnp.concatenate([buf[:, s].reshape(rows, LANES) for s in range(N_SLABS)], axis=1)


def _row_gather(idx_ref, base, src_hbm, dst, sem, straight_line=False):
    def body(i, _):
        for u in range(SUBLANES):
            pltpu.make_async_copy(src_hbm.at[idx_ref[base + i * SUBLANES + u]], dst.at[i, :, u, :],
                                  sem).start(priority=u % 2)
        return 0
    if straight_line:
        for i in range(dst.shape[0]):
            body(i, 0)
    else:
        lax.fori_loop(0, dst.shape[0], body, 0)


def _row_gather_wait(dst, sem):
    pltpu.make_async_copy(dst, dst, sem).wait()


def _moe_sorted_kernel(tg_ref, nvt_ref, pos_ref, h3_hbm, zeros_hbm, wrt_ref, brt_ref,
                       wg_ref, wu_ref, wd_ref, y_ref, xbuf, sem, hb, yacc, src_ref, clear_sem, *, tm):
    k = pl.program_id(0)
    nvt = nvt_ref[0]
    slot = lax.rem(k, 2)

    @pl.when(k == 0)
    def _():
        clear = pltpu.make_async_copy(zeros_hbm, src_ref, clear_sem)
        clear.start()
        clear.wait()

        def invert(t, _):
            src_ref[pos_ref[t]] = t
            return 0

        lax.fori_loop(0, pos_ref.shape[0], invert, 0, unroll=32)

    @pl.when((k == 0) & (nvt > 0))
    def _():
        _row_gather(src_ref, 0, h3_hbm, xbuf.at[0], sem.at[0])

    @pl.when(k < nvt)
    def _():
        _row_gather_wait(xbuf.at[slot], sem.at[slot])
        _row_gather(src_ref, jnp.minimum(k + 1, nvt - 1) * tm, h3_hbm, xbuf.at[1 - slot], sem.at[1 - slot],
                    straight_line=True)

        x = _from_landing(xbuf.at[slot])
        group = tg_ref[k]
        _, comb, _, _ = _route(_router_logits(x, wrt_ref, brt_ref), group.astype(F32))
        lane = lax.broadcasted_iota(jnp.int32, comb.shape, 1)
        hb[...] = x.astype(BF16)
        yacc[...] = jnp.zeros(yacc.shape, F32)
        col_max = jnp.max(comb, axis=0, keepdims=True)
        used = [jnp.max(jnp.where(lane[:1] == group * E_PER_GROUP + e, col_max, 0.0)) for e in range(E_PER_GROUP)]
        for e in range(E_PER_GROUP):
            @pl.when(used[e] > 0.0)
            def _(e=e):
                cw = jnp.sum(jnp.where(lane == group * E_PER_GROUP + e, comb, 0.0), axis=-1, keepdims=True)
                h = hb[...]
                a = _dot(h, wg_ref[0, e])
                u = _dot(h, wu_ref[0, e])
                act = ((a * jax.nn.sigmoid(a)) * u * cw).astype(BF16)
                yacc[...] += _dot(act, wd_ref[0, e * D_EXPERT:(e + 1) * D_EXPERT, :])

        _to_token_tiles(y_ref, yacc[...])

        @pl.when(k + 1 >= nvt)
        def _():
            _row_gather_wait(xbuf.at[1 - slot], sem.at[1 - slot])

    @pl.when(k >= nvt)
    def _():
        y_ref[...] = jnp.zeros(y_ref.shape, F32)


def _moe_sorted(h3, tile_group, n_valid_tiles, pos, n_tiles, tm, wrt, brt, wg, wu, wd):
    wspec = lambda shape: pl.BlockSpec(shape, lambda k, tg, nv, ps: (tg[k],) + (0,) * (len(shape) - 1))
    cspec = lambda shape: pl.BlockSpec(shape, lambda k, tg, nv, ps: (0,) * len(shape),
                                       pipeline_mode=pl.Buffered(1))
    return pl.pallas_call(
        functools.partial(_moe_sorted_kernel, tm=tm),
        grid_spec=pltpu.PrefetchScalarGridSpec(
            num_scalar_prefetch=3,
            grid=(n_tiles,),
            in_specs=[pl.BlockSpec(memory_space=pl.ANY), pl.BlockSpec(memory_space=pl.ANY),
                      cspec((D_MODEL, 2 * LANES)), cspec((1, LANES)),
                      wspec((1, E_PER_GROUP, D_MODEL, D_EXPERT)), wspec((1, E_PER_GROUP, D_MODEL, D_EXPERT)),
                      wspec((1, E_PER_GROUP * D_EXPERT, D_MODEL))],
            out_specs=pl.BlockSpec((tm * N_SLABS, LANES), lambda k, tg, nv, ps: (k, 0)),
            scratch_shapes=[pltpu.VMEM((2,) + _landing_shape(tm), F32), pltpu.SemaphoreType.DMA((2,)),
                            pltpu.VMEM((tm, D_MODEL), BF16), pltpu.VMEM((tm, D_MODEL), F32),
                            pltpu.SMEM((n_tiles * tm,), jnp.int32), pltpu.SemaphoreType.DMA(())]),
        out_shape=jax.ShapeDtypeStruct((n_tiles * tm * N_SLABS, LANES), F32),
        compiler_params=_cparams(("arbitrary",)),
        name="moe_sorted",
    )(tile_group, n_valid_tiles, pos, h3, jnp.zeros((n_tiles * tm,), jnp.int32), wrt, brt, wg, wu, wd)


def _combine_kernel(pos_ref, y3_hbm, x1_ref, o_ref, ybuf, sem, *, tm):
    k = pl.program_id(0)
    slot = lax.rem(k, 2)

    @pl.when(k == 0)
    def _():
        _row_gather(pos_ref, 0, y3_hbm, ybuf.at[0], sem.at[0])

    _row_gather_wait(ybuf.at[slot], sem.at[slot])

    @pl.when(k + 1 < pl.num_programs(0))
    def _():
        _row_gather(pos_ref, (k + 1) * tm, y3_hbm, ybuf.at[1 - slot], sem.at[1 - slot])

    o_ref[...] = x1_ref[...] + _from_landing(ybuf.at[slot])


def _combine(y3, pos, x1, tm):
    t = x1.shape[0]
    return pl.pallas_call(
        functools.partial(_combine_kernel, tm=tm),
        grid_spec=pltpu.PrefetchScalarGridSpec(
            num_scalar_prefetch=1,
            grid=(t // tm,),
            in_specs=[pl.BlockSpec(memory_space=pl.ANY), pl.BlockSpec((tm, D_MODEL), lambda k, ps: (k, 0))],
            out_specs=pl.BlockSpec((tm, D_MODEL), lambda k, ps: (k, 0)),
            scratch_shapes=[pltpu.VMEM((2,) + _landing_shape(tm), F32), pltpu.SemaphoreType.DMA((2,))]),
        out_shape=jax.ShapeDtypeStruct((t, D_MODEL), F32),
        compiler_params=_cparams(("arbitrary",)),
        name="combine",
    )(pos, y3, x1)


def _moe_kernel(h_ref, comb_ref, x1_ref, wg_ref, wu_ref, wd_ref, o_ref, act):
    g = pl.program_id(1)
    comb = comb_ref[0]
    y = _group_experts(h_ref[...], lambda e: comb[:, e:e + 1], wg_ref, wu_ref, wd_ref, act)

    @pl.when(g == 0)
    def _():
        o_ref[...] = x1_ref[...] + y

    @pl.when(g != 0)
    def _():
        o_ref[...] += y


def _moe(h2, comb, x1, tm, wg, wu, wd):
    t = h2.shape[0]
    return pl.pallas_call(
        _moe_kernel,
        grid=(t // tm, N_GROUPS),
        in_specs=[pl.BlockSpec((tm, D_MODEL), lambda i, g: (i, 0)),
                  pl.BlockSpec((1, tm, E_PER_GROUP), lambda i, g: (g, i, 0)),
                  pl.BlockSpec((tm, D_MODEL), lambda i, g: (i, 0)),
                  pl.BlockSpec((1, E_PER_GROUP, D_MODEL, D_EXPERT), lambda i, g: (g, 0, 0, 0)),
                  pl.BlockSpec((1, E_PER_GROUP, D_MODEL, D_EXPERT), lambda i, g: (g, 0, 0, 0)),
                  pl.BlockSpec((1, E_PER_GROUP * D_EXPERT, D_MODEL), lambda i, g: (g, 0, 0))],
        out_specs=pl.BlockSpec((tm, D_MODEL), lambda i, g: (i, 0)),
        out_shape=jax.ShapeDtypeStruct((t, D_MODEL), F32),
        scratch_shapes=[pltpu.VMEM((tm, E_PER_GROUP * D_EXPERT), BF16)],
        compiler_params=_cparams(("parallel", "arbitrary")),
        name="moe",
    )(h2, comb, x1, wg, wu, wd)


def _rope_tables(pos):
    half = DK_R // 2
    inv = ROPE_BASE ** (-jnp.arange(half, dtype=F32) / half)
    ang = pos.astype(F32)[:, None] * inv[None, :]
    c = jnp.cos(ang)
    s = jnp.sin(ang)
    return jnp.concatenate([c, c, c, c], axis=-1), jnp.concatenate([-s, s, -s, s], axis=-1)


def kernel(x_prompt, x_sample, state_ret, cache_fox_k, cache_fox_v, cache_fox_logf, cache_mem_k, cache_mem_v,
           mem_prompt, g_norm1, w_in, b_forget, g_fox_q, g_fox_k, g_mem_q, g_mem_in, w_mem_kv, g_mem_k,
           g_ret_out, w_br_ret, w_br_fox, w_br_mem, w_out, g_norm2, w_route_group, b_route_group,
           w_route_expert, b_route_expert, w_exp_gate, w_exp_up, w_exp_down):
    nb, seq, _ = x_prompt.shape
    nbs, n_new, _ = x_sample.shape
    past = cache_fox_k.shape[2]
    l = 0

    wi = w_in[l]
    o_ff = 2 * RQK_W + 2 * RV_W + 3 * FOX_W
    w_in_r = jnp.concatenate(
        [wi[:, :o_ff], wi[:, o_ff + H_F:], wi[:, o_ff:o_ff + H_F],
         jnp.zeros((D_MODEL, LANES - H_F), F32)], axis=1).astype(BF16)
    bf_pad = jnp.concatenate([b_forget[l], jnp.zeros((LANES - H_F,), F32)])[None, :]
    g1 = g_norm1[l][None, :]
    gfq = jnp.tile(g_fox_q[l], H_F)[None, :]
    gfk = jnp.tile(g_fox_k[l], H_F)[None, :]
    gmq = g_mem_q[l][None, :]
    hid = jnp.arange(MXU_DIM) // D_F
    bd = jnp.where(hid[:, None] == hid[None, :], 1.0 / D_F, 0.0).astype(BF16)
    wr = w_br_ret[l].astype(BF16)
    wf = w_br_fox[l].astype(BF16)
    wm = w_br_mem[l].astype(BF16)
    wo = w_out[l].astype(BF16)
    g2 = g_norm2[l][None, :]
    n_e = N_GROUPS * E_PER_GROUP
    wrt = jnp.concatenate([w_route_expert[l], w_route_group[l],
                           jnp.zeros((D_MODEL, LANES - n_e - N_GROUPS), F32)], axis=1)
    wrt_hi = wrt.astype(BF16)
    wrt2 = jnp.concatenate([wrt_hi, (wrt - wrt_hi.astype(F32)).astype(BF16)], axis=1)
    brt = jnp.concatenate([b_route_expert[l], b_route_group[l],
                           jnp.zeros((LANES - n_e - N_GROUPS,), F32)])[None, :]
    wg = w_exp_gate[l].astype(BF16)
    wu = w_exp_up[l].astype(BF16)
    wd = w_exp_down[l].astype(BF16).reshape(N_GROUPS, E_PER_GROUP * D_EXPERT, D_MODEL)
    gro = g_ret_out[l][None, :]
    prep_w = (g1, w_in_r, bf_pad, gfq, gfk, gmq, bd)
    merge_w = (wr, wf, wm, wo, g2, wrt2, brt)

    tm = 512
    xp = x_prompt.reshape(nb * seq, D_MODEL)
    (rq, rk, rv, rg, fqt, fkt, fvt, lft, mq, gates, fkb, lf) = _prep(
        xp, _rope_tables(jnp.arange(seq)), tm, seq // tm, True, *prep_w)
    o_r, s_fin = _retention(rq, rk, rv, rg, nb, seq, 256, gro, None)
    fcum = _cumsum_cols(lf.reshape(nb, seq, H_F), 256)
    fcum = jnp.swapaxes(fcum.reshape(nb, seq, H_F // 2, 2), 1, 2).reshape(nb * (H_F // 2) * seq, 2)
    o_f = _fox_prompt(fqt, fkb, fvt, fcum, nb, seq, 256)
    mk, mv = _memkv(mem_prompt.reshape(nb * N_MEM, D_MODEL), nb, g_mem_in[l][None, :],
                    w_mem_kv[l].astype(BF16), g_mem_k[l][None, :])
    x1, h3, route, counts = _merge(xp, o_r, o_f, (mq, mk, mv, seq // tm), gates, tm, True, *merge_w)
    t_p = nb * seq
    n_tiles = t_p // tm + N_GROUPS
    n_cls = N_GROUPS * N_PAIRS
    cnt = counts[:, 0, :n_cls].astype(jnp.int32)
    before_tile = jnp.cumsum(cnt, axis=0) - cnt
    cls_tot = jnp.sum(cnt, axis=0)
    grp_tot = jnp.sum(cls_tot.reshape(N_GROUPS, N_PAIRS), axis=1)
    tiles_g = (grp_tot + tm - 1) // tm
    tile_end = jnp.cumsum(tiles_g)
    row_start = (tile_end - tiles_g) * tm
    in_grp = cls_tot.reshape(N_GROUPS, N_PAIRS)
    cls_start = (row_start[:, None] + jnp.cumsum(in_grp, axis=1) - in_grp).reshape(n_cls)
    first_row = jnp.pad((cls_start[None, :] + before_tile).astype(F32), ((0, 0), (0, LANES - n_cls)))
    pos = _positions(route, jnp.broadcast_to(first_row[:, None, :], (t_p // tm, 8, LANES)), tm,
                     min(8, t_p // tm))
    tile_ids = jnp.arange(n_tiles, dtype=jnp.int32)
    tile_group = jnp.minimum(jnp.sum((tile_ids[:, None] >= tile_end[None, :]).astype(jnp.int32), axis=1),
                             N_GROUPS - 1)
    y3 = _moe_sorted(h3.reshape(t_p, N_SLABS, LANES), tile_group, tile_end[-1:], pos, n_tiles, tm,
                     wrt2, brt, wg, wu, wd)
    y_prompt = _combine(y3.reshape(n_tiles * tm, N_SLABS, LANES), pos, x1, tm).reshape(nb, seq, D_MODEL)

    ts = nbs * n_new
    xs = x_sample.reshape(ts, D_MODEL)
    pos_s = jnp.tile(past + jnp.arange(n_new), nbs)
    (rq_s, rk_s, rv_s, rg_s, fq_s, fk_s, fv_s, lf_s, mq_s, gates_s) = _prep(
        xs, _rope_tables(pos_s), ts, 1, False, *prep_w)
    o_r_s, s_new = _retention(rq_s, rk_s, rv_s, rg_s, nbs, n_new, n_new, gro, state_ret[l])
    pad = (-(past + n_new)) % 256
    lf_rows = jnp.concatenate([jnp.swapaxes(cache_fox_logf[l], 1, 2),
                               jnp.swapaxes(lf_s.reshape(nbs, n_new, H_F), 1, 2),
                               jnp.zeros((nbs, H_F, pad), F32)], axis=2).reshape(nbs * H_F, past + n_new + pad)
    f_all = _cumsum(lf_rows, 256)
    feat_major = lambda c: jnp.transpose(c, (0, 2, 3, 1)).reshape(nbs * FOX_W, past)
    o_f_s = _fox_sample(fq_s, feat_major(cache_fox_k[l]), feat_major(cache_fox_v[l]), fk_s, fv_s,
                        f_all[:, :past].reshape(nbs, H_F, past),
                        f_all[:, past:past + n_new].reshape(nbs, H_F, n_new), nbs, n_new, past)
    o_m_s = _memattn(mq_s, cache_mem_k[l].reshape(nbs * N_MEM, MEM_W),
                     cache_mem_v[l].reshape(nbs * N_MEM, MEM_W), nbs, n_new, n_new)
    x1_s, h2_s, comb_s = _merge(xs, o_r_s, o_f_s, o_m_s, gates_s, ts, False, *merge_w)
    y_sample = _moe(h2_s, comb_s, x1_s, ts, wg, wu, wd).reshape(nbs, n_new, D_MODEL)

    token_major = lambda a: jnp.transpose(a.reshape(nb, H_F, D_F, seq), (0, 3, 1, 2))

    return (y_prompt, y_sample,
            s_fin[None], token_major(fkt)[None], token_major(fvt)[None],
            jnp.swapaxes(lft.reshape(nb, H_F, seq), 1, 2)[None],
            mk.reshape(1, nb, N_MEM, H_M, D_M), mv.reshape(1, nb, N_MEM, H_M, D_M),
            s_new[None], fk_s.reshape(1, nbs, n_new, H_F, D_F), fv_s.reshape(1, nbs, n_new, H_F, D_F),
            lf_s.reshape(1, nbs, n_new, H_F))
```

```python
import functools

import jax
import jax.numpy as jnp
from jax import lax
from jax.experimental import pallas as pl
from jax.experimental.pallas import tpu as pltpu

F32 = jnp.float32
BF16 = jnp.bfloat16

D_MODEL = 1024
H_R, DK_R, DV_R = 4, 64, 128
H_F, D_F = 8, 64
H_M, D_M = 4, 128
N_MEM = 256
N_GROUPS, E_PER_GROUP, D_EXPERT = 4, 8, 256
ROPE_BASE = 10000.0
EPS = 1e-6
LOG2E = 1.4426950408889634

RQK_W = H_R * DK_R
RV_W = H_R * DV_R
FOX_W = H_F * D_F
MEM_W = H_M * D_M
GATE_W = 3 * D_MODEL
LANES = 128
MXU_DIM = 256

C_RQ = 0
C_RK = C_RQ + RQK_W
C_RV = C_RK + RQK_W
C_RG = C_RV + RV_W
C_FQ = C_RG + RV_W
C_FK = C_FQ + FOX_W
C_FV = C_FK + FOX_W
C_MQ = C_FV + FOX_W
C_GT = C_MQ + MEM_W
C_FF = C_GT + GATE_W
N_IN_PAD = C_FF + LANES

VMEM_LIMIT = 56 * 1024 * 1024


def _cparams(sem):
    return pltpu.CompilerParams(dimension_semantics=sem, vmem_limit_bytes=VMEM_LIMIT)


def _const_spec(shape):
    nd = len(shape)
    return pl.BlockSpec(shape, lambda *_: (0,) * nd, pipeline_mode=pl.Buffered(1))


def _dot(a, b):
    return jnp.dot(a, b, preferred_element_type=F32)


def _dot_nt(a, b):
    return lax.dot_general(a, b, (((1,), (1,)), ((), ())), preferred_element_type=F32)


def _dot_tn(a, b):
    return lax.dot_general(a, b, (((0,), (0,)), ((), ())), preferred_element_type=F32)


def _lane_rmsnorm(z):
    return z * lax.rsqrt(jnp.mean(z * z, axis=-1, keepdims=True) + EPS)


def _prep_kernel(x_ref, g1_ref, w_ref, cos_ref, sin_ref, bf_ref, gfq_ref, gfk_ref, gmq_ref, bd_ref,
                 rq_ref, rk_ref, rv_ref, rg_ref, fq_ref, fk_ref, fv_ref, lf_ref, mq_ref, gt_ref, *extra,
                 seq_minor):
    x = x_ref[...]
    h = (x * lax.rsqrt(jnp.mean(x * x, axis=-1, keepdims=True) + EPS) * g1_ref[...]).astype(BF16)

    def proj(c0, width):
        return _dot(h, w_ref[:, c0:c0 + width])

    cos = cos_ref[...]
    sin = sin_ref[...]
    lane = lax.broadcasted_iota(jnp.int32, cos.shape, 1)
    first_half = (lane % DK_R) < (DK_R // 2)

    def rope(z):
        swapped = jnp.where(first_half, pltpu.roll(z, LANES - DK_R // 2, 1), pltpu.roll(z, DK_R // 2, 1))
        return z * cos + swapped * sin

    zq = proj(C_RQ, RQK_W)
    zk = proj(C_RK, RQK_W)
    for p in range(RQK_W // LANES):
        sl = slice(p * LANES, (p + 1) * LANES)
        rq_ref[:, sl] = rope(zq[:, sl])
        rk_ref[:, sl] = rope(zk[:, sl]) * (DK_R ** -0.5)
    rv_ref[...] = proj(C_RV, RV_W)
    rg_ref[...] = proj(C_RG, RV_W)

    def head64_norm(z, g_ref):
        zz = (z * z).astype(BF16)
        w = bd_ref.shape[0]
        ms = jnp.concatenate([_dot(zz[:, c:c + w], bd_ref[...]) for c in range(0, FOX_W, w)], axis=1)
        return z * lax.rsqrt(ms + EPS) * g_ref[...]

    fq = head64_norm(proj(C_FQ, FOX_W), gfq_ref) * (D_F ** -0.5 * LOG2E)
    fk = head64_norm(proj(C_FK, FOX_W), gfk_ref)
    fv = proj(C_FV, FOX_W)
    if seq_minor:
        fq_ref[...] = fq.T.astype(BF16)
        fk_ref[...] = fk.T
        fv_ref[...] = fv.T
        extra[0][...] = fk.astype(BF16)
    else:
        fq_ref[...] = fq.astype(BF16)
        fk_ref[...] = fk
        fv_ref[...] = fv

    zm = proj(C_MQ, MEM_W)
    for hh in range(H_M):
        sl = slice(hh * D_M, (hh + 1) * D_M)
        mq_ref[:, sl] = (_lane_rmsnorm(zm[:, sl]) * gmq_ref[...]).astype(BF16)

    for b in range(3):
        gt_ref[:, b * D_MODEL:(b + 1) * D_MODEL] = jax.nn.sigmoid(proj(C_GT + b * D_MODEL, D_MODEL)).astype(BF16)

    v = proj(C_FF, LANES) + bf_ref[...]
    lf = jnp.minimum(v, 0.0) - jnp.log1p(jnp.exp(-jnp.abs(v)))
    if seq_minor:
        lf_ref[...] = lf.T[:H_F, :]
        extra[1][...] = lf[:, :H_F]
    else:
        lf_ref[...] = lf[:, :H_F]


def _prep(x2d, tables, tm, n_pos_tiles, seq_minor, g1, w_in_r, bf_pad, gfq, gfk, gmq, bd):
    t = x2d.shape[0]
    cos_t, sin_t = tables
    row = lambda w: pl.BlockSpec((tm, w), lambda i: (i, 0))
    pos = pl.BlockSpec((tm, LANES), lambda i: (i % n_pos_tiles, 0))
    sds = jax.ShapeDtypeStruct
    if seq_minor:
        nb, seq = t // (n_pos_tiles * tm), n_pos_tiles * tm
        fm = lambda rows_: pl.BlockSpec((rows_, tm), lambda i: (i // n_pos_tiles, i % n_pos_tiles))
        fox_shapes = [sds((nb * FOX_W, seq), BF16), sds((nb * FOX_W, seq), F32), sds((nb * FOX_W, seq), F32),
                      sds((nb * H_F, seq), F32)]
        fox_specs = [fm(FOX_W), fm(FOX_W), fm(FOX_W), fm(H_F)]
        extra_shapes, extra_specs = [sds((t, FOX_W), BF16), sds((t, H_F), F32)], [row(FOX_W), row(H_F)]
    else:
        fox_shapes = [sds((t, FOX_W), BF16), sds((t, FOX_W), F32), sds((t, FOX_W), F32), sds((t, H_F), F32)]
        fox_specs = [row(FOX_W), row(FOX_W), row(FOX_W), row(H_F)]
        extra_shapes, extra_specs = [], []
    out_shapes = [sds((t, RQK_W), F32), sds((t, RQK_W), F32), sds((t, RV_W), F32), sds((t, RV_W), F32),
                  *fox_shapes, sds((t, MEM_W), BF16), sds((t, GATE_W), BF16), *extra_shapes]
    out_specs = [row(RQK_W), row(RQK_W), row(RV_W), row(RV_W), *fox_specs, row(MEM_W), row(GATE_W),
                 *extra_specs]
    return pl.pallas_call(
        functools.partial(_prep_kernel, seq_minor=seq_minor),
        grid=(t // tm,),
        in_specs=[row(D_MODEL), _const_spec((1, D_MODEL)), _const_spec((D_MODEL, N_IN_PAD)), pos, pos,
                  _const_spec((1, LANES)), _const_spec((1, FOX_W)), _const_spec((1, FOX_W)),
                  _const_spec((1, D_M)), _const_spec((MXU_DIM, MXU_DIM))],
        out_specs=out_specs,
        out_shape=out_shapes,
        compiler_params=_cparams(("parallel",)),
        name="prep",
    )(x2d, g1, w_in_r, cos_t, sin_t, bf_pad, gfq, gfk, gmq, bd)


def _ret_kernel(*refs, has_init, ch):
    if has_init:
        (rq_ref, rk_ref, rv_ref, rg_ref, dmat_ref, qdec_ref, kdec_ref, gpow_ref, gro_ref, s0_ref,
         o_ref, sfin_ref) = refs
    else:
        (rq_ref, rk_ref, rv_ref, rg_ref, dmat_ref, qdec_ref, kdec_ref, gpow_ref, gro_ref,
         o_ref, sfin_ref) = refs

    state = []
    for h in range(H_R):
        if has_init:
            pad = jnp.zeros((DK_R, DV_R), F32)
            state.append(jnp.concatenate([s0_ref[0, h], pad] if h % 2 == 0 else [pad, s0_ref[0, h]], axis=0))
        else:
            state.append(jnp.zeros((LANES, DV_R), F32))

    lane = lax.broadcasted_iota(jnp.int32, (ch, LANES), 1)
    for c in range(rq_ref.shape[0] // ch):
        rows = slice(c * ch, (c + 1) * ch)
        for p in range(H_R // 2):
            sl = slice(p * LANES, (p + 1) * LANES)
            q2 = rq_ref[rows, sl]
            k2 = rk_ref[rows, sl]
            kd2 = k2 * kdec_ref[p]
            for hh in range(2):
                h = 2 * p + hh
                mine = (lane >= hh * DK_R) & (lane < (hh + 1) * DK_R)
                qm = jnp.where(mine, q2, 0.0).astype(BF16)
                kdm = jnp.where(mine, kd2, 0.0).astype(BF16)
                v = rv_ref[rows, h * DV_R:(h + 1) * DV_R].astype(BF16)
                sc = _dot_nt(qm, k2.astype(BF16)) * dmat_ref[h]
                o = _dot(sc.astype(BF16), v) + _dot(qm, state[h].astype(BF16)) * qdec_ref[h]
                state[h] = gpow_ref[h] * state[h] + _dot_tn(kdm, v)
                normed = _lane_rmsnorm(o) * gro_ref[...]
                rg = rg_ref[rows, h * DV_R:(h + 1) * DV_R]
                o_ref[rows, h * DV_R:(h + 1) * DV_R] = (normed * (rg * jax.nn.sigmoid(rg))).astype(BF16)

    for h in range(H_R):
        r0 = DK_R * (h % 2)
        sfin_ref[0, h] = state[h][r0:r0 + DK_R, :]


def _retention(rq, rk, rv, rg, n_batch, seq, ch, g_ret_out, state0):
    lg = jnp.log1p(-jnp.exp2(-5.0 - jnp.arange(H_R, dtype=F32)))
    idx = jnp.arange(ch, dtype=F32)
    diff = idx[:, None] - idx[None, :]
    causal = diff >= 0
    dmat = jnp.where(causal[None], jnp.exp(jnp.where(causal, diff, 0.0)[None] * lg[:, None, None]), 0.0)
    q_dec = jnp.exp((idx + 1.0)[None, :] * lg[:, None])
    k_dec = jnp.exp((ch - 1.0 - idx)[None, :] * lg[:, None])
    qdec = jnp.broadcast_to(q_dec[:, :, None], (H_R, ch, DV_R))
    kdec = jnp.broadcast_to(k_dec[:, :, None], (H_R, ch, DK_R))
    kdec = kdec.reshape(H_R // 2, 2, ch, DK_R).transpose(0, 2, 1, 3).reshape(H_R // 2, ch, LANES)
    gpow = jnp.broadcast_to(jnp.exp(ch * lg)[:, None, None], (H_R, 1, DV_R))

    has_init = state0 is not None
    blk = lambda w: pl.BlockSpec((seq, w), lambda b: (b, 0))
    in_specs = [blk(RQK_W), blk(RQK_W), blk(RV_W), blk(RV_W),
                _const_spec((H_R, ch, ch)), _const_spec((H_R, ch, DV_R)),
                _const_spec((H_R // 2, ch, LANES)), _const_spec((H_R, 1, DV_R)), _const_spec((1, DV_R))]
    args = [rq, rk, rv, rg, dmat, qdec, kdec, gpow, g_ret_out]
    if has_init:
        in_specs.append(pl.BlockSpec((1, H_R, DK_R, DV_R), lambda b: (b, 0, 0, 0)))
        args.append(state0)
    return pl.pallas_call(
        functools.partial(_ret_kernel, has_init=has_init, ch=ch),
        grid=(n_batch,),
        in_specs=in_specs,
        out_specs=[blk(RV_W), pl.BlockSpec((1, H_R, DK_R, DV_R), lambda b: (b, 0, 0, 0))],
        out_shape=[jax.ShapeDtypeStruct((n_batch * seq, RV_W), BF16),
                   jax.ShapeDtypeStruct((n_batch, H_R, DK_R, DV_R), F32)],
        compiler_params=_cparams(("parallel",)),
        name="retention",
    )(*args)


def _split3(x):
    hi = x.astype(BF16)
    r1 = x - hi.astype(F32)
    mid = r1.astype(BF16)
    lo = (r1 - mid.astype(F32)).astype(BF16)
    return hi, mid, lo


def _cumsum_kernel(x_ref, o_ref, *, blk):
    rows, n = x_ref.shape
    r = lax.broadcasted_iota(jnp.int32, (blk, blk), 0)
    c = lax.broadcasted_iota(jnp.int32, (blk, blk), 1)
    tri = jnp.where(r <= c, 1.0, 0.0).astype(BF16)
    carry = jnp.zeros((rows, 1), F32)
    for i in range(n // blk):
        hi, mid, lo = _split3(x_ref[:, i * blk:(i + 1) * blk])
        cum = (_dot(hi, tri) + _dot(mid, tri)) + _dot(lo, tri) + carry
        o_ref[:, i * blk:(i + 1) * blk] = cum
        carry = cum[:, blk - 1:blk]


def _cumsum_cols_kernel(x_ref, o_ref, *, blk):
    n = x_ref.shape[1]
    r = lax.broadcasted_iota(jnp.int32, (blk, blk), 0)
    c = lax.broadcasted_iota(jnp.int32, (blk, blk), 1)
    tri = jnp.where(r >= c, 1.0, 0.0).astype(BF16)
    carry = jnp.zeros((1, x_ref.shape[2]), F32)
    for i in range(n // blk):
        hi, mid, lo = _split3(x_ref[0, i * blk:(i + 1) * blk, :])
        cum = (_dot(tri, hi) + _dot(tri, mid)) + _dot(tri, lo) + carry
        o_ref[0, i * blk:(i + 1) * blk, :] = cum
        carry = cum[blk - 1:blk, :]


def _cumsum_cols(x, blk):
    nb, t, w = x.shape
    spec = pl.BlockSpec((1, t, w), lambda b: (b, 0, 0))
    return pl.pallas_call(
        functools.partial(_cumsum_cols_kernel, blk=blk),
        grid=(nb,),
        in_specs=[spec], out_specs=spec,
        out_shape=jax.ShapeDtypeStruct(x.shape, F32),
        compiler_params=_cparams(("parallel",)),
        name="cumsum_cols",
    )(x)


def _cumsum(x, blk):
    r, t = x.shape
    spec = pl.BlockSpec((H_F, t), lambda b: (b, 0))
    return pl.pallas_call(
        functools.partial(_cumsum_kernel, blk=blk),
        grid=(r // H_F,),
        in_specs=[spec], out_specs=spec,
        out_shape=jax.ShapeDtypeStruct(x.shape, F32),
        compiler_params=_cparams(("parallel",)),
        name="cumsum",
    )(x)


def _head_rows_mask(shape, hh):
    sub = lax.broadcasted_iota(jnp.int32, shape, 0)
    return (sub >= hh * D_F) & (sub < (hh + 1) * D_F)


N_AUG = 3
V_ROWS = D_F + 16


def _fox_kernel(qt_ref, k_ref, vt_ref, fcum_ref, o_ref, qa, ka, vb, *, tq):
    seq = k_ref.shape[0]
    tk = tq
    qt = qt_ref[...].astype(F32)
    k = k_ref[...].astype(F32)
    sub = lax.broadcasted_iota(jnp.int32, qt.shape, 0)
    lane = lax.broadcasted_iota(jnp.int32, k.shape, 1)
    for hh in range(2):
        own, oth = hh * D_F, (1 - hh) * D_F
        qa[hh] = jnp.where((sub >= own) & (sub < own + D_F), qt,
                           jnp.where((sub >= oth) & (sub < oth + N_AUG), 1.0, 0.0)).astype(BF16)
        bias = fcum_ref[:, hh:hh + 1] * (-LOG2E)
        kk = jnp.where((lane >= own) & (lane < own + D_F), k, 0.0)
        for a, piece in enumerate(_split3(bias)):
            kk = jnp.where(lane == oth + a, piece.astype(F32), kk)
        ka[hh] = kk.astype(BF16)
    for hh in range(2):
        vb[hh, :D_F, :] = vt_ref[hh * D_F:(hh + 1) * D_F, :].astype(BF16)
        vb[hh, D_F:, :] = jnp.ones((V_ROWS - D_F, seq), BF16)
    causal = (lax.broadcasted_iota(jnp.int32, (tk, tq), 1) >= lax.broadcasted_iota(jnp.int32, (tk, tq), 0))

    pairs = [(i, j) for i in range(seq // tq) for j in range(i + 1)]

    def stage_a(i, j):
        return tuple(_dot(ka[hh, j * tk:(j + 1) * tk, :], qa[hh, :, i * tq:(i + 1) * tq]) for hh in range(2))

    def stage_b(i, j, ts, maxes):
        new_maxes, probs = [], []
        for hh in range(2):
            t = jnp.where(causal, ts[hh], -jnp.inf) if j == i else ts[hh]
            t_max = jnp.max(t, axis=0, keepdims=True)
            if j == 0:
                m_new, alpha = t_max, None
            else:
                m_new = jnp.maximum(maxes[hh], t_max)
                alpha = jnp.exp2(maxes[hh] - m_new)
            new_maxes.append(m_new)
            probs.append((jnp.exp2(t - m_new).astype(BF16), alpha))
        return new_maxes, probs

    def stage_c(i, j, probs, accs):
        out = []
        for hh in range(2):
            p, alpha = probs[hh]
            pv = _dot(vb[hh, :, j * tk:(j + 1) * tk], p)
            out.append(pv if j == 0 else alpha * accs[hh] + pv)
        return out

    scores, probs, maxes, accs = {}, {}, None, None
    for s in range(len(pairs) + 2):
        if s < len(pairs):
            scores[s] = stage_a(*pairs[s])
        if 0 <= s - 1 < len(pairs):
            maxes, probs[s - 1] = stage_b(*pairs[s - 1], scores.pop(s - 1), maxes)
        if 0 <= s - 2 < len(pairs):
            i, j = pairs[s - 2]
            accs = stage_c(i, j, probs.pop(s - 2), accs)
            if j == i:
                o_t = jnp.concatenate([a[:D_F] / a[D_F:D_F + 1] for a in accs], axis=0)
                o_ref[i * tq:(i + 1) * tq, :] = o_t.T.astype(BF16)


def _fox_prompt(fqt, fkb, fvt, fcum, n_batch, seq, tq):
    npair = H_F // 2
    fm_spec = pl.BlockSpec((LANES, seq), lambda b, p: (b * npair + p, 0))
    tok_spec = pl.BlockSpec((seq, LANES), lambda b, p: (b, p))
    return pl.pallas_call(
        functools.partial(_fox_kernel, tq=tq),
        grid=(n_batch, npair),
        in_specs=[fm_spec, tok_spec, fm_spec, pl.BlockSpec((seq, 2), lambda b, p: (b * npair + p, 0))],
        out_specs=tok_spec,
        out_shape=jax.ShapeDtypeStruct((n_batch * seq, FOX_W), BF16),
        scratch_shapes=[pltpu.VMEM((2, LANES, seq), BF16), pltpu.VMEM((2, seq, LANES), BF16),
                        pltpu.VMEM((2, V_ROWS, seq), BF16)],
        compiler_params=_cparams(("parallel", "parallel")),
        name="fox_prompt",
    )(fqt, fkb, fvt, fcum)


def _fox_sample_kernel(q_ref, kct_ref, vct_ref, kn_ref, vn_ref, frc_ref, frn_ref, o_ref):
    n = q_ref.shape[0]
    lane = lax.broadcasted_iota(jnp.int32, (n, LANES), 1)
    rows = lax.broadcasted_iota(jnp.int32, (n, n), 0)
    cols = lax.broadcasted_iota(jnp.int32, (n, n), 1)
    for pair in range(H_F // 2):
        sl = slice(pair * LANES, (pair + 1) * LANES)
        q2 = q_ref[:, sl]
        kct = kct_ref[sl, :]
        vct = vct_ref[sl, :].astype(BF16)
        kn = kn_ref[:, sl].astype(BF16)
        vn = vn_ref[:, sl].astype(BF16)
        outs = []
        for hh in range(2):
            h = 2 * pair + hh
            mine = (lane >= hh * D_F) & (lane < (hh + 1) * D_F)
            kc = jnp.where(_head_rows_mask(kct.shape, hh), kct, 0.0).astype(BF16)
            t_c = _dot(q2, kc) - frc_ref[0, h:h + 1, :] * LOG2E
            t_n = _dot_nt(jnp.where(mine, q2, jnp.zeros_like(q2)), kn) - frn_ref[0, h:h + 1, :] * LOG2E
            t_n = jnp.where(rows >= cols, t_n, -jnp.inf)
            m = jnp.maximum(jnp.max(t_c, axis=-1, keepdims=True), jnp.max(t_n, axis=-1, keepdims=True))
            p_c = jnp.exp2(t_c - m)
            p_n = jnp.exp2(t_n - m)
            l = jnp.sum(p_c, axis=-1, keepdims=True) + jnp.sum(p_n, axis=-1, keepdims=True)
            acc = _dot_nt(p_c.astype(BF16), vct) + _dot(p_n.astype(BF16), vn)
            outs.append(acc / l)
        o_ref[:, sl] = jnp.where(lane < D_F, outs[0], outs[1]).astype(BF16)


def _fox_sample(fq, kct, vct, kn, vn, frow_c, frow_n, n_batch, n_new, past):
    tok = pl.BlockSpec((n_new, FOX_W), lambda b: (b, 0))
    cache = pl.BlockSpec((FOX_W, past), lambda b: (b, 0))
    return pl.pallas_call(
        _fox_sample_kernel,
        grid=(n_batch,),
        in_specs=[tok, cache, cache, tok, tok,
                  pl.BlockSpec((1, H_F, past), lambda b: (b, 0, 0)),
                  pl.BlockSpec((1, H_F, n_new), lambda b: (b, 0, 0))],
        out_specs=tok,
        out_shape=jax.ShapeDtypeStruct((n_batch * n_new, FOX_W), BF16),
        compiler_params=_cparams(("parallel",)),
        name="fox_sample",
    )(fq, kct, vct, kn, vn, frow_c, frow_n)


def _memkv_kernel(m_ref, gin_ref, w_ref, gk_ref, k_ref, v_ref):
    x = m_ref[...]
    h = (x * lax.rsqrt(jnp.mean(x * x, axis=-1, keepdims=True) + EPS) * gin_ref[...]).astype(BF16)
    zk = _dot(h, w_ref[:, :MEM_W])
    for hh in range(H_M):
        sl = slice(hh * D_M, (hh + 1) * D_M)
        k_ref[:, sl] = _lane_rmsnorm(zk[:, sl]) * gk_ref[...]
    v_ref[...] = _dot(h, w_ref[:, MEM_W:])


def _memkv(mem2d, n_batch, g_mem_in, w_mem_kv, g_mem_k):
    blk = lambda w: pl.BlockSpec((N_MEM, w), lambda b: (b, 0))
    return pl.pallas_call(
        _memkv_kernel,
        grid=(n_batch,),
        in_specs=[blk(D_MODEL), _const_spec((1, D_MODEL)), _const_spec((D_MODEL, 2 * MEM_W)),
                  _const_spec((1, D_M))],
        out_specs=[blk(MEM_W), blk(MEM_W)],
        out_shape=[jax.ShapeDtypeStruct((n_batch * N_MEM, MEM_W), F32)] * 2,
        compiler_params=_cparams(("parallel",)),
        name="memkv",
    )(mem2d, g_mem_in, w_mem_kv, g_mem_k)


def _mem_attend(q_ref, k_ref, v_ref):
    for hh in range(H_M):
        sl = slice(hh * D_M, (hh + 1) * D_M)
        s = _dot_nt(q_ref[:, sl], k_ref[:, sl].astype(BF16)) * (D_M ** -0.5)
        m = jnp.max(s, axis=-1, keepdims=True)
        p = jnp.exp(s - m)
        l = jnp.sum(p, axis=-1, keepdims=True)
        yield sl, (_dot(p.astype(BF16), v_ref[:, sl].astype(BF16)) / l).astype(BF16)


def _memattn_kernel(q_ref, k_ref, v_ref, o_ref):
    for sl, o in _mem_attend(q_ref, k_ref, v_ref):
        o_ref[:, sl] = o


def _memattn(mq, mk, mv, n_batch, seq, tq):
    nq = seq // tq
    return pl.pallas_call(
        _memattn_kernel,
        grid=(n_batch, nq),
        in_specs=[pl.BlockSpec((tq, MEM_W), lambda b, i: (b * nq + i, 0)),
                  pl.BlockSpec((N_MEM, MEM_W), lambda b, i: (b, 0)),
                  pl.BlockSpec((N_MEM, MEM_W), lambda b, i: (b, 0))],
        out_specs=pl.BlockSpec((tq, MEM_W), lambda b, i: (b * nq + i, 0)),
        out_shape=jax.ShapeDtypeStruct((n_batch * seq, MEM_W), BF16),
        compiler_params=_cparams(("parallel", "arbitrary")),
        name="memattn",
    )(mq, mk, mv)


def _split2(x):
    hi = x.astype(BF16)
    return hi, (x - hi.astype(F32)).astype(BF16)


N_EXPERTS = N_GROUPS * E_PER_GROUP


N_PAIRS = E_PER_GROUP * (E_PER_GROUP - 1) // 2


def _router_logits(h2, wrt_ref, brt_ref):
    h_hi, h_lo = _split2(h2)
    z = _dot(h_hi, wrt_ref[...])
    return (z[:, :LANES] + z[:, LANES:]) + _dot(h_lo, wrt_ref[:, :LANES]) + brt_ref[...]


def _route(logits, group=None):
    lane = lax.broadcasted_iota(jnp.int32, logits.shape, 1).astype(F32)
    neg = -jnp.inf
    first_idx = lambda mask: jnp.min(jnp.where(mask, lane, float(LANES)), axis=-1, keepdims=True)

    is_g = (lane >= N_EXPERTS) & (lane < N_EXPERTS + N_GROUPS)
    lg = jnp.where(is_g, logits, neg)
    mg = jnp.max(lg, axis=-1, keepdims=True)
    if group is None:
        group = first_idx(lg == mg) - N_EXPERTS
        p_sel = 1.0 / jnp.sum(jnp.exp(lg - mg), axis=-1, keepdims=True)
    else:
        lsel = jnp.max(jnp.where(lane == group + N_EXPERTS, logits, neg), axis=-1, keepdims=True)
        p_sel = jnp.exp(lsel - mg) / jnp.sum(jnp.exp(lg - mg), axis=-1, keepdims=True)

    in_grp = (lane >= group * E_PER_GROUP) & (lane < (group + 1) * E_PER_GROUP)
    le = jnp.where(in_grp, logits, neg)
    v1 = jnp.max(le, axis=-1, keepdims=True)
    i1 = first_idx(le == v1)
    le2 = jnp.where(lane == i1, neg, le)
    v2 = jnp.max(le2, axis=-1, keepdims=True)
    i2 = first_idx(le2 == v2)
    e2 = jnp.exp(v2 - v1)
    w1 = p_sel / (1.0 + e2)
    w2 = p_sel * e2 / (1.0 + e2)
    return group, jnp.where(lane == i1, w1, 0.0) + jnp.where(lane == i2, w2, 0.0), i1, i2


def _merge_kernel(*refs, sorted_moe):
    if sorted_moe:
        (x_ref, or_ref, of_ref, mq_ref, mk_ref, mv_ref, gt_ref, wr_ref, wf_ref, wm_ref, wo_ref, g2_ref,
         wrt_ref, brt_ref, x1_ref, h2_ref, route_ref, *counts_ref) = refs
        o_m = jnp.concatenate([o for _, o in _mem_attend(mq_ref, mk_ref, mv_ref)], axis=1)
    else:
        (x_ref, or_ref, of_ref, om_ref, gt_ref, wr_ref, wf_ref, wm_ref, wo_ref, g2_ref,
         wrt_ref, brt_ref, x1_ref, h2_ref, route_ref) = refs
        o_m = om_ref[...]
    g = lambda b: gt_ref[:, b * D_MODEL:(b + 1) * D_MODEL].astype(F32)
    merged = (g(0) * _dot(or_ref[...], wr_ref[...]) + g(1) * _dot(of_ref[...], wf_ref[...])
              + g(2) * _dot(o_m, wm_ref[...]))
    x1 = x_ref[...] + _dot(merged.astype(BF16), wo_ref[...])
    x1_ref[...] = x1
    h2 = x1 * lax.rsqrt(jnp.mean(x1 * x1, axis=-1, keepdims=True) + EPS) * g2_ref[...]
    group, comb, i1, i2 = _route(_router_logits(h2, wrt_ref, brt_ref))
    if sorted_moe:
        tm = h2.shape[0]
        _to_token_tiles(h2_ref, h2)
        e_lo = jnp.minimum(i1, i2) - group * E_PER_GROUP
        e_hi = jnp.maximum(i1, i2) - group * E_PER_GROUP
        cls = group * N_PAIRS + (e_lo * E_PER_GROUP - e_lo * (e_lo + 1.0) * 0.5 + (e_hi - e_lo - 1.0))
        lane = lax.broadcasted_iota(jnp.int32, (tm, LANES), 1).astype(F32)
        onehot = jnp.where(lane == cls, 1.0, 0.0)
        r = lax.broadcasted_iota(jnp.int32, (tm, tm), 0)
        c = lax.broadcasted_iota(jnp.int32, (tm, tm), 1)
        before = _dot(jnp.where(r > c, 1.0, 0.0).astype(BF16), onehot.astype(BF16))
        rank = jnp.sum(before * onehot, axis=-1, keepdims=True)
        col = lax.broadcasted_iota(jnp.int32, route_ref.shape, 1)
        route_ref[...] = jnp.where(col == 0, group, jnp.where(col == 1, cls, jnp.where(col == 2, rank, 0.0)))
        counts_ref[0][...] = jnp.broadcast_to(jnp.sum(onehot, axis=0, keepdims=True), counts_ref[0].shape)
    else:
        h2_ref[...] = h2.astype(BF16)
        for gi in range(N_GROUPS):
            route_ref[gi] = comb[:, gi * E_PER_GROUP:(gi + 1) * E_PER_GROUP]


def _merge(x2d, o_r, o_f, mem, gates, tm, sorted_moe, wr, wf, wm, wo, g2, wrt, brt):
    t = x2d.shape[0]
    row = lambda w: pl.BlockSpec((tm, w), lambda i: (i, 0))
    sds = jax.ShapeDtypeStruct
    if sorted_moe:
        h2_spec, h2_shape = pl.BlockSpec((tm * N_SLABS, LANES), lambda i: (i, 0)), sds((t * N_SLABS, LANES), F32)
        rt_specs = [row(E_PER_GROUP), pl.BlockSpec((1, 8, LANES), lambda i: (i, 0, 0))]
        rt_shapes = [sds((t, E_PER_GROUP), F32), sds((t // tm, 8, LANES), F32)]
        mq, mk, mv, tiles_per_seq = mem
        mem_args = [mq, mk, mv]
        mem_kv = pl.BlockSpec((N_MEM, MEM_W), lambda i: (i // tiles_per_seq, 0))
        mem_specs = [row(MEM_W), mem_kv, mem_kv]
    else:
        h2_spec, h2_shape = row(D_MODEL), sds((t, D_MODEL), BF16)
        rt_specs = [pl.BlockSpec((N_GROUPS, tm, E_PER_GROUP), lambda i: (0, i, 0))]
        rt_shapes = [sds((N_GROUPS, t, E_PER_GROUP), F32)]
        mem_args, mem_specs = [mem], [row(MEM_W)]
    return pl.pallas_call(
        functools.partial(_merge_kernel, sorted_moe=sorted_moe),
        grid=(t // tm,),
        in_specs=[row(D_MODEL), row(RV_W), row(FOX_W), *mem_specs, row(GATE_W),
                  _const_spec((RV_W, D_MODEL)), _const_spec((FOX_W, D_MODEL)), _const_spec((MEM_W, D_MODEL)),
                  _const_spec((D_MODEL, D_MODEL)), _const_spec((1, D_MODEL)),
                  _const_spec((D_MODEL, 2 * LANES)), _const_spec((1, LANES))],
        out_specs=[row(D_MODEL), h2_spec, *rt_specs],
        out_shape=[sds((t, D_MODEL), F32), h2_shape, *rt_shapes],
        compiler_params=_cparams(("parallel",)),
        name="merge",
    )(x2d, o_r, o_f, *mem_args, gates, wr, wf, wm, wo, g2, wrt, brt)


def _positions_kernel(route_ref, first_ref, pos_ref, *, tm):
    lane = lax.broadcasted_iota(jnp.int32, (tm, LANES), 1).astype(F32)
    for i in range(first_ref.shape[0]):
        route = route_ref[i * tm:(i + 1) * tm, :]
        first = jnp.sum(jnp.where(lane == route[:, 1:2], first_ref[i, 0:1, :], 0.0), axis=-1, keepdims=True)
        pos_ref[i * tm:(i + 1) * tm, :] = jnp.broadcast_to(first + route[:, 2:3], route.shape).astype(jnp.int32)


def _positions(route, first_row, tm, tiles_per_step):
    t = route.shape[0]
    rows = tm * tiles_per_step
    return pl.pallas_call(
        functools.partial(_positions_kernel, tm=tm),
        grid=(t // rows,),
        in_specs=[pl.BlockSpec((rows, E_PER_GROUP), lambda i: (i, 0)),
                  pl.BlockSpec((tiles_per_step, 8, LANES), lambda i: (i, 0, 0))],
        out_specs=pl.BlockSpec((rows, E_PER_GROUP), lambda i: (i, 0)),
        out_shape=jax.ShapeDtypeStruct((t, E_PER_GROUP), jnp.int32),
        compiler_params=_cparams(("parallel",)),
        name="positions",
    )(route, first_row)[:, 0]


def _group_experts(h, cw, wg_ref, wu_ref, wd_ref, act):
    for e in range(E_PER_GROUP):
        a = _dot(h, wg_ref[0, e])
        u = _dot(h, wu_ref[0, e])
        act[:, e * D_EXPERT:(e + 1) * D_EXPERT] = ((a * jax.nn.sigmoid(a)) * u * cw(e)).astype(BF16)
    return _dot(act[...], wd_ref[0])


N_SLABS = D_MODEL // LANES
SUBLANES = 8


def _to_token_tiles(ref, x):
    for s in range(N_SLABS):
        ref[pl.ds(s, x.shape[0], stride=N_SLABS), :] = x[:, s * LANES:(s + 1) * LANES]


def _landing_shape(rows):
    return (rows // SUBLANES, N_SLABS, SUBLANES, LANES)


def _from_landing(buf):
    rows = buf.shape[0] * SUBLANES
    return jnp.concatenate([buf[:, s].reshape(rows, LANES) for s in range(N_SLABS)], axis=1)


def _row_gather(idx_ref, base, src_hbm, dst, sem, straight_line=False):
    def body(i, _):
        for u in range(SUBLANES):
            pltpu.make_async_copy(src_hbm.at[idx_ref[base + i * SUBLANES + u]], dst.at[i, :, u, :],
                                  sem).start(priority=u % 2)
        return 0
    if straight_line:
        for i in range(dst.shape[0]):
            body(i, 0)
    else:
        lax.fori_loop(0, dst.shape[0], body, 0)


def _row_gather_wait(dst, sem):
    pltpu.make_async_copy(dst, dst, sem).wait()


def _moe_sorted_kernel(tg_ref, nvt_ref, pos_ref, h3_hbm, zeros_hbm, wrt_ref, brt_ref,
                       wg_ref, wu_ref, wd_ref, y_ref, xbuf, sem, hb, yacc, src_ref, clear_sem, *, tm):
    k = pl.program_id(0)
    nvt = nvt_ref[0]
    slot = lax.rem(k, 2)

    @pl.when(k == 0)
    def _():
        clear = pltpu.make_async_copy(zeros_hbm, src_ref, clear_sem)
        clear.start()
        clear.wait()

        def invert(t, _):
            src_ref[pos_ref[t]] = t
            return 0

        lax.fori_loop(0, pos_ref.shape[0], invert, 0, unroll=32)

    @pl.when((k == 0) & (nvt > 0))
    def _():
        _row_gather(src_ref, 0, h3_hbm, xbuf.at[0], sem.at[0])

    @pl.when(k < nvt)
    def _():
        _row_gather_wait(xbuf.at[slot], sem.at[slot])
        _row_gather(src_ref, jnp.minimum(k + 1, nvt - 1) * tm, h3_hbm, xbuf.at[1 - slot], sem.at[1 - slot],
                    straight_line=True)

        x = _from_landing(xbuf.at[slot])
        group = tg_ref[k]
        _, comb, _, _ = _route(_router_logits(x, wrt_ref, brt_ref), group.astype(F32))
        lane = lax.broadcasted_iota(jnp.int32, comb.shape, 1)
        hb[...] = x.astype(BF16)
        yacc[...] = jnp.zeros(yacc.shape, F32)
        col_max = jnp.max(comb, axis=0, keepdims=True)
        used = [jnp.max(jnp.where(lane[:1] == group * E_PER_GROUP + e, col_max, 0.0)) for e in range(E_PER_GROUP)]
        for e in range(E_PER_GROUP):
            @pl.when(used[e] > 0.0)
            def _(e=e):
                cw = jnp.sum(jnp.where(lane == group * E_PER_GROUP + e, comb, 0.0), axis=-1, keepdims=True)
                h = hb[...]
                a = _dot(h, wg_ref[0, e])
                u = _dot(h, wu_ref[0, e])
                act = ((a * jax.nn.sigmoid(a)) * u * cw).astype(BF16)
                yacc[...] += _dot(act, wd_ref[0, e * D_EXPERT:(e + 1) * D_EXPERT, :])

        _to_token_tiles(y_ref, yacc[...])

        @pl.when(k + 1 >= nvt)
        def _():
            _row_gather_wait(xbuf.at[1 - slot], sem.at[1 - slot])

    @pl.when(k >= nvt)
    def _():
        y_ref[...] = jnp.zeros(y_ref.shape, F32)


def _moe_sorted(h3, tile_group, n_valid_tiles, pos, n_tiles, tm, wrt, brt, wg, wu, wd):
    wspec = lambda shape: pl.BlockSpec(shape, lambda k, tg, nv, ps: (tg[k],) + (0,) * (len(shape) - 1))
    cspec = lambda shape: pl.BlockSpec(shape, lambda k, tg, nv, ps: (0,) * len(shape),
                                       pipeline_mode=pl.Buffered(1))
    return pl.pallas_call(
        functools.partial(_moe_sorted_kernel, tm=tm),
        grid_spec=pltpu.PrefetchScalarGridSpec(
            num_scalar_prefetch=3,
            grid=(n_tiles,),
            in_specs=[pl.BlockSpec(memory_space=pl.ANY), pl.BlockSpec(memory_space=pl.ANY),
                      cspec((D_MODEL, 2 * LANES)), cspec((1, LANES)),
                      wspec((1, E_PER_GROUP, D_MODEL, D_EXPERT)), wspec((1, E_PER_GROUP, D_MODEL, D_EXPERT)),
                      wspec((1, E_PER_GROUP * D_EXPERT, D_MODEL))],
            out_specs=pl.BlockSpec((tm * N_SLABS, LANES), lambda k, tg, nv, ps: (k, 0)),
            scratch_shapes=[pltpu.VMEM((2,) + _landing_shape(tm), F32), pltpu.SemaphoreType.DMA((2,)),
                            pltpu.VMEM((tm, D_MODEL), BF16), pltpu.VMEM((tm, D_MODEL), F32),
                            pltpu.SMEM((n_tiles * tm,), jnp.int32), pltpu.SemaphoreType.DMA(())]),
        out_shape=jax.ShapeDtypeStruct((n_tiles * tm * N_SLABS, LANES), F32),
        compiler_params=_cparams(("arbitrary",)),
        name="moe_sorted",
    )(tile_group, n_valid_tiles, pos, h3, jnp.zeros((n_tiles * tm,), jnp.int32), wrt, brt, wg, wu, wd)


def _combine_kernel(pos_ref, y3_hbm, x1_ref, o_ref, ybuf, sem, *, tm):
    k = pl.program_id(0)
    slot = lax.rem(k, 2)

    @pl.when(k == 0)
    def _():
        _row_gather(pos_ref, 0, y3_hbm, ybuf.at[0], sem.at[0])

    _row_gather_wait(ybuf.at[slot], sem.at[slot])

    @pl.when(k + 1 < pl.num_programs(0))
    def _():
        _row_gather(pos_ref, (k + 1) * tm, y3_hbm, ybuf.at[1 - slot], sem.at[1 - slot])

    o_ref[...] = x1_ref[...] + _from_landing(ybuf.at[slot])


def _combine(y3, pos, x1, tm):
    t = x1.shape[0]
    return pl.pallas_call(
        functools.partial(_combine_kernel, tm=tm),
        grid_spec=pltpu.PrefetchScalarGridSpec(
            num_scalar_prefetch=1,
            grid=(t // tm,),
            in_specs=[pl.BlockSpec(memory_space=pl.ANY), pl.BlockSpec((tm, D_MODEL), lambda k, ps: (k, 0))],
            out_specs=pl.BlockSpec((tm, D_MODEL), lambda k, ps: (k, 0)),
            scratch_shapes=[pltpu.VMEM((2,) + _landing_shape(tm), F32), pltpu.SemaphoreType.DMA((2,))]),
        out_shape=jax.ShapeDtypeStruct((t, D_MODEL), F32),
        compiler_params=_cparams(("arbitrary",)),
        name="combine",
    )(pos, y3, x1)


def _moe_kernel(h_ref, comb_ref, x1_ref, wg_ref, wu_ref, wd_ref, o_ref, act):
    g = pl.program_id(1)
    comb = comb_ref[0]
    y = _group_experts(h_ref[...], lambda e: comb[:, e:e + 1], wg_ref, wu_ref, wd_ref, act)

    @pl.when(g == 0)
    def _():
        o_ref[...] = x1_ref[...] + y

    @pl.when(g != 0)
    def _():
        o_ref[...] += y


def _moe(h2, comb, x1, tm, wg, wu, wd):
    t = h2.shape[0]
    return pl.pallas_call(
        _moe_kernel,
        grid=(t // tm, N_GROUPS),
        in_specs=[pl.BlockSpec((tm, D_MODEL), lambda i, g: (i, 0)),
                  pl.BlockSpec((1, tm, E_PER_GROUP), lambda i, g: (g, i, 0)),
                  pl.BlockSpec((tm, D_MODEL), lambda i, g: (i, 0)),
                  pl.BlockSpec((1, E_PER_GROUP, D_MODEL, D_EXPERT), lambda i, g: (g, 0, 0, 0)),
                  pl.BlockSpec((1, E_PER_GROUP, D_MODEL, D_EXPERT), lambda i, g: (g, 0, 0, 0)),
                  pl.BlockSpec((1, E_PER_GROUP * D_EXPERT, D_MODEL), lambda i, g: (g, 0, 0))],
        out_specs=pl.BlockSpec((tm, D_MODEL), lambda i, g: (i, 0)),
        out_shape=jax.ShapeDtypeStruct((t, D_MODEL), F32),
        scratch_shapes=[pltpu.VMEM((tm, E_PER_GROUP * D_EXPERT), BF16)],
        compiler_params=_cparams(("parallel", "arbitrary")),
        name="moe",
    )(h2, comb, x1, wg, wu, wd)


def _rope_tables(pos):
    half = DK_R // 2
    inv = ROPE_BASE ** (-jnp.arange(half, dtype=F32) / half)
    ang = pos.astype(F32)[:, None] * inv[None, :]
    c = jnp.cos(ang)
    s = jnp.sin(ang)
    return jnp.concatenate([c, c, c, c], axis=-1), jnp.concatenate([-s, s, -s, s], axis=-1)


def kernel(x_prompt, x_sample, state_ret, cache_fox_k, cache_fox_v, cache_fox_logf, cache_mem_k, cache_mem_v,
           mem_prompt, g_norm1, w_in, b_forget, g_fox_q, g_fox_k, g_mem_q, g_mem_in, w_mem_kv, g_mem_k,
           g_ret_out, w_br_ret, w_br_fox, w_br_mem, w_out, g_norm2, w_route_group, b_route_group,
           w_route_expert, b_route_expert, w_exp_gate, w_exp_up, w_exp_down):
    nb, seq, _ = x_prompt.shape
    nbs, n_new, _ = x_sample.shape
    past = cache_fox_k.shape[2]
    l = 0

    wi = w_in[l]
    o_ff = 2 * RQK_W + 2 * RV_W + 3 * FOX_W
    w_in_r = jnp.concatenate(
        [wi[:, :o_ff], wi[:, o_ff + H_F:], wi[:, o_ff:o_ff + H_F],
         jnp.zeros((D_MODEL, LANES - H_F), F32)], axis=1).astype(BF16)
    bf_pad = jnp.concatenate([b_forget[l], jnp.zeros((LANES - H_F,), F32)])[None, :]
    g1 = g_norm1[l][None, :]
    gfq = jnp.tile(g_fox_q[l], H_F)[None, :]
    gfk = jnp.tile(g_fox_k[l], H_F)[None, :]
    gmq = g_mem_q[l][None, :]
    hid = jnp.arange(MXU_DIM) // D_F
    bd = jnp.where(hid[:, None] == hid[None, :], 1.0 / D_F, 0.0).astype(BF16)
    wr = w_br_ret[l].astype(BF16)
    wf = w_br_fox[l].astype(BF16)
    wm = w_br_mem[l].astype(BF16)
    wo = w_out[l].astype(BF16)
    g2 = g_norm2[l][None, :]
    n_e = N_GROUPS * E_PER_GROUP
    wrt = jnp.concatenate([w_route_expert[l], w_route_group[l],
                           jnp.zeros((D_MODEL, LANES - n_e - N_GROUPS), F32)], axis=1)
    wrt_hi = wrt.astype(BF16)
    wrt2 = jnp.concatenate([wrt_hi, (wrt - wrt_hi.astype(F32)).astype(BF16)], axis=1)
    brt = jnp.concatenate([b_route_expert[l], b_route_group[l],
                           jnp.zeros((LANES - n_e - N_GROUPS,), F32)])[None, :]
    wg = w_exp_gate[l].astype(BF16)
    wu = w_exp_up[l].astype(BF16)
    wd = w_exp_down[l].astype(BF16).reshape(N_GROUPS, E_PER_GROUP * D_EXPERT, D_MODEL)
    gro = g_ret_out[l][None, :]
    prep_w = (g1, w_in_r, bf_pad, gfq, gfk, gmq, bd)
    merge_w = (wr, wf, wm, wo, g2, wrt2, brt)

    tm = 512
    xp = x_prompt.reshape(nb * seq, D_MODEL)
    (rq, rk, rv, rg, fqt, fkt, fvt, lft, mq, gates, fkb, lf) = _prep(
        xp, _rope_tables(jnp.arange(seq)), tm, seq // tm, True, *prep_w)
    o_r, s_fin = _retention(rq, rk, rv, rg, nb, seq, 256, gro, None)
    fcum = _cumsum_cols(lf.reshape(nb, seq, H_F), 256)
    fcum = jnp.swapaxes(fcum.reshape(nb, seq, H_F // 2, 2), 1, 2).reshape(nb * (H_F // 2) * seq, 2)
    o_f = _fox_prompt(fqt, fkb, fvt, fcum, nb, seq, 256)
    mk, mv = _memkv(mem_prompt.reshape(nb * N_MEM, D_MODEL), nb, g_mem_in[l][None, :],
                    w_mem_kv[l].astype(BF16), g_mem_k[l][None, :])
    x1, h3, route, counts = _merge(xp, o_r, o_f, (mq, mk, mv, seq // tm), gates, tm, True, *merge_w)
    t_p = nb * seq
    n_tiles = t_p // tm + N_GROUPS
    n_cls = N_GROUPS * N_PAIRS
    cnt = counts[:, 0, :n_cls].astype(jnp.int32)
    before_tile = jnp.cumsum(cnt, axis=0) - cnt
    cls_tot = jnp.sum(cnt, axis=0)
    grp_tot = jnp.sum(cls_tot.reshape(N_GROUPS, N_PAIRS), axis=1)
    tiles_g = (grp_tot + tm - 1) // tm
    tile_end = jnp.cumsum(tiles_g)
    row_start = (tile_end - tiles_g) * tm
    in_grp = cls_tot.reshape(N_GROUPS, N_PAIRS)
    cls_start = (row_start[:, None] + jnp.cumsum(in_grp, axis=1) - in_grp).reshape(n_cls)
    first_row = jnp.pad((cls_start[None, :] + before_tile).astype(F32), ((0, 0), (0, LANES - n_cls)))
    pos = _positions(route, jnp.broadcast_to(first_row[:, None, :], (t_p // tm, 8, LANES)), tm,
                     min(8, t_p // tm))
    tile_ids = jnp.arange(n_tiles, dtype=jnp.int32)
    tile_group = jnp.minimum(jnp.sum((tile_ids[:, None] >= tile_end[None, :]).astype(jnp.int32), axis=1),
                             N_GROUPS - 1)
    y3 = _moe_sorted(h3.reshape(t_p, N_SLABS, LANES), tile_group, tile_end[-1:], pos, n_tiles, tm,
                     wrt2, brt, wg, wu, wd)
    y_prompt = _combine(y3.reshape(n_tiles * tm, N_SLABS, LANES), pos, x1, 2 * tm).reshape(nb, seq, D_MODEL)

    ts = nbs * n_new
    xs = x_sample.reshape(ts, D_MODEL)
    pos_s = jnp.tile(past + jnp.arange(n_new), nbs)
    (rq_s, rk_s, rv_s, rg_s, fq_s, fk_s, fv_s, lf_s, mq_s, gates_s) = _prep(
        xs, _rope_tables(pos_s), ts, 1, False, *prep_w)
    o_r_s, s_new = _retention(rq_s, rk_s, rv_s, rg_s, nbs, n_new, n_new, gro, state_ret[l])
    pad = (-(past + n_new)) % 256
    lf_rows = jnp.concatenate([jnp.swapaxes(cache_fox_logf[l], 1, 2),
                               jnp.swapaxes(lf_s.reshape(nbs, n_new, H_F), 1, 2),
                               jnp.zeros((nbs, H_F, pad), F32)], axis=2).reshape(nbs * H_F, past + n_new + pad)
    f_all = _cumsum(lf_rows, 256)
    feat_major = lambda c: jnp.transpose(c, (0, 2, 3, 1)).reshape(nbs * FOX_W, past)
    o_f_s = _fox_sample(fq_s, feat_major(cache_fox_k[l]), feat_major(cache_fox_v[l]), fk_s, fv_s,
                        f_all[:, :past].reshape(nbs, H_F, past),
                        f_all[:, past:past + n_new].reshape(nbs, H_F, n_new), nbs, n_new, past)
    o_m_s = _memattn(mq_s, cache_mem_k[l].reshape(nbs * N_MEM, MEM_W),
                     cache_mem_v[l].reshape(nbs * N_MEM, MEM_W), nbs, n_new, n_new)
    x1_s, h2_s, comb_s = _merge(xs, o_r_s, o_f_s, o_m_s, gates_s, ts, False, *merge_w)
    y_sample = _moe(h2_s, comb_s, x1_s, ts, wg, wu, wd).reshape(nbs, n_new, D_MODEL)

    token_major = lambda a: jnp.transpose(a.reshape(nb, H_F, D_F, seq), (0, 3, 1, 2))

    return (y_prompt, y_sample,
            s_fin[None], token_major(fkt)[None], token_major(fvt)[None],
            jnp.swapaxes(lft.reshape(nb, H_F, seq), 1, 2)[None],
            mk.reshape(1, nb, N_MEM, H_M, D_M), mv.reshape(1, nb, N_MEM, H_M, D_M),
            s_new[None], fk_s.reshape(1, nbs, n_new, H_F, D_F), fv_s.reshape(1, nbs, n_new, H_F, D_F),
            lf_s.reshape(1, nbs, n_new, H_F))
```

```python
import functools

import jax
import jax.numpy as jnp
from jax import lax
from jax.experimental import pallas as pl
from jax.experimental.pallas import tpu as pltpu

F32 = jnp.float32
BF16 = jnp.bfloat16

D_MODEL = 1024
H_R, DK_R, DV_R = 4, 64, 128
H_F, D_F = 8, 64
H_M, D_M = 4, 128
N_MEM = 256
N_GROUPS, E_PER_GROUP, D_EXPERT = 4, 8, 256
ROPE_BASE = 10000.0
EPS = 1e-6
LOG2E = 1.4426950408889634

RQK_W = H_R * DK_R
RV_W = H_R * DV_R
FOX_W = H_F * D_F
MEM_W = H_M * D_M
GATE_W = 3 * D_MODEL
LANES = 128
MXU_DIM = 256

C_RQ = 0
C_RK = C_RQ + RQK_W
C_RV = C_RK + RQK_W
C_RG = C_RV + RV_W
C_FQ = C_RG + RV_W
C_FK = C_FQ + FOX_W
C_FV = C_FK + FOX_W
C_MQ = C_FV + FOX_W
C_GT = C_MQ + MEM_W
C_FF = C_GT + GATE_W
N_IN_PAD = C_FF + LANES

VMEM_LIMIT = 56 * 1024 * 1024


def _cparams(sem):
    return pltpu.CompilerParams(dimension_semantics=sem, vmem_limit_bytes=VMEM_LIMIT)


def _const_spec(shape):
    nd = len(shape)
    return pl.BlockSpec(shape, lambda *_: (0,) * nd, pipeline_mode=pl.Buffered(1))


def _dot(a, b):
    return jnp.dot(a, b, preferred_element_type=F32)


def _dot_nt(a, b):
    return lax.dot_general(a, b, (((1,), (1,)), ((), ())), preferred_element_type=F32)


def _dot_tn(a, b):
    return lax.dot_general(a, b, (((0,), (0,)), ((), ())), preferred_element_type=F32)


def _lane_rmsnorm(z):
    return z * lax.rsqrt(jnp.mean(z * z, axis=-1, keepdims=True) + EPS)


def _prep_kernel(x_ref, g1_ref, w_ref, cos_ref, sin_ref, bf_ref, gfq_ref, gfk_ref, gmq_ref, bd_ref,
                 rq_ref, rk_ref, rv_ref, rg_ref, fq_ref, fk_ref, fv_ref, lf_ref, mq_ref, gt_ref, *extra,
                 seq_minor):
    x = x_ref[...]
    h = (x * lax.rsqrt(jnp.mean(x * x, axis=-1, keepdims=True) + EPS) * g1_ref[...]).astype(BF16)

    def proj(c0, width):
        return _dot(h, w_ref[:, c0:c0 + width])

    cos = cos_ref[...]
    sin = sin_ref[...]
    lane = lax.broadcasted_iota(jnp.int32, cos.shape, 1)
    first_half = (lane % DK_R) < (DK_R // 2)

    def rope(z):
        swapped = jnp.where(first_half, pltpu.roll(z, LANES - DK_R // 2, 1), pltpu.roll(z, DK_R // 2, 1))
        return z * cos + swapped * sin

    zq = proj(C_RQ, RQK_W)
    zk = proj(C_RK, RQK_W)
    for p in range(RQK_W // LANES):
        sl = slice(p * LANES, (p + 1) * LANES)
        rq_ref[:, sl] = rope(zq[:, sl])
        rk_ref[:, sl] = rope(zk[:, sl]) * (DK_R ** -0.5)
    rv_ref[...] = proj(C_RV, RV_W)
    rg_ref[...] = proj(C_RG, RV_W)

    def head64_norm(z, g_ref):
        zz = (z * z).astype(BF16)
        w = bd_ref.shape[0]
        ms = jnp.concatenate([_dot(zz[:, c:c + w], bd_ref[...]) for c in range(0, FOX_W, w)], axis=1)
        return z * lax.rsqrt(ms + EPS) * g_ref[...]

    fq = head64_norm(proj(C_FQ, FOX_W), gfq_ref) * (D_F ** -0.5 * LOG2E)
    fk = head64_norm(proj(C_FK, FOX_W), gfk_ref)
    fv = proj(C_FV, FOX_W)
    if seq_minor:
        fq_ref[...] = fq.T.astype(BF16)
        fk_ref[...] = fk.T
        fv_ref[...] = fv.T
        extra[0][...] = fk.astype(BF16)
    else:
        fq_ref[...] = fq.astype(BF16)
        fk_ref[...] = fk
        fv_ref[...] = fv

    zm = proj(C_MQ, MEM_W)
    for hh in range(H_M):
        sl = slice(hh * D_M, (hh + 1) * D_M)
        mq_ref[:, sl] = (_lane_rmsnorm(zm[:, sl]) * gmq_ref[...]).astype(BF16)

    for b in range(3):
        gt_ref[:, b * D_MODEL:(b + 1) * D_MODEL] = jax.nn.sigmoid(proj(C_GT + b * D_MODEL, D_MODEL)).astype(BF16)

    v = proj(C_FF, LANES) + bf_ref[...]
    lf = jnp.minimum(v, 0.0) - jnp.log1p(jnp.exp(-jnp.abs(v)))
    if seq_minor:
        lf_ref[...] = lf.T[:H_F, :]
        extra[1][...] = lf[:, :H_F]
    else:
        lf_ref[...] = lf[:, :H_F]


def _prep(x2d, tables, tm, n_pos_tiles, seq_minor, g1, w_in_r, bf_pad, gfq, gfk, gmq, bd):
    t = x2d.shape[0]
    cos_t, sin_t = tables
    row = lambda w: pl.BlockSpec((tm, w), lambda i: (i, 0))
    pos = pl.BlockSpec((tm, LANES), lambda i: (i % n_pos_tiles, 0))
    sds = jax.ShapeDtypeStruct
    if seq_minor:
        nb, seq = t // (n_pos_tiles * tm), n_pos_tiles * tm
        fm = lambda rows_: pl.BlockSpec((rows_, tm), lambda i: (i // n_pos_tiles, i % n_pos_tiles))
        fox_shapes = [sds((nb * FOX_W, seq), BF16), sds((nb * FOX_W, seq), F32), sds((nb * FOX_W, seq), F32),
                      sds((nb * H_F, seq), F32)]
        fox_specs = [fm(FOX_W), fm(FOX_W), fm(FOX_W), fm(H_F)]
        extra_shapes, extra_specs = [sds((t, FOX_W), BF16), sds((t, H_F), F32)], [row(FOX_W), row(H_F)]
    else:
        fox_shapes = [sds((t, FOX_W), BF16), sds((t, FOX_W), F32), sds((t, FOX_W), F32), sds((t, H_F), F32)]
        fox_specs = [row(FOX_W), row(FOX_W), row(FOX_W), row(H_F)]
        extra_shapes, extra_specs = [], []
    out_shapes = [sds((t, RQK_W), F32), sds((t, RQK_W), F32), sds((t, RV_W), F32), sds((t, RV_W), F32),
                  *fox_shapes, sds((t, MEM_W), BF16), sds((t, GATE_W), BF16), *extra_shapes]
    out_specs = [row(RQK_W), row(RQK_W), row(RV_W), row(RV_W), *fox_specs, row(MEM_W), row(GATE_W),
                 *extra_specs]
    return pl.pallas_call(
        functools.partial(_prep_kernel, seq_minor=seq_minor),
        grid=(t // tm,),
        in_specs=[row(D_MODEL), _const_spec((1, D_MODEL)), _const_spec((D_MODEL, N_IN_PAD)), pos, pos,
                  _const_spec((1, LANES)), _const_spec((1, FOX_W)), _const_spec((1, FOX_W)),
                  _const_spec((1, D_M)), _const_spec((MXU_DIM, MXU_DIM))],
        out_specs=out_specs,
        out_shape=out_shapes,
        compiler_params=_cparams(("parallel",)),
        name="prep",
    )(x2d, g1, w_in_r, cos_t, sin_t, bf_pad, gfq, gfk, gmq, bd)


def _ret_kernel(*refs, has_init, ch):
    if has_init:
        (rq_ref, rk_ref, rv_ref, rg_ref, dmat_ref, qdec_ref, kdec_ref, gpow_ref, gro_ref, s0_ref,
         o_ref, sfin_ref) = refs
    else:
        (rq_ref, rk_ref, rv_ref, rg_ref, dmat_ref, qdec_ref, kdec_ref, gpow_ref, gro_ref,
         o_ref, sfin_ref) = refs

    state = []
    for h in range(H_R):
        if has_init:
            pad = jnp.zeros((DK_R, DV_R), F32)
            state.append(jnp.concatenate([s0_ref[0, h], pad] if h % 2 == 0 else [pad, s0_ref[0, h]], axis=0))
        else:
            state.append(jnp.zeros((LANES, DV_R), F32))

    lane = lax.broadcasted_iota(jnp.int32, (ch, LANES), 1)
    for c in range(rq_ref.shape[0] // ch):
        rows = slice(c * ch, (c + 1) * ch)
        for p in range(H_R // 2):
            sl = slice(p * LANES, (p + 1) * LANES)
            q2 = rq_ref[rows, sl]
            k2 = rk_ref[rows, sl]
            kd2 = k2 * kdec_ref[p]
            for hh in range(2):
                h = 2 * p + hh
                mine = (lane >= hh * DK_R) & (lane < (hh + 1) * DK_R)
                qm = jnp.where(mine, q2, 0.0).astype(BF16)
                kdm = jnp.where(mine, kd2, 0.0).astype(BF16)
                v = rv_ref[rows, h * DV_R:(h + 1) * DV_R].astype(BF16)
                sc = _dot_nt(qm, k2.astype(BF16)) * dmat_ref[h]
                o = _dot(sc.astype(BF16), v) + _dot(qm, state[h].astype(BF16)) * qdec_ref[h]
                state[h] = gpow_ref[h] * state[h] + _dot_tn(kdm, v)
                normed = _lane_rmsnorm(o) * gro_ref[...]
                rg = rg_ref[rows, h * DV_R:(h + 1) * DV_R]
                o_ref[rows, h * DV_R:(h + 1) * DV_R] = (normed * (rg * jax.nn.sigmoid(rg))).astype(BF16)

    for h in range(H_R):
        r0 = DK_R * (h % 2)
        sfin_ref[0, h] = state[h][r0:r0 + DK_R, :]


def _retention(rq, rk, rv, rg, n_batch, seq, ch, g_ret_out, state0):
    lg = jnp.log1p(-jnp.exp2(-5.0 - jnp.arange(H_R, dtype=F32)))
    idx = jnp.arange(ch, dtype=F32)
    diff = idx[:, None] - idx[None, :]
    causal = diff >= 0
    dmat = jnp.where(causal[None], jnp.exp(jnp.where(causal, diff, 0.0)[None] * lg[:, None, None]), 0.0)
    q_dec = jnp.exp((idx + 1.0)[None, :] * lg[:, None])
    k_dec = jnp.exp((ch - 1.0 - idx)[None, :] * lg[:, None])
    qdec = jnp.broadcast_to(q_dec[:, :, None], (H_R, ch, DV_R))
    kdec = jnp.broadcast_to(k_dec[:, :, None], (H_R, ch, DK_R))
    kdec = kdec.reshape(H_R // 2, 2, ch, DK_R).transpose(0, 2, 1, 3).reshape(H_R // 2, ch, LANES)
    gpow = jnp.broadcast_to(jnp.exp(ch * lg)[:, None, None], (H_R, 1, DV_R))

    has_init = state0 is not None
    blk = lambda w: pl.BlockSpec((seq, w), lambda b: (b, 0))
    in_specs = [blk(RQK_W), blk(RQK_W), blk(RV_W), blk(RV_W),
                _const_spec((H_R, ch, ch)), _const_spec((H_R, ch, DV_R)),
                _const_spec((H_R // 2, ch, LANES)), _const_spec((H_R, 1, DV_R)), _const_spec((1, DV_R))]
    args = [rq, rk, rv, rg, dmat, qdec, kdec, gpow, g_ret_out]
    if has_init:
        in_specs.append(pl.BlockSpec((1, H_R, DK_R, DV_R), lambda b: (b, 0, 0, 0)))
        args.append(state0)
    return pl.pallas_call(
        functools.partial(_ret_kernel, has_init=has_init, ch=ch),
        grid=(n_batch,),
        in_specs=in_specs,
        out_specs=[blk(RV_W), pl.BlockSpec((1, H_R, DK_R, DV_R), lambda b: (b, 0, 0, 0))],
        out_shape=[jax.ShapeDtypeStruct((n_batch * seq, RV_W), BF16),
                   jax.ShapeDtypeStruct((n_batch, H_R, DK_R, DV_R), F32)],
        compiler_params=_cparams(("parallel",)),
        name="retention",
    )(*args)


def _split3(x):
    hi = x.astype(BF16)
    r1 = x - hi.astype(F32)
    mid = r1.astype(BF16)
    lo = (r1 - mid.astype(F32)).astype(BF16)
    return hi, mid, lo


def _cumsum_kernel(x_ref, o_ref, *, blk):
    rows, n = x_ref.shape
    r = lax.broadcasted_iota(jnp.int32, (blk, blk), 0)
    c = lax.broadcasted_iota(jnp.int32, (blk, blk), 1)
    tri = jnp.where(r <= c, 1.0, 0.0).astype(BF16)
    carry = jnp.zeros((rows, 1), F32)
    for i in range(n // blk):
        hi, mid, lo = _split3(x_ref[:, i * blk:(i + 1) * blk])
        cum = (_dot(hi, tri) + _dot(mid, tri)) + _dot(lo, tri) + carry
        o_ref[:, i * blk:(i + 1) * blk] = cum
        carry = cum[:, blk - 1:blk]


def _cumsum_cols_kernel(x_ref, o_ref, *, blk):
    n = x_ref.shape[1]
    r = lax.broadcasted_iota(jnp.int32, (blk, blk), 0)
    c = lax.broadcasted_iota(jnp.int32, (blk, blk), 1)
    tri = jnp.where(r >= c, 1.0, 0.0).astype(BF16)
    carry = jnp.zeros((1, x_ref.shape[2]), F32)
    for i in range(n // blk):
        hi, mid, lo = _split3(x_ref[0, i * blk:(i + 1) * blk, :])
        cum = (_dot(tri, hi) + _dot(tri, mid)) + _dot(tri, lo) + carry
        o_ref[0, i * blk:(i + 1) * blk, :] = cum
        carry = cum[blk - 1:blk, :]


def _cumsum_cols(x, blk):
    nb, t, w = x.shape
    spec = pl.BlockSpec((1, t, w), lambda b: (b, 0, 0))
    return pl.pallas_call(
        functools.partial(_cumsum_cols_kernel, blk=blk),
        grid=(nb,),
        in_specs=[spec], out_specs=spec,
        out_shape=jax.ShapeDtypeStruct(x.shape, F32),
        compiler_params=_cparams(("parallel",)),
        name="cumsum_cols",
    )(x)


def _cumsum(x, blk):
    r, t = x.shape
    spec = pl.BlockSpec((H_F, t), lambda b: (b, 0))
    return pl.pallas_call(
        functools.partial(_cumsum_kernel, blk=blk),
        grid=(r // H_F,),
        in_specs=[spec], out_specs=spec,
        out_shape=jax.ShapeDtypeStruct(x.shape, F32),
        compiler_params=_cparams(("parallel",)),
        name="cumsum",
    )(x)


def _head_rows_mask(shape, hh):
    sub = lax.broadcasted_iota(jnp.int32, shape, 0)
    return (sub >= hh * D_F) & (sub < (hh + 1) * D_F)


N_AUG = 3
V_ROWS = D_F + 16


def _fox_kernel(qt_ref, k_ref, vt_ref, fcum_ref, o_ref, qa, ka, vb, *, tq):
    seq = k_ref.shape[0]
    tk = tq
    qt = qt_ref[...].astype(F32)
    k = k_ref[...].astype(F32)
    sub = lax.broadcasted_iota(jnp.int32, qt.shape, 0)
    lane = lax.broadcasted_iota(jnp.int32, k.shape, 1)
    for hh in range(2):
        own, oth = hh * D_F, (1 - hh) * D_F
        qa[hh] = jnp.where((sub >= own) & (sub < own + D_F), qt,
                           jnp.where((sub >= oth) & (sub < oth + N_AUG), 1.0, 0.0)).astype(BF16)
        bias = fcum_ref[:, hh:hh + 1] * (-LOG2E)
        kk = jnp.where((lane >= own) & (lane < own + D_F), k, 0.0)
        for a, piece in enumerate(_split3(bias)):
            kk = jnp.where(lane == oth + a, piece.astype(F32), kk)
        ka[hh] = kk.astype(BF16)
    for hh in range(2):
        vb[hh, :D_F, :] = vt_ref[hh * D_F:(hh + 1) * D_F, :].astype(BF16)
        vb[hh, D_F:, :] = jnp.ones((V_ROWS - D_F, seq), BF16)
    causal = (lax.broadcasted_iota(jnp.int32, (tk, tq), 1) >= lax.broadcasted_iota(jnp.int32, (tk, tq), 0))

    pairs = [(i, j) for i in range(seq // tq) for j in range(i + 1)]

    def stage_a(i, j):
        return tuple(_dot(ka[hh, j * tk:(j + 1) * tk, :], qa[hh, :, i * tq:(i + 1) * tq]) for hh in range(2))

    def stage_b(i, j, ts, maxes):
        new_maxes, probs = [], []
        for hh in range(2):
            t = jnp.where(causal, ts[hh], -jnp.inf) if j == i else ts[hh]
            t_max = jnp.max(t, axis=0, keepdims=True)
            if j == 0:
                m_new, alpha = t_max, None
            else:
                m_new = jnp.maximum(maxes[hh], t_max)
                alpha = jnp.exp2(maxes[hh] - m_new)
            new_maxes.append(m_new)
            probs.append((jnp.exp2(t - m_new).astype(BF16), alpha))
        return new_maxes, probs

    def stage_c(i, j, probs, accs):
        out = []
        for hh in range(2):
            p, alpha = probs[hh]
            pv = _dot(vb[hh, :, j * tk:(j + 1) * tk], p)
            out.append(pv if j == 0 else alpha * accs[hh] + pv)
        return out

    scores, probs, maxes, accs = {}, {}, None, None
    for s in range(len(pairs) + 2):
        if s < len(pairs):
            scores[s] = stage_a(*pairs[s])
        if 0 <= s - 1 < len(pairs):
            maxes, probs[s - 1] = stage_b(*pairs[s - 1], scores.pop(s - 1), maxes)
        if 0 <= s - 2 < len(pairs):
            i, j = pairs[s - 2]
            accs = stage_c(i, j, probs.pop(s - 2), accs)
            if j == i:
                o_t = jnp.concatenate([a[:D_F] / a[D_F:D_F + 1] for a in accs], axis=0)
                o_ref[i * tq:(i + 1) * tq, :] = o_t.T.astype(BF16)


def _fox_prompt(fqt, fkb, fvt, fcum, n_batch, seq, tq):
    npair = H_F // 2
    fm_spec = pl.BlockSpec((LANES, seq), lambda b, p: (b * npair + p, 0))
    tok_spec = pl.BlockSpec((seq, LANES), lambda b, p: (b, p))
    return pl.pallas_call(
        functools.partial(_fox_kernel, tq=tq),
        grid=(n_batch, npair),
        in_specs=[fm_spec, tok_spec, fm_spec, pl.BlockSpec((seq, 2), lambda b, p: (b * npair + p, 0))],
        out_specs=tok_spec,
        out_shape=jax.ShapeDtypeStruct((n_batch * seq, FOX_W), BF16),
        scratch_shapes=[pltpu.VMEM((2, LANES, seq), BF16), pltpu.VMEM((2, seq, LANES), BF16),
                        pltpu.VMEM((2, V_ROWS, seq), BF16)],
        compiler_params=_cparams(("parallel", "parallel")),
        name="fox_prompt",
    )(fqt, fkb, fvt, fcum)


def _fox_sample_kernel(q_ref, kct_ref, vct_ref, kn_ref, vn_ref, frc_ref, frn_ref, o_ref):
    n = q_ref.shape[0]
    lane = lax.broadcasted_iota(jnp.int32, (n, LANES), 1)
    rows = lax.broadcasted_iota(jnp.int32, (n, n), 0)
    cols = lax.broadcasted_iota(jnp.int32, (n, n), 1)
    for pair in range(H_F // 2):
        sl = slice(pair * LANES, (pair + 1) * LANES)
        q2 = q_ref[:, sl]
        kct = kct_ref[sl, :]
        vct = vct_ref[sl, :].astype(BF16)
        kn = kn_ref[:, sl].astype(BF16)
        vn = vn_ref[:, sl].astype(BF16)
        outs = []
        for hh in range(2):
            h = 2 * pair + hh
            mine = (lane >= hh * D_F) & (lane < (hh + 1) * D_F)
            kc = jnp.where(_head_rows_mask(kct.shape, hh), kct, 0.0).astype(BF16)
            t_c = _dot(q2, kc) - frc_ref[0, h:h + 1, :] * LOG2E
            t_n = _dot_nt(jnp.where(mine, q2, jnp.zeros_like(q2)), kn) - frn_ref[0, h:h + 1, :] * LOG2E
            t_n = jnp.where(rows >= cols, t_n, -jnp.inf)
            m = jnp.maximum(jnp.max(t_c, axis=-1, keepdims=True), jnp.max(t_n, axis=-1, keepdims=True))
            p_c = jnp.exp2(t_c - m)
            p_n = jnp.exp2(t_n - m)
            l = jnp.sum(p_c, axis=-1, keepdims=True) + jnp.sum(p_n, axis=-1, keepdims=True)
            acc = _dot_nt(p_c.astype(BF16), vct) + _dot(p_n.astype(BF16), vn)
            outs.append(acc / l)
        o_ref[:, sl] = jnp.where(lane < D_F, outs[0], outs[1]).astype(BF16)


def _fox_sample(fq, kct, vct, kn, vn, frow_c, frow_n, n_batch, n_new, past):
    tok = pl.BlockSpec((n_new, FOX_W), lambda b: (b, 0))
    cache = pl.BlockSpec((FOX_W, past), lambda b: (b, 0))
    return pl.pallas_call(
        _fox_sample_kernel,
        grid=(n_batch,),
        in_specs=[tok, cache, cache, tok, tok,
                  pl.BlockSpec((1, H_F, past), lambda b: (b, 0, 0)),
                  pl.BlockSpec((1, H_F, n_new), lambda b: (b, 0, 0))],
        out_specs=tok,
        out_shape=jax.ShapeDtypeStruct((n_batch * n_new, FOX_W), BF16),
        compiler_params=_cparams(("parallel",)),
        name="fox_sample",
    )(fq, kct, vct, kn, vn, frow_c, frow_n)


def _memkv_kernel(m_ref, gin_ref, w_ref, gk_ref, k_ref, v_ref):
    x = m_ref[...]
    h = (x * lax.rsqrt(jnp.mean(x * x, axis=-1, keepdims=True) + EPS) * gin_ref[...]).astype(BF16)
    zk = _dot(h, w_ref[:, :MEM_W])
    for hh in range(H_M):
        sl = slice(hh * D_M, (hh + 1) * D_M)
        k_ref[:, sl] = _lane_rmsnorm(zk[:, sl]) * gk_ref[...]
    v_ref[...] = _dot(h, w_ref[:, MEM_W:])


def _memkv(mem2d, n_batch, g_mem_in, w_mem_kv, g_mem_k):
    blk = lambda w: pl.BlockSpec((N_MEM, w), lambda b: (b, 0))
    return pl.pallas_call(
        _memkv_kernel,
        grid=(n_batch,),
        in_specs=[blk(D_MODEL), _const_spec((1, D_MODEL)), _const_spec((D_MODEL, 2 * MEM_W)),
                  _const_spec((1, D_M))],
        out_specs=[blk(MEM_W), blk(MEM_W)],
        out_shape=[jax.ShapeDtypeStruct((n_batch * N_MEM, MEM_W), F32)] * 2,
        compiler_params=_cparams(("parallel",)),
        name="memkv",
    )(mem2d, g_mem_in, w_mem_kv, g_mem_k)


def _mem_attend(q_ref, k_ref, v_ref):
    for hh in range(H_M):
        sl = slice(hh * D_M, (hh + 1) * D_M)
        s = _dot_nt(q_ref[:, sl], k_ref[:, sl].astype(BF16)) * (D_M ** -0.5)
        m = jnp.max(s, axis=-1, keepdims=True)
        p = jnp.exp(s - m)
        l = jnp.sum(p, axis=-1, keepdims=True)
        yield sl, (_dot(p.astype(BF16), v_ref[:, sl].astype(BF16)) / l).astype(BF16)


def _memattn_kernel(q_ref, k_ref, v_ref, o_ref):
    for sl, o in _mem_attend(q_ref, k_ref, v_ref):
        o_ref[:, sl] = o


def _memattn(mq, mk, mv, n_batch, seq, tq):
    nq = seq // tq
    return pl.pallas_call(
        _memattn_kernel,
        grid=(n_batch, nq),
        in_specs=[pl.BlockSpec((tq, MEM_W), lambda b, i: (b * nq + i, 0)),
                  pl.BlockSpec((N_MEM, MEM_W), lambda b, i: (b, 0)),
                  pl.BlockSpec((N_MEM, MEM_W), lambda b, i: (b, 0))],
        out_specs=pl.BlockSpec((tq, MEM_W), lambda b, i: (b * nq + i, 0)),
        out_shape=jax.ShapeDtypeStruct((n_batch * seq, MEM_W), BF16),
        compiler_params=_cparams(("parallel", "arbitrary")),
        name="memattn",
    )(mq, mk, mv)


def _split2(x):
    hi = x.astype(BF16)
    return hi, (x - hi.astype(F32)).astype(BF16)


N_EXPERTS = N_GROUPS * E_PER_GROUP


N_PAIRS = E_PER_GROUP * (E_PER_GROUP - 1) // 2


def _router_logits(h2, wrt_ref, brt_ref):
    h_hi, h_lo = _split2(h2)
    z = _dot(h_hi, wrt_ref[...])
    return (z[:, :LANES] + z[:, LANES:]) + _dot(h_lo, wrt_ref[:, :LANES]) + brt_ref[...]


N_LOGIT_ROWS = 40


def _route(logits, group=None):
    lt = logits.T[:N_LOGIT_ROWS, :]
    idx = lax.broadcasted_iota(jnp.int32, lt.shape, 0).astype(F32)
    neg = -jnp.inf
    first_idx = lambda mask: jnp.min(jnp.where(mask, idx, float(LANES)), axis=0, keepdims=True)

    is_g = (idx >= N_EXPERTS) & (idx < N_EXPERTS + N_GROUPS)
    lg = jnp.where(is_g, lt, neg)
    mg = jnp.max(lg, axis=0, keepdims=True)
    if group is None:
        group = first_idx(lg == mg) - N_EXPERTS
        p_sel = 1.0 / jnp.sum(jnp.exp(lg - mg), axis=0, keepdims=True)
    else:
        lsel = jnp.max(jnp.where(idx == group + N_EXPERTS, lt, neg), axis=0, keepdims=True)
        p_sel = jnp.exp(lsel - mg) / jnp.sum(jnp.exp(lg - mg), axis=0, keepdims=True)

    in_grp = (idx >= group * E_PER_GROUP) & (idx < (group + 1) * E_PER_GROUP)
    le = jnp.where(in_grp, lt, neg)
    v1 = jnp.max(le, axis=0, keepdims=True)
    i1 = first_idx(le == v1)
    le2 = jnp.where(idx == i1, neg, le)
    v2 = jnp.max(le2, axis=0, keepdims=True)
    i2 = first_idx(le2 == v2)
    e2 = jnp.exp(v2 - v1)
    return group, i1, i2, p_sel / (1.0 + e2), p_sel * e2 / (1.0 + e2)


def _rows_to_columns(rows):
    n = rows[0].shape[1]
    idx = lax.broadcasted_iota(jnp.int32, (LANES, n), 0)
    stacked = jnp.zeros((LANES, n), F32)
    for r, row in enumerate(rows):
        stacked = jnp.where(idx == r, row, stacked)
    return stacked.T


def _merge_kernel(*refs, sorted_moe):
    if sorted_moe:
        (x_ref, or_ref, of_ref, mq_ref, mk_ref, mv_ref, gt_ref, wr_ref, wf_ref, wm_ref, wo_ref, g2_ref,
         wrt_ref, brt_ref, x1_ref, h2_ref, route_ref, *counts_ref) = refs
        o_m = jnp.concatenate([o for _, o in _mem_attend(mq_ref, mk_ref, mv_ref)], axis=1)
    else:
        (x_ref, or_ref, of_ref, om_ref, gt_ref, wr_ref, wf_ref, wm_ref, wo_ref, g2_ref,
         wrt_ref, brt_ref, x1_ref, h2_ref, route_ref) = refs
        o_m = om_ref[...]
    g = lambda b: gt_ref[:, b * D_MODEL:(b + 1) * D_MODEL].astype(F32)
    merged = (g(0) * _dot(or_ref[...], wr_ref[...]) + g(1) * _dot(of_ref[...], wf_ref[...])
              + g(2) * _dot(o_m, wm_ref[...]))
    x1 = x_ref[...] + _dot(merged.astype(BF16), wo_ref[...])
    x1_ref[...] = x1
    h2 = x1 * lax.rsqrt(jnp.mean(x1 * x1, axis=-1, keepdims=True) + EPS) * g2_ref[...]
    group, i1, i2, w1, w2 = _route(_router_logits(h2, wrt_ref, brt_ref))
    tm = h2.shape[0]
    if sorted_moe:
        _to_token_tiles(h2_ref, h2)
        e_lo = jnp.minimum(i1, i2) - group * E_PER_GROUP
        e_hi = jnp.maximum(i1, i2) - group * E_PER_GROUP
        cls = group * N_PAIRS + (e_lo * E_PER_GROUP - e_lo * (e_lo + 1.0) * 0.5 + (e_hi - e_lo - 1.0))
        cidx = lax.broadcasted_iota(jnp.int32, (LANES, tm), 0).astype(F32)
        onehot = jnp.where(cidx == cls, 1.0, 0.0)
        r = lax.broadcasted_iota(jnp.int32, (tm, tm), 0)
        c = lax.broadcasted_iota(jnp.int32, (tm, tm), 1)
        before = _dot(onehot.astype(BF16), jnp.where(r < c, 1.0, 0.0).astype(BF16))
        rank = jnp.sum(before * onehot, axis=0, keepdims=True)
        ridx = lax.broadcasted_iota(jnp.int32, route_ref.shape, 0)
        route_ref[...] = jnp.where(ridx == 0, group, jnp.where(ridx == 1, cls, jnp.where(ridx == 2, rank, 0.0)))
        counts_ref[0][0] = jnp.broadcast_to(jnp.sum(onehot, axis=1, keepdims=True), (LANES, LANES))
    else:
        h2_ref[...] = h2.astype(BF16)
        cols = _rows_to_columns([i1, i2, w1, w2])
        lane = lax.broadcasted_iota(jnp.int32, (tm, LANES), 1).astype(F32)
        comb = jnp.where(lane == cols[:, 0:1], cols[:, 2:3], 0.0) + jnp.where(lane == cols[:, 1:2], cols[:, 3:4], 0.0)
        for gi in range(N_GROUPS):
            route_ref[gi] = comb[:, gi * E_PER_GROUP:(gi + 1) * E_PER_GROUP]


def _merge(x2d, o_r, o_f, mem, gates, tm, sorted_moe, wr, wf, wm, wo, g2, wrt, brt):
    t = x2d.shape[0]
    row = lambda w: pl.BlockSpec((tm, w), lambda i: (i, 0))
    sds = jax.ShapeDtypeStruct
    if sorted_moe:
        h2_spec, h2_shape = pl.BlockSpec((tm * N_SLABS, LANES), lambda i: (i, 0)), sds((t * N_SLABS, LANES), F32)
        rt_specs = [pl.BlockSpec((SUBLANES, tm), lambda i: (i, 0)),
                    pl.BlockSpec((1, LANES, LANES), lambda i: (i, 0, 0))]
        rt_shapes = [sds((t // tm * SUBLANES, tm), F32), sds((t // tm, LANES, LANES), F32)]
        mq, mk, mv, tiles_per_seq = mem
        mem_args = [mq, mk, mv]
        mem_kv = pl.BlockSpec((N_MEM, MEM_W), lambda i: (i // tiles_per_seq, 0))
        mem_specs = [row(MEM_W), mem_kv, mem_kv]
    else:
        h2_spec, h2_shape = row(D_MODEL), sds((t, D_MODEL), BF16)
        rt_specs = [pl.BlockSpec((N_GROUPS, tm, E_PER_GROUP), lambda i: (0, i, 0))]
        rt_shapes = [sds((N_GROUPS, t, E_PER_GROUP), F32)]
        mem_args, mem_specs = [mem], [row(MEM_W)]
    return pl.pallas_call(
        functools.partial(_merge_kernel, sorted_moe=sorted_moe),
        grid=(t // tm,),
        in_specs=[row(D_MODEL), row(RV_W), row(FOX_W), *mem_specs, row(GATE_W),
                  _const_spec((RV_W, D_MODEL)), _const_spec((FOX_W, D_MODEL)), _const_spec((MEM_W, D_MODEL)),
                  _const_spec((D_MODEL, D_MODEL)), _const_spec((1, D_MODEL)),
                  _const_spec((D_MODEL, 2 * LANES)), _const_spec((1, LANES))],
        out_specs=[row(D_MODEL), h2_spec, *rt_specs],
        out_shape=[sds((t, D_MODEL), F32), h2_shape, *rt_shapes],
        compiler_params=_cparams(("parallel",)),
        name="merge",
    )(x2d, o_r, o_f, *mem_args, gates, wr, wf, wm, wo, g2, wrt, brt)


def _positions_kernel(route_ref, first_ref, pos_ref):
    tm = route_ref.shape[1]
    cidx = lax.broadcasted_iota(jnp.int32, (LANES, tm), 0).astype(F32)
    for i in range(first_ref.shape[0]):
        route = route_ref[i * SUBLANES:(i + 1) * SUBLANES, :]
        first = jnp.concatenate([first_ref[i]] * (tm // LANES), axis=1)
        start = jnp.sum(jnp.where(cidx == route[1:2, :], first, 0.0), axis=0, keepdims=True)
        pos_ref[i * SUBLANES:(i + 1) * SUBLANES, :] = jnp.broadcast_to(start + route[2:3, :],
                                                                       route.shape).astype(jnp.int32)


def _positions(route, first_row, tiles_per_step):
    rows, tm = route.shape
    n = rows // SUBLANES
    out = pl.pallas_call(
        _positions_kernel,
        grid=(n // tiles_per_step,),
        in_specs=[pl.BlockSpec((tiles_per_step * SUBLANES, tm), lambda i: (i, 0)),
                  pl.BlockSpec((tiles_per_step, LANES, LANES), lambda i: (i, 0, 0))],
        out_specs=pl.BlockSpec((tiles_per_step * SUBLANES, tm), lambda i: (i, 0)),
        out_shape=jax.ShapeDtypeStruct((rows, tm), jnp.int32),
        compiler_params=_cparams(("parallel",)),
        name="positions",
    )(route, first_row)
    return out.reshape(n, SUBLANES, tm)[:, 0, :].reshape(n * tm)


def _group_experts(h, cw, wg_ref, wu_ref, wd_ref, act):
    for e in range(E_PER_GROUP):
        a = _dot(h, wg_ref[0, e])
        u = _dot(h, wu_ref[0, e])
        act[:, e * D_EXPERT:(e + 1) * D_EXPERT] = ((a * jax.nn.sigmoid(a)) * u * cw(e)).astype(BF16)
    return _dot(act[...], wd_ref[0])


N_SLABS = D_MODEL // LANES
SUBLANES = 8


def _to_token_tiles(ref, x):
    for s in range(N_SLABS):
        ref[pl.ds(s, x.shape[0], stride=N_SLABS), :] = x[:, s * LANES:(s + 1) * LANES]


def _landing_shape(rows):
    return (rows // SUBLANES, N_SLABS, SUBLANES, LANES)


def _from_landing(buf):
    rows = buf.shape[0] * SUBLANES
    return jnp.concatenate([buf[:, s].reshape(rows, LANES) for s in range(N_SLABS)], axis=1)


def _row_gather(idx_ref, base, src_hbm, dst, sem, straight_line=False):
    def body(i, _):
        for u in range(SUBLANES):
            pltpu.make_async_copy(src_hbm.at[idx_ref[base + i * SUBLANES + u]], dst.at[i, :, u, :],
                                  sem).start(priority=u % 2)
        return 0
    if straight_line:
        for i in range(dst.shape[0]):
            body(i, 0)
    else:
        lax.fori_loop(0, dst.shape[0], body, 0)


def _row_gather_wait(dst, sem):
    pltpu.make_async_copy(dst, dst, sem).wait()


def _moe_sorted_kernel(tg_ref, nvt_ref, pos_ref, h3_hbm, zeros_hbm, wrt_ref, brt_ref,
                       wg_ref, wu_ref, wd_ref, y_ref, xbuf, sem, hb, yacc, src_ref, clear_sem, *, tm):
    k = pl.program_id(0)
    nvt = nvt_ref[0]
    slot = lax.rem(k, 2)

    @pl.when(k == 0)
    def _():
        clear = pltpu.make_async_copy(zeros_hbm, src_ref, clear_sem)
        clear.start()
        clear.wait()

        def invert(t, _):
            src_ref[pos_ref[t]] = t
            return 0

        lax.fori_loop(0, pos_ref.shape[0], invert, 0, unroll=32)

    @pl.when((k == 0) & (nvt > 0))
    def _():
        _row_gather(src_ref, 0, h3_hbm, xbuf.at[0], sem.at[0])

    @pl.when(k < nvt)
    def _():
        _row_gather_wait(xbuf.at[slot], sem.at[slot])
        _row_gather(src_ref, jnp.minimum(k + 1, nvt - 1) * tm, h3_hbm, xbuf.at[1 - slot], sem.at[1 - slot],
                    straight_line=True)

        x = _from_landing(xbuf.at[slot])
        group = tg_ref[k]
        _, i1, i2, w1, w2 = _route(_router_logits(x, wrt_ref, brt_ref), group.astype(F32))
        hb[...] = x.astype(BF16)
        yacc[...] = jnp.zeros(yacc.shape, F32)
        first = (group * E_PER_GROUP).astype(F32)
        used = [jnp.max(jnp.where(i1 == first + e, w1, 0.0) + jnp.where(i2 == first + e, w2, 0.0))
                for e in range(E_PER_GROUP)]
        cols = _rows_to_columns([i1, i2, w1, w2])
        for e in range(E_PER_GROUP):
            @pl.when(used[e] > 0.0)
            def _(e=e):
                cw = (jnp.where(cols[:, 0:1] == first + e, cols[:, 2:3], 0.0)
                      + jnp.where(cols[:, 1:2] == first + e, cols[:, 3:4], 0.0))
                h = hb[...]
                a = _dot(h, wg_ref[0, e])
                u = _dot(h, wu_ref[0, e])
                act = ((a * jax.nn.sigmoid(a)) * u * cw).astype(BF16)
                yacc[...] += _dot(act, wd_ref[0, e * D_EXPERT:(e + 1) * D_EXPERT, :])

        _to_token_tiles(y_ref, yacc[...])

        @pl.when(k + 1 >= nvt)
        def _():
            _row_gather_wait(xbuf.at[1 - slot], sem.at[1 - slot])

    @pl.when(k >= nvt)
    def _():
        y_ref[...] = jnp.zeros(y_ref.shape, F32)


def _moe_sorted(h3, tile_group, n_valid_tiles, pos, n_tiles, tm, wrt, brt, wg, wu, wd):
    wspec = lambda shape: pl.BlockSpec(shape, lambda k, tg, nv, ps: (tg[k],) + (0,) * (len(shape) - 1))
    cspec = lambda shape: pl.BlockSpec(shape, lambda k, tg, nv, ps: (0,) * len(shape),
                                       pipeline_mode=pl.Buffered(1))
    return pl.pallas_call(
        functools.partial(_moe_sorted_kernel, tm=tm),
        grid_spec=pltpu.PrefetchScalarGridSpec(
            num_scalar_prefetch=3,
            grid=(n_tiles,),
            in_specs=[pl.BlockSpec(memory_space=pl.ANY), pl.BlockSpec(memory_space=pl.ANY),
                      cspec((D_MODEL, 2 * LANES)), cspec((1, LANES)),
                      wspec((1, E_PER_GROUP, D_MODEL, D_EXPERT)), wspec((1, E_PER_GROUP, D_MODEL, D_EXPERT)),
                      wspec((1, E_PER_GROUP * D_EXPERT, D_MODEL))],
            out_specs=pl.BlockSpec((tm * N_SLABS, LANES), lambda k, tg, nv, ps: (k, 0)),
            scratch_shapes=[pltpu.VMEM((2,) + _landing_shape(tm), F32), pltpu.SemaphoreType.DMA((2,)),
                            pltpu.VMEM((tm, D_MODEL), BF16), pltpu.VMEM((tm, D_MODEL), F32),
                            pltpu.SMEM((n_tiles * tm,), jnp.int32), pltpu.SemaphoreType.DMA(())]),
        out_shape=jax.ShapeDtypeStruct((n_tiles * tm * N_SLABS, LANES), F32),
        compiler_params=_cparams(("arbitrary",)),
        name="moe_sorted",
    )(tile_group, n_valid_tiles, pos, h3, jnp.zeros((n_tiles * tm,), jnp.int32), wrt, brt, wg, wu, wd)


def _combine_kernel(pos_ref, y3_hbm, x1_ref, o_ref, ybuf, sem, *, tm):
    k = pl.program_id(0)
    slot = lax.rem(k, 2)

    @pl.when(k == 0)
    def _():
        _row_gather(pos_ref, 0, y3_hbm, ybuf.at[0], sem.at[0])

    _row_gather_wait(ybuf.at[slot], sem.at[slot])

    @pl.when(k + 1 < pl.num_programs(0))
    def _():
        _row_gather(pos_ref, (k + 1) * tm, y3_hbm, ybuf.at[1 - slot], sem.at[1 - slot])

    o_ref[...] = x1_ref[...] + _from_landing(ybuf.at[slot])


def _combine(y3, pos, x1, tm):
    t = x1.shape[0]
    return pl.pallas_call(
        functools.partial(_combine_kernel, tm=tm),
        grid_spec=pltpu.PrefetchScalarGridSpec(
            num_scalar_prefetch=1,
            grid=(t // tm,),
            in_specs=[pl.BlockSpec(memory_space=pl.ANY), pl.BlockSpec((tm, D_MODEL), lambda k, ps: (k, 0))],
            out_specs=pl.BlockSpec((tm, D_MODEL), lambda k, ps: (k, 0)),
            scratch_shapes=[pltpu.VMEM((2,) + _landing_shape(tm), F32), pltpu.SemaphoreType.DMA((2,))]),
        out_shape=jax.ShapeDtypeStruct((t, D_MODEL), F32),
        compiler_params=_cparams(("arbitrary",)),
        name="combine",
    )(pos, y3, x1)


def _moe_kernel(h_ref, comb_ref, x1_ref, wg_ref, wu_ref, wd_ref, o_ref, act):
    g = pl.program_id(1)
    comb = comb_ref[0]
    y = _group_experts(h_ref[...], lambda e: comb[:, e:e + 1], wg_ref, wu_ref, wd_ref, act)

    @pl.when(g == 0)
    def _():
        o_ref[...] = x1_ref[...] + y

    @pl.when(g != 0)
    def _():
        o_ref[...] += y


def _moe(h2, comb, x1, tm, wg, wu, wd):
    t = h2.shape[0]
    return pl.pallas_call(
        _moe_kernel,
        grid=(t // tm, N_GROUPS),
        in_specs=[pl.BlockSpec((tm, D_MODEL), lambda i, g: (i, 0)),
                  pl.BlockSpec((1, tm, E_PER_GROUP), lambda i, g: (g, i, 0)),
                  pl.BlockSpec((tm, D_MODEL), lambda i, g: (i, 0)),
                  pl.BlockSpec((1, E_PER_GROUP, D_MODEL, D_EXPERT), lambda i, g: (g, 0, 0, 0)),
                  pl.BlockSpec((1, E_PER_GROUP, D_MODEL, D_EXPERT), lambda i, g: (g, 0, 0, 0)),
                  pl.BlockSpec((1, E_PER_GROUP * D_EXPERT, D_MODEL), lambda i, g: (g, 0, 0))],
        out_specs=pl.BlockSpec((tm, D_MODEL), lambda i, g: (i, 0)),
        out_shape=jax.ShapeDtypeStruct((t, D_MODEL), F32),
        scratch_shapes=[pltpu.VMEM((tm, E_PER_GROUP * D_EXPERT), BF16)],
        compiler_params=_cparams(("parallel", "arbitrary")),
        name="moe",
    )(h2, comb, x1, wg, wu, wd)


def _rope_tables(pos):
    half = DK_R // 2
    inv = ROPE_BASE ** (-jnp.arange(half, dtype=F32) / half)
    ang = pos.astype(F32)[:, None] * inv[None, :]
    c = jnp.cos(ang)
    s = jnp.sin(ang)
    return jnp.concatenate([c, c, c, c], axis=-1), jnp.concatenate([-s, s, -s, s], axis=-1)


def kernel(x_prompt, x_sample, state_ret, cache_fox_k, cache_fox_v, cache_fox_logf, cache_mem_k, cache_mem_v,
           mem_prompt, g_norm1, w_in, b_forget, g_fox_q, g_fox_k, g_mem_q, g_mem_in, w_mem_kv, g_mem_k,
           g_ret_out, w_br_ret, w_br_fox, w_br_mem, w_out, g_norm2, w_route_group, b_route_group,
           w_route_expert, b_route_expert, w_exp_gate, w_exp_up, w_exp_down):
    nb, seq, _ = x_prompt.shape
    nbs, n_new, _ = x_sample.shape
    past = cache_fox_k.shape[2]
    l = 0

    wi = w_in[l]
    o_ff = 2 * RQK_W + 2 * RV_W + 3 * FOX_W
    w_in_r = jnp.concatenate(
        [wi[:, :o_ff], wi[:, o_ff + H_F:], wi[:, o_ff:o_ff + H_F],
         jnp.zeros((D_MODEL, LANES - H_F), F32)], axis=1).astype(BF16)
    bf_pad = jnp.concatenate([b_forget[l], jnp.zeros((LANES - H_F,), F32)])[None, :]
    g1 = g_norm1[l][None, :]
    gfq = jnp.tile(g_fox_q[l], H_F)[None, :]
    gfk = jnp.tile(g_fox_k[l], H_F)[None, :]
    gmq = g_mem_q[l][None, :]
    hid = jnp.arange(MXU_DIM) // D_F
    bd = jnp.where(hid[:, None] == hid[None, :], 1.0 / D_F, 0.0).astype(BF16)
    wr = w_br_ret[l].astype(BF16)
    wf = w_br_fox[l].astype(BF16)
    wm = w_br_mem[l].astype(BF16)
    wo = w_out[l].astype(BF16)
    g2 = g_norm2[l][None, :]
    n_e = N_GROUPS * E_PER_GROUP
    wrt = jnp.concatenate([w_route_expert[l], w_route_group[l],
                           jnp.zeros((D_MODEL, LANES - n_e - N_GROUPS), F32)], axis=1)
    wrt_hi = wrt.astype(BF16)
    wrt2 = jnp.concatenate([wrt_hi, (wrt - wrt_hi.astype(F32)).astype(BF16)], axis=1)
    brt = jnp.concatenate([b_route_expert[l], b_route_group[l],
                           jnp.zeros((LANES - n_e - N_GROUPS,), F32)])[None, :]
    wg = w_exp_gate[l].astype(BF16)
    wu = w_exp_up[l].astype(BF16)
    wd = w_exp_down[l].astype(BF16).reshape(N_GROUPS, E_PER_GROUP * D_EXPERT, D_MODEL)
    gro = g_ret_out[l][None, :]
    prep_w = (g1, w_in_r, bf_pad, gfq, gfk, gmq, bd)
    merge_w = (wr, wf, wm, wo, g2, wrt2, brt)

    tm = 512
    xp = x_prompt.reshape(nb * seq, D_MODEL)
    (rq, rk, rv, rg, fqt, fkt, fvt, lft, mq, gates, fkb, lf) = _prep(
        xp, _rope_tables(jnp.arange(seq)), tm, seq // tm, True, *prep_w)
    o_r, s_fin = _retention(rq, rk, rv, rg, nb, seq, 256, gro, None)
    fcum = _cumsum_cols(lf.reshape(nb, seq, H_F), 256)
    fcum = jnp.swapaxes(fcum.reshape(nb, seq, H_F // 2, 2), 1, 2).reshape(nb * (H_F // 2) * seq, 2)
    o_f = _fox_prompt(fqt, fkb, fvt, fcum, nb, seq, 256)
    mk, mv = _memkv(mem_prompt.reshape(nb * N_MEM, D_MODEL), nb, g_mem_in[l][None, :],
                    w_mem_kv[l].astype(BF16), g_mem_k[l][None, :])
    x1, h3, route, counts = _merge(xp, o_r, o_f, (mq, mk, mv, seq // tm), gates, tm, True, *merge_w)
    t_p = nb * seq
    n_tiles = t_p // tm + N_GROUPS
    n_cls = N_GROUPS * N_PAIRS
    cnt = counts[:, :n_cls, 0].astype(jnp.int32)
    before_tile = jnp.cumsum(cnt, axis=0) - cnt
    cls_tot = jnp.sum(cnt, axis=0)
    grp_tot = jnp.sum(cls_tot.reshape(N_GROUPS, N_PAIRS), axis=1)
    tiles_g = (grp_tot + tm - 1) // tm
    tile_end = jnp.cumsum(tiles_g)
    row_start = (tile_end - tiles_g) * tm
    in_grp = cls_tot.reshape(N_GROUPS, N_PAIRS)
    cls_start = (row_start[:, None] + jnp.cumsum(in_grp, axis=1) - in_grp).reshape(n_cls)
    first_row = jnp.pad((cls_start[None, :] + before_tile).astype(F32), ((0, 0), (0, LANES - n_cls)))
    pos = _positions(route, jnp.broadcast_to(first_row[:, :, None], (t_p // tm, LANES, LANES)),
                     min(8, t_p // tm))
    tile_ids = jnp.arange(n_tiles, dtype=jnp.int32)
    tile_group = jnp.minimum(jnp.sum((tile_ids[:, None] >= tile_end[None, :]).astype(jnp.int32), axis=1),
                             N_GROUPS - 1)
    y3 = _moe_sorted(h3.reshape(t_p, N_SLABS, LANES), tile_group, tile_end[-1:], pos, n_tiles, tm,
                     wrt2, brt, wg, wu, wd)
    y_prompt = _combine(y3.reshape(n_tiles * tm, N_SLABS, LANES), pos, x1, 2 * tm).reshape(nb, seq, D_MODEL)

    ts = nbs * n_new
    xs = x_sample.reshape(ts, D_MODEL)
    pos_s = jnp.tile(past + jnp.arange(n_new), nbs)
    (rq_s, rk_s, rv_s, rg_s, fq_s, fk_s, fv_s, lf_s, mq_s, gates_s) = _prep(
        xs, _rope_tables(pos_s), ts, 1, False, *prep_w)
    o_r_s, s_new = _retention(rq_s, rk_s, rv_s, rg_s, nbs, n_new, n_new, gro, state_ret[l])
    pad = (-(past + n_new)) % 256
    lf_rows = jnp.concatenate([jnp.swapaxes(cache_fox_logf[l], 1, 2),
                               jnp.swapaxes(lf_s.reshape(nbs, n_new, H_F), 1, 2),
                               jnp.zeros((nbs, H_F, pad), F32)], axis=2).reshape(nbs * H_F, past + n_new + pad)
    f_all = _cumsum(lf_rows, 256)
    feat_major = lambda c: jnp.transpose(c, (0, 2, 3, 1)).reshape(nbs * FOX_W, past)
    o_f_s = _fox_sample(fq_s, feat_major(cache_fox_k[l]), feat_major(cache_fox_v[l]), fk_s, fv_s,
                        f_all[:, :past].reshape(nbs, H_F, past),
                        f_all[:, past:past + n_new].reshape(nbs, H_F, n_new), nbs, n_new, past)
    o_m_s = _memattn(mq_s, cache_mem_k[l].reshape(nbs * N_MEM, MEM_W),
                     cache_mem_v[l].reshape(nbs * N_MEM, MEM_W), nbs, n_new, n_new)
    x1_s, h2_s, comb_s = _merge(xs, o_r_s, o_f_s, o_m_s, gates_s, ts, False, *merge_w)
    y_sample = _moe(h2_s, comb_s, x1_s, ts, wg, wu, wd).reshape(nbs, n_new, D_MODEL)

    token_major = lambda a: jnp.transpose(a.reshape(nb, H_F, D_F, seq), (0, 3, 1, 2))

    return (y_prompt, y_sample,
            s_fin[None], token_major(fkt)[None], token_major(fvt)[None],
            jnp.swapaxes(lft.reshape(nb, H_F, seq), 1, 2)[None],
            mk.reshape(1, nb, N_MEM, H_M, D_M), mv.reshape(1, nb, N_MEM, H_M, D_M),
            s_new[None], fk_s.reshape(1, nbs, n_new, H_F, D_F), fv_s.reshape(1, nbs, n_new, H_F, D_F),
            lf_s.reshape(1, nbs, n_new, H_F))
```

```python
import functools

import jax
import jax.numpy as jnp
from jax import lax
from jax.experimental import pallas as pl
from jax.experimental.pallas import tpu as pltpu

F32 = jnp.float32
BF16 = jnp.bfloat16

D_MODEL = 1024
H_R, DK_R, DV_R = 4, 64, 128
H_F, D_F = 8, 64
H_M, D_M = 4, 128
N_MEM = 256
N_GROUPS, E_PER_GROUP, D_EXPERT = 4, 8, 256
ROPE_BASE = 10000.0
EPS = 1e-6
LOG2E = 1.4426950408889634

RQK_W = H_R * DK_R
RV_W = H_R * DV_R
FOX_W = H_F * D_F
MEM_W = H_M * D_M
GATE_W = 3 * D_MODEL
LANES = 128
MXU_DIM = 256

C_RQ = 0
C_RK = C_RQ + RQK_W
C_RV = C_RK + RQK_W
C_RG = C_RV + RV_W
C_FQ = C_RG + RV_W
C_FK = C_FQ + FOX_W
C_FV = C_FK + FOX_W
C_MQ = C_FV + FOX_W
C_GT = C_MQ + MEM_W
C_FF = C_GT + GATE_W
N_IN_PAD = C_FF + LANES

VMEM_LIMIT = 56 * 1024 * 1024


def _cparams(sem):
    return pltpu.CompilerParams(dimension_semantics=sem, vmem_limit_bytes=VMEM_LIMIT)


def _const_spec(shape):
    nd = len(shape)
    return pl.BlockSpec(shape, lambda *_: (0,) * nd, pipeline_mode=pl.Buffered(1))


def _dot(a, b):
    return jnp.dot(a, b, preferred_element_type=F32)


def _dot_nt(a, b):
    return lax.dot_general(a, b, (((1,), (1,)), ((), ())), preferred_element_type=F32)


def _dot_tn(a, b):
    return lax.dot_general(a, b, (((0,), (0,)), ((), ())), preferred_element_type=F32)


def _lane_rmsnorm(z):
    return z * lax.rsqrt(jnp.mean(z * z, axis=-1, keepdims=True) + EPS)


def _prep_kernel(x_ref, g1_ref, w_ref, cos_ref, sin_ref, bf_ref, gfq_ref, gfk_ref, gmq_ref, bd_ref,
                 rq_ref, rk_ref, rv_ref, rg_ref, fq_ref, fk_ref, fv_ref, lf_ref, mq_ref, gt_ref, *extra,
                 seq_minor):
    x = x_ref[...]
    h = (x * lax.rsqrt(jnp.mean(x * x, axis=-1, keepdims=True) + EPS) * g1_ref[...]).astype(BF16)

    def proj(c0, width):
        return _dot(h, w_ref[:, c0:c0 + width])

    cos = cos_ref[...]
    sin = sin_ref[...]
    lane = lax.broadcasted_iota(jnp.int32, cos.shape, 1)
    first_half = (lane % DK_R) < (DK_R // 2)

    def rope(z):
        swapped = jnp.where(first_half, pltpu.roll(z, LANES - DK_R // 2, 1), pltpu.roll(z, DK_R // 2, 1))
        return z * cos + swapped * sin

    zq = proj(C_RQ, RQK_W)
    zk = proj(C_RK, RQK_W)
    for p in range(RQK_W // LANES):
        sl = slice(p * LANES, (p + 1) * LANES)
        rq_ref[:, sl] = rope(zq[:, sl])
        rk_ref[:, sl] = rope(zk[:, sl]) * (DK_R ** -0.5)
    rv_ref[...] = proj(C_RV, RV_W)
    rg_ref[...] = proj(C_RG, RV_W)

    def head64_norm(z, g_ref):
        zz = (z * z).astype(BF16)
        w = bd_ref.shape[0]
        ms = jnp.concatenate([_dot(zz[:, c:c + w], bd_ref[...]) for c in range(0, FOX_W, w)], axis=1)
        return z * lax.rsqrt(ms + EPS) * g_ref[...]

    fq = head64_norm(proj(C_FQ, FOX_W), gfq_ref) * (D_F ** -0.5 * LOG2E)
    fk = head64_norm(proj(C_FK, FOX_W), gfk_ref)
    fv = proj(C_FV, FOX_W)
    if seq_minor:
        fq_ref[...] = fq.T.astype(BF16)
        fk_ref[...] = fk.T
        fv_ref[...] = fv.T
        extra[0][...] = fk.astype(BF16)
    else:
        fq_ref[...] = fq.astype(BF16)
        fk_ref[...] = fk
        fv_ref[...] = fv

    zm = proj(C_MQ, MEM_W)
    for hh in range(H_M):
        sl = slice(hh * D_M, (hh + 1) * D_M)
        mq_ref[:, sl] = (_lane_rmsnorm(zm[:, sl]) * gmq_ref[...]).astype(BF16)

    for b in range(3):
        gt_ref[:, b * D_MODEL:(b + 1) * D_MODEL] = jax.nn.sigmoid(proj(C_GT + b * D_MODEL, D_MODEL)).astype(BF16)

    v = proj(C_FF, LANES) + bf_ref[...]
    lf = jnp.minimum(v, 0.0) - jnp.log1p(jnp.exp(-jnp.abs(v)))
    if seq_minor:
        lf_ref[...] = lf.T[:H_F, :]
    else:
        lf_ref[...] = lf[:, :H_F]


def _prep(x2d, tables, tm, n_pos_tiles, seq_minor, g1, w_in_r, bf_pad, gfq, gfk, gmq, bd):
    t = x2d.shape[0]
    cos_t, sin_t = tables
    row = lambda w: pl.BlockSpec((tm, w), lambda i: (i, 0))
    pos = pl.BlockSpec((tm, LANES), lambda i: (i % n_pos_tiles, 0))
    sds = jax.ShapeDtypeStruct
    if seq_minor:
        nb, seq = t // (n_pos_tiles * tm), n_pos_tiles * tm
        fm = lambda rows_: pl.BlockSpec((rows_, tm), lambda i: (i // n_pos_tiles, i % n_pos_tiles))
        fox_shapes = [sds((nb * FOX_W, seq), BF16), sds((nb * FOX_W, seq), F32), sds((nb * FOX_W, seq), F32),
                      sds((nb * H_F, seq), F32)]
        fox_specs = [fm(FOX_W), fm(FOX_W), fm(FOX_W), fm(H_F)]
        extra_shapes, extra_specs = [sds((t, FOX_W), BF16)], [row(FOX_W)]
    else:
        fox_shapes = [sds((t, FOX_W), BF16), sds((t, FOX_W), F32), sds((t, FOX_W), F32), sds((t, H_F), F32)]
        fox_specs = [row(FOX_W), row(FOX_W), row(FOX_W), row(H_F)]
        extra_shapes, extra_specs = [], []
    out_shapes = [sds((t, RQK_W), F32), sds((t, RQK_W), F32), sds((t, RV_W), F32), sds((t, RV_W), F32),
                  *fox_shapes, sds((t, MEM_W), BF16), sds((t, GATE_W), BF16), *extra_shapes]
    out_specs = [row(RQK_W), row(RQK_W), row(RV_W), row(RV_W), *fox_specs, row(MEM_W), row(GATE_W),
                 *extra_specs]
    return pl.pallas_call(
        functools.partial(_prep_kernel, seq_minor=seq_minor),
        grid=(t // tm,),
        in_specs=[row(D_MODEL), _const_spec((1, D_MODEL)), _const_spec((D_MODEL, N_IN_PAD)), pos, pos,
                  _const_spec((1, LANES)), _const_spec((1, FOX_W)), _const_spec((1, FOX_W)),
                  _const_spec((1, D_M)), _const_spec((MXU_DIM, MXU_DIM))],
        out_specs=out_specs,
        out_shape=out_shapes,
        compiler_params=_cparams(("parallel",)),
        name="prep",
    )(x2d, g1, w_in_r, cos_t, sin_t, bf_pad, gfq, gfk, gmq, bd)


def _ret_kernel(*refs, has_init, ch):
    if has_init:
        (rq_ref, rk_ref, rv_ref, rg_ref, dmat_ref, qdec_ref, kdec_ref, gpow_ref, gro_ref, s0_ref,
         o_ref, sfin_ref) = refs
    else:
        (rq_ref, rk_ref, rv_ref, rg_ref, dmat_ref, qdec_ref, kdec_ref, gpow_ref, gro_ref,
         o_ref, sfin_ref) = refs

    state = []
    for h in range(H_R):
        if has_init:
            pad = jnp.zeros((DK_R, DV_R), F32)
            state.append(jnp.concatenate([s0_ref[0, h], pad] if h % 2 == 0 else [pad, s0_ref[0, h]], axis=0))
        else:
            state.append(jnp.zeros((LANES, DV_R), F32))

    lane = lax.broadcasted_iota(jnp.int32, (ch, LANES), 1)
    for c in range(rq_ref.shape[0] // ch):
        rows = slice(c * ch, (c + 1) * ch)
        for p in range(H_R // 2):
            sl = slice(p * LANES, (p + 1) * LANES)
            q2 = rq_ref[rows, sl]
            k2 = rk_ref[rows, sl]
            kd2 = k2 * kdec_ref[p]
            for hh in range(2):
                h = 2 * p + hh
                mine = (lane >= hh * DK_R) & (lane < (hh + 1) * DK_R)
                qm = jnp.where(mine, q2, 0.0).astype(BF16)
                kdm = jnp.where(mine, kd2, 0.0).astype(BF16)
                v = rv_ref[rows, h * DV_R:(h + 1) * DV_R].astype(BF16)
                sc = _dot_nt(qm, k2.astype(BF16)) * dmat_ref[h]
                o = _dot(sc.astype(BF16), v) + _dot(qm, state[h].astype(BF16)) * qdec_ref[h]
                state[h] = gpow_ref[h] * state[h] + _dot_tn(kdm, v)
                normed = _lane_rmsnorm(o) * gro_ref[...]
                rg = rg_ref[rows, h * DV_R:(h + 1) * DV_R]
                o_ref[rows, h * DV_R:(h + 1) * DV_R] = (normed * (rg * jax.nn.sigmoid(rg))).astype(BF16)

    for h in range(H_R):
        r0 = DK_R * (h % 2)
        sfin_ref[0, h] = state[h][r0:r0 + DK_R, :]


def _retention(rq, rk, rv, rg, n_batch, seq, ch, g_ret_out, state0):
    lg = jnp.log1p(-jnp.exp2(-5.0 - jnp.arange(H_R, dtype=F32)))
    idx = jnp.arange(ch, dtype=F32)
    diff = idx[:, None] - idx[None, :]
    causal = diff >= 0
    dmat = jnp.where(causal[None], jnp.exp(jnp.where(causal, diff, 0.0)[None] * lg[:, None, None]), 0.0)
    q_dec = jnp.exp((idx + 1.0)[None, :] * lg[:, None])
    k_dec = jnp.exp((ch - 1.0 - idx)[None, :] * lg[:, None])
    qdec = jnp.broadcast_to(q_dec[:, :, None], (H_R, ch, DV_R))
    kdec = jnp.broadcast_to(k_dec[:, :, None], (H_R, ch, DK_R))
    kdec = kdec.reshape(H_R // 2, 2, ch, DK_R).transpose(0, 2, 1, 3).reshape(H_R // 2, ch, LANES)
    gpow = jnp.broadcast_to(jnp.exp(ch * lg)[:, None, None], (H_R, 1, DV_R))

    has_init = state0 is not None
    blk = lambda w: pl.BlockSpec((seq, w), lambda b: (b, 0))
    in_specs = [blk(RQK_W), blk(RQK_W), blk(RV_W), blk(RV_W),
                _const_spec((H_R, ch, ch)), _const_spec((H_R, ch, DV_R)),
                _const_spec((H_R // 2, ch, LANES)), _const_spec((H_R, 1, DV_R)), _const_spec((1, DV_R))]
    args = [rq, rk, rv, rg, dmat, qdec, kdec, gpow, g_ret_out]
    if has_init:
        in_specs.append(pl.BlockSpec((1, H_R, DK_R, DV_R), lambda b: (b, 0, 0, 0)))
        args.append(state0)
    return pl.pallas_call(
        functools.partial(_ret_kernel, has_init=has_init, ch=ch),
        grid=(n_batch,),
        in_specs=in_specs,
        out_specs=[blk(RV_W), pl.BlockSpec((1, H_R, DK_R, DV_R), lambda b: (b, 0, 0, 0))],
        out_shape=[jax.ShapeDtypeStruct((n_batch * seq, RV_W), BF16),
                   jax.ShapeDtypeStruct((n_batch, H_R, DK_R, DV_R), F32)],
        compiler_params=_cparams(("parallel",)),
        name="retention",
    )(*args)


def _split3(x):
    hi = x.astype(BF16)
    r1 = x - hi.astype(F32)
    mid = r1.astype(BF16)
    lo = (r1 - mid.astype(F32)).astype(BF16)
    return hi, mid, lo


def _cumsum_kernel(x_ref, o_ref, *, blk):
    rows, n = x_ref.shape
    r = lax.broadcasted_iota(jnp.int32, (blk, blk), 0)
    c = lax.broadcasted_iota(jnp.int32, (blk, blk), 1)
    tri = jnp.where(r <= c, 1.0, 0.0).astype(BF16)
    carry = jnp.zeros((rows, 1), F32)
    for i in range(n // blk):
        hi, mid, lo = _split3(x_ref[:, i * blk:(i + 1) * blk])
        cum = (_dot(hi, tri) + _dot(mid, tri)) + _dot(lo, tri) + carry
        o_ref[:, i * blk:(i + 1) * blk] = cum
        carry = cum[:, blk - 1:blk]


def _cumsum(x, blk):
    r, t = x.shape
    spec = pl.BlockSpec((H_F, t), lambda b: (b, 0))
    return pl.pallas_call(
        functools.partial(_cumsum_kernel, blk=blk),
        grid=(r // H_F,),
        in_specs=[spec], out_specs=spec,
        out_shape=jax.ShapeDtypeStruct(x.shape, F32),
        compiler_params=_cparams(("parallel",)),
        name="cumsum",
    )(x)


def _head_rows_mask(shape, hh):
    sub = lax.broadcasted_iota(jnp.int32, shape, 0)
    return (sub >= hh * D_F) & (sub < (hh + 1) * D_F)


N_AUG = 3
V_ROWS = D_F + 16


def _fox_kernel(qt_ref, k_ref, vt_ref, fcum_ref, o_ref, qa, ka, vb, *, tq):
    seq = k_ref.shape[0]
    tk = tq
    qt = qt_ref[...].astype(F32)
    k = k_ref[...].astype(F32)
    bias_cols = _rows_to_columns([fcum_ref[0, hh:hh + 1, :] * (-LOG2E) for hh in range(2)])
    sub = lax.broadcasted_iota(jnp.int32, qt.shape, 0)
    lane = lax.broadcasted_iota(jnp.int32, k.shape, 1)
    for hh in range(2):
        own, oth = hh * D_F, (1 - hh) * D_F
        qa[hh] = jnp.where((sub >= own) & (sub < own + D_F), qt,
                           jnp.where((sub >= oth) & (sub < oth + N_AUG), 1.0, 0.0)).astype(BF16)
        bias = bias_cols[:, hh:hh + 1]
        kk = jnp.where((lane >= own) & (lane < own + D_F), k, 0.0)
        for a, piece in enumerate(_split3(bias)):
            kk = jnp.where(lane == oth + a, piece.astype(F32), kk)
        ka[hh] = kk.astype(BF16)
    for hh in range(2):
        vb[hh, :D_F, :] = vt_ref[hh * D_F:(hh + 1) * D_F, :].astype(BF16)
        vb[hh, D_F:, :] = jnp.ones((V_ROWS - D_F, seq), BF16)
    causal = (lax.broadcasted_iota(jnp.int32, (tk, tq), 1) >= lax.broadcasted_iota(jnp.int32, (tk, tq), 0))

    pairs = [(i, j) for i in range(seq // tq) for j in range(i + 1)]

    def stage_a(i, j):
        return tuple(_dot(ka[hh, j * tk:(j + 1) * tk, :], qa[hh, :, i * tq:(i + 1) * tq]) for hh in range(2))

    def stage_b(i, j, ts, maxes):
        new_maxes, probs = [], []
        for hh in range(2):
            t = jnp.where(causal, ts[hh], -jnp.inf) if j == i else ts[hh]
            t_max = jnp.max(t, axis=0, keepdims=True)
            if j == 0:
                m_new, alpha = t_max, None
            else:
                m_new = jnp.maximum(maxes[hh], t_max)
                alpha = jnp.exp2(maxes[hh] - m_new)
            new_maxes.append(m_new)
            probs.append((jnp.exp2(t - m_new).astype(BF16), alpha))
        return new_maxes, probs

    def stage_c(i, j, probs, accs):
        out = []
        for hh in range(2):
            p, alpha = probs[hh]
            pv = _dot(vb[hh, :, j * tk:(j + 1) * tk], p)
            out.append(pv if j == 0 else alpha * accs[hh] + pv)
        return out

    scores, probs, maxes, accs = {}, {}, None, None
    for s in range(len(pairs) + 2):
        if s < len(pairs):
            scores[s] = stage_a(*pairs[s])
        if 0 <= s - 1 < len(pairs):
            maxes, probs[s - 1] = stage_b(*pairs[s - 1], scores.pop(s - 1), maxes)
        if 0 <= s - 2 < len(pairs):
            i, j = pairs[s - 2]
            accs = stage_c(i, j, probs.pop(s - 2), accs)
            if j == i:
                o_t = jnp.concatenate([a[:D_F] / a[D_F:D_F + 1] for a in accs], axis=0)
                o_ref[i * tq:(i + 1) * tq, :] = o_t.T.astype(BF16)


def _fox_prompt(fqt, fkb, fvt, fcum, n_batch, seq, tq):
    npair = H_F // 2
    fm_spec = pl.BlockSpec((LANES, seq), lambda b, p: (b * npair + p, 0))
    tok_spec = pl.BlockSpec((seq, LANES), lambda b, p: (b, p))
    return pl.pallas_call(
        functools.partial(_fox_kernel, tq=tq),
        grid=(n_batch, npair),
        in_specs=[fm_spec, tok_spec, fm_spec, pl.BlockSpec((1, 2, seq), lambda b, p: (b * npair + p, 0, 0))],
        out_specs=tok_spec,
        out_shape=jax.ShapeDtypeStruct((n_batch * seq, FOX_W), BF16),
        scratch_shapes=[pltpu.VMEM((2, LANES, seq), BF16), pltpu.VMEM((2, seq, LANES), BF16),
                        pltpu.VMEM((2, V_ROWS, seq), BF16)],
        compiler_params=_cparams(("parallel", "parallel")),
        name="fox_prompt",
    )(fqt, fkb, fvt, fcum)


def _fox_sample_kernel(q_ref, kct_ref, vct_ref, kn_ref, vn_ref, frc_ref, frn_ref, o_ref):
    n = q_ref.shape[0]
    lane = lax.broadcasted_iota(jnp.int32, (n, LANES), 1)
    rows = lax.broadcasted_iota(jnp.int32, (n, n), 0)
    cols = lax.broadcasted_iota(jnp.int32, (n, n), 1)
    for pair in range(H_F // 2):
        sl = slice(pair * LANES, (pair + 1) * LANES)
        q2 = q_ref[:, sl]
        kct = kct_ref[sl, :]
        vct = vct_ref[sl, :].astype(BF16)
        kn = kn_ref[:, sl].astype(BF16)
        vn = vn_ref[:, sl].astype(BF16)
        outs = []
        for hh in range(2):
            h = 2 * pair + hh
            mine = (lane >= hh * D_F) & (lane < (hh + 1) * D_F)
            kc = jnp.where(_head_rows_mask(kct.shape, hh), kct, 0.0).astype(BF16)
            t_c = _dot(q2, kc) - frc_ref[0, h:h + 1, :] * LOG2E
            t_n = _dot_nt(jnp.where(mine, q2, jnp.zeros_like(q2)), kn) - frn_ref[0, h:h + 1, :] * LOG2E
            t_n = jnp.where(rows >= cols, t_n, -jnp.inf)
            m = jnp.maximum(jnp.max(t_c, axis=-1, keepdims=True), jnp.max(t_n, axis=-1, keepdims=True))
            p_c = jnp.exp2(t_c - m)
            p_n = jnp.exp2(t_n - m)
            l = jnp.sum(p_c, axis=-1, keepdims=True) + jnp.sum(p_n, axis=-1, keepdims=True)
            acc = _dot_nt(p_c.astype(BF16), vct) + _dot(p_n.astype(BF16), vn)
            outs.append(acc / l)
        o_ref[:, sl] = jnp.where(lane < D_F, outs[0], outs[1]).astype(BF16)


def _fox_sample(fq, kct, vct, kn, vn, frow_c, frow_n, n_batch, n_new, past):
    tok = pl.BlockSpec((n_new, FOX_W), lambda b: (b, 0))
    cache = pl.BlockSpec((FOX_W, past), lambda b: (b, 0))
    return pl.pallas_call(
        _fox_sample_kernel,
        grid=(n_batch,),
        in_specs=[tok, cache, cache, tok, tok,
                  pl.BlockSpec((1, H_F, past), lambda b: (b, 0, 0)),
                  pl.BlockSpec((1, H_F, n_new), lambda b: (b, 0, 0))],
        out_specs=tok,
        out_shape=jax.ShapeDtypeStruct((n_batch * n_new, FOX_W), BF16),
        compiler_params=_cparams(("parallel",)),
        name="fox_sample",
    )(fq, kct, vct, kn, vn, frow_c, frow_n)


def _memkv_kernel(m_ref, gin_ref, w_ref, gk_ref, k_ref, v_ref):
    x = m_ref[...]
    h = (x * lax.rsqrt(jnp.mean(x * x, axis=-1, keepdims=True) + EPS) * gin_ref[...]).astype(BF16)
    zk = _dot(h, w_ref[:, :MEM_W])
    for hh in range(H_M):
        sl = slice(hh * D_M, (hh + 1) * D_M)
        k_ref[:, sl] = _lane_rmsnorm(zk[:, sl]) * gk_ref[...]
    v_ref[...] = _dot(h, w_ref[:, MEM_W:])


def _memkv(mem2d, n_batch, g_mem_in, w_mem_kv, g_mem_k):
    blk = lambda w: pl.BlockSpec((N_MEM, w), lambda b: (b, 0))
    return pl.pallas_call(
        _memkv_kernel,
        grid=(n_batch,),
        in_specs=[blk(D_MODEL), _const_spec((1, D_MODEL)), _const_spec((D_MODEL, 2 * MEM_W)),
                  _const_spec((1, D_M))],
        out_specs=[blk(MEM_W), blk(MEM_W)],
        out_shape=[jax.ShapeDtypeStruct((n_batch * N_MEM, MEM_W), F32)] * 2,
        compiler_params=_cparams(("parallel",)),
        name="memkv",
    )(mem2d, g_mem_in, w_mem_kv, g_mem_k)


def _mem_attend(q_ref, k_ref, v_ref):
    for hh in range(H_M):
        sl = slice(hh * D_M, (hh + 1) * D_M)
        s = _dot_nt(q_ref[:, sl], k_ref[:, sl].astype(BF16)) * (D_M ** -0.5)
        m = jnp.max(s, axis=-1, keepdims=True)
        p = jnp.exp(s - m)
        l = jnp.sum(p, axis=-1, keepdims=True)
        yield sl, (_dot(p.astype(BF16), v_ref[:, sl].astype(BF16)) / l).astype(BF16)


def _memattn_kernel(q_ref, k_ref, v_ref, o_ref):
    for sl, o in _mem_attend(q_ref, k_ref, v_ref):
        o_ref[:, sl] = o


def _memattn(mq, mk, mv, n_batch, seq, tq):
    nq = seq // tq
    return pl.pallas_call(
        _memattn_kernel,
        grid=(n_batch, nq),
        in_specs=[pl.BlockSpec((tq, MEM_W), lambda b, i: (b * nq + i, 0)),
                  pl.BlockSpec((N_MEM, MEM_W), lambda b, i: (b, 0)),
                  pl.BlockSpec((N_MEM, MEM_W), lambda b, i: (b, 0))],
        out_specs=pl.BlockSpec((tq, MEM_W), lambda b, i: (b * nq + i, 0)),
        out_shape=jax.ShapeDtypeStruct((n_batch * seq, MEM_W), BF16),
        compiler_params=_cparams(("parallel", "arbitrary")),
        name="memattn",
    )(mq, mk, mv)


def _split2(x):
    hi = x.astype(BF16)
    return hi, (x - hi.astype(F32)).astype(BF16)


N_EXPERTS = N_GROUPS * E_PER_GROUP


N_PAIRS = E_PER_GROUP * (E_PER_GROUP - 1) // 2


def _router_logits(h2, wrt_ref, brt_ref):
    h_hi, h_lo = _split2(h2)
    z = _dot(h_hi, wrt_ref[...])
    return (z[:, :LANES] + z[:, LANES:]) + _dot(h_lo, wrt_ref[:, :LANES]) + brt_ref[...]


N_LOGIT_ROWS = 40


def _route(logits, group=None):
    lt = logits.T[:N_LOGIT_ROWS, :]
    idx = lax.broadcasted_iota(jnp.int32, lt.shape, 0).astype(F32)
    neg = -jnp.inf
    first_idx = lambda mask: jnp.min(jnp.where(mask, idx, float(LANES)), axis=0, keepdims=True)

    is_g = (idx >= N_EXPERTS) & (idx < N_EXPERTS + N_GROUPS)
    lg = jnp.where(is_g, lt, neg)
    mg = jnp.max(lg, axis=0, keepdims=True)
    if group is None:
        group = first_idx(lg == mg) - N_EXPERTS
        p_sel = 1.0 / jnp.sum(jnp.exp(lg - mg), axis=0, keepdims=True)
    else:
        lsel = jnp.max(jnp.where(idx == group + N_EXPERTS, lt, neg), axis=0, keepdims=True)
        p_sel = jnp.exp(lsel - mg) / jnp.sum(jnp.exp(lg - mg), axis=0, keepdims=True)

    in_grp = (idx >= group * E_PER_GROUP) & (idx < (group + 1) * E_PER_GROUP)
    le = jnp.where(in_grp, lt, neg)
    v1 = jnp.max(le, axis=0, keepdims=True)
    i1 = first_idx(le == v1)
    le2 = jnp.where(idx == i1, neg, le)
    v2 = jnp.max(le2, axis=0, keepdims=True)
    i2 = first_idx(le2 == v2)
    e2 = jnp.exp(v2 - v1)
    return group, i1, i2, p_sel / (1.0 + e2), p_sel * e2 / (1.0 + e2)


def _rows_to_columns(rows):
    n = rows[0].shape[1]
    idx = lax.broadcasted_iota(jnp.int32, (LANES, n), 0)
    stacked = jnp.zeros((LANES, n), F32)
    for r, row in enumerate(rows):
        stacked = jnp.where(idx == r, row, stacked)
    return stacked.T


def _merge_kernel(*refs, sorted_moe):
    if sorted_moe:
        (x_ref, or_ref, of_ref, mq_ref, mk_ref, mv_ref, gt_ref, wr_ref, wf_ref, wm_ref, wo_ref, g2_ref,
         wrt_ref, brt_ref, x1_ref, h2_ref, route_ref, *counts_ref) = refs
        o_m = jnp.concatenate([o for _, o in _mem_attend(mq_ref, mk_ref, mv_ref)], axis=1)
    else:
        (x_ref, or_ref, of_ref, om_ref, gt_ref, wr_ref, wf_ref, wm_ref, wo_ref, g2_ref,
         wrt_ref, brt_ref, x1_ref, h2_ref, route_ref) = refs
        o_m = om_ref[...]
    g = lambda b: gt_ref[:, b * D_MODEL:(b + 1) * D_MODEL].astype(F32)
    merged = (g(0) * _dot(or_ref[...], wr_ref[...]) + g(1) * _dot(of_ref[...], wf_ref[...])
              + g(2) * _dot(o_m, wm_ref[...]))
    x1 = x_ref[...] + _dot(merged.astype(BF16), wo_ref[...])
    x1_ref[...] = x1
    h2 = x1 * lax.rsqrt(jnp.mean(x1 * x1, axis=-1, keepdims=True) + EPS) * g2_ref[...]
    group, i1, i2, w1, w2 = _route(_router_logits(h2, wrt_ref, brt_ref))
    tm = h2.shape[0]
    if sorted_moe:
        _to_token_tiles(h2_ref, h2)
        e_lo = jnp.minimum(i1, i2) - group * E_PER_GROUP
        e_hi = jnp.maximum(i1, i2) - group * E_PER_GROUP
        cls = group * N_PAIRS + (e_lo * E_PER_GROUP - e_lo * (e_lo + 1.0) * 0.5 + (e_hi - e_lo - 1.0))
        cidx = lax.broadcasted_iota(jnp.int32, (LANES, tm), 0).astype(F32)
        onehot = jnp.where(cidx == cls, 1.0, 0.0)
        r = lax.broadcasted_iota(jnp.int32, (tm, tm), 0)
        c = lax.broadcasted_iota(jnp.int32, (tm, tm), 1)
        before = _dot(onehot.astype(BF16), jnp.where(r < c, 1.0, 0.0).astype(BF16))
        rank = jnp.sum(before * onehot, axis=0, keepdims=True)
        ridx = lax.broadcasted_iota(jnp.int32, route_ref.shape, 0)
        route_ref[...] = jnp.where(ridx == 0, group, jnp.where(ridx == 1, cls, jnp.where(ridx == 2, rank, 0.0)))
        counts_ref[0][0] = jnp.broadcast_to(jnp.sum(onehot, axis=1, keepdims=True), (LANES, LANES))
    else:
        h2_ref[...] = h2.astype(BF16)
        cols = _rows_to_columns([i1, i2, w1, w2])
        lane = lax.broadcasted_iota(jnp.int32, (tm, LANES), 1).astype(F32)
        comb = jnp.where(lane == cols[:, 0:1], cols[:, 2:3], 0.0) + jnp.where(lane == cols[:, 1:2], cols[:, 3:4], 0.0)
        for gi in range(N_GROUPS):
            route_ref[gi] = comb[:, gi * E_PER_GROUP:(gi + 1) * E_PER_GROUP]


def _merge(x2d, o_r, o_f, mem, gates, tm, sorted_moe, wr, wf, wm, wo, g2, wrt, brt):
    t = x2d.shape[0]
    row = lambda w: pl.BlockSpec((tm, w), lambda i: (i, 0))
    sds = jax.ShapeDtypeStruct
    if sorted_moe:
        h2_spec, h2_shape = pl.BlockSpec((tm * N_SLABS, LANES), lambda i: (i, 0)), sds((t * N_SLABS, LANES), F32)
        rt_specs = [pl.BlockSpec((SUBLANES, tm), lambda i: (i, 0)),
                    pl.BlockSpec((1, LANES, LANES), lambda i: (i, 0, 0))]
        rt_shapes = [sds((t // tm * SUBLANES, tm), F32), sds((t // tm, LANES, LANES), F32)]
        mq, mk, mv, tiles_per_seq = mem
        mem_args = [mq, mk, mv]
        mem_kv = pl.BlockSpec((N_MEM, MEM_W), lambda i: (i // tiles_per_seq, 0))
        mem_specs = [row(MEM_W), mem_kv, mem_kv]
    else:
        h2_spec, h2_shape = row(D_MODEL), sds((t, D_MODEL), BF16)
        rt_specs = [pl.BlockSpec((N_GROUPS, tm, E_PER_GROUP), lambda i: (0, i, 0))]
        rt_shapes = [sds((N_GROUPS, t, E_PER_GROUP), F32)]
        mem_args, mem_specs = [mem], [row(MEM_W)]
    return pl.pallas_call(
        functools.partial(_merge_kernel, sorted_moe=sorted_moe),
        grid=(t // tm,),
        in_specs=[row(D_MODEL), row(RV_W), row(FOX_W), *mem_specs, row(GATE_W),
                  _const_spec((RV_W, D_MODEL)), _const_spec((FOX_W, D_MODEL)), _const_spec((MEM_W, D_MODEL)),
                  _const_spec((D_MODEL, D_MODEL)), _const_spec((1, D_MODEL)),
                  _const_spec((D_MODEL, 2 * LANES)), _const_spec((1, LANES))],
        out_specs=[row(D_MODEL), h2_spec, *rt_specs],
        out_shape=[sds((t, D_MODEL), F32), h2_shape, *rt_shapes],
        compiler_params=_cparams(("parallel",)),
        name="merge",
    )(x2d, o_r, o_f, *mem_args, gates, wr, wf, wm, wo, g2, wrt, brt)


def _positions_kernel(route_ref, first_ref, pos_ref):
    tm = route_ref.shape[1]
    cidx = lax.broadcasted_iota(jnp.int32, (LANES, tm), 0).astype(F32)
    for i in range(first_ref.shape[0]):
        route = route_ref[i * SUBLANES:(i + 1) * SUBLANES, :]
        first = jnp.concatenate([first_ref[i]] * (tm // LANES), axis=1)
        start = jnp.sum(jnp.where(cidx == route[1:2, :], first, 0.0), axis=0, keepdims=True)
        pos_ref[i * SUBLANES:(i + 1) * SUBLANES, :] = jnp.broadcast_to(start + route[2:3, :],
                                                                       route.shape).astype(jnp.int32)


def _positions(route, first_row, tiles_per_step):
    rows, tm = route.shape
    n = rows // SUBLANES
    out = pl.pallas_call(
        _positions_kernel,
        grid=(n // tiles_per_step,),
        in_specs=[pl.BlockSpec((tiles_per_step * SUBLANES, tm), lambda i: (i, 0)),
                  pl.BlockSpec((tiles_per_step, LANES, LANES), lambda i: (i, 0, 0))],
        out_specs=pl.BlockSpec((tiles_per_step * SUBLANES, tm), lambda i: (i, 0)),
        out_shape=jax.ShapeDtypeStruct((rows, tm), jnp.int32),
        compiler_params=_cparams(("parallel",)),
        name="positions",
    )(route, first_row)
    return out.reshape(n, SUBLANES, tm)[:, 0, :].reshape(n * tm)


def _group_experts(h, cw, wg_ref, wu_ref, wd_ref, act):
    for e in range(E_PER_GROUP):
        a = _dot(h, wg_ref[0, e])
        u = _dot(h, wu_ref[0, e])
        act[:, e * D_EXPERT:(e + 1) * D_EXPERT] = ((a * jax.nn.sigmoid(a)) * u * cw(e)).astype(BF16)
    return _dot(act[...], wd_ref[0])


N_SLABS = D_MODEL // LANES
SUBLANES = 8


def _to_token_tiles(ref, x):
    for s in range(N_SLABS):
        ref[pl.ds(s, x.shape[0], stride=N_SLABS), :] = x[:, s * LANES:(s + 1) * LANES]


def _landing_shape(rows):
    return (rows // SUBLANES, N_SLABS, SUBLANES, LANES)


def _from_landing(buf):
    rows = buf.shape[0] * SUBLANES
    return jnp.concatenate([buf[:, s].reshape(rows, LANES) for s in range(N_SLABS)], axis=1)


def _row_gather(idx_ref, base, src_hbm, dst, sem, straight_line=False):
    def body(i, _):
        for u in range(SUBLANES):
            pltpu.make_async_copy(src_hbm.at[idx_ref[base + i * SUBLANES + u]], dst.at[i, :, u, :],
                                  sem).start(priority=u % 2)
        return 0
    if straight_line:
        for i in range(dst.shape[0]):
            body(i, 0)
    else:
        lax.fori_loop(0, dst.shape[0], body, 0)


def _row_gather_wait(dst, sem):
    pltpu.make_async_copy(dst, dst, sem).wait()


def _moe_sorted_kernel(tg_ref, nvt_ref, pos_ref, h3_hbm, zeros_hbm, wrt_ref, brt_ref,
                       wg_ref, wu_ref, wd_ref, y_ref, xbuf, sem, hb, yacc, src_ref, clear_sem, *, tm):
    k = pl.program_id(0)
    nvt = nvt_ref[0]
    slot = lax.rem(k, 2)

    @pl.when(k == 0)
    def _():
        clear = pltpu.make_async_copy(zeros_hbm, src_ref, clear_sem)
        clear.start()
        clear.wait()

        def invert(t, _):
            src_ref[pos_ref[t]] = t
            return 0

        lax.fori_loop(0, pos_ref.shape[0], invert, 0, unroll=32)

    @pl.when((k == 0) & (nvt > 0))
    def _():
        _row_gather(src_ref, 0, h3_hbm, xbuf.at[0], sem.at[0])

    @pl.when(k < nvt)
    def _():
        _row_gather_wait(xbuf.at[slot], sem.at[slot])
        _row_gather(src_ref, jnp.minimum(k + 1, nvt - 1) * tm, h3_hbm, xbuf.at[1 - slot], sem.at[1 - slot],
                    straight_line=True)

        x = _from_landing(xbuf.at[slot])
        group = tg_ref[k]
        _, i1, i2, w1, w2 = _route(_router_logits(x, wrt_ref, brt_ref), group.astype(F32))
        hb[...] = x.astype(BF16)
        yacc[...] = jnp.zeros(yacc.shape, F32)
        first = (group * E_PER_GROUP).astype(F32)
        used = [jnp.max(jnp.where(i1 == first + e, w1, 0.0) + jnp.where(i2 == first + e, w2, 0.0))
                for e in range(E_PER_GROUP)]
        cols = _rows_to_columns([i1, i2, w1, w2])
        for e in range(E_PER_GROUP):
            @pl.when(used[e] > 0.0)
            def _(e=e):
                cw = (jnp.where(cols[:, 0:1] == first + e, cols[:, 2:3], 0.0)
                      + jnp.where(cols[:, 1:2] == first + e, cols[:, 3:4], 0.0))
                h = hb[...]
                a = _dot(h, wg_ref[0, e])
                u = _dot(h, wu_ref[0, e])
                act = ((a * jax.nn.sigmoid(a)) * u * cw).astype(BF16)
                yacc[...] += _dot(act, wd_ref[0, e * D_EXPERT:(e + 1) * D_EXPERT, :])

        _to_token_tiles(y_ref, yacc[...])

        @pl.when(k + 1 >= nvt)
        def _():
            _row_gather_wait(xbuf.at[1 - slot], sem.at[1 - slot])

    @pl.when(k >= nvt)
    def _():
        y_ref[...] = jnp.zeros(y_ref.shape, F32)


def _moe_sorted(h3, tile_group, n_valid_tiles, pos, n_tiles, tm, wrt, brt, wg, wu, wd):
    wspec = lambda shape: pl.BlockSpec(shape, lambda k, tg, nv, ps: (tg[k],) + (0,) * (len(shape) - 1))
    cspec = lambda shape: pl.BlockSpec(shape, lambda k, tg, nv, ps: (0,) * len(shape),
                                       pipeline_mode=pl.Buffered(1))
    return pl.pallas_call(
        functools.partial(_moe_sorted_kernel, tm=tm),
        grid_spec=pltpu.PrefetchScalarGridSpec(
            num_scalar_prefetch=3,
            grid=(n_tiles,),
            in_specs=[pl.BlockSpec(memory_space=pl.ANY), pl.BlockSpec(memory_space=pl.ANY),
                      cspec((D_MODEL, 2 * LANES)), cspec((1, LANES)),
                      wspec((1, E_PER_GROUP, D_MODEL, D_EXPERT)), wspec((1, E_PER_GROUP, D_MODEL, D_EXPERT)),
                      wspec((1, E_PER_GROUP * D_EXPERT, D_MODEL))],
            out_specs=pl.BlockSpec((tm * N_SLABS, LANES), lambda k, tg, nv, ps: (k, 0)),
            scratch_shapes=[pltpu.VMEM((2,) + _landing_shape(tm), F32), pltpu.SemaphoreType.DMA((2,)),
                            pltpu.VMEM((tm, D_MODEL), BF16), pltpu.VMEM((tm, D_MODEL), F32),
                            pltpu.SMEM((n_tiles * tm,), jnp.int32), pltpu.SemaphoreType.DMA(())]),
        out_shape=jax.ShapeDtypeStruct((n_tiles * tm * N_SLABS, LANES), F32),
        compiler_params=_cparams(("arbitrary",)),
        name="moe_sorted",
    )(tile_group, n_valid_tiles, pos, h3, jnp.zeros((n_tiles * tm,), jnp.int32), wrt, brt, wg, wu, wd)


def _combine_kernel(pos_ref, y3_hbm, x1_ref, o_ref, ybuf, sem, *, tm):
    k = pl.program_id(0)
    slot = lax.rem(k, 2)

    @pl.when(k == 0)
    def _():
        _row_gather(pos_ref, 0, y3_hbm, ybuf.at[0], sem.at[0])

    _row_gather_wait(ybuf.at[slot], sem.at[slot])

    @pl.when(k + 1 < pl.num_programs(0))
    def _():
        _row_gather(pos_ref, (k + 1) * tm, y3_hbm, ybuf.at[1 - slot], sem.at[1 - slot])

    o_ref[...] = x1_ref[...] + _from_landing(ybuf.at[slot])


def _combine(y3, pos, x1, tm):
    t = x1.shape[0]
    return pl.pallas_call(
        functools.partial(_combine_kernel, tm=tm),
        grid_spec=pltpu.PrefetchScalarGridSpec(
            num_scalar_prefetch=1,
            grid=(t // tm,),
            in_specs=[pl.BlockSpec(memory_space=pl.ANY), pl.BlockSpec((tm, D_MODEL), lambda k, ps: (k, 0))],
            out_specs=pl.BlockSpec((tm, D_MODEL), lambda k, ps: (k, 0)),
            scratch_shapes=[pltpu.VMEM((2,) + _landing_shape(tm), F32), pltpu.SemaphoreType.DMA((2,))]),
        out_shape=jax.ShapeDtypeStruct((t, D_MODEL), F32),
        compiler_params=_cparams(("arbitrary",)),
        name="combine",
    )(pos, y3, x1)


def _moe_kernel(h_ref, comb_ref, x1_ref, wg_ref, wu_ref, wd_ref, o_ref, act):
    g = pl.program_id(1)
    comb = comb_ref[0]
    y = _group_experts(h_ref[...], lambda e: comb[:, e:e + 1], wg_ref, wu_ref, wd_ref, act)

    @pl.when(g == 0)
    def _():
        o_ref[...] = x1_ref[...] + y

    @pl.when(g != 0)
    def _():
        o_ref[...] += y


def _moe(h2, comb, x1, tm, wg, wu, wd):
    t = h2.shape[0]
    return pl.pallas_call(
        _moe_kernel,
        grid=(t // tm, N_GROUPS),
        in_specs=[pl.BlockSpec((tm, D_MODEL), lambda i, g: (i, 0)),
                  pl.BlockSpec((1, tm, E_PER_GROUP), lambda i, g: (g, i, 0)),
                  pl.BlockSpec((tm, D_MODEL), lambda i, g: (i, 0)),
                  pl.BlockSpec((1, E_PER_GROUP, D_MODEL, D_EXPERT), lambda i, g: (g, 0, 0, 0)),
                  pl.BlockSpec((1, E_PER_GROUP, D_MODEL, D_EXPERT), lambda i, g: (g, 0, 0, 0)),
                  pl.BlockSpec((1, E_PER_GROUP * D_EXPERT, D_MODEL), lambda i, g: (g, 0, 0))],
        out_specs=pl.BlockSpec((tm, D_MODEL), lambda i, g: (i, 0)),
        out_shape=jax.ShapeDtypeStruct((t, D_MODEL), F32),
        scratch_shapes=[pltpu.VMEM((tm, E_PER_GROUP * D_EXPERT), BF16)],
        compiler_params=_cparams(("parallel", "arbitrary")),
        name="moe",
    )(h2, comb, x1, wg, wu, wd)


def _rope_tables(pos):
    half = DK_R // 2
    inv = ROPE_BASE ** (-jnp.arange(half, dtype=F32) / half)
    ang = pos.astype(F32)[:, None] * inv[None, :]
    c = jnp.cos(ang)
    s = jnp.sin(ang)
    return jnp.concatenate([c, c, c, c], axis=-1), jnp.concatenate([-s, s, -s, s], axis=-1)


def kernel(x_prompt, x_sample, state_ret, cache_fox_k, cache_fox_v, cache_fox_logf, cache_mem_k, cache_mem_v,
           mem_prompt, g_norm1, w_in, b_forget, g_fox_q, g_fox_k, g_mem_q, g_mem_in, w_mem_kv, g_mem_k,
           g_ret_out, w_br_ret, w_br_fox, w_br_mem, w_out, g_norm2, w_route_group, b_route_group,
           w_route_expert, b_route_expert, w_exp_gate, w_exp_up, w_exp_down):
    nb, seq, _ = x_prompt.shape
    nbs, n_new, _ = x_sample.shape
    past = cache_fox_k.shape[2]
    l = 0

    wi = w_in[l]
    o_ff = 2 * RQK_W + 2 * RV_W + 3 * FOX_W
    w_in_r = jnp.concatenate(
        [wi[:, :o_ff], wi[:, o_ff + H_F:], wi[:, o_ff:o_ff + H_F],
         jnp.zeros((D_MODEL, LANES - H_F), F32)], axis=1).astype(BF16)
    bf_pad = jnp.concatenate([b_forget[l], jnp.zeros((LANES - H_F,), F32)])[None, :]
    g1 = g_norm1[l][None, :]
    gfq = jnp.tile(g_fox_q[l], H_F)[None, :]
    gfk = jnp.tile(g_fox_k[l], H_F)[None, :]
    gmq = g_mem_q[l][None, :]
    hid = jnp.arange(MXU_DIM) // D_F
    bd = jnp.where(hid[:, None] == hid[None, :], 1.0 / D_F, 0.0).astype(BF16)
    wr = w_br_ret[l].astype(BF16)
    wf = w_br_fox[l].astype(BF16)
    wm = w_br_mem[l].astype(BF16)
    wo = w_out[l].astype(BF16)
    g2 = g_norm2[l][None, :]
    n_e = N_GROUPS * E_PER_GROUP
    wrt = jnp.concatenate([w_route_expert[l], w_route_group[l],
                           jnp.zeros((D_MODEL, LANES - n_e - N_GROUPS), F32)], axis=1)
    wrt_hi = wrt.astype(BF16)
    wrt2 = jnp.concatenate([wrt_hi, (wrt - wrt_hi.astype(F32)).astype(BF16)], axis=1)
    brt = jnp.concatenate([b_route_expert[l], b_route_group[l],
                           jnp.zeros((LANES - n_e - N_GROUPS,), F32)])[None, :]
    wg = w_exp_gate[l].astype(BF16)
    wu = w_exp_up[l].astype(BF16)
    wd = w_exp_down[l].astype(BF16).reshape(N_GROUPS, E_PER_GROUP * D_EXPERT, D_MODEL)
    gro = g_ret_out[l][None, :]
    prep_w = (g1, w_in_r, bf_pad, gfq, gfk, gmq, bd)
    merge_w = (wr, wf, wm, wo, g2, wrt2, brt)

    tm = 512
    xp = x_prompt.reshape(nb * seq, D_MODEL)
    (rq, rk, rv, rg, fqt, fkt, fvt, lft, mq, gates, fkb) = _prep(
        xp, _rope_tables(jnp.arange(seq)), tm, seq // tm, True, *prep_w)
    o_r, s_fin = _retention(rq, rk, rv, rg, nb, seq, 256, gro, None)
    fcum = _cumsum(lft, 256).reshape(nb * (H_F // 2), 2, seq)
    o_f = _fox_prompt(fqt, fkb, fvt, fcum, nb, seq, 256)
    mk, mv = _memkv(mem_prompt.reshape(nb * N_MEM, D_MODEL), nb, g_mem_in[l][None, :],
                    w_mem_kv[l].astype(BF16), g_mem_k[l][None, :])
    x1, h3, route, counts = _merge(xp, o_r, o_f, (mq, mk, mv, seq // tm), gates, tm, True, *merge_w)
    t_p = nb * seq
    n_tiles = t_p // tm + N_GROUPS
    n_cls = N_GROUPS * N_PAIRS
    cnt = counts[:, :n_cls, 0].astype(jnp.int32)
    before_tile = jnp.cumsum(cnt, axis=0) - cnt
    cls_tot = jnp.sum(cnt, axis=0)
    grp_tot = jnp.sum(cls_tot.reshape(N_GROUPS, N_PAIRS), axis=1)
    tiles_g = (grp_tot + tm - 1) // tm
    tile_end = jnp.cumsum(tiles_g)
    row_start = (tile_end - tiles_g) * tm
    in_grp = cls_tot.reshape(N_GROUPS, N_PAIRS)
    cls_start = (row_start[:, None] + jnp.cumsum(in_grp, axis=1) - in_grp).reshape(n_cls)
    first_row = jnp.pad((cls_start[None, :] + before_tile).astype(F32), ((0, 0), (0, LANES - n_cls)))
    pos = _positions(route, jnp.broadcast_to(first_row[:, :, None], (t_p // tm, LANES, LANES)),
                     min(8, t_p // tm))
    tile_ids = jnp.arange(n_tiles, dtype=jnp.int32)
    tile_group = jnp.minimum(jnp.sum((tile_ids[:, None] >= tile_end[None, :]).astype(jnp.int32), axis=1),
                             N_GROUPS - 1)
    y3 = _moe_sorted(h3.reshape(t_p, N_SLABS, LANES), tile_group, tile_end[-1:], pos, n_tiles, tm,
                     wrt2, brt, wg, wu, wd)
    y_prompt = _combine(y3.reshape(n_tiles * tm, N_SLABS, LANES), pos, x1, 2 * tm).reshape(nb, seq, D_MODEL)

    ts = nbs * n_new
    xs = x_sample.reshape(ts, D_MODEL)
    pos_s = jnp.tile(past + jnp.arange(n_new), nbs)
    (rq_s, rk_s, rv_s, rg_s, fq_s, fk_s, fv_s, lf_s, mq_s, gates_s) = _prep(
        xs, _rope_tables(pos_s), ts, 1, False, *prep_w)
    o_r_s, s_new = _retention(rq_s, rk_s, rv_s, rg_s, nbs, n_new, n_new, gro, state_ret[l])
    pad = (-(past + n_new)) % 256
    lf_rows = jnp.concatenate([jnp.swapaxes(cache_fox_logf[l], 1, 2),
                               jnp.swapaxes(lf_s.reshape(nbs, n_new, H_F), 1, 2),
                               jnp.zeros((nbs, H_F, pad), F32)], axis=2).reshape(nbs * H_F, past + n_new + pad)
    f_all = _cumsum(lf_rows, 256)
    feat_major = lambda c: jnp.transpose(c, (0, 2, 3, 1)).reshape(nbs * FOX_W, past)
    o_f_s = _fox_sample(fq_s, feat_major(cache_fox_k[l]), feat_major(cache_fox_v[l]), fk_s, fv_s,
                        f_all[:, :past].reshape(nbs, H_F, past),
                        f_all[:, past:past + n_new].reshape(nbs, H_F, n_new), nbs, n_new, past)
    o_m_s = _memattn(mq_s, cache_mem_k[l].reshape(nbs * N_MEM, MEM_W),
                     cache_mem_v[l].reshape(nbs * N_MEM, MEM_W), nbs, n_new, n_new)
    x1_s, h2_s, comb_s = _merge(xs, o_r_s, o_f_s, o_m_s, gates_s, ts, False, *merge_w)
    y_sample = _moe(h2_s, comb_s, x1_s, ts, wg, wu, wd).reshape(nbs, n_new, D_MODEL)

    token_major = lambda a: jnp.transpose(a.reshape(nb, H_F, D_F, seq), (0, 3, 1, 2))

    return (y_prompt, y_sample,
            s_fin[None], token_major(fkt)[None], token_major(fvt)[None],
            jnp.swapaxes(lft.reshape(nb, H_F, seq), 1, 2)[None],
            mk.reshape(1, nb, N_MEM, H_M, D_M), mv.reshape(1, nb, N_MEM, H_M, D_M),
            s_new[None], fk_s.reshape(1, nbs, n_new, H_F, D_F), fv_s.reshape(1, nbs, n_new, H_F, D_F),
            lf_s.reshape(1, nbs, n_new, H_F))
```

```python
import functools

import jax
import jax.numpy as jnp
from jax import lax
from jax.experimental import pallas as pl
from jax.experimental.pallas import tpu as pltpu

F32 = jnp.float32
BF16 = jnp.bfloat16

D_MODEL = 1024
H_R, DK_R, DV_R = 4, 64, 128
H_F, D_F = 8, 64
H_M, D_M = 4, 128
N_MEM = 256
N_GROUPS, E_PER_GROUP, D_EXPERT = 4, 8, 256
ROPE_BASE = 10000.0
EPS = 1e-6
LOG2E = 1.4426950408889634

RQK_W = H_R * DK_R
RV_W = H_R * DV_R
FOX_W = H_F * D_F
MEM_W = H_M * D_M
GATE_W = 3 * D_MODEL
LANES = 128
MXU_DIM = 256

C_RQ = 0
C_RK = C_RQ + RQK_W
C_RV = C_RK + RQK_W
C_RG = C_RV + RV_W
C_FQ = C_RG + RV_W
C_FK = C_FQ + FOX_W
C_FV = C_FK + FOX_W
C_MQ = C_FV + FOX_W
C_GT = C_MQ + MEM_W
C_FF = C_GT + GATE_W

VMEM_LIMIT = 56 * 1024 * 1024


def _cparams(sem):
    return pltpu.CompilerParams(dimension_semantics=sem, vmem_limit_bytes=VMEM_LIMIT)


def _const_spec(shape):
    nd = len(shape)
    return pl.BlockSpec(shape, lambda *_: (0,) * nd, pipeline_mode=pl.Buffered(1))


def _dot(a, b):
    return jnp.dot(a, b, preferred_element_type=F32)


def _dot_nt(a, b):
    return lax.dot_general(a, b, (((1,), (1,)), ((), ())), preferred_element_type=F32)


def _dot_tn(a, b):
    return lax.dot_general(a, b, (((0,), (0,)), ((), ())), preferred_element_type=F32)


def _lane_rmsnorm(z):
    return z * lax.rsqrt(jnp.mean(z * z, axis=-1, keepdims=True) + EPS)


def _prep_kernel(x_ref, g1_ref, wa_ref, wb_ref, wf_ref, cos_ref, sin_ref, bf_ref, gfq_ref, gfk_ref, gmq_ref, bd_ref,
                 rq_ref, rk_ref, rv_ref, rg_ref, fq_ref, fk_ref, fv_ref, lf_ref, mq_ref, gt_ref, *extra,
                 seq_minor):
    x = x_ref[...]
    h = (x * lax.rsqrt(jnp.mean(x * x, axis=-1, keepdims=True) + EPS) * g1_ref[...]).astype(BF16)

    def proj(c0, width):
        if c0 < C_MQ:
            return _dot(h, wa_ref[:, c0:c0 + width])
        if c0 < C_FF:
            return _dot(h, wb_ref[:, c0 - C_MQ:c0 - C_MQ + width])
        return _dot(h, wf_ref[...])

    cos = cos_ref[...]
    sin = sin_ref[...]
    lane = lax.broadcasted_iota(jnp.int32, cos.shape, 1)
    first_half = (lane % DK_R) < (DK_R // 2)

    def rope(z):
        swapped = jnp.where(first_half, pltpu.roll(z, LANES - DK_R // 2, 1), pltpu.roll(z, DK_R // 2, 1))
        return z * cos + swapped * sin

    zq = proj(C_RQ, RQK_W)
    zk = proj(C_RK, RQK_W)
    for p in range(RQK_W // LANES):
        sl = slice(p * LANES, (p + 1) * LANES)
        rq_ref[:, sl] = rope(zq[:, sl])
        rk_ref[:, sl] = rope(zk[:, sl]) * (DK_R ** -0.5)
    rv_ref[...] = proj(C_RV, RV_W)
    rg_ref[...] = proj(C_RG, RV_W)

    def head64_norm(z, g_ref):
        zz = (z * z).astype(BF16)
        w = bd_ref.shape[0]
        ms = jnp.concatenate([_dot(zz[:, c:c + w], bd_ref[...]) for c in range(0, FOX_W, w)], axis=1)
        return z * lax.rsqrt(ms + EPS) * g_ref[...]

    fq = head64_norm(proj(C_FQ, FOX_W), gfq_ref) * (D_F ** -0.5 * LOG2E)
    fk = head64_norm(proj(C_FK, FOX_W), gfk_ref)
    fv = proj(C_FV, FOX_W)
    if seq_minor:
        fq_ref[...] = fq.T.astype(BF16)
        fk_ref[...] = fk.T
        fv_ref[...] = fv.T
        extra[0][...] = fk.astype(BF16)
    else:
        fq_ref[...] = fq.astype(BF16)
        fk_ref[...] = fk
        fv_ref[...] = fv

    zm = proj(C_MQ, MEM_W)
    for hh in range(H_M):
        sl = slice(hh * D_M, (hh + 1) * D_M)
        mq_ref[:, sl] = (_lane_rmsnorm(zm[:, sl]) * gmq_ref[...]).astype(BF16)

    for b in range(3):
        gt_ref[:, b * D_MODEL:(b + 1) * D_MODEL] = jax.nn.sigmoid(proj(C_GT + b * D_MODEL, D_MODEL)).astype(BF16)

    v = proj(C_FF, LANES) + bf_ref[...]
    lf = jnp.minimum(v, 0.0) - jnp.log1p(jnp.exp(-jnp.abs(v)))
    if seq_minor:
        lf_ref[...] = lf.T[:H_F, :]
    else:
        lf_ref[...] = lf[:, :H_F]


def _prep(x2d, tables, tm, n_pos_tiles, seq_minor, g1, w_in_r, bf_pad, gfq, gfk, gmq, bd):
    t = x2d.shape[0]
    cos_t, sin_t = tables
    row = lambda w: pl.BlockSpec((tm, w), lambda i: (i, 0))
    pos = pl.BlockSpec((tm, LANES), lambda i: (i % n_pos_tiles, 0))
    sds = jax.ShapeDtypeStruct
    if seq_minor:
        nb, seq = t // (n_pos_tiles * tm), n_pos_tiles * tm
        fm = lambda rows_: pl.BlockSpec((rows_, tm), lambda i: (i // n_pos_tiles, i % n_pos_tiles))
        fox_shapes = [sds((nb * FOX_W, seq), BF16), sds((nb * FOX_W, seq), F32), sds((nb * FOX_W, seq), F32),
                      sds((nb * H_F, seq), F32)]
        fox_specs = [fm(FOX_W), fm(FOX_W), fm(FOX_W), fm(H_F)]
        extra_shapes, extra_specs = [sds((t, FOX_W), BF16)], [row(FOX_W)]
    else:
        fox_shapes = [sds((t, FOX_W), BF16), sds((t, FOX_W), F32), sds((t, FOX_W), F32), sds((t, H_F), F32)]
        fox_specs = [row(FOX_W), row(FOX_W), row(FOX_W), row(H_F)]
        extra_shapes, extra_specs = [], []
    out_shapes = [sds((t, RQK_W), F32), sds((t, RQK_W), F32), sds((t, RV_W), F32), sds((t, RV_W), F32),
                  *fox_shapes, sds((t, MEM_W), BF16), sds((t, GATE_W), BF16), *extra_shapes]
    out_specs = [row(RQK_W), row(RQK_W), row(RV_W), row(RV_W), *fox_specs, row(MEM_W), row(GATE_W),
                 *extra_specs]
    return pl.pallas_call(
        functools.partial(_prep_kernel, seq_minor=seq_minor),
        grid=(t // tm,),
        in_specs=[row(D_MODEL), _const_spec((1, D_MODEL)), _const_spec((D_MODEL, C_MQ)),
                  _const_spec((D_MODEL, C_FF - C_MQ)), _const_spec((D_MODEL, LANES)), pos, pos,
                  _const_spec((1, LANES)), _const_spec((1, FOX_W)), _const_spec((1, FOX_W)),
                  _const_spec((1, D_M)), _const_spec((MXU_DIM, MXU_DIM))],
        out_specs=out_specs,
        out_shape=out_shapes,
        compiler_params=_cparams(("parallel",)),
        name="prep",
    )(x2d, g1, *w_in_r, cos_t, sin_t, bf_pad, gfq, gfk, gmq, bd)


def _ret_kernel(*refs, has_init, ch):
    if has_init:
        (rq_ref, rk_ref, rv_ref, rg_ref, dmat_ref, qdec_ref, kdec_ref, gpow_ref, gro_ref, s0_ref,
         o_ref, sfin_ref) = refs
    else:
        (rq_ref, rk_ref, rv_ref, rg_ref, dmat_ref, qdec_ref, kdec_ref, gpow_ref, gro_ref,
         o_ref, sfin_ref) = refs

    state = []
    for h in range(H_R):
        if has_init:
            pad = jnp.zeros((DK_R, DV_R), F32)
            state.append(jnp.concatenate([s0_ref[0, h], pad] if h % 2 == 0 else [pad, s0_ref[0, h]], axis=0))
        else:
            state.append(jnp.zeros((LANES, DV_R), F32))

    lane = lax.broadcasted_iota(jnp.int32, (ch, LANES), 1)
    for c in range(rq_ref.shape[0] // ch):
        rows = slice(c * ch, (c + 1) * ch)
        for p in range(H_R // 2):
            sl = slice(p * LANES, (p + 1) * LANES)
            q2 = rq_ref[rows, sl]
            k2 = rk_ref[rows, sl]
            kd2 = k2 * kdec_ref[p]
            for hh in range(2):
                h = 2 * p + hh
                mine = (lane >= hh * DK_R) & (lane < (hh + 1) * DK_R)
                qm = jnp.where(mine, q2, 0.0).astype(BF16)
                kdm = jnp.where(mine, kd2, 0.0).astype(BF16)
                v = rv_ref[rows, h * DV_R:(h + 1) * DV_R].astype(BF16)
                sc = _dot_nt(qm, k2.astype(BF16)) * dmat_ref[h]
                o = _dot(sc.astype(BF16), v) + _dot(qm, state[h].astype(BF16)) * qdec_ref[h]
                state[h] = gpow_ref[h] * state[h] + _dot_tn(kdm, v)
                normed = _lane_rmsnorm(o) * gro_ref[...]
                rg = rg_ref[rows, h * DV_R:(h + 1) * DV_R]
                o_ref[rows, h * DV_R:(h + 1) * DV_R] = (normed * (rg * jax.nn.sigmoid(rg))).astype(BF16)

    for h in range(H_R):
        r0 = DK_R * (h % 2)
        sfin_ref[0, h] = state[h][r0:r0 + DK_R, :]


def _retention(rq, rk, rv, rg, n_batch, seq, ch, g_ret_out, state0):
    lg = jnp.log1p(-jnp.exp2(-5.0 - jnp.arange(H_R, dtype=F32)))
    idx = jnp.arange(ch, dtype=F32)
    diff = idx[:, None] - idx[None, :]
    causal = diff >= 0
    dmat = jnp.where(causal[None], jnp.exp(jnp.where(causal, diff, 0.0)[None] * lg[:, None, None]), 0.0)
    q_dec = jnp.exp((idx + 1.0)[None, :] * lg[:, None])
    k_dec = jnp.exp((ch - 1.0 - idx)[None, :] * lg[:, None])
    qdec = jnp.broadcast_to(q_dec[:, :, None], (H_R, ch, DV_R))
    kdec = jnp.broadcast_to(k_dec[:, :, None], (H_R, ch, DK_R))
    kdec = kdec.reshape(H_R // 2, 2, ch, DK_R).transpose(0, 2, 1, 3).reshape(H_R // 2, ch, LANES)
    gpow = jnp.broadcast_to(jnp.exp(ch * lg)[:, None, None], (H_R, 1, DV_R))

    has_init = state0 is not None
    blk = lambda w: pl.BlockSpec((seq, w), lambda b: (b, 0))
    in_specs = [blk(RQK_W), blk(RQK_W), blk(RV_W), blk(RV_W),
                _const_spec((H_R, ch, ch)), _const_spec((H_R, ch, DV_R)),
                _const_spec((H_R // 2, ch, LANES)), _const_spec((H_R, 1, DV_R)), _const_spec((1, DV_R))]
    args = [rq, rk, rv, rg, dmat, qdec, kdec, gpow, g_ret_out]
    if has_init:
        in_specs.append(pl.BlockSpec((1, H_R, DK_R, DV_R), lambda b: (b, 0, 0, 0)))
        args.append(state0)
    return pl.pallas_call(
        functools.partial(_ret_kernel, has_init=has_init, ch=ch),
        grid=(n_batch,),
        in_specs=in_specs,
        out_specs=[blk(RV_W), pl.BlockSpec((1, H_R, DK_R, DV_R), lambda b: (b, 0, 0, 0))],
        out_shape=[jax.ShapeDtypeStruct((n_batch * seq, RV_W), BF16),
                   jax.ShapeDtypeStruct((n_batch, H_R, DK_R, DV_R), F32)],
        compiler_params=_cparams(("parallel",)),
        name="retention",
    )(*args)


def _split3(x):
    hi = x.astype(BF16)
    r1 = x - hi.astype(F32)
    mid = r1.astype(BF16)
    lo = (r1 - mid.astype(F32)).astype(BF16)
    return hi, mid, lo


def _cumsum_kernel(x_ref, o_ref, *, blk):
    rows, n = x_ref.shape
    r = lax.broadcasted_iota(jnp.int32, (blk, blk), 0)
    c = lax.broadcasted_iota(jnp.int32, (blk, blk), 1)
    tri = jnp.where(r <= c, 1.0, 0.0).astype(BF16)
    carry = jnp.zeros((rows, 1), F32)
    for i in range(n // blk):
        hi, mid, lo = _split3(x_ref[:, i * blk:(i + 1) * blk])
        cum = (_dot(hi, tri) + _dot(mid, tri)) + _dot(lo, tri) + carry
        o_ref[:, i * blk:(i + 1) * blk] = cum
        carry = cum[:, blk - 1:blk]


def _cumsum(x, blk):
    r, t = x.shape
    spec = pl.BlockSpec((H_F, t), lambda b: (b, 0))
    return pl.pallas_call(
        functools.partial(_cumsum_kernel, blk=blk),
        grid=(r // H_F,),
        in_specs=[spec], out_specs=spec,
        out_shape=jax.ShapeDtypeStruct(x.shape, F32),
        compiler_params=_cparams(("parallel",)),
        name="cumsum",
    )(x)


def _head_rows_mask(shape, hh):
    sub = lax.broadcasted_iota(jnp.int32, shape, 0)
    return (sub >= hh * D_F) & (sub < (hh + 1) * D_F)


N_AUG = 3
V_ROWS = D_F + 16


def _fox_kernel(qt_ref, k_ref, vt_ref, fcum_ref, o_ref, qa, ka, vb, *, tq):
    seq = k_ref.shape[0]
    tk = tq
    k = k_ref[...].astype(F32)
    lane = lax.broadcasted_iota(jnp.int32, k.shape, 1)
    aug_rows = lax.broadcasted_iota(jnp.int32, (D_F, seq), 0) < N_AUG
    for hh in range(2):
        own, oth = hh * D_F, (1 - hh) * D_F
        qa[hh, own:own + D_F, :] = qt_ref[own:own + D_F, :]
        qa[hh, oth:oth + D_F, :] = jnp.where(aug_rows, 1.0, 0.0).astype(BF16)
        pieces = [p.astype(F32) for p in _split3(fcum_ref[0, hh:hh + 1, :] * (-LOG2E))]
        ka[hh] = jnp.where((lane >= own) & (lane < own + D_F), k, _rows_to_columns(pieces, oth)).astype(BF16)
    for hh in range(2):
        vb[hh, :D_F, :] = vt_ref[hh * D_F:(hh + 1) * D_F, :].astype(BF16)
        vb[hh, D_F:, :] = jnp.ones((V_ROWS - D_F, seq), BF16)
    causal = (lax.broadcasted_iota(jnp.int32, (tk, tq), 1) >= lax.broadcasted_iota(jnp.int32, (tk, tq), 0))

    pairs = [(i, j) for i in range(seq // tq) for j in range(i + 1)]

    def stage_a(i, j):
        return tuple(_dot(ka[hh, j * tk:(j + 1) * tk, :], qa[hh, :, i * tq:(i + 1) * tq]) for hh in range(2))

    def stage_b(i, j, ts, maxes):
        new_maxes, probs = [], []
        for hh in range(2):
            t = jnp.where(causal, ts[hh], -jnp.inf) if j == i else ts[hh]
            t_max = jnp.max(t, axis=0, keepdims=True)
            if j == 0:
                m_new, alpha = t_max, None
            else:
                m_new = jnp.maximum(maxes[hh], t_max)
                alpha = jnp.exp2(maxes[hh] - m_new)
            new_maxes.append(m_new)
            probs.append((jnp.exp2(t - m_new).astype(BF16), alpha))
        return new_maxes, probs

    def stage_c(i, j, probs, accs):
        out = []
        for hh in range(2):
            p, alpha = probs[hh]
            pv = _dot(vb[hh, :, j * tk:(j + 1) * tk], p)
            out.append(pv if j == 0 else alpha * accs[hh] + pv)
        return out

    scores, probs, maxes, accs = {}, {}, None, None
    for s in range(len(pairs) + 2):
        if s < len(pairs):
            scores[s] = stage_a(*pairs[s])
        if 0 <= s - 1 < len(pairs):
            maxes, probs[s - 1] = stage_b(*pairs[s - 1], scores.pop(s - 1), maxes)
        if 0 <= s - 2 < len(pairs):
            i, j = pairs[s - 2]
            accs = stage_c(i, j, probs.pop(s - 2), accs)
            if j == i:
                o_t = jnp.concatenate([a[:D_F] / a[D_F:D_F + 1] for a in accs], axis=0)
                o_ref[i * tq:(i + 1) * tq, :] = o_t.T.astype(BF16)


def _fox_prompt(fqt, fkb, fvt, fcum, n_batch, seq, tq):
    npair = H_F // 2
    fm_spec = pl.BlockSpec((LANES, seq), lambda b, p: (b * npair + p, 0))
    tok_spec = pl.BlockSpec((seq, LANES), lambda b, p: (b, p))
    return pl.pallas_call(
        functools.partial(_fox_kernel, tq=tq),
        grid=(n_batch, npair),
        in_specs=[fm_spec, tok_spec, fm_spec, pl.BlockSpec((1, 2, seq), lambda b, p: (b * npair + p, 0, 0))],
        out_specs=tok_spec,
        out_shape=jax.ShapeDtypeStruct((n_batch * seq, FOX_W), BF16),
        scratch_shapes=[pltpu.VMEM((2, LANES, seq), BF16), pltpu.VMEM((2, seq, LANES), BF16),
                        pltpu.VMEM((2, V_ROWS, seq), BF16)],
        compiler_params=_cparams(("parallel", "parallel")),
        name="fox_prompt",
    )(fqt, fkb, fvt, fcum)


def _fox_sample_kernel(q_ref, kct_ref, vct_ref, kn_ref, vn_ref, frc_ref, frn_ref, o_ref):
    n = q_ref.shape[0]
    lane = lax.broadcasted_iota(jnp.int32, (n, LANES), 1)
    rows = lax.broadcasted_iota(jnp.int32, (n, n), 0)
    cols = lax.broadcasted_iota(jnp.int32, (n, n), 1)
    for pair in range(H_F // 2):
        sl = slice(pair * LANES, (pair + 1) * LANES)
        q2 = q_ref[:, sl]
        kct = kct_ref[sl, :]
        vct = vct_ref[sl, :].astype(BF16)
        kn = kn_ref[:, sl].astype(BF16)
        vn = vn_ref[:, sl].astype(BF16)
        outs = []
        for hh in range(2):
            h = 2 * pair + hh
            mine = (lane >= hh * D_F) & (lane < (hh + 1) * D_F)
            kc = jnp.where(_head_rows_mask(kct.shape, hh), kct, 0.0).astype(BF16)
            t_c = _dot(q2, kc) - frc_ref[0, h:h + 1, :] * LOG2E
            t_n = _dot_nt(jnp.where(mine, q2, jnp.zeros_like(q2)), kn) - frn_ref[0, h:h + 1, :] * LOG2E
            t_n = jnp.where(rows >= cols, t_n, -jnp.inf)
            m = jnp.maximum(jnp.max(t_c, axis=-1, keepdims=True), jnp.max(t_n, axis=-1, keepdims=True))
            p_c = jnp.exp2(t_c - m)
            p_n = jnp.exp2(t_n - m)
            l = jnp.sum(p_c, axis=-1, keepdims=True) + jnp.sum(p_n, axis=-1, keepdims=True)
            acc = _dot_nt(p_c.astype(BF16), vct) + _dot(p_n.astype(BF16), vn)
            outs.append(acc / l)
        o_ref[:, sl] = jnp.where(lane < D_F, outs[0], outs[1]).astype(BF16)


def _fox_sample(fq, kct, vct, kn, vn, frow_c, frow_n, n_batch, n_new, past):
    tok = pl.BlockSpec((n_new, FOX_W), lambda b: (b, 0))
    cache = pl.BlockSpec((FOX_W, past), lambda b: (b, 0))
    return pl.pallas_call(
        _fox_sample_kernel,
        grid=(n_batch,),
        in_specs=[tok, cache, cache, tok, tok,
                  pl.BlockSpec((1, H_F, past), lambda b: (b, 0, 0)),
                  pl.BlockSpec((1, H_F, n_new), lambda b: (b, 0, 0))],
        out_specs=tok,
        out_shape=jax.ShapeDtypeStruct((n_batch * n_new, FOX_W), BF16),
        compiler_params=_cparams(("parallel",)),
        name="fox_sample",
    )(fq, kct, vct, kn, vn, frow_c, frow_n)


def _memkv_kernel(m_ref, gin_ref, w_ref, gk_ref, k_ref, v_ref):
    x = m_ref[...]
    h = (x * lax.rsqrt(jnp.mean(x * x, axis=-1, keepdims=True) + EPS) * gin_ref[...]).astype(BF16)
    zk = _dot(h, w_ref[:, :MEM_W])
    for hh in range(H_M):
        sl = slice(hh * D_M, (hh + 1) * D_M)
        k_ref[:, sl] = _lane_rmsnorm(zk[:, sl]) * gk_ref[...]
    v_ref[...] = _dot(h, w_ref[:, MEM_W:])


def _memkv(mem2d, n_batch, g_mem_in, w_mem_kv, g_mem_k):
    blk = lambda w: pl.BlockSpec((N_MEM, w), lambda b: (b, 0))
    return pl.pallas_call(
        _memkv_kernel,
        grid=(n_batch,),
        in_specs=[blk(D_MODEL), _const_spec((1, D_MODEL)), _const_spec((D_MODEL, 2 * MEM_W)),
                  _const_spec((1, D_M))],
        out_specs=[blk(MEM_W), blk(MEM_W)],
        out_shape=[jax.ShapeDtypeStruct((n_batch * N_MEM, MEM_W), F32)] * 2,
        compiler_params=_cparams(("parallel",)),
        name="memkv",
    )(mem2d, g_mem_in, w_mem_kv, g_mem_k)


def _mem_attend(q_ref, k_ref, v_ref):
    for hh in range(H_M):
        sl = slice(hh * D_M, (hh + 1) * D_M)
        s = _dot_nt(q_ref[:, sl], k_ref[:, sl].astype(BF16)) * (D_M ** -0.5)
        m = jnp.max(s, axis=-1, keepdims=True)
        p = jnp.exp(s - m)
        l = jnp.sum(p, axis=-1, keepdims=True)
        yield sl, (_dot(p.astype(BF16), v_ref[:, sl].astype(BF16)) / l).astype(BF16)


def _memattn_kernel(q_ref, k_ref, v_ref, o_ref):
    for sl, o in _mem_attend(q_ref, k_ref, v_ref):
        o_ref[:, sl] = o


def _memattn(mq, mk, mv, n_batch, seq, tq):
    nq = seq // tq
    return pl.pallas_call(
        _memattn_kernel,
        grid=(n_batch, nq),
        in_specs=[pl.BlockSpec((tq, MEM_W), lambda b, i: (b * nq + i, 0)),
                  pl.BlockSpec((N_MEM, MEM_W), lambda b, i: (b, 0)),
                  pl.BlockSpec((N_MEM, MEM_W), lambda b, i: (b, 0))],
        out_specs=pl.BlockSpec((tq, MEM_W), lambda b, i: (b * nq + i, 0)),
        out_shape=jax.ShapeDtypeStruct((n_batch * seq, MEM_W), BF16),
        compiler_params=_cparams(("parallel", "arbitrary")),
        name="memattn",
    )(mq, mk, mv)


def _split2(x):
    hi = x.astype(BF16)
    return hi, (x - hi.astype(F32)).astype(BF16)


N_EXPERTS = N_GROUPS * E_PER_GROUP


N_PAIRS = E_PER_GROUP * (E_PER_GROUP - 1) // 2


def _router_logits(h2, wrt_ref, brt_ref):
    h_hi, h_lo = _split2(h2)
    z = _dot(h_hi, wrt_ref[...])
    return (z[:, :LANES] + z[:, LANES:]) + _dot(h_lo, wrt_ref[:, :LANES]) + brt_ref[...]


N_LOGIT_ROWS = 40


def _route(logits, group=None):
    lt = logits.T[:N_LOGIT_ROWS, :]
    idx = lax.broadcasted_iota(jnp.int32, lt.shape, 0).astype(F32)
    neg = -jnp.inf
    first_idx = lambda mask: jnp.min(jnp.where(mask, idx, float(LANES)), axis=0, keepdims=True)

    is_g = (idx >= N_EXPERTS) & (idx < N_EXPERTS + N_GROUPS)
    lg = jnp.where(is_g, lt, neg)
    mg = jnp.max(lg, axis=0, keepdims=True)
    if group is None:
        group = first_idx(lg == mg) - N_EXPERTS
        p_sel = 1.0 / jnp.sum(jnp.exp(lg - mg), axis=0, keepdims=True)
    else:
        lsel = jnp.max(jnp.where(idx == group + N_EXPERTS, lt, neg), axis=0, keepdims=True)
        p_sel = jnp.exp(lsel - mg) / jnp.sum(jnp.exp(lg - mg), axis=0, keepdims=True)

    in_grp = (idx >= group * E_PER_GROUP) & (idx < (group + 1) * E_PER_GROUP)
    le = jnp.where(in_grp, lt, neg)
    v1 = jnp.max(le, axis=0, keepdims=True)
    i1 = first_idx(le == v1)
    le2 = jnp.where(idx == i1, neg, le)
    v2 = jnp.max(le2, axis=0, keepdims=True)
    i2 = first_idx(le2 == v2)
    e2 = jnp.exp(v2 - v1)
    return group, i1, i2, p_sel / (1.0 + e2), p_sel * e2 / (1.0 + e2)


def _rows_to_columns(rows, first=0):
    n = rows[0].shape[1]
    idx = lax.broadcasted_iota(jnp.int32, (LANES, n), 0)
    stacked = jnp.zeros((LANES, n), F32)
    for r, row in enumerate(rows):
        stacked = jnp.where(idx == first + r, row, stacked)
    return stacked.T


def _merge_kernel(*refs, sorted_moe):
    if sorted_moe:
        (x_ref, or_ref, of_ref, mq_ref, mk_ref, mv_ref, gt_ref, wr_ref, wf_ref, wm_ref, wo_ref, g2_ref,
         wrt_ref, brt_ref, x1_ref, h2_ref, route_ref, *counts_ref) = refs
        o_m = jnp.concatenate([o for _, o in _mem_attend(mq_ref, mk_ref, mv_ref)], axis=1)
    else:
        (x_ref, or_ref, of_ref, om_ref, gt_ref, wr_ref, wf_ref, wm_ref, wo_ref, g2_ref,
         wrt_ref, brt_ref, x1_ref, h2_ref, route_ref) = refs
        o_m = om_ref[...]
    g = lambda b: gt_ref[:, b * D_MODEL:(b + 1) * D_MODEL].astype(F32)
    merged = (g(0) * _dot(or_ref[...], wr_ref[...]) + g(1) * _dot(of_ref[...], wf_ref[...])
              + g(2) * _dot(o_m, wm_ref[...]))
    x1 = x_ref[...] + _dot(merged.astype(BF16), wo_ref[...])
    x1_ref[...] = x1
    h2 = x1 * lax.rsqrt(jnp.mean(x1 * x1, axis=-1, keepdims=True) + EPS) * g2_ref[...]
    group, i1, i2, w1, w2 = _route(_router_logits(h2, wrt_ref, brt_ref))
    tm = h2.shape[0]
    if sorted_moe:
        _to_token_tiles(h2_ref, h2)
        e_lo = jnp.minimum(i1, i2) - group * E_PER_GROUP
        e_hi = jnp.maximum(i1, i2) - group * E_PER_GROUP
        cls = group * N_PAIRS + (e_lo * E_PER_GROUP - e_lo * (e_lo + 1.0) * 0.5 + (e_hi - e_lo - 1.0))
        cidx = lax.broadcasted_iota(jnp.int32, (LANES, tm), 0).astype(F32)
        onehot = jnp.where(cidx == cls, 1.0, 0.0)
        r = lax.broadcasted_iota(jnp.int32, (tm, tm), 0)
        c = lax.broadcasted_iota(jnp.int32, (tm, tm), 1)
        before = _dot(onehot.astype(BF16), jnp.where(r < c, 1.0, 0.0).astype(BF16))
        rank = jnp.sum(before * onehot, axis=0, keepdims=True)
        ridx = lax.broadcasted_iota(jnp.int32, route_ref.shape, 0)
        route_ref[...] = jnp.where(ridx == 0, group, jnp.where(ridx == 1, cls, jnp.where(ridx == 2, rank, 0.0)))
        counts_ref[0][0] = jnp.broadcast_to(jnp.sum(onehot, axis=1, keepdims=True), (LANES, LANES))
    else:
        h2_ref[...] = h2.astype(BF16)
        cols = _rows_to_columns([i1, i2, w1, w2])
        lane = lax.broadcasted_iota(jnp.int32, (tm, LANES), 1).astype(F32)
        comb = jnp.where(lane == cols[:, 0:1], cols[:, 2:3], 0.0) + jnp.where(lane == cols[:, 1:2], cols[:, 3:4], 0.0)
        for gi in range(N_GROUPS):
            route_ref[gi] = comb[:, gi * E_PER_GROUP:(gi + 1) * E_PER_GROUP]


def _merge(x2d, o_r, o_f, mem, gates, tm, sorted_moe, wr, wf, wm, wo, g2, wrt, brt):
    t = x2d.shape[0]
    row = lambda w: pl.BlockSpec((tm, w), lambda i: (i, 0))
    sds = jax.ShapeDtypeStruct
    if sorted_moe:
        h2_spec, h2_shape = pl.BlockSpec((tm * N_SLABS, LANES), lambda i: (i, 0)), sds((t * N_SLABS, LANES), F32)
        rt_specs = [pl.BlockSpec((SUBLANES, tm), lambda i: (i, 0)),
                    pl.BlockSpec((1, LANES, LANES), lambda i: (i, 0, 0))]
        rt_shapes = [sds((t // tm * SUBLANES, tm), F32), sds((t // tm, LANES, LANES), F32)]
        mq, mk, mv, tiles_per_seq = mem
        mem_args = [mq, mk, mv]
        mem_kv = pl.BlockSpec((N_MEM, MEM_W), lambda i: (i // tiles_per_seq, 0))
        mem_specs = [row(MEM_W), mem_kv, mem_kv]
    else:
        h2_spec, h2_shape = row(D_MODEL), sds((t, D_MODEL), BF16)
        rt_specs = [pl.BlockSpec((N_GROUPS, tm, E_PER_GROUP), lambda i: (0, i, 0))]
        rt_shapes = [sds((N_GROUPS, t, E_PER_GROUP), F32)]
        mem_args, mem_specs = [mem], [row(MEM_W)]
    return pl.pallas_call(
        functools.partial(_merge_kernel, sorted_moe=sorted_moe),
        grid=(t // tm,),
        in_specs=[row(D_MODEL), row(RV_W), row(FOX_W), *mem_specs, row(GATE_W),
                  _const_spec((RV_W, D_MODEL)), _const_spec((FOX_W, D_MODEL)), _const_spec((MEM_W, D_MODEL)),
                  _const_spec((D_MODEL, D_MODEL)), _const_spec((1, D_MODEL)),
                  _const_spec((D_MODEL, 2 * LANES)), _const_spec((1, LANES))],
        out_specs=[row(D_MODEL), h2_spec, *rt_specs],
        out_shape=[sds((t, D_MODEL), F32), h2_shape, *rt_shapes],
        compiler_params=_cparams(("parallel",)),
        name="merge",
    )(x2d, o_r, o_f, *mem_args, gates, wr, wf, wm, wo, g2, wrt, brt)


def _positions_kernel(route_ref, first_ref, pos_ref):
    tm = route_ref.shape[1]
    cidx = lax.broadcasted_iota(jnp.int32, (LANES, tm), 0).astype(F32)
    for i in range(first_ref.shape[0]):
        route = route_ref[i * SUBLANES:(i + 1) * SUBLANES, :]
        first = jnp.concatenate([first_ref[i]] * (tm // LANES), axis=1)
        start = jnp.sum(jnp.where(cidx == route[1:2, :], first, 0.0), axis=0, keepdims=True)
        pos_ref[i * SUBLANES:(i + 1) * SUBLANES, :] = jnp.broadcast_to(start + route[2:3, :],
                                                                       route.shape).astype(jnp.int32)


def _positions(route, first_row, tiles_per_step):
    rows, tm = route.shape
    n = rows // SUBLANES
    out = pl.pallas_call(
        _positions_kernel,
        grid=(n // tiles_per_step,),
        in_specs=[pl.BlockSpec((tiles_per_step * SUBLANES, tm), lambda i: (i, 0)),
                  pl.BlockSpec((tiles_per_step, LANES, LANES), lambda i: (i, 0, 0))],
        out_specs=pl.BlockSpec((tiles_per_step * SUBLANES, tm), lambda i: (i, 0)),
        out_shape=jax.ShapeDtypeStruct((rows, tm), jnp.int32),
        compiler_params=_cparams(("parallel",)),
        name="positions",
    )(route, first_row)
    return out.reshape(n, SUBLANES, tm)[:, 0, :].reshape(n * tm)


def _group_experts(h, cw, wg_ref, wu_ref, wd_ref, act):
    for e in range(E_PER_GROUP):
        a = _dot(h, wg_ref[0, e])
        u = _dot(h, wu_ref[0, e])
        act[:, e * D_EXPERT:(e + 1) * D_EXPERT] = ((a * jax.nn.sigmoid(a)) * u * cw(e)).astype(BF16)
    return _dot(act[...], wd_ref[0])


N_SLABS = D_MODEL // LANES
SUBLANES = 8


def _to_token_tiles(ref, x):
    for s in range(N_SLABS):
        ref[pl.ds(s, x.shape[0], stride=N_SLABS), :] = x[:, s * LANES:(s + 1) * LANES]


def _landing_shape(rows):
    return (rows // SUBLANES, N_SLABS, SUBLANES, LANES)


def _from_landing(buf):
    rows = buf.shape[0] * SUBLANES
    return jnp.concatenate([buf[:, s].reshape(rows, LANES) for s in range(N_SLABS)], axis=1)


def _row_gather(idx_ref, base, src_hbm, dst, sem, straight_line=False):
    def body(i, _):
        for u in range(SUBLANES):
            pltpu.make_async_copy(src_hbm.at[idx_ref[base + i * SUBLANES + u]], dst.at[i, :, u, :],
                                  sem).start(priority=u % 2)
        return 0
    if straight_line:
        for i in range(dst.shape[0]):
            body(i, 0)
    else:
        lax.fori_loop(0, dst.shape[0], body, 0)


def _row_gather_wait(dst, sem):
    pltpu.make_async_copy(dst, dst, sem).wait()


def _moe_sorted_kernel(tg_ref, nvt_ref, pos_ref, h3_hbm, zeros_hbm, wrt_ref, brt_ref,
                       wg_ref, wu_ref, wd_ref, y_ref, xbuf, sem, hb, yacc, src_ref, clear_sem, *, tm):
    k = pl.program_id(0)
    nvt = nvt_ref[0]
    slot = lax.rem(k, 2)

    @pl.when(k == 0)
    def _():
        clear = pltpu.make_async_copy(zeros_hbm, src_ref, clear_sem)
        clear.start()
        clear.wait()

        def invert(t, _):
            src_ref[pos_ref[t]] = t
            return 0

        lax.fori_loop(0, pos_ref.shape[0], invert, 0, unroll=32)

    @pl.when((k == 0) & (nvt > 0))
    def _():
        _row_gather(src_ref, 0, h3_hbm, xbuf.at[0], sem.at[0])

    @pl.when(k < nvt)
    def _():
        _row_gather_wait(xbuf.at[slot], sem.at[slot])
        _row_gather(src_ref, jnp.minimum(k + 1, nvt - 1) * tm, h3_hbm, xbuf.at[1 - slot], sem.at[1 - slot],
                    straight_line=True)

        x = _from_landing(xbuf.at[slot])
        group = tg_ref[k]
        _, i1, i2, w1, w2 = _route(_router_logits(x, wrt_ref, brt_ref), group.astype(F32))
        hb[...] = x.astype(BF16)
        yacc[...] = jnp.zeros(yacc.shape, F32)
        first = (group * E_PER_GROUP).astype(F32)
        used = [jnp.max(jnp.where(i1 == first + e, w1, 0.0) + jnp.where(i2 == first + e, w2, 0.0))
                for e in range(E_PER_GROUP)]
        cols = _rows_to_columns([i1, i2, w1, w2])
        for e in range(E_PER_GROUP):
            @pl.when(used[e] > 0.0)
            def _(e=e):
                cw = (jnp.where(cols[:, 0:1] == first + e, cols[:, 2:3], 0.0)
                      + jnp.where(cols[:, 1:2] == first + e, cols[:, 3:4], 0.0))
                h = hb[...]
                a = _dot(h, wg_ref[0, e])
                u = _dot(h, wu_ref[0, e])
                act = ((a * jax.nn.sigmoid(a)) * u * cw).astype(BF16)
                yacc[...] += _dot(act, wd_ref[0, e * D_EXPERT:(e + 1) * D_EXPERT, :])

        _to_token_tiles(y_ref, yacc[...])

        @pl.when(k + 1 >= nvt)
        def _():
            _row_gather_wait(xbuf.at[1 - slot], sem.at[1 - slot])

    @pl.when(k >= nvt)
    def _():
        y_ref[...] = jnp.zeros(y_ref.shape, F32)


def _moe_sorted(h3, tile_group, n_valid_tiles, pos, n_tiles, tm, wrt, brt, wg, wu, wd):
    wspec = lambda shape: pl.BlockSpec(shape, lambda k, tg, nv, ps: (tg[k],) + (0,) * (len(shape) - 1))
    cspec = lambda shape: pl.BlockSpec(shape, lambda k, tg, nv, ps: (0,) * len(shape),
                                       pipeline_mode=pl.Buffered(1))
    return pl.pallas_call(
        functools.partial(_moe_sorted_kernel, tm=tm),
        grid_spec=pltpu.PrefetchScalarGridSpec(
            num_scalar_prefetch=3,
            grid=(n_tiles,),
            in_specs=[pl.BlockSpec(memory_space=pl.ANY), pl.BlockSpec(memory_space=pl.ANY),
                      cspec((D_MODEL, 2 * LANES)), cspec((1, LANES)),
                      wspec((1, E_PER_GROUP, D_MODEL, D_EXPERT)), wspec((1, E_PER_GROUP, D_MODEL, D_EXPERT)),
                      wspec((1, E_PER_GROUP * D_EXPERT, D_MODEL))],
            out_specs=pl.BlockSpec((tm * N_SLABS, LANES), lambda k, tg, nv, ps: (k, 0)),
            scratch_shapes=[pltpu.VMEM((2,) + _landing_shape(tm), F32), pltpu.SemaphoreType.DMA((2,)),
                            pltpu.VMEM((tm, D_MODEL), BF16), pltpu.VMEM((tm, D_MODEL), F32),
                            pltpu.SMEM((n_tiles * tm,), jnp.int32), pltpu.SemaphoreType.DMA(())]),
        out_shape=jax.ShapeDtypeStruct((n_tiles * tm * N_SLABS, LANES), F32),
        compiler_params=_cparams(("arbitrary",)),
        name="moe_sorted",
    )(tile_group, n_valid_tiles, pos, h3, jnp.zeros((n_tiles * tm,), jnp.int32), wrt, brt, wg, wu, wd)


def _combine_kernel(pos_ref, y3_hbm, x1_ref, o_ref, ybuf, sem, *, tm):
    k = pl.program_id(0)
    slot = lax.rem(k, 2)

    @pl.when(k == 0)
    def _():
        _row_gather(pos_ref, 0, y3_hbm, ybuf.at[0], sem.at[0])

    _row_gather_wait(ybuf.at[slot], sem.at[slot])

    @pl.when(k + 1 < pl.num_programs(0))
    def _():
        _row_gather(pos_ref, (k + 1) * tm, y3_hbm, ybuf.at[1 - slot], sem.at[1 - slot])

    o_ref[...] = x1_ref[...] + _from_landing(ybuf.at[slot])


def _combine(y3, pos, x1, tm):
    t = x1.shape[0]
    return pl.pallas_call(
        functools.partial(_combine_kernel, tm=tm),
        grid_spec=pltpu.PrefetchScalarGridSpec(
            num_scalar_prefetch=1,
            grid=(t // tm,),
            in_specs=[pl.BlockSpec(memory_space=pl.ANY), pl.BlockSpec((tm, D_MODEL), lambda k, ps: (k, 0))],
            out_specs=pl.BlockSpec((tm, D_MODEL), lambda k, ps: (k, 0)),
            scratch_shapes=[pltpu.VMEM((2,) + _landing_shape(tm), F32), pltpu.SemaphoreType.DMA((2,))]),
        out_shape=jax.ShapeDtypeStruct((t, D_MODEL), F32),
        compiler_params=_cparams(("arbitrary",)),
        name="combine",
    )(pos, y3, x1)


def _moe_kernel(h_ref, comb_ref, x1_ref, wg_ref, wu_ref, wd_ref, o_ref, act):
    g = pl.program_id(1)
    comb = comb_ref[0]
    y = _group_experts(h_ref[...], lambda e: comb[:, e:e + 1], wg_ref, wu_ref, wd_ref, act)

    @pl.when(g == 0)
    def _():
        o_ref[...] = x1_ref[...] + y

    @pl.when(g != 0)
    def _():
        o_ref[...] += y


def _moe(h2, comb, x1, tm, wg, wu, wd):
    t = h2.shape[0]
    return pl.pallas_call(
        _moe_kernel,
        grid=(t // tm, N_GROUPS),
        in_specs=[pl.BlockSpec((tm, D_MODEL), lambda i, g: (i, 0)),
                  pl.BlockSpec((1, tm, E_PER_GROUP), lambda i, g: (g, i, 0)),
                  pl.BlockSpec((tm, D_MODEL), lambda i, g: (i, 0)),
                  pl.BlockSpec((1, E_PER_GROUP, D_MODEL, D_EXPERT), lambda i, g: (g, 0, 0, 0)),
                  pl.BlockSpec((1, E_PER_GROUP, D_MODEL, D_EXPERT), lambda i, g: (g, 0, 0, 0)),
                  pl.BlockSpec((1, E_PER_GROUP * D_EXPERT, D_MODEL), lambda i, g: (g, 0, 0))],
        out_specs=pl.BlockSpec((tm, D_MODEL), lambda i, g: (i, 0)),
        out_shape=jax.ShapeDtypeStruct((t, D_MODEL), F32),
        scratch_shapes=[pltpu.VMEM((tm, E_PER_GROUP * D_EXPERT), BF16)],
        compiler_params=_cparams(("parallel", "arbitrary")),
        name="moe",
    )(h2, comb, x1, wg, wu, wd)


def _rope_tables(pos):
    half = DK_R // 2
    inv = ROPE_BASE ** (-jnp.arange(half, dtype=F32) / half)
    ang = pos.astype(F32)[:, None] * inv[None, :]
    c = jnp.cos(ang)
    s = jnp.sin(ang)
    return jnp.concatenate([c, c, c, c], axis=-1), jnp.concatenate([-s, s, -s, s], axis=-1)


def kernel(x_prompt, x_sample, state_ret, cache_fox_k, cache_fox_v, cache_fox_logf, cache_mem_k, cache_mem_v,
           mem_prompt, g_norm1, w_in, b_forget, g_fox_q, g_fox_k, g_mem_q, g_mem_in, w_mem_kv, g_mem_k,
           g_ret_out, w_br_ret, w_br_fox, w_br_mem, w_out, g_norm2, w_route_group, b_route_group,
           w_route_expert, b_route_expert, w_exp_gate, w_exp_up, w_exp_down):
    nb, seq, _ = x_prompt.shape
    nbs, n_new, _ = x_sample.shape
    past = cache_fox_k.shape[2]
    l = 0

    wi = w_in[l]
    o_ff = 2 * RQK_W + 2 * RV_W + 3 * FOX_W
    w_in_r = (wi[:, :o_ff].astype(BF16), wi[:, o_ff + H_F:].astype(BF16),
              jnp.pad(wi[:, o_ff:o_ff + H_F], ((0, 0), (0, LANES - H_F))).astype(BF16))
    bf_pad = jnp.concatenate([b_forget[l], jnp.zeros((LANES - H_F,), F32)])[None, :]
    g1 = g_norm1[l][None, :]
    gfq = jnp.tile(g_fox_q[l], H_F)[None, :]
    gfk = jnp.tile(g_fox_k[l], H_F)[None, :]
    gmq = g_mem_q[l][None, :]
    hid = jnp.arange(MXU_DIM) // D_F
    bd = jnp.where(hid[:, None] == hid[None, :], 1.0 / D_F, 0.0).astype(BF16)
    wr = w_br_ret[l].astype(BF16)
    wf = w_br_fox[l].astype(BF16)
    wm = w_br_mem[l].astype(BF16)
    wo = w_out[l].astype(BF16)
    g2 = g_norm2[l][None, :]
    n_e = N_GROUPS * E_PER_GROUP
    wrt = jnp.concatenate([w_route_expert[l], w_route_group[l],
                           jnp.zeros((D_MODEL, LANES - n_e - N_GROUPS), F32)], axis=1)
    wrt_hi = wrt.astype(BF16)
    wrt2 = jnp.concatenate([wrt_hi, (wrt - wrt_hi.astype(F32)).astype(BF16)], axis=1)
    brt = jnp.concatenate([b_route_expert[l], b_route_group[l],
                           jnp.zeros((LANES - n_e - N_GROUPS,), F32)])[None, :]
    wg = w_exp_gate[l].astype(BF16)
    wu = w_exp_up[l].astype(BF16)
    wd = w_exp_down[l].astype(BF16).reshape(N_GROUPS, E_PER_GROUP * D_EXPERT, D_MODEL)
    gro = g_ret_out[l][None, :]
    prep_w = (g1, w_in_r, bf_pad, gfq, gfk, gmq, bd)
    merge_w = (wr, wf, wm, wo, g2, wrt2, brt)

    tm = 512
    xp = x_prompt.reshape(nb * seq, D_MODEL)
    (rq, rk, rv, rg, fqt, fkt, fvt, lft, mq, gates, fkb) = _prep(
        xp, _rope_tables(jnp.arange(seq)), tm, seq // tm, True, *prep_w)
    o_r, s_fin = _retention(rq, rk, rv, rg, nb, seq, 256, gro, None)
    fcum = _cumsum(lft, 256).reshape(nb * (H_F // 2), 2, seq)
    o_f = _fox_prompt(fqt, fkb, fvt, fcum, nb, seq, 256)
    mk, mv = _memkv(mem_prompt.reshape(nb * N_MEM, D_MODEL), nb, g_mem_in[l][None, :],
                    w_mem_kv[l].astype(BF16), g_mem_k[l][None, :])
    x1, h3, route, counts = _merge(xp, o_r, o_f, (mq, mk, mv, seq // tm), gates, tm, True, *merge_w)
    t_p = nb * seq
    n_tiles = t_p // tm + N_GROUPS
    n_cls = N_GROUPS * N_PAIRS
    cnt = counts[:, :n_cls, 0].astype(jnp.int32)
    before_tile = jnp.cumsum(cnt, axis=0) - cnt
    cls_tot = jnp.sum(cnt, axis=0)
    grp_tot = jnp.sum(cls_tot.reshape(N_GROUPS, N_PAIRS), axis=1)
    tiles_g = (grp_tot + tm - 1) // tm
    tile_end = jnp.cumsum(tiles_g)
    row_start = (tile_end - tiles_g) * tm
    in_grp = cls_tot.reshape(N_GROUPS, N_PAIRS)
    cls_start = (row_start[:, None] + jnp.cumsum(in_grp, axis=1) - in_grp).reshape(n_cls)
    first_row = jnp.pad((cls_start[None, :] + before_tile).astype(F32), ((0, 0), (0, LANES - n_cls)))
    pos = _positions(route, jnp.broadcast_to(first_row[:, :, None], (t_p // tm, LANES, LANES)),
                     min(8, t_p // tm))
    tile_ids = jnp.arange(n_tiles, dtype=jnp.int32)
    tile_group = jnp.minimum(jnp.sum((tile_ids[:, None] >= tile_end[None, :]).astype(jnp.int32), axis=1),
                             N_GROUPS - 1)
    y3 = _moe_sorted(h3.reshape(t_p, N_SLABS, LANES), tile_group, tile_end[-1:], pos, n_tiles, tm,
                     wrt2, brt, wg, wu, wd)
    y_prompt = _combine(y3.reshape(n_tiles * tm, N_SLABS, LANES), pos, x1, 2 * tm).reshape(nb, seq, D_MODEL)

    ts = nbs * n_new
    xs = x_sample.reshape(ts, D_MODEL)
    pos_s = jnp.tile(past + jnp.arange(n_new), nbs)
    (rq_s, rk_s, rv_s, rg_s, fq_s, fk_s, fv_s, lf_s, mq_s, gates_s) = _prep(
        xs, _rope_tables(pos_s), ts, 1, False, *prep_w)
    o_r_s, s_new = _retention(rq_s, rk_s, rv_s, rg_s, nbs, n_new, n_new, gro, state_ret[l])
    pad = (-(past + n_new)) % 256
    lf_rows = jnp.concatenate([jnp.swapaxes(cache_fox_logf[l], 1, 2),
                               jnp.swapaxes(lf_s.reshape(nbs, n_new, H_F), 1, 2),
                               jnp.zeros((nbs, H_F, pad), F32)], axis=2).reshape(nbs * H_F, past + n_new + pad)
    f_all = _cumsum(lf_rows, 256)
    feat_major = lambda c: jnp.transpose(c, (0, 2, 3, 1)).reshape(nbs * FOX_W, past)
    o_f_s = _fox_sample(fq_s, feat_major(cache_fox_k[l]), feat_major(cache_fox_v[l]), fk_s, fv_s,
                        f_all[:, :past].reshape(nbs, H_F, past),
                        f_all[:, past:past + n_new].reshape(nbs, H_F, n_new), nbs, n_new, past)
    o_m_s = _memattn(mq_s, cache_mem_k[l].reshape(nbs * N_MEM, MEM_W),
                     cache_mem_v[l].reshape(nbs * N_MEM, MEM_W), nbs, n_new, n_new)
    x1_s, h2_s, comb_s = _merge(xs, o_r_s, o_f_s, o_m_s, gates_s, ts, False, *merge_w)
    y_sample = _moe(h2_s, comb_s, x1_s, ts, wg, wu, wd).reshape(nbs, n_new, D_MODEL)

    token_major = lambda a: jnp.transpose(a.reshape(nb, H_F, D_F, seq), (0, 3, 1, 2))

    return (y_prompt, y_sample,
            s_fin[None], token_major(fkt)[None], token_major(fvt)[None],
            jnp.swapaxes(lft.reshape(nb, H_F, seq), 1, 2)[None],
            mk.reshape(1, nb, N_MEM, H_M, D_M), mv.reshape(1, nb, N_MEM, H_M, D_M),
            s_new[None], fk_s.reshape(1, nbs, n_new, H_F, D_F), fv_s.reshape(1, nbs, n_new, H_F, D_F),
            lf_s.reshape(1, nbs, n_new, H_F))
```

```python
import functools

import jax
import jax.numpy as jnp
from jax import lax
from jax.experimental import pallas as pl
from jax.experimental.pallas import tpu as pltpu

F32 = jnp.float32
BF16 = jnp.bfloat16

D_MODEL = 1024
H_R, DK_R, DV_R = 4, 64, 128
H_F, D_F = 8, 64
H_M, D_M = 4, 128
N_MEM = 256
N_GROUPS, E_PER_GROUP, D_EXPERT = 4, 8, 256
ROPE_BASE = 10000.0
EPS = 1e-6
LOG2E = 1.4426950408889634

RQK_W = H_R * DK_R
RV_W = H_R * DV_R
FOX_W = H_F * D_F
MEM_W = H_M * D_M
GATE_W = 3 * D_MODEL
LANES = 128
MXU_DIM = 256

C_RQ = 0
C_RK = C_RQ + RQK_W
C_RV = C_RK + RQK_W
C_RG = C_RV + RV_W
C_FQ = C_RG + RV_W
C_FK = C_FQ + FOX_W
C_FV = C_FK + FOX_W
C_MQ = C_FV + FOX_W
C_GT = C_MQ + MEM_W
C_FF = C_GT + GATE_W

VMEM_LIMIT = 56 * 1024 * 1024


def _cparams(sem):
    return pltpu.CompilerParams(dimension_semantics=sem, vmem_limit_bytes=VMEM_LIMIT)


def _const_spec(shape):
    nd = len(shape)
    return pl.BlockSpec(shape, lambda *_: (0,) * nd, pipeline_mode=pl.Buffered(1))


def _dot(a, b):
    return jnp.dot(a, b, preferred_element_type=F32)


def _dot_nt(a, b):
    return lax.dot_general(a, b, (((1,), (1,)), ((), ())), preferred_element_type=F32)


def _dot_tn(a, b):
    return lax.dot_general(a, b, (((0,), (0,)), ((), ())), preferred_element_type=F32)


def _lane_rmsnorm(z):
    return z * lax.rsqrt(jnp.mean(z * z, axis=-1, keepdims=True) + EPS)


def _prep_kernel(x_ref, g1_ref, wa_ref, wb_ref, wf_ref, cos_ref, sin_ref, bf_ref, gfq_ref, gfk_ref, gmq_ref, bd_ref,
                 rq_ref, rk_ref, rv_ref, rg_ref, fq_ref, fk_ref, fv_ref, lf_ref, mq_ref, gt_ref, *extra,
                 seq_minor):
    x = x_ref[...]
    h = (x * lax.rsqrt(jnp.mean(x * x, axis=-1, keepdims=True) + EPS) * g1_ref[...]).astype(BF16)

    def proj(c0, width):
        if c0 < C_MQ:
            return _dot(h, wa_ref[:, c0:c0 + width])
        if c0 < C_FF:
            return _dot(h, wb_ref[:, c0 - C_MQ:c0 - C_MQ + width])
        return _dot(h, wf_ref[...])

    cos = cos_ref[...]
    sin = sin_ref[...]
    lane = lax.broadcasted_iota(jnp.int32, cos.shape, 1)
    first_half = (lane % DK_R) < (DK_R // 2)

    def rope(z):
        swapped = jnp.where(first_half, pltpu.roll(z, LANES - DK_R // 2, 1), pltpu.roll(z, DK_R // 2, 1))
        return z * cos + swapped * sin

    zq = proj(C_RQ, RQK_W)
    zk = proj(C_RK, RQK_W)
    for p in range(RQK_W // LANES):
        sl = slice(p * LANES, (p + 1) * LANES)
        rq_ref[:, sl] = rope(zq[:, sl])
        rk_ref[:, sl] = rope(zk[:, sl]) * (DK_R ** -0.5)
    rv_ref[...] = proj(C_RV, RV_W)
    rg_ref[...] = proj(C_RG, RV_W)

    def head64_norm(z, g_ref):
        zz = (z * z).astype(BF16)
        w = bd_ref.shape[0]
        ms = jnp.concatenate([_dot(zz[:, c:c + w], bd_ref[...]) for c in range(0, FOX_W, w)], axis=1)
        return z * lax.rsqrt(ms + EPS) * g_ref[...]

    fq = head64_norm(proj(C_FQ, FOX_W), gfq_ref) * (D_F ** -0.5 * LOG2E)
    fk = head64_norm(proj(C_FK, FOX_W), gfk_ref)
    fv = proj(C_FV, FOX_W)
    if seq_minor:
        fq_ref[...] = fq.T.astype(BF16)
        fk_ref[...] = fk.T
        fv_ref[...] = fv.T
        extra[0][...] = fk.astype(BF16)
    else:
        fq_ref[...] = fq.astype(BF16)
        fk_ref[...] = fk
        fv_ref[...] = fv

    zm = proj(C_MQ, MEM_W)
    for hh in range(H_M):
        sl = slice(hh * D_M, (hh + 1) * D_M)
        mq_ref[:, sl] = (_lane_rmsnorm(zm[:, sl]) * gmq_ref[...]).astype(BF16)

    for b in range(3):
        gt_ref[:, b * D_MODEL:(b + 1) * D_MODEL] = jax.nn.sigmoid(proj(C_GT + b * D_MODEL, D_MODEL)).astype(BF16)

    v = proj(C_FF, LANES) + bf_ref[...]
    lf = jnp.minimum(v, 0.0) - jnp.log1p(jnp.exp(-jnp.abs(v)))
    if seq_minor:
        lf_ref[...] = lf.T[:H_F, :]
    else:
        lf_ref[...] = lf[:, :H_F]


def _prep(x2d, tables, tm, n_pos_tiles, seq_minor, g1, w_in_r, bf_pad, gfq, gfk, gmq, bd):
    t = x2d.shape[0]
    cos_t, sin_t = tables
    row = lambda w: pl.BlockSpec((tm, w), lambda i: (i, 0))
    pos = pl.BlockSpec((tm, LANES), lambda i: (i % n_pos_tiles, 0))
    sds = jax.ShapeDtypeStruct
    if seq_minor:
        nb, seq = t // (n_pos_tiles * tm), n_pos_tiles * tm
        fm = lambda rows_: pl.BlockSpec((rows_, tm), lambda i: (i // n_pos_tiles, i % n_pos_tiles))
        fox_shapes = [sds((nb * FOX_W, seq), BF16), sds((nb * FOX_W, seq), F32), sds((nb * FOX_W, seq), F32),
                      sds((nb * H_F, seq), F32)]
        fox_specs = [fm(FOX_W), fm(FOX_W), fm(FOX_W), fm(H_F)]
        extra_shapes, extra_specs = [sds((t, FOX_W), BF16)], [row(FOX_W)]
    else:
        fox_shapes = [sds((t, FOX_W), BF16), sds((t, FOX_W), F32), sds((t, FOX_W), F32), sds((t, H_F), F32)]
        fox_specs = [row(FOX_W), row(FOX_W), row(FOX_W), row(H_F)]
        extra_shapes, extra_specs = [], []
    out_shapes = [sds((t, RQK_W), F32), sds((t, RQK_W), F32), sds((t, RV_W), F32), sds((t, RV_W), F32),
                  *fox_shapes, sds((t, MEM_W), BF16), sds((t, GATE_W), BF16), *extra_shapes]
    out_specs = [row(RQK_W), row(RQK_W), row(RV_W), row(RV_W), *fox_specs, row(MEM_W), row(GATE_W),
                 *extra_specs]
    return pl.pallas_call(
        functools.partial(_prep_kernel, seq_minor=seq_minor),
        grid=(t // tm,),
        in_specs=[row(D_MODEL), _const_spec((1, D_MODEL)), _const_spec((D_MODEL, C_MQ)),
                  _const_spec((D_MODEL, C_FF - C_MQ)), _const_spec((D_MODEL, LANES)), pos, pos,
                  _const_spec((1, LANES)), _const_spec((1, FOX_W)), _const_spec((1, FOX_W)),
                  _const_spec((1, D_M)), _const_spec((MXU_DIM, MXU_DIM))],
        out_specs=out_specs,
        out_shape=out_shapes,
        compiler_params=_cparams(("parallel",)),
        name="prep",
    )(x2d, g1, *w_in_r, cos_t, sin_t, bf_pad, gfq, gfk, gmq, bd)


def _ret_kernel(*refs, has_init, ch):
    if has_init:
        (rq_ref, rk_ref, rv_ref, rg_ref, dmat_ref, qdec_ref, kdec_ref, gpow_ref, gro_ref, s0_ref,
         o_ref, sfin_ref) = refs
    else:
        (rq_ref, rk_ref, rv_ref, rg_ref, dmat_ref, qdec_ref, kdec_ref, gpow_ref, gro_ref,
         o_ref, sfin_ref) = refs

    state = []
    for h in range(H_R):
        if has_init:
            pad = jnp.zeros((DK_R, DV_R), F32)
            state.append(jnp.concatenate([s0_ref[0, h], pad] if h % 2 == 0 else [pad, s0_ref[0, h]], axis=0))
        else:
            state.append(jnp.zeros((LANES, DV_R), F32))

    lane = lax.broadcasted_iota(jnp.int32, (ch, LANES), 1)
    for c in range(rq_ref.shape[0] // ch):
        rows = slice(c * ch, (c + 1) * ch)
        for p in range(H_R // 2):
            sl = slice(p * LANES, (p + 1) * LANES)
            q2 = rq_ref[rows, sl]
            k2 = rk_ref[rows, sl]
            kd2 = k2 * kdec_ref[p]
            for hh in range(2):
                h = 2 * p + hh
                mine = (lane >= hh * DK_R) & (lane < (hh + 1) * DK_R)
                qm = jnp.where(mine, q2, 0.0).astype(BF16)
                kdm = jnp.where(mine, kd2, 0.0).astype(BF16)
                v = rv_ref[rows, h * DV_R:(h + 1) * DV_R].astype(BF16)
                sc = _dot_nt(qm, k2.astype(BF16)) * dmat_ref[h]
                o = _dot(sc.astype(BF16), v) + _dot(qm, state[h].astype(BF16)) * qdec_ref[h]
                state[h] = gpow_ref[h] * state[h] + _dot_tn(kdm, v)
                normed = _lane_rmsnorm(o) * gro_ref[...]
                rg = rg_ref[rows, h * DV_R:(h + 1) * DV_R]
                o_ref[rows, h * DV_R:(h + 1) * DV_R] = (normed * (rg * jax.nn.sigmoid(rg))).astype(BF16)

    for h in range(H_R):
        r0 = DK_R * (h % 2)
        sfin_ref[0, h] = state[h][r0:r0 + DK_R, :]


def _retention(rq, rk, rv, rg, n_batch, seq, ch, g_ret_out, state0):
    lg = jnp.log1p(-jnp.exp2(-5.0 - jnp.arange(H_R, dtype=F32)))
    idx = jnp.arange(ch, dtype=F32)
    diff = idx[:, None] - idx[None, :]
    causal = diff >= 0
    dmat = jnp.where(causal[None], jnp.exp(jnp.where(causal, diff, 0.0)[None] * lg[:, None, None]), 0.0)
    q_dec = jnp.exp((idx + 1.0)[None, :] * lg[:, None])
    k_dec = jnp.exp((ch - 1.0 - idx)[None, :] * lg[:, None])
    qdec = jnp.broadcast_to(q_dec[:, :, None], (H_R, ch, DV_R))
    kdec = jnp.broadcast_to(k_dec[:, :, None], (H_R, ch, DK_R))
    kdec = kdec.reshape(H_R // 2, 2, ch, DK_R).transpose(0, 2, 1, 3).reshape(H_R // 2, ch, LANES)
    gpow = jnp.broadcast_to(jnp.exp(ch * lg)[:, None, None], (H_R, 1, DV_R))

    has_init = state0 is not None
    blk = lambda w: pl.BlockSpec((seq, w), lambda b: (b, 0))
    in_specs = [blk(RQK_W), blk(RQK_W), blk(RV_W), blk(RV_W),
                _const_spec((H_R, ch, ch)), _const_spec((H_R, ch, DV_R)),
                _const_spec((H_R // 2, ch, LANES)), _const_spec((H_R, 1, DV_R)), _const_spec((1, DV_R))]
    args = [rq, rk, rv, rg, dmat, qdec, kdec, gpow, g_ret_out]
    if has_init:
        in_specs.append(pl.BlockSpec((1, H_R, DK_R, DV_R), lambda b: (b, 0, 0, 0)))
        args.append(state0)
    return pl.pallas_call(
        functools.partial(_ret_kernel, has_init=has_init, ch=ch),
        grid=(n_batch,),
        in_specs=in_specs,
        out_specs=[blk(RV_W), pl.BlockSpec((1, H_R, DK_R, DV_R), lambda b: (b, 0, 0, 0))],
        out_shape=[jax.ShapeDtypeStruct((n_batch * seq, RV_W), BF16),
                   jax.ShapeDtypeStruct((n_batch, H_R, DK_R, DV_R), F32)],
        compiler_params=_cparams(("parallel",)),
        name="retention",
    )(*args)


def _split3(x):
    hi = x.astype(BF16)
    r1 = x - hi.astype(F32)
    mid = r1.astype(BF16)
    lo = (r1 - mid.astype(F32)).astype(BF16)
    return hi, mid, lo


def _cumsum_kernel(x_ref, o_ref, *, blk):
    rows, n = x_ref.shape
    r = lax.broadcasted_iota(jnp.int32, (blk, blk), 0)
    c = lax.broadcasted_iota(jnp.int32, (blk, blk), 1)
    tri = jnp.where(r <= c, 1.0, 0.0).astype(BF16)
    carry = jnp.zeros((rows, 1), F32)
    for i in range(n // blk):
        hi, mid, lo = _split3(x_ref[:, i * blk:(i + 1) * blk])
        cum = (_dot(hi, tri) + _dot(mid, tri)) + _dot(lo, tri) + carry
        o_ref[:, i * blk:(i + 1) * blk] = cum
        carry = cum[:, blk - 1:blk]


def _cumsum(x, blk):
    r, t = x.shape
    spec = pl.BlockSpec((H_F, t), lambda b: (b, 0))
    return pl.pallas_call(
        functools.partial(_cumsum_kernel, blk=blk),
        grid=(r // H_F,),
        in_specs=[spec], out_specs=spec,
        out_shape=jax.ShapeDtypeStruct(x.shape, F32),
        compiler_params=_cparams(("parallel",)),
        name="cumsum",
    )(x)


def _head_rows_mask(shape, hh):
    sub = lax.broadcasted_iota(jnp.int32, shape, 0)
    return (sub >= hh * D_F) & (sub < (hh + 1) * D_F)


N_AUG = 3
V_ROWS = D_F + 16


def _fox_kernel(qt_ref, k_ref, vt_ref, fcum_ref, o_ref, qa, ka, vb, *, tq):
    seq = k_ref.shape[0]
    tk = tq
    k = k_ref[...].astype(F32)
    lane = lax.broadcasted_iota(jnp.int32, k.shape, 1)
    aug_rows = lax.broadcasted_iota(jnp.int32, (D_F, seq), 0) < N_AUG
    for hh in range(2):
        own, oth = hh * D_F, (1 - hh) * D_F
        qa[hh, own:own + D_F, :] = qt_ref[own:own + D_F, :]
        qa[hh, oth:oth + D_F, :] = jnp.where(aug_rows, 1.0, 0.0).astype(BF16)
        pieces = [p.astype(F32) for p in _split3(fcum_ref[0, hh:hh + 1, :] * (-LOG2E))]
        ka[hh] = jnp.where((lane >= own) & (lane < own + D_F), k, _rows_to_columns(pieces, oth)).astype(BF16)
    for hh in range(2):
        vb[hh, :D_F, :] = vt_ref[hh * D_F:(hh + 1) * D_F, :].astype(BF16)
        vb[hh, D_F:, :] = jnp.ones((V_ROWS - D_F, seq), BF16)
    causal = (lax.broadcasted_iota(jnp.int32, (tk, tq), 1) >= lax.broadcasted_iota(jnp.int32, (tk, tq), 0))

    pairs = [(i, j) for i in range(seq // tq) for j in range(i + 1)]

    def stage_a(i, j):
        return tuple(_dot(ka[hh, j * tk:(j + 1) * tk, :], qa[hh, :, i * tq:(i + 1) * tq]) for hh in range(2))

    def stage_b(i, j, ts, maxes):
        new_maxes, probs = [], []
        for hh in range(2):
            t = jnp.where(causal, ts[hh], -jnp.inf) if j == i else ts[hh]
            t_max = jnp.max(t, axis=0, keepdims=True)
            if j == 0:
                m_new, alpha = t_max, None
            else:
                m_new = jnp.maximum(maxes[hh], t_max)
                alpha = jnp.exp2(maxes[hh] - m_new)
            new_maxes.append(m_new)
            probs.append((jnp.exp2(t - m_new).astype(BF16), alpha))
        return new_maxes, probs

    def stage_c(i, j, probs, accs):
        out = []
        for hh in range(2):
            p, alpha = probs[hh]
            pv = _dot(vb[hh, :, j * tk:(j + 1) * tk], p)
            out.append(pv if j == 0 else alpha * accs[hh] + pv)
        return out

    scores, probs, maxes, accs = {}, {}, None, None
    for s in range(len(pairs) + 2):
        if s < len(pairs):
            scores[s] = stage_a(*pairs[s])
        if 0 <= s - 1 < len(pairs):
            maxes, probs[s - 1] = stage_b(*pairs[s - 1], scores.pop(s - 1), maxes)
        if 0 <= s - 2 < len(pairs):
            i, j = pairs[s - 2]
            accs = stage_c(i, j, probs.pop(s - 2), accs)
            if j == i:
                o_t = jnp.concatenate([a[:D_F] / a[D_F:D_F + 1] for a in accs], axis=0)
                o_ref[i * tq:(i + 1) * tq, :] = o_t.T.astype(BF16)


def _fox_prompt(fqt, fkb, fvt, fcum, n_batch, seq, tq):
    npair = H_F // 2
    fm_spec = pl.BlockSpec((LANES, seq), lambda b, p: (b * npair + p, 0))
    tok_spec = pl.BlockSpec((seq, LANES), lambda b, p: (b, p))
    return pl.pallas_call(
        functools.partial(_fox_kernel, tq=tq),
        grid=(n_batch, npair),
        in_specs=[fm_spec, tok_spec, fm_spec, pl.BlockSpec((1, 2, seq), lambda b, p: (b * npair + p, 0, 0))],
        out_specs=tok_spec,
        out_shape=jax.ShapeDtypeStruct((n_batch * seq, FOX_W), BF16),
        scratch_shapes=[pltpu.VMEM((2, LANES, seq), BF16), pltpu.VMEM((2, seq, LANES), BF16),
                        pltpu.VMEM((2, V_ROWS, seq), BF16)],
        compiler_params=_cparams(("parallel", "parallel")),
        name="fox_prompt",
    )(fqt, fkb, fvt, fcum)


def _fox_sample_kernel(q_ref, kct_ref, vct_ref, kn_ref, vn_ref, frc_ref, frn_ref, o_ref):
    n = q_ref.shape[0]
    lane = lax.broadcasted_iota(jnp.int32, (n, LANES), 1)
    rows = lax.broadcasted_iota(jnp.int32, (n, n), 0)
    cols = lax.broadcasted_iota(jnp.int32, (n, n), 1)
    for pair in range(H_F // 2):
        sl = slice(pair * LANES, (pair + 1) * LANES)
        q2 = q_ref[:, sl]
        kct = kct_ref[sl, :]
        vct = vct_ref[sl, :].astype(BF16)
        kn = kn_ref[:, sl].astype(BF16)
        vn = vn_ref[:, sl].astype(BF16)
        outs = []
        for hh in range(2):
            h = 2 * pair + hh
            mine = (lane >= hh * D_F) & (lane < (hh + 1) * D_F)
            kc = jnp.where(_head_rows_mask(kct.shape, hh), kct, 0.0).astype(BF16)
            t_c = _dot(q2, kc) - frc_ref[0, h:h + 1, :] * LOG2E
            t_n = _dot_nt(jnp.where(mine, q2, jnp.zeros_like(q2)), kn) - frn_ref[0, h:h + 1, :] * LOG2E
            t_n = jnp.where(rows >= cols, t_n, -jnp.inf)
            m = jnp.maximum(jnp.max(t_c, axis=-1, keepdims=True), jnp.max(t_n, axis=-1, keepdims=True))
            p_c = jnp.exp2(t_c - m)
            p_n = jnp.exp2(t_n - m)
            l = jnp.sum(p_c, axis=-1, keepdims=True) + jnp.sum(p_n, axis=-1, keepdims=True)
            acc = _dot_nt(p_c.astype(BF16), vct) + _dot(p_n.astype(BF16), vn)
            outs.append(acc / l)
        o_ref[:, sl] = jnp.where(lane < D_F, outs[0], outs[1]).astype(BF16)


def _fox_sample(fq, kct, vct, kn, vn, frow_c, frow_n, n_batch, n_new, past):
    tok = pl.BlockSpec((n_new, FOX_W), lambda b: (b, 0))
    cache = pl.BlockSpec((FOX_W, past), lambda b: (b, 0))
    return pl.pallas_call(
        _fox_sample_kernel,
        grid=(n_batch,),
        in_specs=[tok, cache, cache, tok, tok,
                  pl.BlockSpec((1, H_F, past), lambda b: (b, 0, 0)),
                  pl.BlockSpec((1, H_F, n_new), lambda b: (b, 0, 0))],
        out_specs=tok,
        out_shape=jax.ShapeDtypeStruct((n_batch * n_new, FOX_W), BF16),
        compiler_params=_cparams(("parallel",)),
        name="fox_sample",
    )(fq, kct, vct, kn, vn, frow_c, frow_n)


def _memkv_kernel(m_ref, gin_ref, w_ref, gk_ref, k_ref, v_ref):
    x = m_ref[...]
    h = (x * lax.rsqrt(jnp.mean(x * x, axis=-1, keepdims=True) + EPS) * gin_ref[...]).astype(BF16)
    zk = _dot(h, w_ref[:, :MEM_W])
    for hh in range(H_M):
        sl = slice(hh * D_M, (hh + 1) * D_M)
        k_ref[:, sl] = _lane_rmsnorm(zk[:, sl]) * gk_ref[...]
    v_ref[...] = _dot(h, w_ref[:, MEM_W:])


def _memkv(mem2d, n_batch, g_mem_in, w_mem_kv, g_mem_k):
    blk = lambda w: pl.BlockSpec((N_MEM, w), lambda b: (b, 0))
    return pl.pallas_call(
        _memkv_kernel,
        grid=(n_batch,),
        in_specs=[blk(D_MODEL), _const_spec((1, D_MODEL)), _const_spec((D_MODEL, 2 * MEM_W)),
                  _const_spec((1, D_M))],
        out_specs=[blk(MEM_W), blk(MEM_W)],
        out_shape=[jax.ShapeDtypeStruct((n_batch * N_MEM, MEM_W), F32)] * 2,
        compiler_params=_cparams(("parallel",)),
        name="memkv",
    )(mem2d, g_mem_in, w_mem_kv, g_mem_k)


def _mem_attend(q_ref, k_ref, v_ref):
    for hh in range(H_M):
        sl = slice(hh * D_M, (hh + 1) * D_M)
        s = _dot_nt(q_ref[:, sl], k_ref[:, sl].astype(BF16)) * (D_M ** -0.5)
        m = jnp.max(s, axis=-1, keepdims=True)
        p = jnp.exp(s - m)
        l = jnp.sum(p, axis=-1, keepdims=True)
        yield sl, (_dot(p.astype(BF16), v_ref[:, sl].astype(BF16)) / l).astype(BF16)


def _memattn_kernel(q_ref, k_ref, v_ref, o_ref):
    for sl, o in _mem_attend(q_ref, k_ref, v_ref):
        o_ref[:, sl] = o


def _memattn(mq, mk, mv, n_batch, seq, tq):
    nq = seq // tq
    return pl.pallas_call(
        _memattn_kernel,
        grid=(n_batch, nq),
        in_specs=[pl.BlockSpec((tq, MEM_W), lambda b, i: (b * nq + i, 0)),
                  pl.BlockSpec((N_MEM, MEM_W), lambda b, i: (b, 0)),
                  pl.BlockSpec((N_MEM, MEM_W), lambda b, i: (b, 0))],
        out_specs=pl.BlockSpec((tq, MEM_W), lambda b, i: (b * nq + i, 0)),
        out_shape=jax.ShapeDtypeStruct((n_batch * seq, MEM_W), BF16),
        compiler_params=_cparams(("parallel", "arbitrary")),
        name="memattn",
    )(mq, mk, mv)


def _split2(x):
    hi = x.astype(BF16)
    return hi, (x - hi.astype(F32)).astype(BF16)


N_EXPERTS = N_GROUPS * E_PER_GROUP


N_PAIRS = E_PER_GROUP * (E_PER_GROUP - 1) // 2


def _router_logits(h2, wrt_ref, brt_ref):
    h_hi, h_lo = _split2(h2)
    z = _dot(h_hi, wrt_ref[...])
    return (z[:, :LANES] + z[:, LANES:]) + _dot(h_lo, wrt_ref[:, :LANES]) + brt_ref[...]


N_LOGIT_ROWS = 40


def _route(logits, group=None):
    lt = logits.T[:N_LOGIT_ROWS, :]
    idx = lax.broadcasted_iota(jnp.int32, lt.shape, 0).astype(F32)
    neg = -jnp.inf
    first_idx = lambda mask: jnp.min(jnp.where(mask, idx, float(LANES)), axis=0, keepdims=True)

    is_g = (idx >= N_EXPERTS) & (idx < N_EXPERTS + N_GROUPS)
    lg = jnp.where(is_g, lt, neg)
    mg = jnp.max(lg, axis=0, keepdims=True)
    if group is None:
        group = first_idx(lg == mg) - N_EXPERTS
        p_sel = 1.0 / jnp.sum(jnp.exp(lg - mg), axis=0, keepdims=True)
    else:
        lsel = jnp.max(jnp.where(idx == group + N_EXPERTS, lt, neg), axis=0, keepdims=True)
        p_sel = jnp.exp(lsel - mg) / jnp.sum(jnp.exp(lg - mg), axis=0, keepdims=True)

    in_grp = (idx >= group * E_PER_GROUP) & (idx < (group + 1) * E_PER_GROUP)
    le = jnp.where(in_grp, lt, neg)
    v1 = jnp.max(le, axis=0, keepdims=True)
    i1 = first_idx(le == v1)
    le2 = jnp.where(idx == i1, neg, le)
    v2 = jnp.max(le2, axis=0, keepdims=True)
    i2 = first_idx(le2 == v2)
    e2 = jnp.exp(v2 - v1)
    return group, i1, i2, p_sel / (1.0 + e2), p_sel * e2 / (1.0 + e2)


def _rows_to_columns(rows, first=0):
    n = rows[0].shape[1]
    idx = lax.broadcasted_iota(jnp.int32, (LANES, n), 0)
    stacked = jnp.zeros((LANES, n), F32)
    for r, row in enumerate(rows):
        stacked = jnp.where(idx == first + r, row, stacked)
    return stacked.T


def _merge_kernel(*refs, sorted_moe):
    if sorted_moe:
        (x_ref, or_ref, of_ref, mq_ref, mk_ref, mv_ref, gt_ref, wr_ref, wf_ref, wm_ref, wo_ref, g2_ref,
         wrt_ref, brt_ref, x1_ref, h2_ref, route_ref, *counts_ref) = refs
        o_m = jnp.concatenate([o for _, o in _mem_attend(mq_ref, mk_ref, mv_ref)], axis=1)
    else:
        (x_ref, or_ref, of_ref, om_ref, gt_ref, wr_ref, wf_ref, wm_ref, wo_ref, g2_ref,
         wrt_ref, brt_ref, x1_ref, h2_ref, route_ref) = refs
        o_m = om_ref[...]
    g = lambda b: gt_ref[:, b * D_MODEL:(b + 1) * D_MODEL].astype(F32)
    merged = (g(0) * _dot(or_ref[...], wr_ref[...]) + g(1) * _dot(of_ref[...], wf_ref[...])
              + g(2) * _dot(o_m, wm_ref[...]))
    x1 = x_ref[...] + _dot(merged.astype(BF16), wo_ref[...])
    x1_ref[...] = x1
    h2 = x1 * lax.rsqrt(jnp.mean(x1 * x1, axis=-1, keepdims=True) + EPS) * g2_ref[...]
    group, i1, i2, w1, w2 = _route(_router_logits(h2, wrt_ref, brt_ref))
    tm = h2.shape[0]
    if sorted_moe:
        _to_token_tiles(h2_ref, h2)
        e_lo = jnp.minimum(i1, i2) - group * E_PER_GROUP
        e_hi = jnp.maximum(i1, i2) - group * E_PER_GROUP
        cls = group * N_PAIRS + (e_lo * E_PER_GROUP - e_lo * (e_lo + 1.0) * 0.5 + (e_hi - e_lo - 1.0))
        cidx = lax.broadcasted_iota(jnp.int32, (LANES, tm), 0).astype(F32)
        onehot = jnp.where(cidx == cls, 1.0, 0.0)
        r = lax.broadcasted_iota(jnp.int32, (tm, tm), 0)
        c = lax.broadcasted_iota(jnp.int32, (tm, tm), 1)
        before = _dot(onehot.astype(BF16), jnp.where(r < c, 1.0, 0.0).astype(BF16))
        rank = jnp.sum(before * onehot, axis=0, keepdims=True)
        ridx = lax.broadcasted_iota(jnp.int32, route_ref.shape, 0)
        route_ref[...] = jnp.where(ridx == 0, group, jnp.where(ridx == 1, cls, jnp.where(ridx == 2, rank, 0.0)))
        counts_ref[0][0] = jnp.broadcast_to(jnp.sum(onehot, axis=1, keepdims=True), (LANES, LANES))
    else:
        h2_ref[...] = h2.astype(BF16)
        cols = _rows_to_columns([i1, i2, w1, w2])
        lane = lax.broadcasted_iota(jnp.int32, (tm, LANES), 1).astype(F32)
        comb = jnp.where(lane == cols[:, 0:1], cols[:, 2:3], 0.0) + jnp.where(lane == cols[:, 1:2], cols[:, 3:4], 0.0)
        for gi in range(N_GROUPS):
            route_ref[gi] = comb[:, gi * E_PER_GROUP:(gi + 1) * E_PER_GROUP]


def _merge(x2d, o_r, o_f, mem, gates, tm, sorted_moe, wr, wf, wm, wo, g2, wrt, brt):
    t = x2d.shape[0]
    row = lambda w: pl.BlockSpec((tm, w), lambda i: (i, 0))
    sds = jax.ShapeDtypeStruct
    if sorted_moe:
        h2_spec, h2_shape = pl.BlockSpec((tm * N_SLABS, LANES), lambda i: (i, 0)), sds((t * N_SLABS, LANES), F32)
        rt_specs = [pl.BlockSpec((SUBLANES, tm), lambda i: (i, 0)),
                    pl.BlockSpec((1, LANES, LANES), lambda i: (i, 0, 0))]
        rt_shapes = [sds((t // tm * SUBLANES, tm), F32), sds((t // tm, LANES, LANES), F32)]
        mq, mk, mv, tiles_per_seq = mem
        mem_args = [mq, mk, mv]
        mem_kv = pl.BlockSpec((N_MEM, MEM_W), lambda i: (i // tiles_per_seq, 0))
        mem_specs = [row(MEM_W), mem_kv, mem_kv]
    else:
        h2_spec, h2_shape = row(D_MODEL), sds((t, D_MODEL), BF16)
        rt_specs = [pl.BlockSpec((N_GROUPS, tm, E_PER_GROUP), lambda i: (0, i, 0))]
        rt_shapes = [sds((N_GROUPS, t, E_PER_GROUP), F32)]
        mem_args, mem_specs = [mem], [row(MEM_W)]
    return pl.pallas_call(
        functools.partial(_merge_kernel, sorted_moe=sorted_moe),
        grid=(t // tm,),
        in_specs=[row(D_MODEL), row(RV_W), row(FOX_W), *mem_specs, row(GATE_W),
                  _const_spec((RV_W, D_MODEL)), _const_spec((FOX_W, D_MODEL)), _const_spec((MEM_W, D_MODEL)),
                  _const_spec((D_MODEL, D_MODEL)), _const_spec((1, D_MODEL)),
                  _const_spec((D_MODEL, 2 * LANES)), _const_spec((1, LANES))],
        out_specs=[row(D_MODEL), h2_spec, *rt_specs],
        out_shape=[sds((t, D_MODEL), F32), h2_shape, *rt_shapes],
        compiler_params=_cparams(("parallel",)),
        name="merge",
    )(x2d, o_r, o_f, *mem_args, gates, wr, wf, wm, wo, g2, wrt, brt)


def _positions_kernel(route_ref, first_ref, pos_ref):
    tm = route_ref.shape[1]
    cidx = lax.broadcasted_iota(jnp.int32, (LANES, tm), 0).astype(F32)
    for i in range(first_ref.shape[0]):
        route = route_ref[i * SUBLANES:(i + 1) * SUBLANES, :]
        first = jnp.concatenate([first_ref[i]] * (tm // LANES), axis=1)
        start = jnp.sum(jnp.where(cidx == route[1:2, :], first, 0.0), axis=0, keepdims=True)
        pos_ref[i * SUBLANES:(i + 1) * SUBLANES, :] = jnp.broadcast_to(start + route[2:3, :],
                                                                       route.shape).astype(jnp.int32)


def _positions(route, first_row, tiles_per_step):
    rows, tm = route.shape
    n = rows // SUBLANES
    out = pl.pallas_call(
        _positions_kernel,
        grid=(n // tiles_per_step,),
        in_specs=[pl.BlockSpec((tiles_per_step * SUBLANES, tm), lambda i: (i, 0)),
                  pl.BlockSpec((tiles_per_step, LANES, LANES), lambda i: (i, 0, 0))],
        out_specs=pl.BlockSpec((tiles_per_step * SUBLANES, tm), lambda i: (i, 0)),
        out_shape=jax.ShapeDtypeStruct((rows, tm), jnp.int32),
        compiler_params=_cparams(("parallel",)),
        name="positions",
    )(route, first_row)
    return out.reshape(n, SUBLANES, tm)[:, 0, :].reshape(n * tm)


def _group_experts(h, cw, wg_ref, wu_ref, wd_ref, act):
    for e in range(E_PER_GROUP):
        a = _dot(h, wg_ref[0, e])
        u = _dot(h, wu_ref[0, e])
        act[:, e * D_EXPERT:(e + 1) * D_EXPERT] = ((a * jax.nn.sigmoid(a)) * u * cw(e)).astype(BF16)
    return _dot(act[...], wd_ref[0])


N_SLABS = D_MODEL // LANES
SUBLANES = 8


def _to_token_tiles(ref, x):
    for s in range(N_SLABS):
        ref[pl.ds(s, x.shape[0], stride=N_SLABS), :] = x[:, s * LANES:(s + 1) * LANES]


def _landing_shape(rows):
    return (rows // SUBLANES, N_SLABS, SUBLANES, LANES)


def _from_landing(buf):
    rows = buf.shape[0] * SUBLANES
    return jnp.concatenate([buf[:, s].reshape(rows, LANES) for s in range(N_SLABS)], axis=1)


def _row_gather(idx_ref, base, src_hbm, dst, sem, part=None):
    def body(i, _):
        for u in range(SUBLANES):
            pltpu.make_async_copy(src_hbm.at[idx_ref[base + i * SUBLANES + u]], dst.at[i, :, u, :],
                                  sem).start(priority=u % 2)
        return 0
    if part is None:
        lax.fori_loop(0, dst.shape[0], body, 0)
    else:
        j, n = part
        per = dst.shape[0] // n
        for i in range(j * per, (j + 1) * per):
            body(i, 0)


def _row_gather_wait(dst, sem):
    pltpu.make_async_copy(dst, dst, sem).wait()


def _moe_sorted_kernel(tg_ref, nvt_ref, pos_ref, h3_hbm, zeros_hbm, wrt_ref, brt_ref,
                       wg_ref, wu_ref, wd_ref, y_ref, xbuf, sem, hb, yacc, src_ref, clear_sem, *, tm):
    k = pl.program_id(0)
    nvt = nvt_ref[0]
    slot = lax.rem(k, 2)

    @pl.when(k == 0)
    def _():
        clear = pltpu.make_async_copy(zeros_hbm, src_ref, clear_sem)
        clear.start()
        clear.wait()

        def invert(t, _):
            src_ref[pos_ref[t]] = t
            return 0

        lax.fori_loop(0, pos_ref.shape[0], invert, 0, unroll=32)

    @pl.when((k == 0) & (nvt > 0))
    def _():
        _row_gather(src_ref, 0, h3_hbm, xbuf.at[0], sem.at[0])

    @pl.when(k < nvt)
    def _():
        _row_gather_wait(xbuf.at[slot], sem.at[slot])

        def fetch_part(e):
            _row_gather(src_ref, jnp.minimum(k + 1, nvt - 1) * tm, h3_hbm, xbuf.at[1 - slot],
                        sem.at[1 - slot], part=(e, E_PER_GROUP))

        x = _from_landing(xbuf.at[slot])
        group = tg_ref[k]
        _, i1, i2, w1, w2 = _route(_router_logits(x, wrt_ref, brt_ref), group.astype(F32))
        hb[...] = x.astype(BF16)
        yacc[...] = jnp.zeros(yacc.shape, F32)
        first = (group * E_PER_GROUP).astype(F32)
        used = [jnp.max(jnp.where(i1 == first + e, w1, 0.0) + jnp.where(i2 == first + e, w2, 0.0))
                for e in range(E_PER_GROUP)]
        cols = _rows_to_columns([i1, i2, w1, w2])
        for e in range(E_PER_GROUP):
            is_used = used[e] > 0.0

            @pl.when(jnp.logical_not(is_used))
            def _(e=e):
                fetch_part(e)

            @pl.when(is_used)
            def _(e=e):
                fetch_part(e)
                cw = (jnp.where(cols[:, 0:1] == first + e, cols[:, 2:3], 0.0)
                      + jnp.where(cols[:, 1:2] == first + e, cols[:, 3:4], 0.0))
                h = hb[...]
                a = _dot(h, wg_ref[0, e])
                u = _dot(h, wu_ref[0, e])
                act = ((a * jax.nn.sigmoid(a)) * u * cw).astype(BF16)
                yacc[...] += _dot(act, wd_ref[0, e * D_EXPERT:(e + 1) * D_EXPERT, :])

        _to_token_tiles(y_ref, yacc[...])

        @pl.when(k + 1 >= nvt)
        def _():
            _row_gather_wait(xbuf.at[1 - slot], sem.at[1 - slot])

    @pl.when(k >= nvt)
    def _():
        y_ref[...] = jnp.zeros(y_ref.shape, F32)


def _moe_sorted(h3, tile_group, n_valid_tiles, pos, n_tiles, tm, wrt, brt, wg, wu, wd):
    wspec = lambda shape: pl.BlockSpec(shape, lambda k, tg, nv, ps: (tg[k],) + (0,) * (len(shape) - 1))
    cspec = lambda shape: pl.BlockSpec(shape, lambda k, tg, nv, ps: (0,) * len(shape),
                                       pipeline_mode=pl.Buffered(1))
    return pl.pallas_call(
        functools.partial(_moe_sorted_kernel, tm=tm),
        grid_spec=pltpu.PrefetchScalarGridSpec(
            num_scalar_prefetch=3,
            grid=(n_tiles,),
            in_specs=[pl.BlockSpec(memory_space=pl.ANY), pl.BlockSpec(memory_space=pl.ANY),
                      cspec((D_MODEL, 2 * LANES)), cspec((1, LANES)),
                      wspec((1, E_PER_GROUP, D_MODEL, D_EXPERT)), wspec((1, E_PER_GROUP, D_MODEL, D_EXPERT)),
                      wspec((1, E_PER_GROUP * D_EXPERT, D_MODEL))],
            out_specs=pl.BlockSpec((tm * N_SLABS, LANES), lambda k, tg, nv, ps: (k, 0)),
            scratch_shapes=[pltpu.VMEM((2,) + _landing_shape(tm), F32), pltpu.SemaphoreType.DMA((2,)),
                            pltpu.VMEM((tm, D_MODEL), BF16), pltpu.VMEM((tm, D_MODEL), F32),
                            pltpu.SMEM((n_tiles * tm,), jnp.int32), pltpu.SemaphoreType.DMA(())]),
        out_shape=jax.ShapeDtypeStruct((n_tiles * tm * N_SLABS, LANES), F32),
        compiler_params=_cparams(("arbitrary",)),
        name="moe_sorted",
    )(tile_group, n_valid_tiles, pos, h3, jnp.zeros((n_tiles * tm,), jnp.int32), wrt, brt, wg, wu, wd)


def _combine_kernel(pos_ref, y3_hbm, x1_ref, o_ref, ybuf, sem, *, tm):
    k = pl.program_id(0)
    slot = lax.rem(k, 2)

    @pl.when(k == 0)
    def _():
        _row_gather(pos_ref, 0, y3_hbm, ybuf.at[0], sem.at[0])

    _row_gather_wait(ybuf.at[slot], sem.at[slot])

    @pl.when(k + 1 < pl.num_programs(0))
    def _():
        _row_gather(pos_ref, (k + 1) * tm, y3_hbm, ybuf.at[1 - slot], sem.at[1 - slot])

    o_ref[...] = x1_ref[...] + _from_landing(ybuf.at[slot])


def _combine(y3, pos, x1, tm):
    t = x1.shape[0]
    return pl.pallas_call(
        functools.partial(_combine_kernel, tm=tm),
        grid_spec=pltpu.PrefetchScalarGridSpec(
            num_scalar_prefetch=1,
            grid=(t // tm,),
            in_specs=[pl.BlockSpec(memory_space=pl.ANY), pl.BlockSpec((tm, D_MODEL), lambda k, ps: (k, 0))],
            out_specs=pl.BlockSpec((tm, D_MODEL), lambda k, ps: (k, 0)),
            scratch_shapes=[pltpu.VMEM((2,) + _landing_shape(tm), F32), pltpu.SemaphoreType.DMA((2,))]),
        out_shape=jax.ShapeDtypeStruct((t, D_MODEL), F32),
        compiler_params=_cparams(("arbitrary",)),
        name="combine",
    )(pos, y3, x1)


def _moe_kernel(h_ref, comb_ref, x1_ref, wg_ref, wu_ref, wd_ref, o_ref, act):
    g = pl.program_id(1)
    comb = comb_ref[0]
    y = _group_experts(h_ref[...], lambda e: comb[:, e:e + 1], wg_ref, wu_ref, wd_ref, act)

    @pl.when(g == 0)
    def _():
        o_ref[...] = x1_ref[...] + y

    @pl.when(g != 0)
    def _():
        o_ref[...] += y


def _moe(h2, comb, x1, tm, wg, wu, wd):
    t = h2.shape[0]
    return pl.pallas_call(
        _moe_kernel,
        grid=(t // tm, N_GROUPS),
        in_specs=[pl.BlockSpec((tm, D_MODEL), lambda i, g: (i, 0)),
                  pl.BlockSpec((1, tm, E_PER_GROUP), lambda i, g: (g, i, 0)),
                  pl.BlockSpec((tm, D_MODEL), lambda i, g: (i, 0)),
                  pl.BlockSpec((1, E_PER_GROUP, D_MODEL, D_EXPERT), lambda i, g: (g, 0, 0, 0)),
                  pl.BlockSpec((1, E_PER_GROUP, D_MODEL, D_EXPERT), lambda i, g: (g, 0, 0, 0)),
                  pl.BlockSpec((1, E_PER_GROUP * D_EXPERT, D_MODEL), lambda i, g: (g, 0, 0))],
        out_specs=pl.BlockSpec((tm, D_MODEL), lambda i, g: (i, 0)),
        out_shape=jax.ShapeDtypeStruct((t, D_MODEL), F32),
        scratch_shapes=[pltpu.VMEM((tm, E_PER_GROUP * D_EXPERT), BF16)],
        compiler_params=_cparams(("parallel", "arbitrary")),
        name="moe",
    )(h2, comb, x1, wg, wu, wd)


def _rope_tables(pos):
    half = DK_R // 2
    inv = ROPE_BASE ** (-jnp.arange(half, dtype=F32) / half)
    ang = pos.astype(F32)[:, None] * inv[None, :]
    c = jnp.cos(ang)
    s = jnp.sin(ang)
    return jnp.concatenate([c, c, c, c], axis=-1), jnp.concatenate([-s, s, -s, s], axis=-1)


def kernel(x_prompt, x_sample, state_ret, cache_fox_k, cache_fox_v, cache_fox_logf, cache_mem_k, cache_mem_v,
           mem_prompt, g_norm1, w_in, b_forget, g_fox_q, g_fox_k, g_mem_q, g_mem_in, w_mem_kv, g_mem_k,
           g_ret_out, w_br_ret, w_br_fox, w_br_mem, w_out, g_norm2, w_route_group, b_route_group,
           w_route_expert, b_route_expert, w_exp_gate, w_exp_up, w_exp_down):
    nb, seq, _ = x_prompt.shape
    nbs, n_new, _ = x_sample.shape
    past = cache_fox_k.shape[2]
    l = 0

    wi = w_in[l]
    o_ff = 2 * RQK_W + 2 * RV_W + 3 * FOX_W
    w_in_r = (wi[:, :o_ff].astype(BF16), wi[:, o_ff + H_F:].astype(BF16),
              jnp.pad(wi[:, o_ff:o_ff + H_F], ((0, 0), (0, LANES - H_F))).astype(BF16))
    bf_pad = jnp.concatenate([b_forget[l], jnp.zeros((LANES - H_F,), F32)])[None, :]
    g1 = g_norm1[l][None, :]
    gfq = jnp.tile(g_fox_q[l], H_F)[None, :]
    gfk = jnp.tile(g_fox_k[l], H_F)[None, :]
    gmq = g_mem_q[l][None, :]
    hid = jnp.arange(MXU_DIM) // D_F
    bd = jnp.where(hid[:, None] == hid[None, :], 1.0 / D_F, 0.0).astype(BF16)
    wr = w_br_ret[l].astype(BF16)
    wf = w_br_fox[l].astype(BF16)
    wm = w_br_mem[l].astype(BF16)
    wo = w_out[l].astype(BF16)
    g2 = g_norm2[l][None, :]
    n_e = N_GROUPS * E_PER_GROUP
    wrt = jnp.concatenate([w_route_expert[l], w_route_group[l],
                           jnp.zeros((D_MODEL, LANES - n_e - N_GROUPS), F32)], axis=1)
    wrt_hi = wrt.astype(BF16)
    wrt2 = jnp.concatenate([wrt_hi, (wrt - wrt_hi.astype(F32)).astype(BF16)], axis=1)
    brt = jnp.concatenate([b_route_expert[l], b_route_group[l],
                           jnp.zeros((LANES - n_e - N_GROUPS,), F32)])[None, :]
    wg = w_exp_gate[l].astype(BF16)
    wu = w_exp_up[l].astype(BF16)
    wd = w_exp_down[l].astype(BF16).reshape(N_GROUPS, E_PER_GROUP * D_EXPERT, D_MODEL)
    gro = g_ret_out[l][None, :]
    prep_w = (g1, w_in_r, bf_pad, gfq, gfk, gmq, bd)
    merge_w = (wr, wf, wm, wo, g2, wrt2, brt)

    tm = 512
    xp = x_prompt.reshape(nb * seq, D_MODEL)
    (rq, rk, rv, rg, fqt, fkt, fvt, lft, mq, gates, fkb) = _prep(
        xp, _rope_tables(jnp.arange(seq)), tm, seq // tm, True, *prep_w)
    o_r, s_fin = _retention(rq, rk, rv, rg, nb, seq, 256, gro, None)
    fcum = _cumsum(lft, 256).reshape(nb * (H_F // 2), 2, seq)
    o_f = _fox_prompt(fqt, fkb, fvt, fcum, nb, seq, 256)
    mk, mv = _memkv(mem_prompt.reshape(nb * N_MEM, D_MODEL), nb, g_mem_in[l][None, :],
                    w_mem_kv[l].astype(BF16), g_mem_k[l][None, :])
    x1, h3, route, counts = _merge(xp, o_r, o_f, (mq, mk, mv, seq // tm), gates, tm, True, *merge_w)
    t_p = nb * seq
    n_tiles = t_p // tm + N_GROUPS
    n_cls = N_GROUPS * N_PAIRS
    cnt = counts[:, :n_cls, 0].astype(jnp.int32)
    before_tile = jnp.cumsum(cnt, axis=0) - cnt
    cls_tot = jnp.sum(cnt, axis=0)
    grp_tot = jnp.sum(cls_tot.reshape(N_GROUPS, N_PAIRS), axis=1)
    tiles_g = (grp_tot + tm - 1) // tm
    tile_end = jnp.cumsum(tiles_g)
    row_start = (tile_end - tiles_g) * tm
    in_grp = cls_tot.reshape(N_GROUPS, N_PAIRS)
    cls_start = (row_start[:, None] + jnp.cumsum(in_grp, axis=1) - in_grp).reshape(n_cls)
    first_row = jnp.pad((cls_start[None, :] + before_tile).astype(F32), ((0, 0), (0, LANES - n_cls)))
    pos = _positions(route, jnp.broadcast_to(first_row[:, :, None], (t_p // tm, LANES, LANES)),
                     min(8, t_p // tm))
    tile_ids = jnp.arange(n_tiles, dtype=jnp.int32)
    tile_group = jnp.minimum(jnp.sum((tile_ids[:, None] >= tile_end[None, :]).astype(jnp.int32), axis=1),
                             N_GROUPS - 1)
    y3 = _moe_sorted(h3.reshape(t_p, N_SLABS, LANES), tile_group, tile_end[-1:], pos, n_tiles, tm,
                     wrt2, brt, wg, wu, wd)
    y_prompt = _combine(y3.reshape(n_tiles * tm, N_SLABS, LANES), pos, x1, 2 * tm).reshape(nb, seq, D_MODEL)

    ts = nbs * n_new
    xs = x_sample.reshape(ts, D_MODEL)
    pos_s = jnp.tile(past + jnp.arange(n_new), nbs)
    (rq_s, rk_s, rv_s, rg_s, fq_s, fk_s, fv_s, lf_s, mq_s, gates_s) = _prep(
        xs, _rope_tables(pos_s), ts, 1, False, *prep_w)
    o_r_s, s_new = _retention(rq_s, rk_s, rv_s, rg_s, nbs, n_new, n_new, gro, state_ret[l])
    pad = (-(past + n_new)) % 256
    lf_rows = jnp.concatenate([jnp.swapaxes(cache_fox_logf[l], 1, 2),
                               jnp.swapaxes(lf_s.reshape(nbs, n_new, H_F), 1, 2),
                               jnp.zeros((nbs, H_F, pad), F32)], axis=2).reshape(nbs * H_F, past + n_new + pad)
    f_all = _cumsum(lf_rows, 256)
    feat_major = lambda c: jnp.transpose(c, (0, 2, 3, 1)).reshape(nbs * FOX_W, past)
    o_f_s = _fox_sample(fq_s, feat_major(cache_fox_k[l]), feat_major(cache_fox_v[l]), fk_s, fv_s,
                        f_all[:, :past].reshape(nbs, H_F, past),
                        f_all[:, past:past + n_new].reshape(nbs, H_F, n_new), nbs, n_new, past)
    o_m_s = _memattn(mq_s, cache_mem_k[l].reshape(nbs * N_MEM, MEM_W),
                     cache_mem_v[l].reshape(nbs * N_MEM, MEM_W), nbs, n_new, n_new)
    x1_s, h2_s, comb_s = _merge(xs, o_r_s, o_f_s, o_m_s, gates_s, ts, False, *merge_w)
    y_sample = _moe(h2_s, comb_s, x1_s, ts, wg, wu, wd).reshape(nbs, n_new, D_MODEL)

    token_major = lambda a: jnp.transpose(a.reshape(nb, H_F, D_F, seq), (0, 3, 1, 2))

    return (y_prompt, y_sample,
            s_fin[None], token_major(fkt)[None], token_major(fvt)[None],
            jnp.swapaxes(lft.reshape(nb, H_F, seq), 1, 2)[None],
            mk.reshape(1, nb, N_MEM, H_M, D_M), mv.reshape(1, nb, N_MEM, H_M, D_M),
            s_new[None], fk_s.reshape(1, nbs, n_new, H_F, D_F), fv_s.reshape(1, nbs, n_new, H_F, D_F),
            lf_s.reshape(1, nbs, n_new, H_F))
```

```python
import functools

import jax
import jax.numpy as jnp
from jax import lax
from jax.experimental import pallas as pl
from jax.experimental.pallas import tpu as pltpu

F32 = jnp.float32
BF16 = jnp.bfloat16

D_MODEL = 1024
H_R, DK_R, DV_R = 4, 64, 128
H_F, D_F = 8, 64
H_M, D_M = 4, 128
N_MEM = 256
N_GROUPS, E_PER_GROUP, D_EXPERT = 4, 8, 256
ROPE_BASE = 10000.0
EPS = 1e-6
LOG2E = 1.4426950408889634

RQK_W = H_R * DK_R
RV_W = H_R * DV_R
FOX_W = H_F * D_F
MEM_W = H_M * D_M
GATE_W = 3 * D_MODEL
LANES = 128
MXU_DIM = 256

C_RQ = 0
C_RK = C_RQ + RQK_W
C_RV = C_RK + RQK_W
C_RG = C_RV + RV_W
C_FQ = C_RG + RV_W
C_FK = C_FQ + FOX_W
C_FV = C_FK + FOX_W
C_MQ = C_FV + FOX_W
C_GT = C_MQ + MEM_W
C_FF = C_GT + GATE_W

VMEM_LIMIT = 56 * 1024 * 1024


def _cparams(sem):
    return pltpu.CompilerParams(dimension_semantics=sem, vmem_limit_bytes=VMEM_LIMIT)


def _const_spec(shape):
    nd = len(shape)
    return pl.BlockSpec(shape, lambda *_: (0,) * nd, pipeline_mode=pl.Buffered(1))


def _dot(a, b):
    return jnp.dot(a, b, preferred_element_type=F32)


def _dot_nt(a, b):
    return lax.dot_general(a, b, (((1,), (1,)), ((), ())), preferred_element_type=F32)


def _dot_tn(a, b):
    return lax.dot_general(a, b, (((0,), (0,)), ((), ())), preferred_element_type=F32)


def _lane_rmsnorm(z):
    return z * lax.rsqrt(jnp.mean(z * z, axis=-1, keepdims=True) + EPS)


def _prep_kernel(x_ref, g1_ref, wa_ref, wb_ref, wf_ref, cos_ref, sin_ref, bf_ref, gfq_ref, gfk_ref, gmq_ref, bd_ref,
                 rq_ref, rk_ref, rv_ref, rg_ref, fq_ref, fk_ref, fv_ref, lf_ref, mq_ref, gt_ref, *extra,
                 seq_minor):
    x = x_ref[...]
    h = (x * lax.rsqrt(jnp.mean(x * x, axis=-1, keepdims=True) + EPS) * g1_ref[...]).astype(BF16)

    def proj(c0, width):
        if c0 < C_MQ:
            return _dot(h, wa_ref[:, c0:c0 + width])
        if c0 < C_FF:
            return _dot(h, wb_ref[:, c0 - C_MQ:c0 - C_MQ + width])
        return _dot(h, wf_ref[...])

    cos = cos_ref[...]
    sin = sin_ref[...]
    lane = lax.broadcasted_iota(jnp.int32, cos.shape, 1)
    first_half = (lane % DK_R) < (DK_R // 2)

    def rope(z):
        swapped = jnp.where(first_half, pltpu.roll(z, LANES - DK_R // 2, 1), pltpu.roll(z, DK_R // 2, 1))
        return z * cos + swapped * sin

    v = proj(C_FF, LANES) + bf_ref[...]
    lf = jnp.minimum(v, 0.0) - jnp.log1p(jnp.exp(-jnp.abs(v)))
    if seq_minor:
        lf_ref[...] = lf.T[:H_F, :]
    else:
        lf_ref[...] = lf[:, :H_F]

    def head64_norm(z, g_ref):
        zz = (z * z).astype(BF16)
        w = bd_ref.shape[0]
        ms = jnp.concatenate([_dot(zz[:, c:c + w], bd_ref[...]) for c in range(0, FOX_W, w)], axis=1)
        return z * lax.rsqrt(ms + EPS) * g_ref[...]

    fq = head64_norm(proj(C_FQ, FOX_W), gfq_ref) * (D_F ** -0.5 * LOG2E)
    fk = head64_norm(proj(C_FK, FOX_W), gfk_ref)
    fv = proj(C_FV, FOX_W)
    if seq_minor:
        fq_ref[...] = fq.T.astype(BF16)
        fk_ref[...] = fk.T
        fv_ref[...] = fv.T
        extra[0][...] = fk.astype(BF16)
    else:
        fq_ref[...] = fq.astype(BF16)
        fk_ref[...] = fk
        fv_ref[...] = fv

    zm = proj(C_MQ, MEM_W)
    for hh in range(H_M):
        sl = slice(hh * D_M, (hh + 1) * D_M)
        mq_ref[:, sl] = (_lane_rmsnorm(zm[:, sl]) * gmq_ref[...]).astype(BF16)

    for b in range(3):
        gt_ref[:, b * D_MODEL:(b + 1) * D_MODEL] = jax.nn.sigmoid(proj(C_GT + b * D_MODEL, D_MODEL)).astype(BF16)

    zq = proj(C_RQ, RQK_W)
    zk = proj(C_RK, RQK_W)
    for p in range(RQK_W // LANES):
        sl = slice(p * LANES, (p + 1) * LANES)
        rq_ref[:, sl] = rope(zq[:, sl])
        rk_ref[:, sl] = rope(zk[:, sl]) * (DK_R ** -0.5)
    rv_ref[...] = proj(C_RV, RV_W)
    rg_ref[...] = proj(C_RG, RV_W)


def _prep(x2d, tables, tm, n_pos_tiles, seq_minor, g1, w_in_r, bf_pad, gfq, gfk, gmq, bd):
    t = x2d.shape[0]
    cos_t, sin_t = tables
    row = lambda w: pl.BlockSpec((tm, w), lambda i: (i, 0))
    pos = pl.BlockSpec((tm, LANES), lambda i: (i % n_pos_tiles, 0))
    sds = jax.ShapeDtypeStruct
    if seq_minor:
        nb, seq = t // (n_pos_tiles * tm), n_pos_tiles * tm
        fm = lambda rows_: pl.BlockSpec((rows_, tm), lambda i: (i // n_pos_tiles, i % n_pos_tiles))
        fox_shapes = [sds((nb * FOX_W, seq), BF16), sds((nb * FOX_W, seq), F32), sds((nb * FOX_W, seq), F32),
                      sds((nb * H_F, seq), F32)]
        fox_specs = [fm(FOX_W), fm(FOX_W), fm(FOX_W), fm(H_F)]
        extra_shapes, extra_specs = [sds((t, FOX_W), BF16)], [row(FOX_W)]
    else:
        fox_shapes = [sds((t, FOX_W), BF16), sds((t, FOX_W), F32), sds((t, FOX_W), F32), sds((t, H_F), F32)]
        fox_specs = [row(FOX_W), row(FOX_W), row(FOX_W), row(H_F)]
        extra_shapes, extra_specs = [], []
    out_shapes = [sds((t, RQK_W), F32), sds((t, RQK_W), F32), sds((t, RV_W), F32), sds((t, RV_W), F32),
                  *fox_shapes, sds((t, MEM_W), BF16), sds((t, GATE_W), BF16), *extra_shapes]
    out_specs = [row(RQK_W), row(RQK_W), row(RV_W), row(RV_W), *fox_specs, row(MEM_W), row(GATE_W),
                 *extra_specs]
    return pl.pallas_call(
        functools.partial(_prep_kernel, seq_minor=seq_minor),
        grid=(t // tm,),
        in_specs=[row(D_MODEL), _const_spec((1, D_MODEL)), _const_spec((D_MODEL, C_MQ)),
                  _const_spec((D_MODEL, C_FF - C_MQ)), _const_spec((D_MODEL, LANES)), pos, pos,
                  _const_spec((1, LANES)), _const_spec((1, FOX_W)), _const_spec((1, FOX_W)),
                  _const_spec((1, D_M)), _const_spec((MXU_DIM, MXU_DIM))],
        out_specs=out_specs,
        out_shape=out_shapes,
        compiler_params=_cparams(("parallel",)),
        name="prep",
    )(x2d, g1, *w_in_r, cos_t, sin_t, bf_pad, gfq, gfk, gmq, bd)


def _ret_kernel(*refs, has_init, ch):
    if has_init:
        (rq_ref, rk_ref, rv_ref, rg_ref, dmat_ref, qdec_ref, kdec_ref, gpow_ref, gro_ref, s0_ref,
         o_ref, sfin_ref) = refs
    else:
        (rq_ref, rk_ref, rv_ref, rg_ref, dmat_ref, qdec_ref, kdec_ref, gpow_ref, gro_ref,
         o_ref, sfin_ref) = refs

    state = []
    for h in range(H_R):
        if has_init:
            pad = jnp.zeros((DK_R, DV_R), F32)
            state.append(jnp.concatenate([s0_ref[0, h], pad] if h % 2 == 0 else [pad, s0_ref[0, h]], axis=0))
        else:
            state.append(jnp.zeros((LANES, DV_R), F32))

    lane = lax.broadcasted_iota(jnp.int32, (ch, LANES), 1)
    for c in range(rq_ref.shape[0] // ch):
        rows = slice(c * ch, (c + 1) * ch)
        for p in range(H_R // 2):
            sl = slice(p * LANES, (p + 1) * LANES)
            q2 = rq_ref[rows, sl]
            k2 = rk_ref[rows, sl]
            kd2 = k2 * kdec_ref[p]
            for hh in range(2):
                h = 2 * p + hh
                mine = (lane >= hh * DK_R) & (lane < (hh + 1) * DK_R)
                qm = jnp.where(mine, q2, 0.0).astype(BF16)
                kdm = jnp.where(mine, kd2, 0.0).astype(BF16)
                v = rv_ref[rows, h * DV_R:(h + 1) * DV_R].astype(BF16)
                sc = _dot_nt(qm, k2.astype(BF16)) * dmat_ref[h]
                o = _dot(sc.astype(BF16), v) + _dot(qm, state[h].astype(BF16)) * qdec_ref[h]
                state[h] = gpow_ref[h] * state[h] + _dot_tn(kdm, v)
                normed = _lane_rmsnorm(o) * gro_ref[...]
                rg = rg_ref[rows, h * DV_R:(h + 1) * DV_R]
                o_ref[rows, h * DV_R:(h + 1) * DV_R] = (normed * (rg * jax.nn.sigmoid(rg))).astype(BF16)

    for h in range(H_R):
        r0 = DK_R * (h % 2)
        sfin_ref[0, h] = state[h][r0:r0 + DK_R, :]


def _retention(rq, rk, rv, rg, n_batch, seq, ch, g_ret_out, state0):
    lg = jnp.log1p(-jnp.exp2(-5.0 - jnp.arange(H_R, dtype=F32)))
    idx = jnp.arange(ch, dtype=F32)
    diff = idx[:, None] - idx[None, :]
    causal = diff >= 0
    dmat = jnp.where(causal[None], jnp.exp(jnp.where(causal, diff, 0.0)[None] * lg[:, None, None]), 0.0)
    q_dec = jnp.exp((idx + 1.0)[None, :] * lg[:, None])
    k_dec = jnp.exp((ch - 1.0 - idx)[None, :] * lg[:, None])
    qdec = jnp.broadcast_to(q_dec[:, :, None], (H_R, ch, DV_R))
    kdec = jnp.broadcast_to(k_dec[:, :, None], (H_R, ch, DK_R))
    kdec = kdec.reshape(H_R // 2, 2, ch, DK_R).transpose(0, 2, 1, 3).reshape(H_R // 2, ch, LANES)
    gpow = jnp.broadcast_to(jnp.exp(ch * lg)[:, None, None], (H_R, 1, DV_R))

    has_init = state0 is not None
    blk = lambda w: pl.BlockSpec((seq, w), lambda b: (b, 0))
    in_specs = [blk(RQK_W), blk(RQK_W), blk(RV_W), blk(RV_W),
                _const_spec((H_R, ch, ch)), _const_spec((H_R, ch, DV_R)),
                _const_spec((H_R // 2, ch, LANES)), _const_spec((H_R, 1, DV_R)), _const_spec((1, DV_R))]
    args = [rq, rk, rv, rg, dmat, qdec, kdec, gpow, g_ret_out]
    if has_init:
        in_specs.append(pl.BlockSpec((1, H_R, DK_R, DV_R), lambda b: (b, 0, 0, 0)))
        args.append(state0)
    return pl.pallas_call(
        functools.partial(_ret_kernel, has_init=has_init, ch=ch),
        grid=(n_batch,),
        in_specs=in_specs,
        out_specs=[blk(RV_W), pl.BlockSpec((1, H_R, DK_R, DV_R), lambda b: (b, 0, 0, 0))],
        out_shape=[jax.ShapeDtypeStruct((n_batch * seq, RV_W), BF16),
                   jax.ShapeDtypeStruct((n_batch, H_R, DK_R, DV_R), F32)],
        compiler_params=_cparams(("parallel",)),
        name="retention",
    )(*args)


def _split3(x):
    hi = x.astype(BF16)
    r1 = x - hi.astype(F32)
    mid = r1.astype(BF16)
    lo = (r1 - mid.astype(F32)).astype(BF16)
    return hi, mid, lo


def _cumsum_kernel(x_ref, o_ref, *, blk):
    rows, n = x_ref.shape
    r = lax.broadcasted_iota(jnp.int32, (blk, blk), 0)
    c = lax.broadcasted_iota(jnp.int32, (blk, blk), 1)
    tri = jnp.where(r <= c, 1.0, 0.0).astype(BF16)
    carry = jnp.zeros((rows, 1), F32)
    for i in range(n // blk):
        hi, mid, lo = _split3(x_ref[:, i * blk:(i + 1) * blk])
        cum = (_dot(hi, tri) + _dot(mid, tri)) + _dot(lo, tri) + carry
        o_ref[:, i * blk:(i + 1) * blk] = cum
        carry = cum[:, blk - 1:blk]


def _cumsum(x, blk):
    r, t = x.shape
    spec = pl.BlockSpec((H_F, t), lambda b: (b, 0))
    return pl.pallas_call(
        functools.partial(_cumsum_kernel, blk=blk),
        grid=(r // H_F,),
        in_specs=[spec], out_specs=spec,
        out_shape=jax.ShapeDtypeStruct(x.shape, F32),
        compiler_params=_cparams(("parallel",)),
        name="cumsum",
    )(x)


def _head_rows_mask(shape, hh):
    sub = lax.broadcasted_iota(jnp.int32, shape, 0)
    return (sub >= hh * D_F) & (sub < (hh + 1) * D_F)


N_AUG = 3
V_ROWS = D_F + 16


def _fox_kernel(qt_ref, k_ref, vt_ref, fcum_ref, o_ref, qa, ka, vb, *, tq):
    seq = k_ref.shape[0]
    tk = tq
    k = k_ref[...].astype(F32)
    lane = lax.broadcasted_iota(jnp.int32, k.shape, 1)
    aug_rows = lax.broadcasted_iota(jnp.int32, (D_F, seq), 0) < N_AUG
    for hh in range(2):
        own, oth = hh * D_F, (1 - hh) * D_F
        qa[hh, own:own + D_F, :] = qt_ref[own:own + D_F, :]
        qa[hh, oth:oth + D_F, :] = jnp.where(aug_rows, 1.0, 0.0).astype(BF16)
        pieces = [p.astype(F32) for p in _split3(fcum_ref[0, hh:hh + 1, :] * (-LOG2E))]
        ka[hh] = jnp.where((lane >= own) & (lane < own + D_F), k, _rows_to_columns(pieces, oth)).astype(BF16)
    for hh in range(2):
        vb[hh, :D_F, :] = vt_ref[hh * D_F:(hh + 1) * D_F, :].astype(BF16)
        vb[hh, D_F:, :] = jnp.ones((V_ROWS - D_F, seq), BF16)
    causal = (lax.broadcasted_iota(jnp.int32, (tk, tq), 1) >= lax.broadcasted_iota(jnp.int32, (tk, tq), 0))

    pairs = [(i, j) for i in range(seq // tq) for j in range(i + 1)]

    def stage_a(i, j):
        return tuple(_dot(ka[hh, j * tk:(j + 1) * tk, :], qa[hh, :, i * tq:(i + 1) * tq]) for hh in range(2))

    def stage_b(i, j, ts, maxes):
        new_maxes, probs = [], []
        for hh in range(2):
            t = jnp.where(causal, ts[hh], -jnp.inf) if j == i else ts[hh]
            t_max = jnp.max(t, axis=0, keepdims=True)
            if j == 0:
                m_new, alpha = t_max, None
            else:
                m_new = jnp.maximum(maxes[hh], t_max)
                alpha = jnp.exp2(maxes[hh] - m_new)
            new_maxes.append(m_new)
            probs.append((jnp.exp2(t - m_new).astype(BF16), alpha))
        return new_maxes, probs

    def stage_c(i, j, probs, accs):
        out = []
        for hh in range(2):
            p, alpha = probs[hh]
            pv = _dot(vb[hh, :, j * tk:(j + 1) * tk], p)
            out.append(pv if j == 0 else alpha * accs[hh] + pv)
        return out

    scores, probs, maxes, accs = {}, {}, None, None
    for s in range(len(pairs) + 2):
        if s < len(pairs):
            scores[s] = stage_a(*pairs[s])
        if 0 <= s - 1 < len(pairs):
            maxes, probs[s - 1] = stage_b(*pairs[s - 1], scores.pop(s - 1), maxes)
        if 0 <= s - 2 < len(pairs):
            i, j = pairs[s - 2]
            accs = stage_c(i, j, probs.pop(s - 2), accs)
            if j == i:
                o_t = jnp.concatenate([a[:D_F] / a[D_F:D_F + 1] for a in accs], axis=0)
                o_ref[i * tq:(i + 1) * tq, :] = o_t.T.astype(BF16)


def _fox_prompt(fqt, fkb, fvt, fcum, n_batch, seq, tq):
    npair = H_F // 2
    fm_spec = pl.BlockSpec((LANES, seq), lambda b, p: (b * npair + p, 0))
    tok_spec = pl.BlockSpec((seq, LANES), lambda b, p: (b, p))
    return pl.pallas_call(
        functools.partial(_fox_kernel, tq=tq),
        grid=(n_batch, npair),
        in_specs=[fm_spec, tok_spec, fm_spec, pl.BlockSpec((1, 2, seq), lambda b, p: (b * npair + p, 0, 0))],
        out_specs=tok_spec,
        out_shape=jax.ShapeDtypeStruct((n_batch * seq, FOX_W), BF16),
        scratch_shapes=[pltpu.VMEM((2, LANES, seq), BF16), pltpu.VMEM((2, seq, LANES), BF16),
                        pltpu.VMEM((2, V_ROWS, seq), BF16)],
        compiler_params=_cparams(("parallel", "parallel")),
        name="fox_prompt",
    )(fqt, fkb, fvt, fcum)


def _fox_sample_kernel(q_ref, kct_ref, vct_ref, kn_ref, vn_ref, frc_ref, frn_ref, o_ref):
    n = q_ref.shape[0]
    lane = lax.broadcasted_iota(jnp.int32, (n, LANES), 1)
    rows = lax.broadcasted_iota(jnp.int32, (n, n), 0)
    cols = lax.broadcasted_iota(jnp.int32, (n, n), 1)
    for pair in range(H_F // 2):
        sl = slice(pair * LANES, (pair + 1) * LANES)
        q2 = q_ref[:, sl]
        kct = kct_ref[sl, :]
        vct = vct_ref[sl, :].astype(BF16)
        kn = kn_ref[:, sl].astype(BF16)
        vn = vn_ref[:, sl].astype(BF16)
        outs = []
        for hh in range(2):
            h = 2 * pair + hh
            mine = (lane >= hh * D_F) & (lane < (hh + 1) * D_F)
            kc = jnp.where(_head_rows_mask(kct.shape, hh), kct, 0.0).astype(BF16)
            t_c = _dot(q2, kc) - frc_ref[0, h:h + 1, :] * LOG2E
            t_n = _dot_nt(jnp.where(mine, q2, jnp.zeros_like(q2)), kn) - frn_ref[0, h:h + 1, :] * LOG2E
            t_n = jnp.where(rows >= cols, t_n, -jnp.inf)
            m = jnp.maximum(jnp.max(t_c, axis=-1, keepdims=True), jnp.max(t_n, axis=-1, keepdims=True))
            p_c = jnp.exp2(t_c - m)
            p_n = jnp.exp2(t_n - m)
            l = jnp.sum(p_c, axis=-1, keepdims=True) + jnp.sum(p_n, axis=-1, keepdims=True)
            acc = _dot_nt(p_c.astype(BF16), vct) + _dot(p_n.astype(BF16), vn)
            outs.append(acc / l)
        o_ref[:, sl] = jnp.where(lane < D_F, outs[0], outs[1]).astype(BF16)


def _fox_sample(fq, kct, vct, kn, vn, frow_c, frow_n, n_batch, n_new, past):
    tok = pl.BlockSpec((n_new, FOX_W), lambda b: (b, 0))
    cache = pl.BlockSpec((FOX_W, past), lambda b: (b, 0))
    return pl.pallas_call(
        _fox_sample_kernel,
        grid=(n_batch,),
        in_specs=[tok, cache, cache, tok, tok,
                  pl.BlockSpec((1, H_F, past), lambda b: (b, 0, 0)),
                  pl.BlockSpec((1, H_F, n_new), lambda b: (b, 0, 0))],
        out_specs=tok,
        out_shape=jax.ShapeDtypeStruct((n_batch * n_new, FOX_W), BF16),
        compiler_params=_cparams(("parallel",)),
        name="fox_sample",
    )(fq, kct, vct, kn, vn, frow_c, frow_n)


def _memkv_kernel(m_ref, gin_ref, w_ref, gk_ref, k_ref, v_ref):
    x = m_ref[...]
    h = (x * lax.rsqrt(jnp.mean(x * x, axis=-1, keepdims=True) + EPS) * gin_ref[...]).astype(BF16)
    zk = _dot(h, w_ref[:, :MEM_W])
    for hh in range(H_M):
        sl = slice(hh * D_M, (hh + 1) * D_M)
        k_ref[:, sl] = _lane_rmsnorm(zk[:, sl]) * gk_ref[...]
    v_ref[...] = _dot(h, w_ref[:, MEM_W:])


def _memkv(mem2d, n_batch, g_mem_in, w_mem_kv, g_mem_k):
    blk = lambda w: pl.BlockSpec((N_MEM, w), lambda b: (b, 0))
    return pl.pallas_call(
        _memkv_kernel,
        grid=(n_batch,),
        in_specs=[blk(D_MODEL), _const_spec((1, D_MODEL)), _const_spec((D_MODEL, 2 * MEM_W)),
                  _const_spec((1, D_M))],
        out_specs=[blk(MEM_W), blk(MEM_W)],
        out_shape=[jax.ShapeDtypeStruct((n_batch * N_MEM, MEM_W), F32)] * 2,
        compiler_params=_cparams(("parallel",)),
        name="memkv",
    )(mem2d, g_mem_in, w_mem_kv, g_mem_k)


def _mem_attend(q_ref, k_ref, v_ref):
    for hh in range(H_M):
        sl = slice(hh * D_M, (hh + 1) * D_M)
        s = _dot_nt(q_ref[:, sl], k_ref[:, sl].astype(BF16)) * (D_M ** -0.5)
        m = jnp.max(s, axis=-1, keepdims=True)
        p = jnp.exp(s - m)
        l = jnp.sum(p, axis=-1, keepdims=True)
        yield sl, (_dot(p.astype(BF16), v_ref[:, sl].astype(BF16)) / l).astype(BF16)


def _memattn_kernel(q_ref, k_ref, v_ref, o_ref):
    for sl, o in _mem_attend(q_ref, k_ref, v_ref):
        o_ref[:, sl] = o


def _memattn(mq, mk, mv, n_batch, seq, tq):
    nq = seq // tq
    return pl.pallas_call(
        _memattn_kernel,
        grid=(n_batch, nq),
        in_specs=[pl.BlockSpec((tq, MEM_W), lambda b, i: (b * nq + i, 0)),
                  pl.BlockSpec((N_MEM, MEM_W), lambda b, i: (b, 0)),
                  pl.BlockSpec((N_MEM, MEM_W), lambda b, i: (b, 0))],
        out_specs=pl.BlockSpec((tq, MEM_W), lambda b, i: (b * nq + i, 0)),
        out_shape=jax.ShapeDtypeStruct((n_batch * seq, MEM_W), BF16),
        compiler_params=_cparams(("parallel", "arbitrary")),
        name="memattn",
    )(mq, mk, mv)


def _split2(x):
    hi = x.astype(BF16)
    return hi, (x - hi.astype(F32)).astype(BF16)


N_EXPERTS = N_GROUPS * E_PER_GROUP


N_PAIRS = E_PER_GROUP * (E_PER_GROUP - 1) // 2


def _router_logits(h2, wrt_ref, brt_ref):
    h_hi, h_lo = _split2(h2)
    z = _dot(h_hi, wrt_ref[...])
    return (z[:, :LANES] + z[:, LANES:]) + _dot(h_lo, wrt_ref[:, :LANES]) + brt_ref[...]


N_LOGIT_ROWS = 40


def _route(logits, group=None):
    lt = logits.T[:N_LOGIT_ROWS, :]
    idx = lax.broadcasted_iota(jnp.int32, lt.shape, 0).astype(F32)
    neg = -jnp.inf
    first_idx = lambda mask: jnp.min(jnp.where(mask, idx, float(LANES)), axis=0, keepdims=True)

    is_g = (idx >= N_EXPERTS) & (idx < N_EXPERTS + N_GROUPS)
    lg = jnp.where(is_g, lt, neg)
    mg = jnp.max(lg, axis=0, keepdims=True)
    if group is None:
        group = first_idx(lg == mg) - N_EXPERTS
        p_sel = 1.0 / jnp.sum(jnp.exp(lg - mg), axis=0, keepdims=True)
    else:
        lsel = jnp.max(jnp.where(idx == group + N_EXPERTS, lt, neg), axis=0, keepdims=True)
        p_sel = jnp.exp(lsel - mg) / jnp.sum(jnp.exp(lg - mg), axis=0, keepdims=True)

    in_grp = (idx >= group * E_PER_GROUP) & (idx < (group + 1) * E_PER_GROUP)
    le = jnp.where(in_grp, lt, neg)
    v1 = jnp.max(le, axis=0, keepdims=True)
    i1 = first_idx(le == v1)
    le2 = jnp.where(idx == i1, neg, le)
    v2 = jnp.max(le2, axis=0, keepdims=True)
    i2 = first_idx(le2 == v2)
    e2 = jnp.exp(v2 - v1)
    return group, i1, i2, p_sel / (1.0 + e2), p_sel * e2 / (1.0 + e2)


def _rows_to_columns(rows, first=0):
    n = rows[0].shape[1]
    idx = lax.broadcasted_iota(jnp.int32, (LANES, n), 0)
    stacked = jnp.zeros((LANES, n), F32)
    for r, row in enumerate(rows):
        stacked = jnp.where(idx == first + r, row, stacked)
    return stacked.T


def _merge_kernel(*refs, sorted_moe):
    if sorted_moe:
        (x_ref, or_ref, of_ref, mq_ref, mk_ref, mv_ref, gt_ref, wr_ref, wf_ref, wm_ref, wo_ref, g2_ref,
         wrt_ref, brt_ref, x1_ref, h2_ref, route_ref, *counts_ref) = refs
        o_m = jnp.concatenate([o for _, o in _mem_attend(mq_ref, mk_ref, mv_ref)], axis=1)
    else:
        (x_ref, or_ref, of_ref, om_ref, gt_ref, wr_ref, wf_ref, wm_ref, wo_ref, g2_ref,
         wrt_ref, brt_ref, x1_ref, h2_ref, route_ref) = refs
        o_m = om_ref[...]
    g = lambda b: gt_ref[:, b * D_MODEL:(b + 1) * D_MODEL].astype(F32)
    merged = (g(0) * _dot(or_ref[...], wr_ref[...]) + g(1) * _dot(of_ref[...], wf_ref[...])
              + g(2) * _dot(o_m, wm_ref[...]))
    x1 = x_ref[...] + _dot(merged.astype(BF16), wo_ref[...])
    x1_ref[...] = x1
    h2 = x1 * lax.rsqrt(jnp.mean(x1 * x1, axis=-1, keepdims=True) + EPS) * g2_ref[...]
    group, i1, i2, w1, w2 = _route(_router_logits(h2, wrt_ref, brt_ref))
    tm = h2.shape[0]
    if sorted_moe:
        _to_token_tiles(h2_ref, h2)
        e_lo = jnp.minimum(i1, i2) - group * E_PER_GROUP
        e_hi = jnp.maximum(i1, i2) - group * E_PER_GROUP
        cls = group * N_PAIRS + (e_lo * E_PER_GROUP - e_lo * (e_lo + 1.0) * 0.5 + (e_hi - e_lo - 1.0))
        cidx = lax.broadcasted_iota(jnp.int32, (LANES, tm), 0).astype(F32)
        onehot = jnp.where(cidx == cls, 1.0, 0.0)
        r = lax.broadcasted_iota(jnp.int32, (tm, tm), 0)
        c = lax.broadcasted_iota(jnp.int32, (tm, tm), 1)
        before = _dot(onehot.astype(BF16), jnp.where(r < c, 1.0, 0.0).astype(BF16))
        rank = jnp.sum(before * onehot, axis=0, keepdims=True)
        ridx = lax.broadcasted_iota(jnp.int32, route_ref.shape, 0)
        route_ref[...] = jnp.where(ridx == 0, group, jnp.where(ridx == 1, cls, jnp.where(ridx == 2, rank, 0.0)))
        counts_ref[0][0] = jnp.broadcast_to(jnp.sum(onehot, axis=1, keepdims=True), (LANES, LANES))
    else:
        h2_ref[...] = h2.astype(BF16)
        cols = _rows_to_columns([i1, i2, w1, w2])
        lane = lax.broadcasted_iota(jnp.int32, (tm, LANES), 1).astype(F32)
        comb = jnp.where(lane == cols[:, 0:1], cols[:, 2:3], 0.0) + jnp.where(lane == cols[:, 1:2], cols[:, 3:4], 0.0)
        for gi in range(N_GROUPS):
            route_ref[gi] = comb[:, gi * E_PER_GROUP:(gi + 1) * E_PER_GROUP]


def _merge(x2d, o_r, o_f, mem, gates, tm, sorted_moe, wr, wf, wm, wo, g2, wrt, brt):
    t = x2d.shape[0]
    row = lambda w: pl.BlockSpec((tm, w), lambda i: (i, 0))
    sds = jax.ShapeDtypeStruct
    if sorted_moe:
        h2_spec, h2_shape = pl.BlockSpec((tm * N_SLABS, LANES), lambda i: (i, 0)), sds((t * N_SLABS, LANES), F32)
        rt_specs = [pl.BlockSpec((SUBLANES, tm), lambda i: (i, 0)),
                    pl.BlockSpec((1, LANES, LANES), lambda i: (i, 0, 0))]
        rt_shapes = [sds((t // tm * SUBLANES, tm), F32), sds((t // tm, LANES, LANES), F32)]
        mq, mk, mv, tiles_per_seq = mem
        mem_args = [mq, mk, mv]
        mem_kv = pl.BlockSpec((N_MEM, MEM_W), lambda i: (i // tiles_per_seq, 0))
        mem_specs = [row(MEM_W), mem_kv, mem_kv]
    else:
        h2_spec, h2_shape = row(D_MODEL), sds((t, D_MODEL), BF16)
        rt_specs = [pl.BlockSpec((N_GROUPS, tm, E_PER_GROUP), lambda i: (0, i, 0))]
        rt_shapes = [sds((N_GROUPS, t, E_PER_GROUP), F32)]
        mem_args, mem_specs = [mem], [row(MEM_W)]
    return pl.pallas_call(
        functools.partial(_merge_kernel, sorted_moe=sorted_moe),
        grid=(t // tm,),
        in_specs=[row(D_MODEL), row(RV_W), row(FOX_W), *mem_specs, row(GATE_W),
                  _const_spec((RV_W, D_MODEL)), _const_spec((FOX_W, D_MODEL)), _const_spec((MEM_W, D_MODEL)),
                  _const_spec((D_MODEL, D_MODEL)), _const_spec((1, D_MODEL)),
                  _const_spec((D_MODEL, 2 * LANES)), _const_spec((1, LANES))],
        out_specs=[row(D_MODEL), h2_spec, *rt_specs],
        out_shape=[sds((t, D_MODEL), F32), h2_shape, *rt_shapes],
        compiler_params=_cparams(("parallel",)),
        name="merge",
    )(x2d, o_r, o_f, *mem_args, gates, wr, wf, wm, wo, g2, wrt, brt)


def _positions_kernel(route_ref, first_ref, pos_ref):
    tm = route_ref.shape[1]
    cidx = lax.broadcasted_iota(jnp.int32, (LANES, tm), 0).astype(F32)
    for i in range(first_ref.shape[0]):
        route = route_ref[i * SUBLANES:(i + 1) * SUBLANES, :]
        first = jnp.concatenate([first_ref[i]] * (tm // LANES), axis=1)
        start = jnp.sum(jnp.where(cidx == route[1:2, :], first, 0.0), axis=0, keepdims=True)
        pos_ref[i * SUBLANES:(i + 1) * SUBLANES, :] = jnp.broadcast_to(start + route[2:3, :],
                                                                       route.shape).astype(jnp.int32)


def _positions(route, first_row, tiles_per_step):
    rows, tm = route.shape
    n = rows // SUBLANES
    out = pl.pallas_call(
        _positions_kernel,
        grid=(n // tiles_per_step,),
        in_specs=[pl.BlockSpec((tiles_per_step * SUBLANES, tm), lambda i: (i, 0)),
                  pl.BlockSpec((tiles_per_step, LANES, LANES), lambda i: (i, 0, 0))],
        out_specs=pl.BlockSpec((tiles_per_step * SUBLANES, tm), lambda i: (i, 0)),
        out_shape=jax.ShapeDtypeStruct((rows, tm), jnp.int32),
        compiler_params=_cparams(("parallel",)),
        name="positions",
    )(route, first_row)
    return out.reshape(n, SUBLANES, tm)[:, 0, :].reshape(n * tm)


def _group_experts(h, cw, wg_ref, wu_ref, wd_ref, act):
    for e in range(E_PER_GROUP):
        a = _dot(h, wg_ref[0, e])
        u = _dot(h, wu_ref[0, e])
        act[:, e * D_EXPERT:(e + 1) * D_EXPERT] = ((a * jax.nn.sigmoid(a)) * u * cw(e)).astype(BF16)
    return _dot(act[...], wd_ref[0])


N_SLABS = D_MODEL // LANES
SUBLANES = 8


def _to_token_tiles(ref, x):
    for s in range(N_SLABS):
        ref[pl.ds(s, x.shape[0], stride=N_SLABS), :] = x[:, s * LANES:(s + 1) * LANES]


def _landing_shape(rows):
    return (rows // SUBLANES, N_SLABS, SUBLANES, LANES)


def _from_landing(buf):
    rows = buf.shape[0] * SUBLANES
    return jnp.concatenate([buf[:, s].reshape(rows, LANES) for s in range(N_SLABS)], axis=1)


def _row_gather(idx_ref, base, src_hbm, dst, sem, part=None):
    def body(i, _):
        for u in range(SUBLANES):
            pltpu.make_async_copy(src_hbm.at[idx_ref[base + i * SUBLANES + u]], dst.at[i, :, u, :],
                                  sem).start(priority=u % 2)
        return 0
    if part is None:
        lax.fori_loop(0, dst.shape[0], body, 0)
    else:
        j, n = part
        per = dst.shape[0] // n
        for i in range(j * per, (j + 1) * per):
            body(i, 0)


def _row_gather_wait(dst, sem):
    pltpu.make_async_copy(dst, dst, sem).wait()


def _moe_sorted_kernel(tg_ref, nvt_ref, pos_ref, h3_hbm, zeros_hbm, wrt_ref, brt_ref,
                       wg_ref, wu_ref, wd_ref, y_ref, xbuf, sem, hb, yacc, src_ref, clear_sem, *, tm):
    k = pl.program_id(0)
    nvt = nvt_ref[0]
    slot = lax.rem(k, 2)

    @pl.when(k == 0)
    def _():
        clear = pltpu.make_async_copy(zeros_hbm, src_ref, clear_sem)
        clear.start()
        clear.wait()

        def invert(t, _):
            src_ref[pos_ref[t]] = t
            return 0

        lax.fori_loop(0, pos_ref.shape[0], invert, 0, unroll=32)

    @pl.when((k == 0) & (nvt > 0))
    def _():
        _row_gather(src_ref, 0, h3_hbm, xbuf.at[0], sem.at[0])

    @pl.when(k < nvt)
    def _():
        _row_gather_wait(xbuf.at[slot], sem.at[slot])

        def fetch_part(e):
            _row_gather(src_ref, jnp.minimum(k + 1, nvt - 1) * tm, h3_hbm, xbuf.at[1 - slot],
                        sem.at[1 - slot], part=(e, E_PER_GROUP))

        x = _from_landing(xbuf.at[slot])
        group = tg_ref[k]
        _, i1, i2, w1, w2 = _route(_router_logits(x, wrt_ref, brt_ref), group.astype(F32))
        hb[...] = x.astype(BF16)
        yacc[...] = jnp.zeros(yacc.shape, F32)
        first = (group * E_PER_GROUP).astype(F32)
        used = [jnp.max(jnp.where(i1 == first + e, w1, 0.0) + jnp.where(i2 == first + e, w2, 0.0))
                for e in range(E_PER_GROUP)]
        cols = _rows_to_columns([i1, i2, w1, w2])
        for e in range(E_PER_GROUP):
            is_used = used[e] > 0.0

            @pl.when(jnp.logical_not(is_used))
            def _(e=e):
                fetch_part(e)

            @pl.when(is_used)
            def _(e=e):
                fetch_part(e)
                cw = (jnp.where(cols[:, 0:1] == first + e, cols[:, 2:3], 0.0)
                      + jnp.where(cols[:, 1:2] == first + e, cols[:, 3:4], 0.0))
                h = hb[...]
                a = _dot(h, wg_ref[0, e])
                u = _dot(h, wu_ref[0, e])
                act = ((a * jax.nn.sigmoid(a)) * u * cw).astype(BF16)
                yacc[...] += _dot(act, wd_ref[0, e * D_EXPERT:(e + 1) * D_EXPERT, :])

        _to_token_tiles(y_ref, yacc[...])

        @pl.when(k + 1 >= nvt)
        def _():
            _row_gather_wait(xbuf.at[1 - slot], sem.at[1 - slot])

    @pl.when(k >= nvt)
    def _():
        y_ref[...] = jnp.zeros(y_ref.shape, F32)


def _moe_sorted(h3, tile_group, n_valid_tiles, pos, n_tiles, tm, wrt, brt, wg, wu, wd):
    wspec = lambda shape: pl.BlockSpec(shape, lambda k, tg, nv, ps: (tg[k],) + (0,) * (len(shape) - 1))
    cspec = lambda shape: pl.BlockSpec(shape, lambda k, tg, nv, ps: (0,) * len(shape),
                                       pipeline_mode=pl.Buffered(1))
    return pl.pallas_call(
        functools.partial(_moe_sorted_kernel, tm=tm),
        grid_spec=pltpu.PrefetchScalarGridSpec(
            num_scalar_prefetch=3,
            grid=(n_tiles,),
            in_specs=[pl.BlockSpec(memory_space=pl.ANY), pl.BlockSpec(memory_space=pl.ANY),
                      cspec((D_MODEL, 2 * LANES)), cspec((1, LANES)),
                      wspec((1, E_PER_GROUP, D_MODEL, D_EXPERT)), wspec((1, E_PER_GROUP, D_MODEL, D_EXPERT)),
                      wspec((1, E_PER_GROUP * D_EXPERT, D_MODEL))],
            out_specs=pl.BlockSpec((tm * N_SLABS, LANES), lambda k, tg, nv, ps: (k, 0)),
            scratch_shapes=[pltpu.VMEM((2,) + _landing_shape(tm), F32), pltpu.SemaphoreType.DMA((2,)),
                            pltpu.VMEM((tm, D_MODEL), BF16), pltpu.VMEM((tm, D_MODEL), F32),
                            pltpu.SMEM((n_tiles * tm,), jnp.int32), pltpu.SemaphoreType.DMA(())]),
        out_shape=jax.ShapeDtypeStruct((n_tiles * tm * N_SLABS, LANES), F32),
        compiler_params=_cparams(("arbitrary",)),
        name="moe_sorted",
    )(tile_group, n_valid_tiles, pos, h3, jnp.zeros((n_tiles * tm,), jnp.int32), wrt, brt, wg, wu, wd)


def _combine_kernel(pos_ref, y3_hbm, x1_ref, o_ref, ybuf, sem, *, tm):
    k = pl.program_id(0)
    slot = lax.rem(k, 2)

    @pl.when(k == 0)
    def _():
        _row_gather(pos_ref, 0, y3_hbm, ybuf.at[0], sem.at[0])

    _row_gather_wait(ybuf.at[slot], sem.at[slot])

    @pl.when(k + 1 < pl.num_programs(0))
    def _():
        _row_gather(pos_ref, (k + 1) * tm, y3_hbm, ybuf.at[1 - slot], sem.at[1 - slot])

    o_ref[...] = x1_ref[...] + _from_landing(ybuf.at[slot])


def _combine(y3, pos, x1, tm):
    t = x1.shape[0]
    return pl.pallas_call(
        functools.partial(_combine_kernel, tm=tm),
        grid_spec=pltpu.PrefetchScalarGridSpec(
            num_scalar_prefetch=1,
            grid=(t // tm,),
            in_specs=[pl.BlockSpec(memory_space=pl.ANY), pl.BlockSpec((tm, D_MODEL), lambda k, ps: (k, 0))],
            out_specs=pl.BlockSpec((tm, D_MODEL), lambda k, ps: (k, 0)),
            scratch_shapes=[pltpu.VMEM((2,) + _landing_shape(tm), F32), pltpu.SemaphoreType.DMA((2,))]),
        out_shape=jax.ShapeDtypeStruct((t, D_MODEL), F32),
        compiler_params=_cparams(("arbitrary",)),
        name="combine",
    )(pos, y3, x1)


def _moe_kernel(h_ref, comb_ref, x1_ref, wg_ref, wu_ref, wd_ref, o_ref, act):
    g = pl.program_id(1)
    comb = comb_ref[0]
    y = _group_experts(h_ref[...], lambda e: comb[:, e:e + 1], wg_ref, wu_ref, wd_ref, act)

    @pl.when(g == 0)
    def _():
        o_ref[...] = x1_ref[...] + y

    @pl.when(g != 0)
    def _():
        o_ref[...] += y


def _moe(h2, comb, x1, tm, wg, wu, wd):
    t = h2.shape[0]
    return pl.pallas_call(
        _moe_kernel,
        grid=(t // tm, N_GROUPS),
        in_specs=[pl.BlockSpec((tm, D_MODEL), lambda i, g: (i, 0)),
                  pl.BlockSpec((1, tm, E_PER_GROUP), lambda i, g: (g, i, 0)),
                  pl.BlockSpec((tm, D_MODEL), lambda i, g: (i, 0)),
                  pl.BlockSpec((1, E_PER_GROUP, D_MODEL, D_EXPERT), lambda i, g: (g, 0, 0, 0)),
                  pl.BlockSpec((1, E_PER_GROUP, D_MODEL, D_EXPERT), lambda i, g: (g, 0, 0, 0)),
                  pl.BlockSpec((1, E_PER_GROUP * D_EXPERT, D_MODEL), lambda i, g: (g, 0, 0))],
        out_specs=pl.BlockSpec((tm, D_MODEL), lambda i, g: (i, 0)),
        out_shape=jax.ShapeDtypeStruct((t, D_MODEL), F32),
        scratch_shapes=[pltpu.VMEM((tm, E_PER_GROUP * D_EXPERT), BF16)],
        compiler_params=_cparams(("parallel", "arbitrary")),
        name="moe",
    )(h2, comb, x1, wg, wu, wd)


def _rope_tables(pos):
    half = DK_R // 2
    inv = ROPE_BASE ** (-jnp.arange(half, dtype=F32) / half)
    ang = pos.astype(F32)[:, None] * inv[None, :]
    c = jnp.cos(ang)
    s = jnp.sin(ang)
    return jnp.concatenate([c, c, c, c], axis=-1), jnp.concatenate([-s, s, -s, s], axis=-1)


def kernel(x_prompt, x_sample, state_ret, cache_fox_k, cache_fox_v, cache_fox_logf, cache_mem_k, cache_mem_v,
           mem_prompt, g_norm1, w_in, b_forget, g_fox_q, g_fox_k, g_mem_q, g_mem_in, w_mem_kv, g_mem_k,
           g_ret_out, w_br_ret, w_br_fox, w_br_mem, w_out, g_norm2, w_route_group, b_route_group,
           w_route_expert, b_route_expert, w_exp_gate, w_exp_up, w_exp_down):
    nb, seq, _ = x_prompt.shape
    nbs, n_new, _ = x_sample.shape
    past = cache_fox_k.shape[2]
    l = 0

    wi = w_in[l]
    o_ff = 2 * RQK_W + 2 * RV_W + 3 * FOX_W
    w_in_r = (wi[:, :o_ff].astype(BF16), wi[:, o_ff + H_F:].astype(BF16),
              jnp.pad(wi[:, o_ff:o_ff + H_F], ((0, 0), (0, LANES - H_F))).astype(BF16))
    bf_pad = jnp.concatenate([b_forget[l], jnp.zeros((LANES - H_F,), F32)])[None, :]
    g1 = g_norm1[l][None, :]
    gfq = jnp.tile(g_fox_q[l], H_F)[None, :]
    gfk = jnp.tile(g_fox_k[l], H_F)[None, :]
    gmq = g_mem_q[l][None, :]
    hid = jnp.arange(MXU_DIM) // D_F
    bd = jnp.where(hid[:, None] == hid[None, :], 1.0 / D_F, 0.0).astype(BF16)
    wr = w_br_ret[l].astype(BF16)
    wf = w_br_fox[l].astype(BF16)
    wm = w_br_mem[l].astype(BF16)
    wo = w_out[l].astype(BF16)
    g2 = g_norm2[l][None, :]
    n_e = N_GROUPS * E_PER_GROUP
    wrt = jnp.concatenate([w_route_expert[l], w_route_group[l],
                           jnp.zeros((D_MODEL, LANES - n_e - N_GROUPS), F32)], axis=1)
    wrt_hi = wrt.astype(BF16)
    wrt2 = jnp.concatenate([wrt_hi, (wrt - wrt_hi.astype(F32)).astype(BF16)], axis=1)
    brt = jnp.concatenate([b_route_expert[l], b_route_group[l],
                           jnp.zeros((LANES - n_e - N_GROUPS,), F32)])[None, :]
    wg = w_exp_gate[l].astype(BF16)
    wu = w_exp_up[l].astype(BF16)
    wd = w_exp_down[l].astype(BF16).reshape(N_GROUPS, E_PER_GROUP * D_EXPERT, D_MODEL)
    gro = g_ret_out[l][None, :]
    prep_w = (g1, w_in_r, bf_pad, gfq, gfk, gmq, bd)
    merge_w = (wr, wf, wm, wo, g2, wrt2, brt)

    tm = 512
    xp = x_prompt.reshape(nb * seq, D_MODEL)
    (rq, rk, rv, rg, fqt, fkt, fvt, lft, mq, gates, fkb) = _prep(
        xp, _rope_tables(jnp.arange(seq)), tm, seq // tm, True, *prep_w)
    o_r, s_fin = _retention(rq, rk, rv, rg, nb, seq, 256, gro, None)
    fcum = _cumsum(lft, 256).reshape(nb * (H_F // 2), 2, seq)
    o_f = _fox_prompt(fqt, fkb, fvt, fcum, nb, seq, 256)
    mk, mv = _memkv(mem_prompt.reshape(nb * N_MEM, D_MODEL), nb, g_mem_in[l][None, :],
                    w_mem_kv[l].astype(BF16), g_mem_k[l][None, :])
    x1, h3, route, counts = _merge(xp, o_r, o_f, (mq, mk, mv, seq // tm), gates, tm, True, *merge_w)
    t_p = nb * seq
    n_tiles = t_p // tm + N_GROUPS
    n_cls = N_GROUPS * N_PAIRS
    cnt = counts[:, :n_cls, 0].astype(jnp.int32)
    before_tile = jnp.cumsum(cnt, axis=0) - cnt
    cls_tot = jnp.sum(cnt, axis=0)
    grp_tot = jnp.sum(cls_tot.reshape(N_GROUPS, N_PAIRS), axis=1)
    tiles_g = (grp_tot + tm - 1) // tm
    tile_end = jnp.cumsum(tiles_g)
    row_start = (tile_end - tiles_g) * tm
    in_grp = cls_tot.reshape(N_GROUPS, N_PAIRS)
    cls_start = (row_start[:, None] + jnp.cumsum(in_grp, axis=1) - in_grp).reshape(n_cls)
    first_row = jnp.pad((cls_start[None, :] + before_tile).astype(F32), ((0, 0), (0, LANES - n_cls)))
    pos = _positions(route, jnp.broadcast_to(first_row[:, :, None], (t_p // tm, LANES, LANES)),
                     min(8, t_p // tm))
    tile_ids = jnp.arange(n_tiles, dtype=jnp.int32)
    tile_group = jnp.minimum(jnp.sum((tile_ids[:, None] >= tile_end[None, :]).astype(jnp.int32), axis=1),
                             N_GROUPS - 1)
    y3 = _moe_sorted(h3.reshape(t_p, N_SLABS, LANES), tile_group, tile_end[-1:], pos, n_tiles, tm,
                     wrt2, brt, wg, wu, wd)
    y_prompt = _combine(y3.reshape(n_tiles * tm, N_SLABS, LANES), pos, x1, 2 * tm).reshape(nb, seq, D_MODEL)

    ts = nbs * n_new
    xs = x_sample.reshape(ts, D_MODEL)
    pos_s = jnp.tile(past + jnp.arange(n_new), nbs)
    (rq_s, rk_s, rv_s, rg_s, fq_s, fk_s, fv_s, lf_s, mq_s, gates_s) = _prep(
        xs, _rope_tables(pos_s), ts, 1, False, *prep_w)
    o_r_s, s_new = _retention(rq_s, rk_s, rv_s, rg_s, nbs, n_new, n_new, gro, state_ret[l])
    pad = (-(past + n_new)) % 256
    lf_rows = jnp.concatenate([jnp.swapaxes(cache_fox_logf[l], 1, 2),
                               jnp.swapaxes(lf_s.reshape(nbs, n_new, H_F), 1, 2),
                               jnp.zeros((nbs, H_F, pad), F32)], axis=2).reshape(nbs * H_F, past + n_new + pad)
    f_all = _cumsum(lf_rows, 256)
    feat_major = lambda c: jnp.transpose(c, (0, 2, 3, 1)).reshape(nbs * FOX_W, past)
    o_f_s = _fox_sample(fq_s, feat_major(cache_fox_k[l]), feat_major(cache_fox_v[l]), fk_s, fv_s,
                        f_all[:, :past].reshape(nbs, H_F, past),
                        f_all[:, past:past + n_new].reshape(nbs, H_F, n_new), nbs, n_new, past)
    o_m_s = _memattn(mq_s, cache_mem_k[l].reshape(nbs * N_MEM, MEM_W),
                     cache_mem_v[l].reshape(nbs * N_MEM, MEM_W), nbs, n_new, n_new)
    x1_s, h2_s, comb_s = _merge(xs, o_r_s, o_f_s, o_m_s, gates_s, ts, False, *merge_w)
    y_sample = _moe(h2_s, comb_s, x1_s, ts, wg, wu, wd).reshape(nbs, n_new, D_MODEL)

    token_major = lambda a: jnp.transpose(a.reshape(nb, H_F, D_F, seq), (0, 3, 1, 2))

    return (y_prompt, y_sample,
            s_fin[None], token_major(fkt)[None], token_major(fvt)[None],
            jnp.swapaxes(lft.reshape(nb, H_F, seq), 1, 2)[None],
            mk.reshape(1, nb, N_MEM, H_M, D_M), mv.reshape(1, nb, N_MEM, H_M, D_M),
            s_new[None], fk_s.reshape(1, nbs, n_new, H_F, D_F), fv_s.reshape(1, nbs, n_new, H_F, D_F),
            lf_s.reshape(1, nbs, n_new, H_F))
```

```python
import functools

import jax
import jax.numpy as jnp
from jax import lax
from jax.experimental import pallas as pl
from jax.experimental.pallas import tpu as pltpu

F32 = jnp.float32
BF16 = jnp.bfloat16

D_MODEL = 1024
H_R, DK_R, DV_R = 4, 64, 128
H_F, D_F = 8, 64
H_M, D_M = 4, 128
N_MEM = 256
N_GROUPS, E_PER_GROUP, D_EXPERT = 4, 8, 256
ROPE_BASE = 10000.0
EPS = 1e-6
LOG2E = 1.4426950408889634

RQK_W = H_R * DK_R
RV_W = H_R * DV_R
FOX_W = H_F * D_F
MEM_W = H_M * D_M
GATE_W = 3 * D_MODEL
LANES = 128
MXU_DIM = 256
TOKEN_TILE = 512

C_RQ = 0
C_RK = C_RQ + RQK_W
C_RV = C_RK + RQK_W
C_RG = C_RV + RV_W
C_FQ = C_RG + RV_W
C_FK = C_FQ + FOX_W
C_FV = C_FK + FOX_W
C_MQ = C_FV + FOX_W
C_GT = C_MQ + MEM_W
C_FF = C_GT + GATE_W

VMEM_LIMIT = 56 * 1024 * 1024


def _cparams(sem):
    return pltpu.CompilerParams(dimension_semantics=sem, vmem_limit_bytes=VMEM_LIMIT)


def _const_spec(shape):
    nd = len(shape)
    return pl.BlockSpec(shape, lambda *_: (0,) * nd, pipeline_mode=pl.Buffered(1))


def _dot(a, b):
    return jnp.dot(a, b, preferred_element_type=F32)


def _dot_nt(a, b):
    return lax.dot_general(a, b, (((1,), (1,)), ((), ())), preferred_element_type=F32)


def _dot_tn(a, b):
    return lax.dot_general(a, b, (((0,), (0,)), ((), ())), preferred_element_type=F32)


def _lane_rmsnorm(z):
    return z * lax.rsqrt(jnp.mean(z * z, axis=-1, keepdims=True) + EPS)


def _prep_kernel(x_ref, g1_ref, wa_ref, wb_ref, wf_ref, cos_ref, sin_ref, bf_ref, gfq_ref, gfk_ref, gmq_ref, bd_ref,
                 rq_ref, rk_ref, rv_ref, rg_ref, fq_ref, fk_ref, fv_ref, lf_ref, mq_ref, gt_ref, *extra,
                 seq_minor):
    x = x_ref[...]
    h = (x * lax.rsqrt(jnp.mean(x * x, axis=-1, keepdims=True) + EPS) * g1_ref[...]).astype(BF16)

    def proj(c0, width):
        if c0 < C_MQ:
            return _dot(h, wa_ref[:, c0:c0 + width])
        if c0 < C_FF:
            return _dot(h, wb_ref[:, c0 - C_MQ:c0 - C_MQ + width])
        return _dot(h, wf_ref[...])

    cos = cos_ref[...]
    sin = sin_ref[...]
    lane = lax.broadcasted_iota(jnp.int32, cos.shape, 1)
    first_half = (lane % DK_R) < (DK_R // 2)

    def rope(z):
        swapped = jnp.where(first_half, pltpu.roll(z, LANES - DK_R // 2, 1), pltpu.roll(z, DK_R // 2, 1))
        return z * cos + swapped * sin

    v = proj(C_FF, LANES) + bf_ref[...]
    lf = jnp.minimum(v, 0.0) - jnp.log1p(jnp.exp(-jnp.abs(v)))
    if seq_minor:
        lf_ref[...] = lf.T[:H_F, :]
    else:
        lf_ref[...] = lf[:, :H_F]

    def head64_norm(z, g_ref):
        zz = (z * z).astype(BF16)
        w = bd_ref.shape[0]
        ms = jnp.concatenate([_dot(zz[:, c:c + w], bd_ref[...]) for c in range(0, FOX_W, w)], axis=1)
        return z * lax.rsqrt(ms + EPS) * g_ref[...]

    fq = head64_norm(proj(C_FQ, FOX_W), gfq_ref) * (D_F ** -0.5 * LOG2E)
    fk = head64_norm(proj(C_FK, FOX_W), gfk_ref)
    fv = proj(C_FV, FOX_W)
    if seq_minor:
        fq_ref[...] = fq.T.astype(BF16)
        fk_ref[...] = fk.T
        fv_ref[...] = fv.T
        extra[0][...] = fk.astype(BF16)
    else:
        fq_ref[...] = fq.astype(BF16)
        fk_ref[...] = fk
        fv_ref[...] = fv

    zm = proj(C_MQ, MEM_W)
    for hh in range(H_M):
        sl = slice(hh * D_M, (hh + 1) * D_M)
        mq_ref[:, sl] = (_lane_rmsnorm(zm[:, sl]) * gmq_ref[...]).astype(BF16)

    for b in range(3):
        gt_ref[:, b * D_MODEL:(b + 1) * D_MODEL] = jax.nn.sigmoid(proj(C_GT + b * D_MODEL, D_MODEL)).astype(BF16)

    zq = proj(C_RQ, RQK_W)
    zk = proj(C_RK, RQK_W)
    for p in range(RQK_W // LANES):
        sl = slice(p * LANES, (p + 1) * LANES)
        rq_ref[:, sl] = rope(zq[:, sl])
        rk_ref[:, sl] = rope(zk[:, sl]) * (DK_R ** -0.5)
    rv_ref[...] = proj(C_RV, RV_W)
    rg_ref[...] = proj(C_RG, RV_W)


def _prep(x2d, tables, tm, n_pos_tiles, seq_minor, g1, w_in_r, bf_pad, gfq, gfk, gmq, bd):
    t = x2d.shape[0]
    cos_t, sin_t = tables
    row = lambda w: pl.BlockSpec((tm, w), lambda i: (i, 0))
    pos = pl.BlockSpec((tm, LANES), lambda i: (i % n_pos_tiles, 0))
    sds = jax.ShapeDtypeStruct
    if seq_minor:
        nb, seq = t // (n_pos_tiles * tm), n_pos_tiles * tm
        fm = lambda rows_: pl.BlockSpec((rows_, tm), lambda i: (i // n_pos_tiles, i % n_pos_tiles))
        fox_shapes = [sds((nb * FOX_W, seq), BF16), sds((nb * FOX_W, seq), F32), sds((nb * FOX_W, seq), F32),
                      sds((nb * H_F, seq), F32)]
        fox_specs = [fm(FOX_W), fm(FOX_W), fm(FOX_W), fm(H_F)]
        extra_shapes, extra_specs = [sds((t, FOX_W), BF16)], [row(FOX_W)]
    else:
        fox_shapes = [sds((t, FOX_W), BF16), sds((t, FOX_W), F32), sds((t, FOX_W), F32), sds((t, H_F), F32)]
        fox_specs = [row(FOX_W), row(FOX_W), row(FOX_W), row(H_F)]
        extra_shapes, extra_specs = [], []
    out_shapes = [sds((t, RQK_W), F32), sds((t, RQK_W), F32), sds((t, RV_W), F32), sds((t, RV_W), F32),
                  *fox_shapes, sds((t, MEM_W), BF16), sds((t, GATE_W), BF16), *extra_shapes]
    out_specs = [row(RQK_W), row(RQK_W), row(RV_W), row(RV_W), *fox_specs, row(MEM_W), row(GATE_W),
                 *extra_specs]
    return pl.pallas_call(
        functools.partial(_prep_kernel, seq_minor=seq_minor),
        grid=(t // tm,),
        in_specs=[row(D_MODEL), _const_spec((1, D_MODEL)), _const_spec((D_MODEL, C_MQ)),
                  _const_spec((D_MODEL, C_FF - C_MQ)), _const_spec((D_MODEL, LANES)), pos, pos,
                  _const_spec((1, LANES)), _const_spec((1, FOX_W)), _const_spec((1, FOX_W)),
                  _const_spec((1, D_M)), _const_spec((MXU_DIM, MXU_DIM))],
        out_specs=out_specs,
        out_shape=out_shapes,
        compiler_params=_cparams(("parallel",)),
        name="prep",
    )(x2d, g1, *w_in_r, cos_t, sin_t, bf_pad, gfq, gfk, gmq, bd)


def _ret_kernel(*refs, has_init, ch):
    if has_init:
        (rq_ref, rk_ref, rv_ref, rg_ref, dmat_ref, qdec_ref, kdec_ref, gpow_ref, gro_ref, s0_ref,
         o_ref, sfin_ref) = refs
    else:
        (rq_ref, rk_ref, rv_ref, rg_ref, dmat_ref, qdec_ref, kdec_ref, gpow_ref, gro_ref,
         o_ref, sfin_ref) = refs

    state = []
    for h in range(H_R):
        if has_init:
            pad = jnp.zeros((DK_R, DV_R), F32)
            state.append(jnp.concatenate([s0_ref[0, h], pad] if h % 2 == 0 else [pad, s0_ref[0, h]], axis=0))
        else:
            state.append(jnp.zeros((LANES, DV_R), F32))

    lane = lax.broadcasted_iota(jnp.int32, (ch, LANES), 1)
    for c in range(rq_ref.shape[0] // ch):
        rows = slice(c * ch, (c + 1) * ch)
        for p in range(H_R // 2):
            sl = slice(p * LANES, (p + 1) * LANES)
            q2 = rq_ref[rows, sl]
            k2 = rk_ref[rows, sl]
            kd2 = k2 * kdec_ref[p]
            for hh in range(2):
                h = 2 * p + hh
                mine = (lane >= hh * DK_R) & (lane < (hh + 1) * DK_R)
                qm = jnp.where(mine, q2, 0.0).astype(BF16)
                kdm = jnp.where(mine, kd2, 0.0).astype(BF16)
                v = rv_ref[rows, h * DV_R:(h + 1) * DV_R].astype(BF16)
                sc = _dot_nt(qm, k2.astype(BF16)) * dmat_ref[h]
                o = _dot(sc.astype(BF16), v) + _dot(qm, state[h].astype(BF16)) * qdec_ref[h]
                state[h] = gpow_ref[h] * state[h] + _dot_tn(kdm, v)
                normed = _lane_rmsnorm(o) * gro_ref[...]
                rg = rg_ref[rows, h * DV_R:(h + 1) * DV_R]
                o_ref[rows, h * DV_R:(h + 1) * DV_R] = (normed * (rg * jax.nn.sigmoid(rg))).astype(BF16)

    for h in range(H_R):
        r0 = DK_R * (h % 2)
        sfin_ref[0, h] = state[h][r0:r0 + DK_R, :]


def _retention(rq, rk, rv, rg, n_batch, seq, ch, g_ret_out, state0):
    lg = jnp.log1p(-jnp.exp2(-5.0 - jnp.arange(H_R, dtype=F32)))
    idx = jnp.arange(ch, dtype=F32)
    diff = idx[:, None] - idx[None, :]
    causal = diff >= 0
    dmat = jnp.where(causal[None], jnp.exp(jnp.where(causal, diff, 0.0)[None] * lg[:, None, None]), 0.0)
    q_dec = jnp.exp((idx + 1.0)[None, :] * lg[:, None])
    k_dec = jnp.exp((ch - 1.0 - idx)[None, :] * lg[:, None])
    qdec = jnp.broadcast_to(q_dec[:, :, None], (H_R, ch, DV_R))
    kdec = jnp.broadcast_to(k_dec[:, :, None], (H_R, ch, DK_R))
    kdec = kdec.reshape(H_R // 2, 2, ch, DK_R).transpose(0, 2, 1, 3).reshape(H_R // 2, ch, LANES)
    gpow = jnp.broadcast_to(jnp.exp(ch * lg)[:, None, None], (H_R, 1, DV_R))

    has_init = state0 is not None
    blk = lambda w: pl.BlockSpec((seq, w), lambda b: (b, 0))
    in_specs = [blk(RQK_W), blk(RQK_W), blk(RV_W), blk(RV_W),
                _const_spec((H_R, ch, ch)), _const_spec((H_R, ch, DV_R)),
                _const_spec((H_R // 2, ch, LANES)), _const_spec((H_R, 1, DV_R)), _const_spec((1, DV_R))]
    args = [rq, rk, rv, rg, dmat, qdec, kdec, gpow, g_ret_out]
    if has_init:
        in_specs.append(pl.BlockSpec((1, H_R, DK_R, DV_R), lambda b: (b, 0, 0, 0)))
        args.append(state0)
    return pl.pallas_call(
        functools.partial(_ret_kernel, has_init=has_init, ch=ch),
        grid=(n_batch,),
        in_specs=in_specs,
        out_specs=[blk(RV_W), pl.BlockSpec((1, H_R, DK_R, DV_R), lambda b: (b, 0, 0, 0))],
        out_shape=[jax.ShapeDtypeStruct((n_batch * seq, RV_W), BF16),
                   jax.ShapeDtypeStruct((n_batch, H_R, DK_R, DV_R), F32)],
        compiler_params=_cparams(("parallel",)),
        name="retention",
    )(*args)


def _split3(x):
    hi = x.astype(BF16)
    r1 = x - hi.astype(F32)
    mid = r1.astype(BF16)
    lo = (r1 - mid.astype(F32)).astype(BF16)
    return hi, mid, lo


def _cumsum_kernel(x_ref, o_ref, *, blk):
    rows, n = x_ref.shape
    r = lax.broadcasted_iota(jnp.int32, (blk, blk), 0)
    c = lax.broadcasted_iota(jnp.int32, (blk, blk), 1)
    tri = jnp.where(r <= c, 1.0, 0.0).astype(BF16)
    carry = jnp.zeros((rows, 1), F32)
    for i in range(n // blk):
        hi, mid, lo = _split3(x_ref[:, i * blk:(i + 1) * blk])
        cum = (_dot(hi, tri) + _dot(mid, tri)) + _dot(lo, tri) + carry
        o_ref[:, i * blk:(i + 1) * blk] = cum
        carry = cum[:, blk - 1:blk]


def _cumsum(x, blk):
    r, t = x.shape
    spec = pl.BlockSpec((H_F, t), lambda b: (b, 0))
    return pl.pallas_call(
        functools.partial(_cumsum_kernel, blk=blk),
        grid=(r // H_F,),
        in_specs=[spec], out_specs=spec,
        out_shape=jax.ShapeDtypeStruct(x.shape, F32),
        compiler_params=_cparams(("parallel",)),
        name="cumsum",
    )(x)


def _head_rows_mask(shape, hh):
    sub = lax.broadcasted_iota(jnp.int32, shape, 0)
    return (sub >= hh * D_F) & (sub < (hh + 1) * D_F)


N_AUG = 3
V_ROWS = D_F + 16


def _fox_kernel(qt_ref, k_ref, vt_ref, fcum_ref, o_ref, qa, ka, vb, *, tq):
    seq = k_ref.shape[0]
    tk = tq
    k = k_ref[...].astype(F32)
    lane = lax.broadcasted_iota(jnp.int32, k.shape, 1)
    aug_rows = lax.broadcasted_iota(jnp.int32, (D_F, seq), 0) < N_AUG
    for hh in range(2):
        own, oth = hh * D_F, (1 - hh) * D_F
        qa[hh, own:own + D_F, :] = qt_ref[own:own + D_F, :]
        qa[hh, oth:oth + D_F, :] = jnp.where(aug_rows, 1.0, 0.0).astype(BF16)
        pieces = [p.astype(F32) for p in _split3(fcum_ref[0, hh:hh + 1, :] * (-LOG2E))]
        ka[hh] = jnp.where((lane >= own) & (lane < own + D_F), k, _rows_to_columns(pieces, oth)).astype(BF16)
    for hh in range(2):
        vb[hh, :D_F, :] = vt_ref[hh * D_F:(hh + 1) * D_F, :].astype(BF16)
        vb[hh, D_F:, :] = jnp.ones((V_ROWS - D_F, seq), BF16)
    causal = (lax.broadcasted_iota(jnp.int32, (tk, tq), 1) >= lax.broadcasted_iota(jnp.int32, (tk, tq), 0))

    pairs = [(i, j) for i in range(seq // tq) for j in range(i + 1)]

    def stage_a(i, j):
        return tuple(_dot(ka[hh, j * tk:(j + 1) * tk, :], qa[hh, :, i * tq:(i + 1) * tq]) for hh in range(2))

    def stage_b(i, j, ts, maxes):
        new_maxes, probs = [], []
        for hh in range(2):
            t = jnp.where(causal, ts[hh], -jnp.inf) if j == i else ts[hh]
            t_max = jnp.max(t, axis=0, keepdims=True)
            if j == 0:
                m_new, alpha = t_max, None
            else:
                m_new = jnp.maximum(maxes[hh], t_max)
                alpha = jnp.exp2(maxes[hh] - m_new)
            new_maxes.append(m_new)
            probs.append((jnp.exp2(t - m_new).astype(BF16), alpha))
        return new_maxes, probs

    def stage_c(i, j, probs, accs):
        out = []
        for hh in range(2):
            p, alpha = probs[hh]
            pv = _dot(vb[hh, :, j * tk:(j + 1) * tk], p)
            out.append(pv if j == 0 else alpha * accs[hh] + pv)
        return out

    scores, probs, maxes, accs = {}, {}, None, None
    for s in range(len(pairs) + 2):
        if s < len(pairs):
            scores[s] = stage_a(*pairs[s])
        if 0 <= s - 1 < len(pairs):
            maxes, probs[s - 1] = stage_b(*pairs[s - 1], scores.pop(s - 1), maxes)
        if 0 <= s - 2 < len(pairs):
            i, j = pairs[s - 2]
            accs = stage_c(i, j, probs.pop(s - 2), accs)
            if j == i:
                o_t = jnp.concatenate([a[:D_F] / a[D_F:D_F + 1] for a in accs], axis=0)
                o_ref[i * tq:(i + 1) * tq, :] = o_t.T.astype(BF16)


def _fox_prompt(fqt, fkb, fvt, fcum, n_batch, seq, tq):
    npair = H_F // 2
    fm_spec = pl.BlockSpec((LANES, seq), lambda b, p: (b * npair + p, 0))
    tok_spec = pl.BlockSpec((seq, LANES), lambda b, p: (b, p))
    return pl.pallas_call(
        functools.partial(_fox_kernel, tq=tq),
        grid=(n_batch, npair),
        in_specs=[fm_spec, tok_spec, fm_spec, pl.BlockSpec((1, 2, seq), lambda b, p: (b * npair + p, 0, 0))],
        out_specs=tok_spec,
        out_shape=jax.ShapeDtypeStruct((n_batch * seq, FOX_W), BF16),
        scratch_shapes=[pltpu.VMEM((2, LANES, seq), BF16), pltpu.VMEM((2, seq, LANES), BF16),
                        pltpu.VMEM((2, V_ROWS, seq), BF16)],
        compiler_params=_cparams(("parallel", "parallel")),
        name="fox_prompt",
    )(fqt, fkb, fvt, fcum)


def _fox_sample_kernel(q_ref, kct_ref, vct_ref, kn_ref, vn_ref, frc_ref, frn_ref, o_ref):
    n = q_ref.shape[0]
    lane = lax.broadcasted_iota(jnp.int32, (n, LANES), 1)
    rows = lax.broadcasted_iota(jnp.int32, (n, n), 0)
    cols = lax.broadcasted_iota(jnp.int32, (n, n), 1)
    for pair in range(H_F // 2):
        sl = slice(pair * LANES, (pair + 1) * LANES)
        q2 = q_ref[:, sl]
        kct = kct_ref[sl, :]
        vct = vct_ref[sl, :].astype(BF16)
        kn = kn_ref[:, sl].astype(BF16)
        vn = vn_ref[:, sl].astype(BF16)
        outs = []
        for hh in range(2):
            h = 2 * pair + hh
            mine = (lane >= hh * D_F) & (lane < (hh + 1) * D_F)
            kc = jnp.where(_head_rows_mask(kct.shape, hh), kct, 0.0).astype(BF16)
            t_c = _dot(q2, kc) - frc_ref[0, h:h + 1, :] * LOG2E
            t_n = _dot_nt(jnp.where(mine, q2, jnp.zeros_like(q2)), kn) - frn_ref[0, h:h + 1, :] * LOG2E
            t_n = jnp.where(rows >= cols, t_n, -jnp.inf)
            m = jnp.maximum(jnp.max(t_c, axis=-1, keepdims=True), jnp.max(t_n, axis=-1, keepdims=True))
            p_c = jnp.exp2(t_c - m)
            p_n = jnp.exp2(t_n - m)
            l = jnp.sum(p_c, axis=-1, keepdims=True) + jnp.sum(p_n, axis=-1, keepdims=True)
            acc = _dot_nt(p_c.astype(BF16), vct) + _dot(p_n.astype(BF16), vn)
            outs.append(acc / l)
        o_ref[:, sl] = jnp.where(lane < D_F, outs[0], outs[1]).astype(BF16)


def _fox_sample(fq, kct, vct, kn, vn, frow_c, frow_n, n_batch, n_new, past):
    tok = pl.BlockSpec((n_new, FOX_W), lambda b: (b, 0))
    cache = pl.BlockSpec((FOX_W, past), lambda b: (b, 0))
    return pl.pallas_call(
        _fox_sample_kernel,
        grid=(n_batch,),
        in_specs=[tok, cache, cache, tok, tok,
                  pl.BlockSpec((1, H_F, past), lambda b: (b, 0, 0)),
                  pl.BlockSpec((1, H_F, n_new), lambda b: (b, 0, 0))],
        out_specs=tok,
        out_shape=jax.ShapeDtypeStruct((n_batch * n_new, FOX_W), BF16),
        compiler_params=_cparams(("parallel",)),
        name="fox_sample",
    )(fq, kct, vct, kn, vn, frow_c, frow_n)


def _memkv_kernel(m_ref, gin_ref, w_ref, gk_ref, k_ref, v_ref):
    x = m_ref[...]
    h = (x * lax.rsqrt(jnp.mean(x * x, axis=-1, keepdims=True) + EPS) * gin_ref[...]).astype(BF16)
    zk = _dot(h, w_ref[:, :MEM_W])
    for hh in range(H_M):
        sl = slice(hh * D_M, (hh + 1) * D_M)
        k_ref[:, sl] = _lane_rmsnorm(zk[:, sl]) * gk_ref[...]
    v_ref[...] = _dot(h, w_ref[:, MEM_W:])


def _memkv(mem2d, n_batch, g_mem_in, w_mem_kv, g_mem_k):
    blk = lambda w: pl.BlockSpec((N_MEM, w), lambda b: (b, 0))
    return pl.pallas_call(
        _memkv_kernel,
        grid=(n_batch,),
        in_specs=[blk(D_MODEL), _const_spec((1, D_MODEL)), _const_spec((D_MODEL, 2 * MEM_W)),
                  _const_spec((1, D_M))],
        out_specs=[blk(MEM_W), blk(MEM_W)],
        out_shape=[jax.ShapeDtypeStruct((n_batch * N_MEM, MEM_W), F32)] * 2,
        compiler_params=_cparams(("parallel",)),
        name="memkv",
    )(mem2d, g_mem_in, w_mem_kv, g_mem_k)


def _mem_attend(q_ref, k_ref, v_ref):
    for hh in range(H_M):
        sl = slice(hh * D_M, (hh + 1) * D_M)
        s = _dot_nt(q_ref[:, sl], k_ref[:, sl].astype(BF16)) * (D_M ** -0.5)
        m = jnp.max(s, axis=-1, keepdims=True)
        p = jnp.exp(s - m)
        l = jnp.sum(p, axis=-1, keepdims=True)
        yield sl, (_dot(p.astype(BF16), v_ref[:, sl].astype(BF16)) / l).astype(BF16)


def _memattn_kernel(q_ref, k_ref, v_ref, o_ref):
    for sl, o in _mem_attend(q_ref, k_ref, v_ref):
        o_ref[:, sl] = o


def _memattn(mq, mk, mv, n_batch, seq, tq):
    nq = seq // tq
    return pl.pallas_call(
        _memattn_kernel,
        grid=(n_batch, nq),
        in_specs=[pl.BlockSpec((tq, MEM_W), lambda b, i: (b * nq + i, 0)),
                  pl.BlockSpec((N_MEM, MEM_W), lambda b, i: (b, 0)),
                  pl.BlockSpec((N_MEM, MEM_W), lambda b, i: (b, 0))],
        out_specs=pl.BlockSpec((tq, MEM_W), lambda b, i: (b * nq + i, 0)),
        out_shape=jax.ShapeDtypeStruct((n_batch * seq, MEM_W), BF16),
        compiler_params=_cparams(("parallel", "arbitrary")),
        name="memattn",
    )(mq, mk, mv)


def _split2(x):
    hi = x.astype(BF16)
    return hi, (x - hi.astype(F32)).astype(BF16)


N_EXPERTS = N_GROUPS * E_PER_GROUP


N_PAIRS = E_PER_GROUP * (E_PER_GROUP - 1) // 2


def _router_logits(h2, wrt_ref, brt_ref):
    h_hi, h_lo = _split2(h2)
    z = _dot(h_hi, wrt_ref[...])
    return (z[:, :LANES] + z[:, LANES:]) + _dot(h_lo, wrt_ref[:, :LANES]) + brt_ref[...]


N_LOGIT_ROWS = 40


def _route(logits, group=None):
    lt = logits.T[:N_LOGIT_ROWS, :]
    idx = lax.broadcasted_iota(jnp.int32, lt.shape, 0).astype(F32)
    neg = -jnp.inf
    first_idx = lambda mask: jnp.min(jnp.where(mask, idx, float(LANES)), axis=0, keepdims=True)

    is_g = (idx >= N_EXPERTS) & (idx < N_EXPERTS + N_GROUPS)
    lg = jnp.where(is_g, lt, neg)
    mg = jnp.max(lg, axis=0, keepdims=True)
    if group is None:
        group = first_idx(lg == mg) - N_EXPERTS
        p_sel = 1.0 / jnp.sum(jnp.exp(lg - mg), axis=0, keepdims=True)
    else:
        lsel = jnp.max(jnp.where(idx == group + N_EXPERTS, lt, neg), axis=0, keepdims=True)
        p_sel = jnp.exp(lsel - mg) / jnp.sum(jnp.exp(lg - mg), axis=0, keepdims=True)

    in_grp = (idx >= group * E_PER_GROUP) & (idx < (group + 1) * E_PER_GROUP)
    le = jnp.where(in_grp, lt, neg)
    v1 = jnp.max(le, axis=0, keepdims=True)
    i1 = first_idx(le == v1)
    le2 = jnp.where(idx == i1, neg, le)
    v2 = jnp.max(le2, axis=0, keepdims=True)
    i2 = first_idx(le2 == v2)
    e2 = jnp.exp(v2 - v1)
    return group, i1, i2, p_sel / (1.0 + e2), p_sel * e2 / (1.0 + e2)


def _rows_to_columns(rows, first=0):
    n = rows[0].shape[1]
    idx = lax.broadcasted_iota(jnp.int32, (LANES, n), 0)
    stacked = jnp.zeros((LANES, n), F32)
    for r, row in enumerate(rows):
        stacked = jnp.where(idx == first + r, row, stacked)
    return stacked.T


def _merge_kernel(*refs, sorted_moe):
    if sorted_moe:
        (x_ref, or_ref, of_ref, mq_ref, mk_ref, mv_ref, gt_ref, wr_ref, wf_ref, wm_ref, wo_ref, g2_ref,
         wrt_ref, brt_ref, x1_ref, h2_ref, route_ref, *counts_ref) = refs
        o_m = jnp.concatenate([o for _, o in _mem_attend(mq_ref, mk_ref, mv_ref)], axis=1)
    else:
        (x_ref, or_ref, of_ref, om_ref, gt_ref, wr_ref, wf_ref, wm_ref, wo_ref, g2_ref,
         wrt_ref, brt_ref, x1_ref, h2_ref, route_ref) = refs
        o_m = om_ref[...]
    g = lambda b: gt_ref[:, b * D_MODEL:(b + 1) * D_MODEL].astype(F32)
    merged = (g(0) * _dot(or_ref[...], wr_ref[...]) + g(1) * _dot(of_ref[...], wf_ref[...])
              + g(2) * _dot(o_m, wm_ref[...]))
    x1 = x_ref[...] + _dot(merged.astype(BF16), wo_ref[...])
    x1_ref[...] = x1
    h2 = x1 * lax.rsqrt(jnp.mean(x1 * x1, axis=-1, keepdims=True) + EPS) * g2_ref[...]
    group, i1, i2, w1, w2 = _route(_router_logits(h2, wrt_ref, brt_ref))
    tm = h2.shape[0]
    if sorted_moe:
        _to_token_tiles(h2_ref, h2)
        e_lo = jnp.minimum(i1, i2) - group * E_PER_GROUP
        e_hi = jnp.maximum(i1, i2) - group * E_PER_GROUP
        cls = group * N_PAIRS + (e_lo * E_PER_GROUP - e_lo * (e_lo + 1.0) * 0.5 + (e_hi - e_lo - 1.0))
        cidx = lax.broadcasted_iota(jnp.int32, (LANES, tm), 0).astype(F32)
        onehot = jnp.where(cidx == cls, 1.0, 0.0)
        r = lax.broadcasted_iota(jnp.int32, (tm, tm), 0)
        c = lax.broadcasted_iota(jnp.int32, (tm, tm), 1)
        before = _dot(onehot.astype(BF16), jnp.where(r < c, 1.0, 0.0).astype(BF16))
        rank = jnp.sum(before * onehot, axis=0, keepdims=True)
        ridx = lax.broadcasted_iota(jnp.int32, route_ref.shape, 0)
        route_ref[...] = jnp.where(ridx == 0, group, jnp.where(ridx == 1, cls, jnp.where(ridx == 2, rank, 0.0)))
        counts_ref[0][0] = jnp.broadcast_to(jnp.sum(onehot, axis=1, keepdims=True), (LANES, LANES))
    else:
        h2_ref[...] = h2.astype(BF16)
        cols = _rows_to_columns([i1, i2, w1, w2])
        lane = lax.broadcasted_iota(jnp.int32, (tm, LANES), 1).astype(F32)
        comb = jnp.where(lane == cols[:, 0:1], cols[:, 2:3], 0.0) + jnp.where(lane == cols[:, 1:2], cols[:, 3:4], 0.0)
        for gi in range(N_GROUPS):
            route_ref[gi] = comb[:, gi * E_PER_GROUP:(gi + 1) * E_PER_GROUP]


def _merge(x2d, o_r, o_f, mem, gates, tm, sorted_moe, wr, wf, wm, wo, g2, wrt, brt):
    t = x2d.shape[0]
    row = lambda w: pl.BlockSpec((tm, w), lambda i: (i, 0))
    sds = jax.ShapeDtypeStruct
    if sorted_moe:
        h2_spec, h2_shape = pl.BlockSpec((tm * N_SLABS, LANES), lambda i: (i, 0)), sds((t * N_SLABS, LANES), F32)
        rt_specs = [pl.BlockSpec((SUBLANES, tm), lambda i: (i, 0)),
                    pl.BlockSpec((1, LANES, LANES), lambda i: (i, 0, 0))]
        rt_shapes = [sds((t // tm * SUBLANES, tm), F32), sds((t // tm, LANES, LANES), F32)]
        mq, mk, mv, tiles_per_seq = mem
        mem_args = [mq, mk, mv]
        mem_kv = pl.BlockSpec((N_MEM, MEM_W), lambda i: (i // tiles_per_seq, 0))
        mem_specs = [row(MEM_W), mem_kv, mem_kv]
    else:
        h2_spec, h2_shape = row(D_MODEL), sds((t, D_MODEL), BF16)
        rt_specs = [pl.BlockSpec((N_GROUPS, tm, E_PER_GROUP), lambda i: (0, i, 0))]
        rt_shapes = [sds((N_GROUPS, t, E_PER_GROUP), F32)]
        mem_args, mem_specs = [mem], [row(MEM_W)]
    return pl.pallas_call(
        functools.partial(_merge_kernel, sorted_moe=sorted_moe),
        grid=(t // tm,),
        in_specs=[row(D_MODEL), row(RV_W), row(FOX_W), *mem_specs, row(GATE_W),
                  _const_spec((RV_W, D_MODEL)), _const_spec((FOX_W, D_MODEL)), _const_spec((MEM_W, D_MODEL)),
                  _const_spec((D_MODEL, D_MODEL)), _const_spec((1, D_MODEL)),
                  _const_spec((D_MODEL, 2 * LANES)), _const_spec((1, LANES))],
        out_specs=[row(D_MODEL), h2_spec, *rt_specs],
        out_shape=[sds((t, D_MODEL), F32), h2_shape, *rt_shapes],
        compiler_params=_cparams(("parallel",)),
        name="merge",
    )(x2d, o_r, o_f, *mem_args, gates, wr, wf, wm, wo, g2, wrt, brt)


def _positions_kernel(route_ref, first_ref, pos_ref):
    tm = route_ref.shape[1]
    cidx = lax.broadcasted_iota(jnp.int32, (LANES, tm), 0).astype(F32)
    for i in range(first_ref.shape[0]):
        route = route_ref[i * SUBLANES:(i + 1) * SUBLANES, :]
        first = jnp.concatenate([first_ref[i]] * (tm // LANES), axis=1)
        start = jnp.sum(jnp.where(cidx == route[1:2, :], first, 0.0), axis=0, keepdims=True)
        pos_ref[i * SUBLANES:(i + 1) * SUBLANES, :] = jnp.broadcast_to(start + route[2:3, :],
                                                                       route.shape).astype(jnp.int32)


def _positions(route, first_row, tiles_per_step):
    rows, tm = route.shape
    n = rows // SUBLANES
    out = pl.pallas_call(
        _positions_kernel,
        grid=(n // tiles_per_step,),
        in_specs=[pl.BlockSpec((tiles_per_step * SUBLANES, tm), lambda i: (i, 0)),
                  pl.BlockSpec((tiles_per_step, LANES, LANES), lambda i: (i, 0, 0))],
        out_specs=pl.BlockSpec((tiles_per_step * SUBLANES, tm), lambda i: (i, 0)),
        out_shape=jax.ShapeDtypeStruct((rows, tm), jnp.int32),
        compiler_params=_cparams(("parallel",)),
        name="positions",
    )(route, first_row)
    return out.reshape(n, SUBLANES, tm)[:, 0, :].reshape(n * tm)


def _group_experts(h, cw, wg_ref, wu_ref, wd_ref, act):
    for e in range(E_PER_GROUP):
        a = _dot(h, wg_ref[0, e])
        u = _dot(h, wu_ref[0, e])
        act[:, e * D_EXPERT:(e + 1) * D_EXPERT] = ((a * jax.nn.sigmoid(a)) * u * cw(e)).astype(BF16)
    return _dot(act[...], wd_ref[0])


N_SLABS = D_MODEL // LANES
SUBLANES = 8


def _to_token_tiles(ref, x):
    for s in range(N_SLABS):
        ref[pl.ds(s, x.shape[0], stride=N_SLABS), :] = x[:, s * LANES:(s + 1) * LANES]


def _landing_shape(rows):
    return (rows // SUBLANES, N_SLABS, SUBLANES, LANES)


def _from_landing(buf):
    rows = buf.shape[0] * SUBLANES
    return jnp.concatenate([buf[:, s].reshape(rows, LANES) for s in range(N_SLABS)], axis=1)


def _row_gather(idx_ref, base, src_hbm, dst, sem, part=None):
    def body(i, _):
        for u in range(SUBLANES):
            pltpu.make_async_copy(src_hbm.at[idx_ref[base + i * SUBLANES + u]], dst.at[i, :, u, :],
                                  sem).start(priority=u % 2)
        return 0
    if part is None:
        lax.fori_loop(0, dst.shape[0], body, 0)
    else:
        j, n = part
        per = dst.shape[0] // n
        for i in range(j * per, (j + 1) * per):
            body(i, 0)


def _row_gather_wait(dst, sem):
    pltpu.make_async_copy(dst, dst, sem).wait()


def _moe_sorted_kernel(tg_ref, nvt_ref, pos_ref, h3_hbm, zeros_hbm, wrt_ref, brt_ref,
                       wg_ref, wu_ref, wd_ref, y_ref, xbuf, sem, hb, yacc, src_ref, clear_sem, *, tm):
    k = pl.program_id(0)
    nvt = nvt_ref[0]
    slot = lax.rem(k, 2)

    @pl.when(k == 0)
    def _():
        clear = pltpu.make_async_copy(zeros_hbm, src_ref, clear_sem)
        clear.start()
        clear.wait()

        def invert(t, _):
            src_ref[pos_ref[t]] = t
            return 0

        lax.fori_loop(0, pos_ref.shape[0], invert, 0, unroll=32)

    @pl.when((k == 0) & (nvt > 0))
    def _():
        _row_gather(src_ref, 0, h3_hbm, xbuf.at[0], sem.at[0])

    @pl.when(k < nvt)
    def _():
        _row_gather_wait(xbuf.at[slot], sem.at[slot])

        def fetch_part(e):
            _row_gather(src_ref, jnp.minimum(k + 1, nvt - 1) * tm, h3_hbm, xbuf.at[1 - slot],
                        sem.at[1 - slot], part=(e, E_PER_GROUP))

        x = _from_landing(xbuf.at[slot])
        group = tg_ref[k]
        _, i1, i2, w1, w2 = _route(_router_logits(x, wrt_ref, brt_ref), group.astype(F32))
        hb[...] = x.astype(BF16)
        yacc[...] = jnp.zeros(yacc.shape, F32)
        first = (group * E_PER_GROUP).astype(F32)
        used = [jnp.max(jnp.where(i1 == first + e, w1, 0.0) + jnp.where(i2 == first + e, w2, 0.0))
                for e in range(E_PER_GROUP)]
        cols = _rows_to_columns([i1, i2, w1, w2])
        for e in range(E_PER_GROUP):
            is_used = used[e] > 0.0

            @pl.when(jnp.logical_not(is_used))
            def _(e=e):
                fetch_part(e)

            @pl.when(is_used)
            def _(e=e):
                fetch_part(e)
                cw = (jnp.where(cols[:, 0:1] == first + e, cols[:, 2:3], 0.0)
                      + jnp.where(cols[:, 1:2] == first + e, cols[:, 3:4], 0.0))
                h = hb[...]
                a = _dot(h, wg_ref[0, e])
                u = _dot(h, wu_ref[0, e])
                act = ((a * jax.nn.sigmoid(a)) * u * cw).astype(BF16)
                yacc[...] += _dot(act, wd_ref[0, e * D_EXPERT:(e + 1) * D_EXPERT, :])

        _to_token_tiles(y_ref, yacc[...])

        @pl.when(k + 1 >= nvt)
        def _():
            _row_gather_wait(xbuf.at[1 - slot], sem.at[1 - slot])

    @pl.when(k >= nvt)
    def _():
        y_ref[...] = jnp.zeros(y_ref.shape, F32)


def _moe_sorted(h3, tile_group, n_valid_tiles, pos, n_tiles, tm, wrt, brt, wg, wu, wd):
    wspec = lambda shape: pl.BlockSpec(shape, lambda k, tg, nv, ps: (tg[k],) + (0,) * (len(shape) - 1))
    cspec = lambda shape: pl.BlockSpec(shape, lambda k, tg, nv, ps: (0,) * len(shape),
                                       pipeline_mode=pl.Buffered(1))
    return pl.pallas_call(
        functools.partial(_moe_sorted_kernel, tm=tm),
        grid_spec=pltpu.PrefetchScalarGridSpec(
            num_scalar_prefetch=3,
            grid=(n_tiles,),
            in_specs=[pl.BlockSpec(memory_space=pl.ANY), pl.BlockSpec(memory_space=pl.ANY),
                      cspec((D_MODEL, 2 * LANES)), cspec((1, LANES)),
                      wspec((1, E_PER_GROUP, D_MODEL, D_EXPERT)), wspec((1, E_PER_GROUP, D_MODEL, D_EXPERT)),
                      wspec((1, E_PER_GROUP * D_EXPERT, D_MODEL))],
            out_specs=pl.BlockSpec((tm * N_SLABS, LANES), lambda k, tg, nv, ps: (k, 0)),
            scratch_shapes=[pltpu.VMEM((2,) + _landing_shape(tm), F32), pltpu.SemaphoreType.DMA((2,)),
                            pltpu.VMEM((tm, D_MODEL), BF16), pltpu.VMEM((tm, D_MODEL), F32),
                            pltpu.SMEM((n_tiles * tm,), jnp.int32), pltpu.SemaphoreType.DMA(())]),
        out_shape=jax.ShapeDtypeStruct((n_tiles * tm * N_SLABS, LANES), F32),
        compiler_params=_cparams(("arbitrary",)),
        name="moe_sorted",
    )(tile_group, n_valid_tiles, pos, h3, jnp.zeros((n_tiles * tm,), jnp.int32), wrt, brt, wg, wu, wd)


def _combine_kernel(pos_ref, y3_hbm, x1_ref, o_ref, ybuf, sem, *, tm):
    k = pl.program_id(0)
    slot = lax.rem(k, 2)

    @pl.when(k == 0)
    def _():
        _row_gather(pos_ref, 0, y3_hbm, ybuf.at[0], sem.at[0])

    _row_gather_wait(ybuf.at[slot], sem.at[slot])

    @pl.when(k + 1 < pl.num_programs(0))
    def _():
        _row_gather(pos_ref, (k + 1) * tm, y3_hbm, ybuf.at[1 - slot], sem.at[1 - slot])

    o_ref[...] = x1_ref[...] + _from_landing(ybuf.at[slot])


def _combine(y3, pos, x1, tm):
    t = x1.shape[0]
    return pl.pallas_call(
        functools.partial(_combine_kernel, tm=tm),
        grid_spec=pltpu.PrefetchScalarGridSpec(
            num_scalar_prefetch=1,
            grid=(t // tm,),
            in_specs=[pl.BlockSpec(memory_space=pl.ANY), pl.BlockSpec((tm, D_MODEL), lambda k, ps: (k, 0))],
            out_specs=pl.BlockSpec((tm, D_MODEL), lambda k, ps: (k, 0)),
            scratch_shapes=[pltpu.VMEM((2,) + _landing_shape(tm), F32), pltpu.SemaphoreType.DMA((2,))]),
        out_shape=jax.ShapeDtypeStruct((t, D_MODEL), F32),
        compiler_params=_cparams(("arbitrary",)),
        name="combine",
    )(pos, y3, x1)


def _moe_kernel(h_ref, comb_ref, x1_ref, wg_ref, wu_ref, wd_ref, o_ref, act):
    g = pl.program_id(1)
    comb = comb_ref[0]
    y = _group_experts(h_ref[...], lambda e: comb[:, e:e + 1], wg_ref, wu_ref, wd_ref, act)

    @pl.when(g == 0)
    def _():
        o_ref[...] = x1_ref[...] + y

    @pl.when(g != 0)
    def _():
        o_ref[...] += y


def _moe(h2, comb, x1, tm, wg, wu, wd):
    t = h2.shape[0]
    return pl.pallas_call(
        _moe_kernel,
        grid=(t // tm, N_GROUPS),
        in_specs=[pl.BlockSpec((tm, D_MODEL), lambda i, g: (i, 0)),
                  pl.BlockSpec((1, tm, E_PER_GROUP), lambda i, g: (g, i, 0)),
                  pl.BlockSpec((tm, D_MODEL), lambda i, g: (i, 0)),
                  pl.BlockSpec((1, E_PER_GROUP, D_MODEL, D_EXPERT), lambda i, g: (g, 0, 0, 0)),
                  pl.BlockSpec((1, E_PER_GROUP, D_MODEL, D_EXPERT), lambda i, g: (g, 0, 0, 0)),
                  pl.BlockSpec((1, E_PER_GROUP * D_EXPERT, D_MODEL), lambda i, g: (g, 0, 0))],
        out_specs=pl.BlockSpec((tm, D_MODEL), lambda i, g: (i, 0)),
        out_shape=jax.ShapeDtypeStruct((t, D_MODEL), F32),
        scratch_shapes=[pltpu.VMEM((tm, E_PER_GROUP * D_EXPERT), BF16)],
        compiler_params=_cparams(("parallel", "arbitrary")),
        name="moe",
    )(h2, comb, x1, wg, wu, wd)


def _rope_tables(pos):
    half = DK_R // 2
    inv = ROPE_BASE ** (-jnp.arange(half, dtype=F32) / half)
    ang = pos.astype(F32)[:, None] * inv[None, :]
    c = jnp.cos(ang)
    s = jnp.sin(ang)
    return jnp.concatenate([c, c, c, c], axis=-1), jnp.concatenate([-s, s, -s, s], axis=-1)


def kernel(x_prompt, x_sample, state_ret, cache_fox_k, cache_fox_v, cache_fox_logf, cache_mem_k, cache_mem_v,
           mem_prompt, g_norm1, w_in, b_forget, g_fox_q, g_fox_k, g_mem_q, g_mem_in, w_mem_kv, g_mem_k,
           g_ret_out, w_br_ret, w_br_fox, w_br_mem, w_out, g_norm2, w_route_group, b_route_group,
           w_route_expert, b_route_expert, w_exp_gate, w_exp_up, w_exp_down):
    nb, seq, _ = x_prompt.shape
    nbs, n_new, _ = x_sample.shape
    past = cache_fox_k.shape[2]
    l = 0

    wi = w_in[l]
    o_ff = 2 * RQK_W + 2 * RV_W + 3 * FOX_W
    w_in_r = (wi[:, :o_ff].astype(BF16), wi[:, o_ff + H_F:].astype(BF16),
              jnp.pad(wi[:, o_ff:o_ff + H_F], ((0, 0), (0, LANES - H_F))).astype(BF16))
    bf_pad = jnp.concatenate([b_forget[l], jnp.zeros((LANES - H_F,), F32)])[None, :]
    g1 = g_norm1[l][None, :]
    gfq = jnp.tile(g_fox_q[l], H_F)[None, :]
    gfk = jnp.tile(g_fox_k[l], H_F)[None, :]
    gmq = g_mem_q[l][None, :]
    hid = jnp.arange(MXU_DIM) // D_F
    bd = jnp.where(hid[:, None] == hid[None, :], 1.0 / D_F, 0.0).astype(BF16)
    wr = w_br_ret[l].astype(BF16)
    wf = w_br_fox[l].astype(BF16)
    wm = w_br_mem[l].astype(BF16)
    wo = w_out[l].astype(BF16)
    g2 = g_norm2[l][None, :]
    n_e = N_GROUPS * E_PER_GROUP
    wrt = jnp.concatenate([w_route_expert[l], w_route_group[l],
                           jnp.zeros((D_MODEL, LANES - n_e - N_GROUPS), F32)], axis=1)
    wrt_hi = wrt.astype(BF16)
    wrt2 = jnp.concatenate([wrt_hi, (wrt - wrt_hi.astype(F32)).astype(BF16)], axis=1)
    brt = jnp.concatenate([b_route_expert[l], b_route_group[l],
                           jnp.zeros((LANES - n_e - N_GROUPS,), F32)])[None, :]
    wg = w_exp_gate[l].astype(BF16)
    wu = w_exp_up[l].astype(BF16)
    wd = w_exp_down[l].astype(BF16).reshape(N_GROUPS, E_PER_GROUP * D_EXPERT, D_MODEL)
    gro = g_ret_out[l][None, :]
    prep_w = (g1, w_in_r, bf_pad, gfq, gfk, gmq, bd)
    merge_w = (wr, wf, wm, wo, g2, wrt2, brt)

    tm = TOKEN_TILE
    xp = x_prompt.reshape(nb * seq, D_MODEL)
    (rq, rk, rv, rg, fqt, fkt, fvt, lft, mq, gates, fkb) = _prep(
        xp, _rope_tables(jnp.arange(seq)), tm, seq // tm, True, *prep_w)
    o_r, s_fin = _retention(rq, rk, rv, rg, nb, seq, MXU_DIM, gro, None)
    fcum = _cumsum(lft, MXU_DIM).reshape(nb * (H_F // 2), 2, seq)
    o_f = _fox_prompt(fqt, fkb, fvt, fcum, nb, seq, MXU_DIM)
    mk, mv = _memkv(mem_prompt.reshape(nb * N_MEM, D_MODEL), nb, g_mem_in[l][None, :],
                    w_mem_kv[l].astype(BF16), g_mem_k[l][None, :])
    x1, h3, route, counts = _merge(xp, o_r, o_f, (mq, mk, mv, seq // tm), gates, tm, True, *merge_w)
    t_p = nb * seq
    n_tiles = t_p // tm + N_GROUPS
    n_cls = N_GROUPS * N_PAIRS
    cnt = counts[:, :n_cls, 0].astype(jnp.int32)
    before_tile = jnp.cumsum(cnt, axis=0) - cnt
    cls_tot = jnp.sum(cnt, axis=0)
    grp_tot = jnp.sum(cls_tot.reshape(N_GROUPS, N_PAIRS), axis=1)
    tiles_g = (grp_tot + tm - 1) // tm
    tile_end = jnp.cumsum(tiles_g)
    row_start = (tile_end - tiles_g) * tm
    in_grp = cls_tot.reshape(N_GROUPS, N_PAIRS)
    cls_start = (row_start[:, None] + jnp.cumsum(in_grp, axis=1) - in_grp).reshape(n_cls)
    first_row = jnp.pad((cls_start[None, :] + before_tile).astype(F32), ((0, 0), (0, LANES - n_cls)))
    pos = _positions(route, jnp.broadcast_to(first_row[:, :, None], (t_p // tm, LANES, LANES)),
                     min(8, t_p // tm))
    tile_ids = jnp.arange(n_tiles, dtype=jnp.int32)
    tile_group = jnp.minimum(jnp.sum((tile_ids[:, None] >= tile_end[None, :]).astype(jnp.int32), axis=1),
                             N_GROUPS - 1)
    y3 = _moe_sorted(h3.reshape(t_p, N_SLABS, LANES), tile_group, tile_end[-1:], pos, n_tiles, tm,
                     wrt2, brt, wg, wu, wd)
    y_prompt = _combine(y3.reshape(n_tiles * tm, N_SLABS, LANES), pos, x1, 2 * tm).reshape(nb, seq, D_MODEL)

    ts = nbs * n_new
    xs = x_sample.reshape(ts, D_MODEL)
    pos_s = jnp.tile(past + jnp.arange(n_new), nbs)
    (rq_s, rk_s, rv_s, rg_s, fq_s, fk_s, fv_s, lf_s, mq_s, gates_s) = _prep(
        xs, _rope_tables(pos_s), ts, 1, False, *prep_w)
    o_r_s, s_new = _retention(rq_s, rk_s, rv_s, rg_s, nbs, n_new, n_new, gro, state_ret[l])
    pad = (-(past + n_new)) % MXU_DIM
    lf_rows = jnp.concatenate([jnp.swapaxes(cache_fox_logf[l], 1, 2),
                               jnp.swapaxes(lf_s.reshape(nbs, n_new, H_F), 1, 2),
                               jnp.zeros((nbs, H_F, pad), F32)], axis=2).reshape(nbs * H_F, past + n_new + pad)
    f_all = _cumsum(lf_rows, MXU_DIM)
    feat_major = lambda c: jnp.transpose(c, (0, 2, 3, 1)).reshape(nbs * FOX_W, past)
    o_f_s = _fox_sample(fq_s, feat_major(cache_fox_k[l]), feat_major(cache_fox_v[l]), fk_s, fv_s,
                        f_all[:, :past].reshape(nbs, H_F, past),
                        f_all[:, past:past + n_new].reshape(nbs, H_F, n_new), nbs, n_new, past)
    o_m_s = _memattn(mq_s, cache_mem_k[l].reshape(nbs * N_MEM, MEM_W),
                     cache_mem_v[l].reshape(nbs * N_MEM, MEM_W), nbs, n_new, n_new)
    x1_s, h2_s, comb_s = _merge(xs, o_r_s, o_f_s, o_m_s, gates_s, ts, False, *merge_w)
    y_sample = _moe(h2_s, comb_s, x1_s, ts, wg, wu, wd).reshape(nbs, n_new, D_MODEL)

    token_major = lambda a: jnp.transpose(a.reshape(nb, H_F, D_F, seq), (0, 3, 1, 2))

    return (y_prompt, y_sample,
            s_fin[None], token_major(fkt)[None], token_major(fvt)[None],
            jnp.swapaxes(lft.reshape(nb, H_F, seq), 1, 2)[None],
            mk.reshape(1, nb, N_MEM, H_M, D_M), mv.reshape(1, nb, N_MEM, H_M, D_M),
            s_new[None], fk_s.reshape(1, nbs, n_new, H_F, D_F), fv_s.reshape(1, nbs, n_new, H_F, D_F),
            lf_s.reshape(1, nbs, n_new, H_F))
```

```python
import functools

import jax
import jax.numpy as jnp
from jax import lax
from jax.experimental import pallas as pl
from jax.experimental.pallas import tpu as pltpu

F32 = jnp.float32
BF16 = jnp.bfloat16

D_MODEL = 1024
H_R, DK_R, DV_R = 4, 64, 128
H_F, D_F = 8, 64
H_M, D_M = 4, 128
N_MEM = 256
N_GROUPS, E_PER_GROUP, D_EXPERT = 4, 8, 256
ROPE_BASE = 10000.0
EPS = 1e-6
LOG2E = 1.4426950408889634

RQK_W = H_R * DK_R
RV_W = H_R * DV_R
FOX_W = H_F * D_F
MEM_W = H_M * D_M
GATE_W = 3 * D_MODEL
LANES = 128
MXU_DIM = 256
TOKEN_TILE = 512

C_RQ = 0
C_RK = C_RQ + RQK_W
C_RV = C_RK + RQK_W
C_RG = C_RV + RV_W
C_FQ = C_RG + RV_W
C_FK = C_FQ + FOX_W
C_FV = C_FK + FOX_W
C_MQ = C_FV + FOX_W
C_GT = C_MQ + MEM_W
C_FF = C_GT + GATE_W

VMEM_LIMIT = 56 * 1024 * 1024


def _cparams(sem):
    return pltpu.CompilerParams(dimension_semantics=sem, vmem_limit_bytes=VMEM_LIMIT)


def _const_spec(shape):
    nd = len(shape)
    return pl.BlockSpec(shape, lambda *_: (0,) * nd, pipeline_mode=pl.Buffered(1))


def _dot(a, b):
    return jnp.dot(a, b, preferred_element_type=F32)


def _dot_nt(a, b):
    return lax.dot_general(a, b, (((1,), (1,)), ((), ())), preferred_element_type=F32)


def _dot_tn(a, b):
    return lax.dot_general(a, b, (((0,), (0,)), ((), ())), preferred_element_type=F32)


def _lane_rmsnorm(z):
    return z * lax.rsqrt(jnp.mean(z * z, axis=-1, keepdims=True) + EPS)


def _prep_kernel(x_ref, g1_ref, wa_ref, wb_ref, wf_ref, cos_ref, sin_ref, bf_ref, gfq_ref, gfk_ref, gmq_ref, bd_ref,
                 rq_ref, rk_ref, rv_ref, rg_ref, fq_ref, fk_ref, fv_ref, lf_ref, mq_ref, gt_ref, *extra,
                 seq_minor):
    x = x_ref[...]
    h = (x * lax.rsqrt(jnp.mean(x * x, axis=-1, keepdims=True) + EPS) * g1_ref[...]).astype(BF16)

    def proj(c0, width):
        if c0 < C_MQ:
            return _dot(h, wa_ref[:, c0:c0 + width])
        if c0 < C_FF:
            return _dot(h, wb_ref[:, c0 - C_MQ:c0 - C_MQ + width])
        return _dot(h, wf_ref[...])

    cos = cos_ref[...]
    sin = sin_ref[...]
    lane = lax.broadcasted_iota(jnp.int32, cos.shape, 1)
    first_half = (lane % DK_R) < (DK_R // 2)

    def rope(z):
        swapped = jnp.where(first_half, pltpu.roll(z, LANES - DK_R // 2, 1), pltpu.roll(z, DK_R // 2, 1))
        return z * cos + swapped * sin

    v = proj(C_FF, LANES) + bf_ref[...]
    lf = jnp.minimum(v, 0.0) - jnp.log1p(jnp.exp(-jnp.abs(v)))
    if seq_minor:
        lf_ref[...] = lf.T[:H_F, :]
    else:
        lf_ref[...] = lf[:, :H_F]

    def head64_norm(z, g_ref):
        zz = (z * z).astype(BF16)
        w = bd_ref.shape[0]
        ms = jnp.concatenate([_dot(zz[:, c:c + w], bd_ref[...]) for c in range(0, FOX_W, w)], axis=1)
        return z * lax.rsqrt(ms + EPS) * g_ref[...]

    fq = head64_norm(proj(C_FQ, FOX_W), gfq_ref) * (D_F ** -0.5 * LOG2E)
    fk = head64_norm(proj(C_FK, FOX_W), gfk_ref)
    fv = proj(C_FV, FOX_W)
    if seq_minor:
        fq_ref[...] = fq.T.astype(BF16)
        fk_ref[...] = fk.T
        fv_ref[...] = fv.T
        extra[0][...] = fk.astype(BF16)
    else:
        fq_ref[...] = fq.astype(BF16)
        fk_ref[...] = fk
        fv_ref[...] = fv

    zm = proj(C_MQ, MEM_W)
    for hh in range(H_M):
        sl = slice(hh * D_M, (hh + 1) * D_M)
        mq_ref[:, sl] = (_lane_rmsnorm(zm[:, sl]) * gmq_ref[...]).astype(BF16)

    for b in range(3):
        gt_ref[:, b * D_MODEL:(b + 1) * D_MODEL] = jax.nn.sigmoid(proj(C_GT + b * D_MODEL, D_MODEL)).astype(BF16)

    zq = proj(C_RQ, RQK_W)
    zk = proj(C_RK, RQK_W)
    for p in range(RQK_W // LANES):
        sl = slice(p * LANES, (p + 1) * LANES)
        rq_ref[:, sl] = rope(zq[:, sl])
        rk_ref[:, sl] = rope(zk[:, sl]) * (DK_R ** -0.5)
    rv_ref[...] = proj(C_RV, RV_W)
    rg_ref[...] = proj(C_RG, RV_W)


def _prep(x2d, tables, tm, n_pos_tiles, seq_minor, g1, w_in_r, bf_pad, gfq, gfk, gmq, bd):
    t = x2d.shape[0]
    cos_t, sin_t = tables
    row = lambda w: pl.BlockSpec((tm, w), lambda i: (i, 0))
    pos = pl.BlockSpec((tm, LANES), lambda i: (i % n_pos_tiles, 0))
    sds = jax.ShapeDtypeStruct
    if seq_minor:
        nb, seq = t // (n_pos_tiles * tm), n_pos_tiles * tm
        fm = lambda rows_: pl.BlockSpec((rows_, tm), lambda i: (i // n_pos_tiles, i % n_pos_tiles))
        fox_shapes = [sds((nb * FOX_W, seq), BF16), sds((nb * FOX_W, seq), F32), sds((nb * FOX_W, seq), F32),
                      sds((nb * H_F, seq), F32)]
        fox_specs = [fm(FOX_W), fm(FOX_W), fm(FOX_W), fm(H_F)]
        extra_shapes, extra_specs = [sds((t, FOX_W), BF16)], [row(FOX_W)]
    else:
        fox_shapes = [sds((t, FOX_W), BF16), sds((t, FOX_W), F32), sds((t, FOX_W), F32), sds((t, H_F), F32)]
        fox_specs = [row(FOX_W), row(FOX_W), row(FOX_W), row(H_F)]
        extra_shapes, extra_specs = [], []
    out_shapes = [sds((t, RQK_W), F32), sds((t, RQK_W), F32), sds((t, RV_W), F32), sds((t, RV_W), F32),
                  *fox_shapes, sds((t, MEM_W), BF16), sds((t, GATE_W), BF16), *extra_shapes]
    out_specs = [row(RQK_W), row(RQK_W), row(RV_W), row(RV_W), *fox_specs, row(MEM_W), row(GATE_W),
                 *extra_specs]
    return pl.pallas_call(
        functools.partial(_prep_kernel, seq_minor=seq_minor),
        grid=(t // tm,),
        in_specs=[row(D_MODEL), _const_spec((1, D_MODEL)), _const_spec((D_MODEL, C_MQ)),
                  _const_spec((D_MODEL, C_FF - C_MQ)), _const_spec((D_MODEL, LANES)), pos, pos,
                  _const_spec((1, LANES)), _const_spec((1, FOX_W)), _const_spec((1, FOX_W)),
                  _const_spec((1, D_M)), _const_spec((MXU_DIM, MXU_DIM))],
        out_specs=out_specs,
        out_shape=out_shapes,
        compiler_params=_cparams(("parallel",)),
        name="prep",
    )(x2d, g1, *w_in_r, cos_t, sin_t, bf_pad, gfq, gfk, gmq, bd)


def _ret_kernel(*refs, has_init, ch):
    if has_init:
        (rq_ref, rk_ref, rv_ref, rg_ref, dmat_ref, qdec_ref, kdec_ref, gpow_ref, gro_ref, s0_ref,
         o_ref, sfin_ref) = refs
    else:
        (rq_ref, rk_ref, rv_ref, rg_ref, dmat_ref, qdec_ref, kdec_ref, gpow_ref, gro_ref,
         o_ref, sfin_ref) = refs

    state = []
    for h in range(H_R):
        if has_init:
            pad = jnp.zeros((DK_R, DV_R), F32)
            state.append(jnp.concatenate([s0_ref[0, h], pad] if h % 2 == 0 else [pad, s0_ref[0, h]], axis=0))
        else:
            state.append(jnp.zeros((LANES, DV_R), F32))

    lane = lax.broadcasted_iota(jnp.int32, (ch, LANES), 1)
    for c in range(rq_ref.shape[0] // ch):
        rows = slice(c * ch, (c + 1) * ch)
        for p in range(H_R // 2):
            sl = slice(p * LANES, (p + 1) * LANES)
            q2 = rq_ref[rows, sl]
            k2 = rk_ref[rows, sl]
            kd2 = k2 * kdec_ref[p]
            for hh in range(2):
                h = 2 * p + hh
                mine = (lane >= hh * DK_R) & (lane < (hh + 1) * DK_R)
                qm = jnp.where(mine, q2, 0.0).astype(BF16)
                kdm = jnp.where(mine, kd2, 0.0).astype(BF16)
                v = rv_ref[rows, h * DV_R:(h + 1) * DV_R].astype(BF16)
                sc = _dot_nt(qm, k2.astype(BF16)) * dmat_ref[h]
                o = _dot(sc.astype(BF16), v) + _dot(qm, state[h].astype(BF16)) * qdec_ref[h]
                state[h] = gpow_ref[h] * state[h] + _dot_tn(kdm, v)
                normed = _lane_rmsnorm(o) * gro_ref[...]
                rg = rg_ref[rows, h * DV_R:(h + 1) * DV_R]
                o_ref[rows, h * DV_R:(h + 1) * DV_R] = (normed * (rg * jax.nn.sigmoid(rg))).astype(BF16)

    for h in range(H_R):
        r0 = DK_R * (h % 2)
        sfin_ref[0, h] = state[h][r0:r0 + DK_R, :]


def _retention(rq, rk, rv, rg, n_batch, seq, ch, g_ret_out, state0):
    lg = jnp.log1p(-jnp.exp2(-5.0 - jnp.arange(H_R, dtype=F32)))
    idx = jnp.arange(ch, dtype=F32)
    diff = idx[:, None] - idx[None, :]
    causal = diff >= 0
    dmat = jnp.where(causal[None], jnp.exp(jnp.where(causal, diff, 0.0)[None] * lg[:, None, None]), 0.0)
    q_dec = jnp.exp((idx + 1.0)[None, :] * lg[:, None])
    k_dec = jnp.exp((ch - 1.0 - idx)[None, :] * lg[:, None])
    qdec = jnp.broadcast_to(q_dec[:, :, None], (H_R, ch, DV_R))
    kdec = jnp.broadcast_to(k_dec[:, :, None], (H_R, ch, DK_R))
    kdec = kdec.reshape(H_R // 2, 2, ch, DK_R).transpose(0, 2, 1, 3).reshape(H_R // 2, ch, LANES)
    gpow = jnp.broadcast_to(jnp.exp(ch * lg)[:, None, None], (H_R, 1, DV_R))

    has_init = state0 is not None
    blk = lambda w: pl.BlockSpec((seq, w), lambda b: (b, 0))
    in_specs = [blk(RQK_W), blk(RQK_W), blk(RV_W), blk(RV_W),
                _const_spec((H_R, ch, ch)), _const_spec((H_R, ch, DV_R)),
                _const_spec((H_R // 2, ch, LANES)), _const_spec((H_R, 1, DV_R)), _const_spec((1, DV_R))]
    args = [rq, rk, rv, rg, dmat, qdec, kdec, gpow, g_ret_out]
    if has_init:
        in_specs.append(pl.BlockSpec((1, H_R, DK_R, DV_R), lambda b: (b, 0, 0, 0)))
        args.append(state0)
    return pl.pallas_call(
        functools.partial(_ret_kernel, has_init=has_init, ch=ch),
        grid=(n_batch,),
        in_specs=in_specs,
        out_specs=[blk(RV_W), pl.BlockSpec((1, H_R, DK_R, DV_R), lambda b: (b, 0, 0, 0))],
        out_shape=[jax.ShapeDtypeStruct((n_batch * seq, RV_W), BF16),
                   jax.ShapeDtypeStruct((n_batch, H_R, DK_R, DV_R), F32)],
        compiler_params=_cparams(("parallel",)),
        name="retention",
    )(*args)


def _split3(x):
    hi = x.astype(BF16)
    r1 = x - hi.astype(F32)
    mid = r1.astype(BF16)
    lo = (r1 - mid.astype(F32)).astype(BF16)
    return hi, mid, lo


def _cumsum_kernel(x_ref, o_ref, *, blk):
    rows, n = x_ref.shape
    r = lax.broadcasted_iota(jnp.int32, (blk, blk), 0)
    c = lax.broadcasted_iota(jnp.int32, (blk, blk), 1)
    tri = jnp.where(r <= c, 1.0, 0.0).astype(BF16)
    carry = jnp.zeros((rows, 1), F32)
    for i in range(n // blk):
        hi, mid, lo = _split3(x_ref[:, i * blk:(i + 1) * blk])
        cum = (_dot(hi, tri) + _dot(mid, tri)) + _dot(lo, tri) + carry
        o_ref[:, i * blk:(i + 1) * blk] = cum
        carry = cum[:, blk - 1:blk]


def _cumsum(x, blk):
    r, t = x.shape
    spec = pl.BlockSpec((H_F, t), lambda b: (b, 0))
    return pl.pallas_call(
        functools.partial(_cumsum_kernel, blk=blk),
        grid=(r // H_F,),
        in_specs=[spec], out_specs=spec,
        out_shape=jax.ShapeDtypeStruct(x.shape, F32),
        compiler_params=_cparams(("parallel",)),
        name="cumsum",
    )(x)


def _head_rows_mask(shape, hh):
    sub = lax.broadcasted_iota(jnp.int32, shape, 0)
    return (sub >= hh * D_F) & (sub < (hh + 1) * D_F)


N_AUG = 3
V_ROWS = D_F + 16


def _fox_kernel(qt_ref, k_ref, vt_ref, fcum_ref, o_ref, qa, ka, vb, *, tq):
    seq = k_ref.shape[0]
    tk = tq
    k = k_ref[...].astype(F32)
    lane = lax.broadcasted_iota(jnp.int32, k.shape, 1)
    aug_rows = lax.broadcasted_iota(jnp.int32, (D_F, seq), 0) < N_AUG
    for hh in range(2):
        own, oth = hh * D_F, (1 - hh) * D_F
        qa[hh, own:own + D_F, :] = qt_ref[own:own + D_F, :]
        qa[hh, oth:oth + D_F, :] = jnp.where(aug_rows, 1.0, 0.0).astype(BF16)
        pieces = [p.astype(F32) for p in _split3(fcum_ref[0, hh:hh + 1, :] * (-LOG2E))]
        ka[hh] = jnp.where((lane >= own) & (lane < own + D_F), k, _rows_to_columns(pieces, oth)).astype(BF16)
    for hh in range(2):
        vb[hh, :D_F, :] = vt_ref[hh * D_F:(hh + 1) * D_F, :].astype(BF16)
        vb[hh, D_F:, :] = jnp.ones((V_ROWS - D_F, seq), BF16)
    causal = (lax.broadcasted_iota(jnp.int32, (tk, tq), 1) >= lax.broadcasted_iota(jnp.int32, (tk, tq), 0))

    pairs = [(i, j) for i in range(seq // tq) for j in range(i + 1)]

    def stage_a(i, j):
        return tuple(_dot(ka[hh, j * tk:(j + 1) * tk, :], qa[hh, :, i * tq:(i + 1) * tq]) for hh in range(2))

    def stage_b(i, j, ts, maxes):
        new_maxes, probs = [], []
        for hh in range(2):
            t = jnp.where(causal, ts[hh], -jnp.inf) if j == i else ts[hh]
            t_max = jnp.max(t, axis=0, keepdims=True)
            if j == 0:
                m_new, alpha = t_max, None
            else:
                m_new = jnp.maximum(maxes[hh], t_max)
                alpha = jnp.exp2(maxes[hh] - m_new)
            new_maxes.append(m_new)
            probs.append((jnp.exp2(t - m_new).astype(BF16), alpha))
        return new_maxes, probs

    def stage_c(i, j, probs, accs):
        out = []
        for hh in range(2):
            p, alpha = probs[hh]
            pv = _dot(vb[hh, :, j * tk:(j + 1) * tk], p)
            out.append(pv if j == 0 else alpha * accs[hh] + pv)
        return out

    scores, probs, maxes, accs = {}, {}, None, None
    for s in range(len(pairs) + 2):
        if s < len(pairs):
            scores[s] = stage_a(*pairs[s])
        if 0 <= s - 1 < len(pairs):
            maxes, probs[s - 1] = stage_b(*pairs[s - 1], scores.pop(s - 1), maxes)
        if 0 <= s - 2 < len(pairs):
            i, j = pairs[s - 2]
            accs = stage_c(i, j, probs.pop(s - 2), accs)
            if j == i:
                o_t = jnp.concatenate([a[:D_F] / a[D_F:D_F + 1] for a in accs], axis=0)
                o_ref[i * tq:(i + 1) * tq, :] = o_t.T.astype(BF16)


def _fox_prompt(fqt, fkb, fvt, fcum, n_batch, seq, tq):
    npair = H_F // 2
    fm_spec = pl.BlockSpec((LANES, seq), lambda b, p: (b * npair + p, 0))
    tok_spec = pl.BlockSpec((seq, LANES), lambda b, p: (b, p))
    return pl.pallas_call(
        functools.partial(_fox_kernel, tq=tq),
        grid=(n_batch, npair),
        in_specs=[fm_spec, tok_spec, fm_spec, pl.BlockSpec((1, 2, seq), lambda b, p: (b * npair + p, 0, 0))],
        out_specs=tok_spec,
        out_shape=jax.ShapeDtypeStruct((n_batch * seq, FOX_W), BF16),
        scratch_shapes=[pltpu.VMEM((2, LANES, seq), BF16), pltpu.VMEM((2, seq, LANES), BF16),
                        pltpu.VMEM((2, V_ROWS, seq), BF16)],
        compiler_params=_cparams(("parallel", "parallel")),
        name="fox_prompt",
    )(fqt, fkb, fvt, fcum)


def _fox_sample_kernel(q_ref, kct_ref, vct_ref, kn_ref, vn_ref, frc_ref, frn_ref, o_ref):
    n = q_ref.shape[0]
    lane = lax.broadcasted_iota(jnp.int32, (n, LANES), 1)
    rows = lax.broadcasted_iota(jnp.int32, (n, n), 0)
    cols = lax.broadcasted_iota(jnp.int32, (n, n), 1)
    for pair in range(H_F // 2):
        sl = slice(pair * LANES, (pair + 1) * LANES)
        q2 = q_ref[:, sl]
        kct = kct_ref[sl, :]
        vct = vct_ref[sl, :].astype(BF16)
        kn = kn_ref[:, sl].astype(BF16)
        vn = vn_ref[:, sl].astype(BF16)
        outs = []
        for hh in range(2):
            h = 2 * pair + hh
            mine = (lane >= hh * D_F) & (lane < (hh + 1) * D_F)
            kc = jnp.where(_head_rows_mask(kct.shape, hh), kct, 0.0).astype(BF16)
            t_c = _dot(q2, kc) - frc_ref[0, h:h + 1, :] * LOG2E
            t_n = _dot_nt(jnp.where(mine, q2, jnp.zeros_like(q2)), kn) - frn_ref[0, h:h + 1, :] * LOG2E
            t_n = jnp.where(rows >= cols, t_n, -jnp.inf)
            m = jnp.maximum(jnp.max(t_c, axis=-1, keepdims=True), jnp.max(t_n, axis=-1, keepdims=True))
            p_c = jnp.exp2(t_c - m)
            p_n = jnp.exp2(t_n - m)
            l = jnp.sum(p_c, axis=-1, keepdims=True) + jnp.sum(p_n, axis=-1, keepdims=True)
            acc = _dot_nt(p_c.astype(BF16), vct) + _dot(p_n.astype(BF16), vn)
            outs.append(acc / l)
        o_ref[:, sl] = jnp.where(lane < D_F, outs[0], outs[1]).astype(BF16)


def _fox_sample(fq, kct, vct, kn, vn, frow_c, frow_n, n_batch, n_new, past):
    tok = pl.BlockSpec((n_new, FOX_W), lambda b: (b, 0))
    cache = pl.BlockSpec((FOX_W, past), lambda b: (b, 0))
    return pl.pallas_call(
        _fox_sample_kernel,
        grid=(n_batch,),
        in_specs=[tok, cache, cache, tok, tok,
                  pl.BlockSpec((1, H_F, past), lambda b: (b, 0, 0)),
                  pl.BlockSpec((1, H_F, n_new), lambda b: (b, 0, 0))],
        out_specs=tok,
        out_shape=jax.ShapeDtypeStruct((n_batch * n_new, FOX_W), BF16),
        compiler_params=_cparams(("parallel",)),
        name="fox_sample",
    )(fq, kct, vct, kn, vn, frow_c, frow_n)


def _memkv_kernel(m_ref, gin_ref, w_ref, gk_ref, k_ref, v_ref):
    x = m_ref[...]
    h = (x * lax.rsqrt(jnp.mean(x * x, axis=-1, keepdims=True) + EPS) * gin_ref[...]).astype(BF16)
    zk = _dot(h, w_ref[:, :MEM_W])
    for hh in range(H_M):
        sl = slice(hh * D_M, (hh + 1) * D_M)
        k_ref[:, sl] = _lane_rmsnorm(zk[:, sl]) * gk_ref[...]
    v_ref[...] = _dot(h, w_ref[:, MEM_W:])


def _memkv(mem2d, n_batch, g_mem_in, w_mem_kv, g_mem_k):
    blk = lambda w: pl.BlockSpec((N_MEM, w), lambda b: (b, 0))
    return pl.pallas_call(
        _memkv_kernel,
        grid=(n_batch,),
        in_specs=[blk(D_MODEL), _const_spec((1, D_MODEL)), _const_spec((D_MODEL, 2 * MEM_W)),
                  _const_spec((1, D_M))],
        out_specs=[blk(MEM_W), blk(MEM_W)],
        out_shape=[jax.ShapeDtypeStruct((n_batch * N_MEM, MEM_W), F32)] * 2,
        compiler_params=_cparams(("parallel",)),
        name="memkv",
    )(mem2d, g_mem_in, w_mem_kv, g_mem_k)


def _mem_attend(q_ref, k_ref, v_ref):
    for hh in range(H_M):
        sl = slice(hh * D_M, (hh + 1) * D_M)
        s = _dot_nt(q_ref[:, sl], k_ref[:, sl].astype(BF16)) * (D_M ** -0.5)
        m = jnp.max(s, axis=-1, keepdims=True)
        p = jnp.exp(s - m)
        l = jnp.sum(p, axis=-1, keepdims=True)
        yield sl, (_dot(p.astype(BF16), v_ref[:, sl].astype(BF16)) / l).astype(BF16)


def _memattn_kernel(q_ref, k_ref, v_ref, o_ref):
    for sl, o in _mem_attend(q_ref, k_ref, v_ref):
        o_ref[:, sl] = o


def _memattn(mq, mk, mv, n_batch, seq, tq):
    nq = seq // tq
    return pl.pallas_call(
        _memattn_kernel,
        grid=(n_batch, nq),
        in_specs=[pl.BlockSpec((tq, MEM_W), lambda b, i: (b * nq + i, 0)),
                  pl.BlockSpec((N_MEM, MEM_W), lambda b, i: (b, 0)),
                  pl.BlockSpec((N_MEM, MEM_W), lambda b, i: (b, 0))],
        out_specs=pl.BlockSpec((tq, MEM_W), lambda b, i: (b * nq + i, 0)),
        out_shape=jax.ShapeDtypeStruct((n_batch * seq, MEM_W), BF16),
        compiler_params=_cparams(("parallel", "arbitrary")),
        name="memattn",
    )(mq, mk, mv)


def _split2(x):
    hi = x.astype(BF16)
    return hi, (x - hi.astype(F32)).astype(BF16)


N_EXPERTS = N_GROUPS * E_PER_GROUP


N_PAIRS = E_PER_GROUP * (E_PER_GROUP - 1) // 2


def _router_logits(h2, wrt_ref, brt_ref):
    h_hi, h_lo = _split2(h2)
    z = _dot(h_hi, wrt_ref[...])
    return (z[:, :LANES] + z[:, LANES:]) + _dot(h_lo, wrt_ref[:, :LANES]) + brt_ref[...]


N_LOGIT_ROWS = 40


def _route(logits, group=None):
    lt = logits.T[:N_LOGIT_ROWS, :]
    idx = lax.broadcasted_iota(jnp.int32, lt.shape, 0).astype(F32)
    neg = -jnp.inf
    first_idx = lambda mask: jnp.min(jnp.where(mask, idx, float(LANES)), axis=0, keepdims=True)

    is_g = (idx >= N_EXPERTS) & (idx < N_EXPERTS + N_GROUPS)
    lg = jnp.where(is_g, lt, neg)
    mg = jnp.max(lg, axis=0, keepdims=True)
    if group is None:
        group = first_idx(lg == mg) - N_EXPERTS
        p_sel = 1.0 / jnp.sum(jnp.exp(lg - mg), axis=0, keepdims=True)
    else:
        lsel = jnp.max(jnp.where(idx == group + N_EXPERTS, lt, neg), axis=0, keepdims=True)
        p_sel = jnp.exp(lsel - mg) / jnp.sum(jnp.exp(lg - mg), axis=0, keepdims=True)

    in_grp = (idx >= group * E_PER_GROUP) & (idx < (group + 1) * E_PER_GROUP)
    le = jnp.where(in_grp, lt, neg)
    v1 = jnp.max(le, axis=0, keepdims=True)
    i1 = first_idx(le == v1)
    le2 = jnp.where(idx == i1, neg, le)
    v2 = jnp.max(le2, axis=0, keepdims=True)
    i2 = first_idx(le2 == v2)
    e2 = jnp.exp(v2 - v1)
    return group, i1, i2, p_sel / (1.0 + e2), p_sel * e2 / (1.0 + e2)


def _rows_to_columns(rows, first=0):
    n = rows[0].shape[1]
    idx = lax.broadcasted_iota(jnp.int32, (LANES, n), 0)
    stacked = jnp.zeros((LANES, n), F32)
    for r, row in enumerate(rows):
        stacked = jnp.where(idx == first + r, row, stacked)
    return stacked.T


def _merge_kernel(*refs, sorted_moe):
    if sorted_moe:
        (x_ref, or_ref, of_ref, mq_ref, mk_ref, mv_ref, gt_ref, wr_ref, wf_ref, wm_ref, wo_ref, g2_ref,
         wrt_ref, brt_ref, x1_ref, h2_ref, route_ref, *counts_ref) = refs
        o_m = jnp.concatenate([o for _, o in _mem_attend(mq_ref, mk_ref, mv_ref)], axis=1)
    else:
        (x_ref, or_ref, of_ref, om_ref, gt_ref, wr_ref, wf_ref, wm_ref, wo_ref, g2_ref,
         wrt_ref, brt_ref, x1_ref, h2_ref, route_ref) = refs
        o_m = om_ref[...]
    g = lambda b: gt_ref[:, b * D_MODEL:(b + 1) * D_MODEL].astype(F32)
    merged = (g(0) * _dot(or_ref[...], wr_ref[...]) + g(1) * _dot(of_ref[...], wf_ref[...])
              + g(2) * _dot(o_m, wm_ref[...]))
    x1 = x_ref[...] + _dot(merged.astype(BF16), wo_ref[...])
    x1_ref[...] = x1
    h2 = x1 * lax.rsqrt(jnp.mean(x1 * x1, axis=-1, keepdims=True) + EPS) * g2_ref[...]
    group, i1, i2, w1, w2 = _route(_router_logits(h2, wrt_ref, brt_ref))
    tm = h2.shape[0]
    if sorted_moe:
        _to_token_tiles(h2_ref, h2)
        e_lo = jnp.minimum(i1, i2) - group * E_PER_GROUP
        e_hi = jnp.maximum(i1, i2) - group * E_PER_GROUP
        cls = group * N_PAIRS + (e_lo * E_PER_GROUP - e_lo * (e_lo + 1.0) * 0.5 + (e_hi - e_lo - 1.0))
        cidx = lax.broadcasted_iota(jnp.int32, (LANES, tm), 0).astype(F32)
        onehot = jnp.where(cidx == cls, 1.0, 0.0)
        r = lax.broadcasted_iota(jnp.int32, (tm, tm), 0)
        c = lax.broadcasted_iota(jnp.int32, (tm, tm), 1)
        before = _dot(onehot.astype(BF16), jnp.where(r < c, 1.0, 0.0).astype(BF16))
        rank = jnp.sum(before * onehot, axis=0, keepdims=True)
        ridx = lax.broadcasted_iota(jnp.int32, route_ref.shape, 0)
        route_ref[...] = jnp.where(ridx == 0, group, jnp.where(ridx == 1, cls, jnp.where(ridx == 2, rank, 0.0)))
        counts_ref[0][0] = jnp.broadcast_to(jnp.sum(onehot, axis=1, keepdims=True), (LANES, LANES))
    else:
        h2_ref[...] = h2.astype(BF16)
        cols = _rows_to_columns([i1, i2, w1, w2])
        lane = lax.broadcasted_iota(jnp.int32, (tm, LANES), 1).astype(F32)
        comb = jnp.where(lane == cols[:, 0:1], cols[:, 2:3], 0.0) + jnp.where(lane == cols[:, 1:2], cols[:, 3:4], 0.0)
        for gi in range(N_GROUPS):
            route_ref[gi] = comb[:, gi * E_PER_GROUP:(gi + 1) * E_PER_GROUP]


def _merge(x2d, o_r, o_f, mem, gates, tm, sorted_moe, wr, wf, wm, wo, g2, wrt, brt):
    t = x2d.shape[0]
    row = lambda w: pl.BlockSpec((tm, w), lambda i: (i, 0))
    sds = jax.ShapeDtypeStruct
    if sorted_moe:
        h2_spec, h2_shape = pl.BlockSpec((tm * N_SLABS, LANES), lambda i: (i, 0)), sds((t * N_SLABS, LANES), F32)
        rt_specs = [pl.BlockSpec((SUBLANES, tm), lambda i: (i, 0)),
                    pl.BlockSpec((1, LANES, LANES), lambda i: (i, 0, 0))]
        rt_shapes = [sds((t // tm * SUBLANES, tm), F32), sds((t // tm, LANES, LANES), F32)]
        mq, mk, mv, tiles_per_seq = mem
        mem_args = [mq, mk, mv]
        mem_kv = pl.BlockSpec((N_MEM, MEM_W), lambda i: (i // tiles_per_seq, 0))
        mem_specs = [row(MEM_W), mem_kv, mem_kv]
    else:
        h2_spec, h2_shape = row(D_MODEL), sds((t, D_MODEL), BF16)
        rt_specs = [pl.BlockSpec((N_GROUPS, tm, E_PER_GROUP), lambda i: (0, i, 0))]
        rt_shapes = [sds((N_GROUPS, t, E_PER_GROUP), F32)]
        mem_args, mem_specs = [mem], [row(MEM_W)]
    return pl.pallas_call(
        functools.partial(_merge_kernel, sorted_moe=sorted_moe),
        grid=(t // tm,),
        in_specs=[row(D_MODEL), row(RV_W), row(FOX_W), *mem_specs, row(GATE_W),
                  _const_spec((RV_W, D_MODEL)), _const_spec((FOX_W, D_MODEL)), _const_spec((MEM_W, D_MODEL)),
                  _const_spec((D_MODEL, D_MODEL)), _const_spec((1, D_MODEL)),
                  _const_spec((D_MODEL, 2 * LANES)), _const_spec((1, LANES))],
        out_specs=[row(D_MODEL), h2_spec, *rt_specs],
        out_shape=[sds((t, D_MODEL), F32), h2_shape, *rt_shapes],
        compiler_params=_cparams(("parallel",)),
        name="merge",
    )(x2d, o_r, o_f, *mem_args, gates, wr, wf, wm, wo, g2, wrt, brt)


def _positions_kernel(route_ref, first_ref, pos_ref):
    tm = route_ref.shape[1]
    cidx = lax.broadcasted_iota(jnp.int32, (LANES, tm), 0).astype(F32)
    for i in range(first_ref.shape[0]):
        route = route_ref[i * SUBLANES:(i + 1) * SUBLANES, :]
        first = jnp.concatenate([first_ref[i]] * (tm // LANES), axis=1)
        start = jnp.sum(jnp.where(cidx == route[1:2, :], first, 0.0), axis=0, keepdims=True)
        pos_ref[i * SUBLANES:(i + 1) * SUBLANES, :] = jnp.broadcast_to(start + route[2:3, :],
                                                                       route.shape).astype(jnp.int32)


def _positions(route, first_row, tiles_per_step):
    rows, tm = route.shape
    n = rows // SUBLANES
    out = pl.pallas_call(
        _positions_kernel,
        grid=(n // tiles_per_step,),
        in_specs=[pl.BlockSpec((tiles_per_step * SUBLANES, tm), lambda i: (i, 0)),
                  pl.BlockSpec((tiles_per_step, LANES, LANES), lambda i: (i, 0, 0))],
        out_specs=pl.BlockSpec((tiles_per_step * SUBLANES, tm), lambda i: (i, 0)),
        out_shape=jax.ShapeDtypeStruct((rows, tm), jnp.int32),
        compiler_params=_cparams(("parallel",)),
        name="positions",
    )(route, first_row)
    return out.reshape(n, SUBLANES, tm)[:, 0, :].reshape(n * tm)


def _group_experts(h, cw, wg_ref, wu_ref, wd_ref, act):
    for e in range(E_PER_GROUP):
        a = _dot(h, wg_ref[0, e])
        u = _dot(h, wu_ref[0, e])
        act[:, e * D_EXPERT:(e + 1) * D_EXPERT] = ((a * jax.nn.sigmoid(a)) * u * cw(e)).astype(BF16)
    return _dot(act[...], wd_ref[0])


N_SLABS = D_MODEL // LANES
SUBLANES = 8


def _to_token_tiles(ref, x):
    for s in range(N_SLABS):
        ref[pl.ds(s, x.shape[0], stride=N_SLABS), :] = x[:, s * LANES:(s + 1) * LANES]


def _landing_shape(rows):
    return (rows // SUBLANES, N_SLABS, SUBLANES, LANES)


def _from_landing(buf):
    rows = buf.shape[0] * SUBLANES
    return jnp.concatenate([buf[:, s].reshape(rows, LANES) for s in range(N_SLABS)], axis=1)


def _row_gather(idx_ref, base, src_hbm, dst, sem, part=None):
    def body(i, _):
        for u in range(SUBLANES):
            pltpu.make_async_copy(src_hbm.at[idx_ref[base + i * SUBLANES + u]], dst.at[i, :, u, :],
                                  sem).start(priority=u % 2)
        return 0
    if part is None:
        lax.fori_loop(0, dst.shape[0], body, 0)
    else:
        j, n = part
        per = dst.shape[0] // n
        for i in range(j * per, (j + 1) * per):
            body(i, 0)


def _row_gather_wait(dst, sem):
    pltpu.make_async_copy(dst, dst, sem).wait()


def _moe_sorted_kernel(tg_ref, nvt_ref, pos_ref, h3_hbm, zeros_hbm, wrt_ref, brt_ref,
                       wg_ref, wu_ref, wd_ref, y_ref, xbuf, sem, hb, yacc, src_ref, clear_sem, *, tm):
    k = pl.program_id(0)
    nvt = nvt_ref[0]
    slot = lax.rem(k, 2)

    @pl.when(k == 0)
    def _():
        clear = pltpu.make_async_copy(zeros_hbm, src_ref, clear_sem)
        clear.start()
        clear.wait()

        def invert(t, _):
            src_ref[pos_ref[t]] = t
            return 0

        lax.fori_loop(0, pos_ref.shape[0], invert, 0, unroll=32)

    @pl.when((k == 0) & (nvt > 0))
    def _():
        _row_gather(src_ref, 0, h3_hbm, xbuf.at[0], sem.at[0])

    @pl.when(k < nvt)
    def _():
        _row_gather_wait(xbuf.at[slot], sem.at[slot])

        def fetch_part(e):
            _row_gather(src_ref, jnp.minimum(k + 1, nvt - 1) * tm, h3_hbm, xbuf.at[1 - slot],
                        sem.at[1 - slot], part=(e, E_PER_GROUP))

        x = _from_landing(xbuf.at[slot])
        group = tg_ref[k]
        _, i1, i2, w1, w2 = _route(_router_logits(x, wrt_ref, brt_ref), group.astype(F32))
        hb[...] = x.astype(BF16)
        yacc[...] = jnp.zeros(yacc.shape, F32)
        first = (group * E_PER_GROUP).astype(F32)
        used = [jnp.max(jnp.where(i1 == first + e, w1, 0.0) + jnp.where(i2 == first + e, w2, 0.0))
                for e in range(E_PER_GROUP)]
        cols = _rows_to_columns([i1, i2, w1, w2])
        for e in range(E_PER_GROUP):
            is_used = used[e] > 0.0

            @pl.when(jnp.logical_not(is_used))
            def _(e=e):
                fetch_part(e)

            @pl.when(is_used)
            def _(e=e):
                fetch_part(e)
                cw = (jnp.where(cols[:, 0:1] == first + e, cols[:, 2:3], 0.0)
                      + jnp.where(cols[:, 1:2] == first + e, cols[:, 3:4], 0.0))
                h = hb[...]
                a = _dot(h, wg_ref[0, e])
                u = _dot(h, wu_ref[0, e])
                act = ((a * jax.nn.sigmoid(a)) * u * cw).astype(BF16)
                yacc[...] += _dot(act, wd_ref[0, e * D_EXPERT:(e + 1) * D_EXPERT, :])

        y_ref[...] = yacc[...]

        @pl.when(k + 1 >= nvt)
        def _():
            _row_gather_wait(xbuf.at[1 - slot], sem.at[1 - slot])

    @pl.when(k >= nvt)
    def _():
        y_ref[...] = jnp.zeros(y_ref.shape, F32)


def _moe_sorted(h3, tile_group, n_valid_tiles, pos, n_tiles, tm, wrt, brt, wg, wu, wd):
    wspec = lambda shape: pl.BlockSpec(shape, lambda k, tg, nv, ps: (tg[k],) + (0,) * (len(shape) - 1))
    cspec = lambda shape: pl.BlockSpec(shape, lambda k, tg, nv, ps: (0,) * len(shape),
                                       pipeline_mode=pl.Buffered(1))
    return pl.pallas_call(
        functools.partial(_moe_sorted_kernel, tm=tm),
        grid_spec=pltpu.PrefetchScalarGridSpec(
            num_scalar_prefetch=3,
            grid=(n_tiles,),
            in_specs=[pl.BlockSpec(memory_space=pl.ANY), pl.BlockSpec(memory_space=pl.ANY),
                      cspec((D_MODEL, 2 * LANES)), cspec((1, LANES)),
                      wspec((1, E_PER_GROUP, D_MODEL, D_EXPERT)), wspec((1, E_PER_GROUP, D_MODEL, D_EXPERT)),
                      wspec((1, E_PER_GROUP * D_EXPERT, D_MODEL))],
            out_specs=pl.BlockSpec((tm, D_MODEL), lambda k, tg, nv, ps: (k, 0)),
            scratch_shapes=[pltpu.VMEM((2,) + _landing_shape(tm), F32), pltpu.SemaphoreType.DMA((2,)),
                            pltpu.VMEM((tm, D_MODEL), BF16), pltpu.VMEM((tm, D_MODEL), F32),
                            pltpu.SMEM((n_tiles * tm,), jnp.int32), pltpu.SemaphoreType.DMA(())]),
        out_shape=jax.ShapeDtypeStruct((n_tiles * tm, D_MODEL), F32),
        compiler_params=_cparams(("arbitrary",)),
        name="moe_sorted",
    )(tile_group, n_valid_tiles, pos, h3, jnp.zeros((n_tiles * tm,), jnp.int32), wrt, brt, wg, wu, wd)


def _flat_row_gather(idx_ref, base, src_hbm, dst, sem):
    def body(r, _):
        pltpu.make_async_copy(src_hbm.at[pl.ds(idx_ref[base + r], 1), :], dst.at[pl.ds(r, 1), :], sem).start()
        return 0
    lax.fori_loop(0, dst.shape[0], body, 0, unroll=8)


def _combine_kernel(pos_ref, y_hbm, x1_ref, o_ref, ybuf, sem, *, tm):
    k = pl.program_id(0)
    slot = lax.rem(k, 2)

    @pl.when(k == 0)
    def _():
        _flat_row_gather(pos_ref, 0, y_hbm, ybuf.at[0], sem.at[0])

    _row_gather_wait(ybuf.at[slot], sem.at[slot])

    @pl.when(k + 1 < pl.num_programs(0))
    def _():
        _flat_row_gather(pos_ref, (k + 1) * tm, y_hbm, ybuf.at[1 - slot], sem.at[1 - slot])

    o_ref[...] = x1_ref[...] + ybuf[slot]


def _combine(y3, pos, x1, tm):
    t = x1.shape[0]
    return pl.pallas_call(
        functools.partial(_combine_kernel, tm=tm),
        grid_spec=pltpu.PrefetchScalarGridSpec(
            num_scalar_prefetch=1,
            grid=(t // tm,),
            in_specs=[pl.BlockSpec(memory_space=pl.ANY), pl.BlockSpec((tm, D_MODEL), lambda k, ps: (k, 0))],
            out_specs=pl.BlockSpec((tm, D_MODEL), lambda k, ps: (k, 0)),
            scratch_shapes=[pltpu.VMEM((2, tm, D_MODEL), F32), pltpu.SemaphoreType.DMA((2,))]),
        out_shape=jax.ShapeDtypeStruct((t, D_MODEL), F32),
        compiler_params=_cparams(("arbitrary",)),
        name="combine",
    )(pos, y3, x1)


def _moe_kernel(h_ref, comb_ref, x1_ref, wg_ref, wu_ref, wd_ref, o_ref, act):
    g = pl.program_id(1)
    comb = comb_ref[0]
    y = _group_experts(h_ref[...], lambda e: comb[:, e:e + 1], wg_ref, wu_ref, wd_ref, act)

    @pl.when(g == 0)
    def _():
        o_ref[...] = x1_ref[...] + y

    @pl.when(g != 0)
    def _():
        o_ref[...] += y


def _moe(h2, comb, x1, tm, wg, wu, wd):
    t = h2.shape[0]
    return pl.pallas_call(
        _moe_kernel,
        grid=(t // tm, N_GROUPS),
        in_specs=[pl.BlockSpec((tm, D_MODEL), lambda i, g: (i, 0)),
                  pl.BlockSpec((1, tm, E_PER_GROUP), lambda i, g: (g, i, 0)),
                  pl.BlockSpec((tm, D_MODEL), lambda i, g: (i, 0)),
                  pl.BlockSpec((1, E_PER_GROUP, D_MODEL, D_EXPERT), lambda i, g: (g, 0, 0, 0)),
                  pl.BlockSpec((1, E_PER_GROUP, D_MODEL, D_EXPERT), lambda i, g: (g, 0, 0, 0)),
                  pl.BlockSpec((1, E_PER_GROUP * D_EXPERT, D_MODEL), lambda i, g: (g, 0, 0))],
        out_specs=pl.BlockSpec((tm, D_MODEL), lambda i, g: (i, 0)),
        out_shape=jax.ShapeDtypeStruct((t, D_MODEL), F32),
        scratch_shapes=[pltpu.VMEM((tm, E_PER_GROUP * D_EXPERT), BF16)],
        compiler_params=_cparams(("parallel", "arbitrary")),
        name="moe",
    )(h2, comb, x1, wg, wu, wd)


def _rope_tables(pos):
    half = DK_R // 2
    inv = ROPE_BASE ** (-jnp.arange(half, dtype=F32) / half)
    ang = pos.astype(F32)[:, None] * inv[None, :]
    c = jnp.cos(ang)
    s = jnp.sin(ang)
    return jnp.concatenate([c, c, c, c], axis=-1), jnp.concatenate([-s, s, -s, s], axis=-1)


def kernel(x_prompt, x_sample, state_ret, cache_fox_k, cache_fox_v, cache_fox_logf, cache_mem_k, cache_mem_v,
           mem_prompt, g_norm1, w_in, b_forget, g_fox_q, g_fox_k, g_mem_q, g_mem_in, w_mem_kv, g_mem_k,
           g_ret_out, w_br_ret, w_br_fox, w_br_mem, w_out, g_norm2, w_route_group, b_route_group,
           w_route_expert, b_route_expert, w_exp_gate, w_exp_up, w_exp_down):
    nb, seq, _ = x_prompt.shape
    nbs, n_new, _ = x_sample.shape
    past = cache_fox_k.shape[2]
    l = 0

    wi = w_in[l]
    o_ff = 2 * RQK_W + 2 * RV_W + 3 * FOX_W
    w_in_r = (wi[:, :o_ff].astype(BF16), wi[:, o_ff + H_F:].astype(BF16),
              jnp.pad(wi[:, o_ff:o_ff + H_F], ((0, 0), (0, LANES - H_F))).astype(BF16))
    bf_pad = jnp.concatenate([b_forget[l], jnp.zeros((LANES - H_F,), F32)])[None, :]
    g1 = g_norm1[l][None, :]
    gfq = jnp.tile(g_fox_q[l], H_F)[None, :]
    gfk = jnp.tile(g_fox_k[l], H_F)[None, :]
    gmq = g_mem_q[l][None, :]
    hid = jnp.arange(MXU_DIM) // D_F
    bd = jnp.where(hid[:, None] == hid[None, :], 1.0 / D_F, 0.0).astype(BF16)
    wr = w_br_ret[l].astype(BF16)
    wf = w_br_fox[l].astype(BF16)
    wm = w_br_mem[l].astype(BF16)
    wo = w_out[l].astype(BF16)
    g2 = g_norm2[l][None, :]
    n_e = N_GROUPS * E_PER_GROUP
    wrt = jnp.concatenate([w_route_expert[l], w_route_group[l],
                           jnp.zeros((D_MODEL, LANES - n_e - N_GROUPS), F32)], axis=1)
    wrt_hi = wrt.astype(BF16)
    wrt2 = jnp.concatenate([wrt_hi, (wrt - wrt_hi.astype(F32)).astype(BF16)], axis=1)
    brt = jnp.concatenate([b_route_expert[l], b_route_group[l],
                           jnp.zeros((LANES - n_e - N_GROUPS,), F32)])[None, :]
    wg = w_exp_gate[l].astype(BF16)
    wu = w_exp_up[l].astype(BF16)
    wd = w_exp_down[l].astype(BF16).reshape(N_GROUPS, E_PER_GROUP * D_EXPERT, D_MODEL)
    gro = g_ret_out[l][None, :]
    prep_w = (g1, w_in_r, bf_pad, gfq, gfk, gmq, bd)
    merge_w = (wr, wf, wm, wo, g2, wrt2, brt)

    tm = TOKEN_TILE
    xp = x_prompt.reshape(nb * seq, D_MODEL)
    (rq, rk, rv, rg, fqt, fkt, fvt, lft, mq, gates, fkb) = _prep(
        xp, _rope_tables(jnp.arange(seq)), tm, seq // tm, True, *prep_w)
    o_r, s_fin = _retention(rq, rk, rv, rg, nb, seq, MXU_DIM, gro, None)
    fcum = _cumsum(lft, MXU_DIM).reshape(nb * (H_F // 2), 2, seq)
    o_f = _fox_prompt(fqt, fkb, fvt, fcum, nb, seq, MXU_DIM)
    mk, mv = _memkv(mem_prompt.reshape(nb * N_MEM, D_MODEL), nb, g_mem_in[l][None, :],
                    w_mem_kv[l].astype(BF16), g_mem_k[l][None, :])
    x1, h3, route, counts = _merge(xp, o_r, o_f, (mq, mk, mv, seq // tm), gates, tm, True, *merge_w)
    t_p = nb * seq
    n_tiles = t_p // tm + N_GROUPS
    n_cls = N_GROUPS * N_PAIRS
    cnt = counts[:, :n_cls, 0].astype(jnp.int32)
    before_tile = jnp.cumsum(cnt, axis=0) - cnt
    cls_tot = jnp.sum(cnt, axis=0)
    grp_tot = jnp.sum(cls_tot.reshape(N_GROUPS, N_PAIRS), axis=1)
    tiles_g = (grp_tot + tm - 1) // tm
    tile_end = jnp.cumsum(tiles_g)
    row_start = (tile_end - tiles_g) * tm
    in_grp = cls_tot.reshape(N_GROUPS, N_PAIRS)
    cls_start = (row_start[:, None] + jnp.cumsum(in_grp, axis=1) - in_grp).reshape(n_cls)
    first_row = jnp.pad((cls_start[None, :] + before_tile).astype(F32), ((0, 0), (0, LANES - n_cls)))
    pos = _positions(route, jnp.broadcast_to(first_row[:, :, None], (t_p // tm, LANES, LANES)),
                     min(8, t_p // tm))
    tile_ids = jnp.arange(n_tiles, dtype=jnp.int32)
    tile_group = jnp.minimum(jnp.sum((tile_ids[:, None] >= tile_end[None, :]).astype(jnp.int32), axis=1),
                             N_GROUPS - 1)
    y3 = _moe_sorted(h3.reshape(t_p, N_SLABS, LANES), tile_group, tile_end[-1:], pos, n_tiles, tm,
                     wrt2, brt, wg, wu, wd)
    y_prompt = _combine(y3, pos, x1, 2 * tm).reshape(nb, seq, D_MODEL)

    ts = nbs * n_new
    xs = x_sample.reshape(ts, D_MODEL)
    pos_s = jnp.tile(past + jnp.arange(n_new), nbs)
    (rq_s, rk_s, rv_s, rg_s, fq_s, fk_s, fv_s, lf_s, mq_s, gates_s) = _prep(
        xs, _rope_tables(pos_s), ts, 1, False, *prep_w)
    o_r_s, s_new = _retention(rq_s, rk_s, rv_s, rg_s, nbs, n_new, n_new, gro, state_ret[l])
    pad = (-(past + n_new)) % MXU_DIM
    lf_rows = jnp.concatenate([jnp.swapaxes(cache_fox_logf[l], 1, 2),
                               jnp.swapaxes(lf_s.reshape(nbs, n_new, H_F), 1, 2),
                               jnp.zeros((nbs, H_F, pad), F32)], axis=2).reshape(nbs * H_F, past + n_new + pad)
    f_all = _cumsum(lf_rows, MXU_DIM)
    feat_major = lambda c: jnp.transpose(c, (0, 2, 3, 1)).reshape(nbs * FOX_W, past)
    o_f_s = _fox_sample(fq_s, feat_major(cache_fox_k[l]), feat_major(cache_fox_v[l]), fk_s, fv_s,
                        f_all[:, :past].reshape(nbs, H_F, past),
                        f_all[:, past:past + n_new].reshape(nbs, H_F, n_new), nbs, n_new, past)
    o_m_s = _memattn(mq_s, cache_mem_k[l].reshape(nbs * N_MEM, MEM_W),
                     cache_mem_v[l].reshape(nbs * N_MEM, MEM_W), nbs, n_new, n_new)
    x1_s, h2_s, comb_s = _merge(xs, o_r_s, o_f_s, o_m_s, gates_s, ts, False, *merge_w)
    y_sample = _moe(h2_s, comb_s, x1_s, ts, wg, wu, wd).reshape(nbs, n_new, D_MODEL)

    token_major = lambda a: jnp.transpose(a.reshape(nb, H_F, D_F, seq), (0, 3, 1, 2))

    return (y_prompt, y_sample,
            s_fin[None], token_major(fkt)[None], token_major(fvt)[None],
            jnp.swapaxes(lft.reshape(nb, H_F, seq), 1, 2)[None],
            mk.reshape(1, nb, N_MEM, H_M, D_M), mv.reshape(1, nb, N_MEM, H_M, D_M),
            s_new[None], fk_s.reshape(1, nbs, n_new, H_F, D_F), fv_s.reshape(1, nbs, n_new, H_F, D_F),
            lf_s.reshape(1, nbs, n_new, H_F))
```
